```python
import math, functools
import jax, jax.numpy as jnp
from jax import lax
import numpy as np

D_MODEL = 1024
BATCH = 2
SEQ = 16384
DEPTH = 2

N_EVEN = (DEPTH + 1) // 2
N_ODD = DEPTH // 2

A_HEADS = 8
A_HEAD_DIM = 64
A_WIDTH = A_HEADS * A_HEAD_DIM
MOBA_BLOCK = 256
MOBA_TOPK = 3
Q_CHUNK = 64
ROPE_THETA = 10000.0

B_HEADS = 8
B_HEAD_K = 32
B_HEAD_V = 64
B_KEY_WIDTH = B_HEADS * B_HEAD_K
B_VAL_WIDTH = B_HEADS * B_HEAD_V
GLA_GATE_RANK = 16
GLA_GATE_NORM = 16.0
GLA_CHUNK = 64

MIX_WIDTH = A_WIDTH + B_VAL_WIDTH
EVEN_PROJ = 3 * A_WIDTH + 2 * B_KEY_WIDTH + B_VAL_WIDTH + GLA_GATE_RANK + B_VAL_WIDTH
EVEN_SPLITS = (
    A_WIDTH,
    2 * A_WIDTH,
    3 * A_WIDTH,
    3 * A_WIDTH + B_KEY_WIDTH,
    3 * A_WIDTH + 2 * B_KEY_WIDTH,
    3 * A_WIDTH + 2 * B_KEY_WIDTH + B_VAL_WIDTH,
    3 * A_WIDTH + 2 * B_KEY_WIDTH + B_VAL_WIDTH + GLA_GATE_RANK,
)

D_RNN = D_MODEL
RG_BLOCKS = 4
RG_BLOCK_W = D_RNN // RG_BLOCKS
RG_CONV = 4
LRU_C = 8.0

D_FF = 3 * D_MODEL
FFN_CONV = 3

NORM_EPS = 1e-6
NEG_INF = -1e30

kernel_name = "hybrid_moba_gla_rglru_convffn"


def rms_norm(x, g):
    xf = x.astype(jnp.float32)
    y = xf * lax.rsqrt(jnp.mean(xf * xf, axis=-1, keepdims=True) + NORM_EPS)
    return (y * g.astype(jnp.float32)).astype(x.dtype)


def causal_dwconv(x, w, b):
    width = w.shape[0]
    seq = x.shape[1]
    xp = jnp.pad(x, ((0, 0), (width - 1, 0), (0, 0)))
    y = b
    for i in range(width):
        y = y + xp[:, i:i + seq] * w[i]
    return y


def rotary(x, pos):
    hd = x.shape[-1]
    half = hd // 2
    inv = ROPE_THETA ** (-jnp.arange(half, dtype=jnp.float32) / half)
    ang = pos.astype(jnp.float32)[:, None] * inv[None, :]
    cos, sin = jnp.cos(ang), jnp.sin(ang)
    xf = x.astype(jnp.float32)
    x1, x2 = xf[..., :half], xf[..., half:]
    return jnp.concatenate([x1 * cos - x2 * sin, x2 * cos + x1 * sin], axis=-1).astype(x.dtype)


def moba_attention(q, k, v):
    bsz, nh, seq, hd = q.shape
    nb = -(-seq // MOBA_BLOCK)
    s_pad = nb * MOBA_BLOCK
    pad = ((0, 0), (0, 0), (0, s_pad - seq), (0, 0))
    kp = jnp.pad(k, pad)
    vp = jnp.pad(v, pad)
    kb = kp.reshape(bsz, nh, nb, MOBA_BLOCK, hd)
    vb = vp.reshape(bsz, nh, nb, MOBA_BLOCK, hd)
    kmean = jnp.mean(kb.astype(jnp.float32), axis=3)
    n_sel = min(MOBA_TOPK, nb)
    scale = hd ** -0.5
    bi = jnp.arange(bsz)[:, None, None, None]
    hi = jnp.arange(nh)[None, :, None, None]
    blk_ids = jnp.arange(nb)
    n_chunks = seq // Q_CHUNK

    def one_chunk(c):
        start = c * Q_CHUNK
        qc = lax.dynamic_slice_in_dim(q, start, Q_CHUNK, axis=2).astype(jnp.float32)
        qpos = start + jnp.arange(Q_CHUNK)
        blk = start // MOBA_BLOCK
        gate = jnp.einsum('bhqd,bhnd->bhqn', qc, kmean)
        gate = jnp.where(blk_ids < blk, gate, -jnp.inf)
        _, idx = lax.top_k(gate, n_sel)
        valid = idx < blk
        kg = kb[bi, hi, idx].astype(jnp.float32)
        vg = vb[bi, hi, idx].astype(jnp.float32)
        s_sel = jnp.einsum('bhqd,bhqnkd->bhqnk', qc, kg) * scale
        s_sel = jnp.where(valid[..., None], s_sel, NEG_INF)
        own_start = blk * MOBA_BLOCK
        k_own = lax.dynamic_slice_in_dim(kp, own_start, MOBA_BLOCK, axis=2).astype(jnp.float32)
        v_own = lax.dynamic_slice_in_dim(vp, own_start, MOBA_BLOCK, axis=2).astype(jnp.float32)
        kpos = own_start + jnp.arange(MOBA_BLOCK)
        s_own = jnp.einsum('bhqd,bhkd->bhqk', qc, k_own) * scale
        s_own = jnp.where(kpos[None, :] <= qpos[:, None], s_own, NEG_INF)
        s = jnp.concatenate([s_sel.reshape(bsz, nh, Q_CHUNK, n_sel * MOBA_BLOCK), s_own], axis=-1)
        p = jax.nn.softmax(s, axis=-1)
        p_sel = p[..., :n_sel * MOBA_BLOCK].reshape(bsz, nh, Q_CHUNK, n_sel, MOBA_BLOCK)
        p_own = p[..., n_sel * MOBA_BLOCK:]
        o = (jnp.einsum('bhqnk,bhqnkd->bhqd', p_sel, vg)
             + jnp.einsum('bhqk,bhkd->bhqd', p_own, v_own))
        return o.astype(q.dtype)

    out = lax.map(one_chunk, jnp.arange(n_chunks))
    return out.transpose(1, 2, 0, 3, 4).reshape(bsz, nh, seq, hd)


def gla_attention(q, k, v, log_a):
    bsz, nh, seq, dk = q.shape
    dv = v.shape[-1]
    nc = seq // GLA_CHUNK

    def to_chunks(t):
        return t.astype(jnp.float32).reshape(bsz, nh, nc, GLA_CHUNK, t.shape[-1]).transpose(2, 0, 1, 3, 4)

    qc, kc, vc, gc = to_chunks(q), to_chunks(k), to_chunks(v), to_chunks(log_a)
    causal = jnp.tril(jnp.ones((GLA_CHUNK, GLA_CHUNK), dtype=bool))

    def step(state, inp):
        qi, ki, vi, gi = inp
        b = jnp.cumsum(gi, axis=2)
        diff = b[:, :, :, None, :] - b[:, :, None, :, :]
        decay = jnp.exp(jnp.where(causal[:, :, None], diff, -jnp.inf))
        attn = jnp.einsum('bhtd,bhsd,bhtsd->bhts', qi, ki, decay)
        o = (jnp.einsum('bhts,bhsv->bhtv', attn, vi)
             + jnp.einsum('bhtd,bhdv->bhtv', qi * jnp.exp(b), state))
        b_last = b[:, :, -1:, :]
        state = (jnp.exp(b_last[:, :, 0, :, None]) * state
                 + jnp.einsum('bhsd,bhsv->bhdv', ki * jnp.exp(b_last - b), vi))
        return state, o

    state0 = jnp.zeros((bsz, nh, dk, dv), jnp.float32)
    _, o = lax.scan(step, state0, (qc, kc, vc, gc))
    return o.transpose(1, 2, 0, 3, 4).reshape(bsz, nh, seq, dv).astype(v.dtype)


def even_mixer(x, w_in, w_gk2, b_gk2, gla_norm_g, w_out, pos):
    bsz, seq, _ = x.shape
    proj = x @ w_in
    aq, ak, av, bq, bk, bv, bgk, bog = jnp.split(proj, EVEN_SPLITS, axis=-1)

    def heads(t, n):
        return t.reshape(bsz, seq, n, -1).transpose(0, 2, 1, 3)

    o_a = moba_attention(rotary(heads(aq, A_HEADS), pos), rotary(heads(ak, A_HEADS), pos),
                         heads(av, A_HEADS))
    o_a = o_a.transpose(0, 2, 1, 3).reshape(bsz, seq, A_WIDTH)
    log_a = jax.nn.log_sigmoid((bgk @ w_gk2 + b_gk2).astype(jnp.float32)) / GLA_GATE_NORM
    o_b = gla_attention(heads(bq, B_HEADS) * (B_HEAD_K ** -0.5), heads(bk, B_HEADS),
                        heads(bv, B_HEADS), heads(log_a, B_HEADS))
    o_b = rms_norm(o_b, gla_norm_g[:, None, :])
    o_b = o_b.transpose(0, 2, 1, 3).reshape(bsz, seq, B_VAL_WIDTH) * jax.nn.silu(bog)
    return jnp.concatenate([o_a, o_b], axis=-1) @ w_out


def lru_combine(left, right):
    a1, b1 = left
    a2, b2 = right
    return a1 * a2, a2 * b1 + b2


def odd_mixer(x, w_in, conv_w, conv_b, w_a, b_a, w_x, b_x, lam, w_out):
    bsz, seq, _ = x.shape
    gate_branch, xr = jnp.split(x @ w_in, 2, axis=-1)
    xr = causal_dwconv(xr, conv_w, conv_b)
    xb = xr.reshape(bsz, seq, RG_BLOCKS, RG_BLOCK_W)
    r = jax.nn.sigmoid(jnp.einsum('bsnc,ncd->bsnd', xb, w_a).reshape(bsz, seq, D_RNN) + b_a)
    i = jax.nn.sigmoid(jnp.einsum('bsnc,ncd->bsnd', xb, w_x).reshape(bsz, seq, D_RNN) + b_x)
    log_a = (-LRU_C * r.astype(jnp.float32)) * jax.nn.softplus(-lam.astype(jnp.float32))
    a = jnp.exp(log_a)
    u = jnp.sqrt(-jnp.expm1(2.0 * log_a)) * (i * xr).astype(jnp.float32)
    _, h = lax.associative_scan(lru_combine, (a, u), axis=1)
    y = h.astype(x.dtype) * jax.nn.gelu(gate_branch)
    return y @ w_out


def conv_ffn(x, w_up, conv_w, conv_b, w_down):
    h = causal_dwconv(x @ w_up, conv_w, conv_b)
    u, g = jnp.split(h, 2, axis=-1)
    return (u * jax.nn.gelu(g)) @ w_down


def setup_inputs(seed: int = 0) -> dict:
    key = jax.random.key(seed)
    ks = jax.random.split(key, 24)

    def nrm(k, shape, scale):
        return jax.random.normal(k, shape, jnp.float32) * scale

    u = jax.random.uniform(ks[14], (N_ODD, D_RNN), jnp.float32, 0.9, 0.999)
    sig = u ** (1.0 / LRU_C)
    lam = jnp.log(sig) - jnp.log1p(-sig)
    return {
        "x": nrm(ks[0], (BATCH, SEQ, D_MODEL), 1.0),
        "mix_norm_g": 1.0 + nrm(ks[1], (DEPTH, D_MODEL), 0.02),
        "ffn_norm_g": 1.0 + nrm(ks[2], (DEPTH, D_MODEL), 0.02),
        "final_norm_g": 1.0 + nrm(ks[3], (D_MODEL,), 0.02),
        "ev_w_in": nrm(ks[4], (N_EVEN, D_MODEL, EVEN_PROJ), D_MODEL ** -0.5),
        "ev_w_gk2": nrm(ks[5], (N_EVEN, GLA_GATE_RANK, B_KEY_WIDTH), GLA_GATE_RANK ** -0.5),
        "ev_b_gk2": nrm(ks[6], (N_EVEN, B_KEY_WIDTH), 0.1),
        "ev_gla_norm_g": 1.0 + nrm(ks[7], (N_EVEN, B_HEADS, B_HEAD_V), 0.02),
        "ev_w_out": nrm(ks[8], (N_EVEN, MIX_WIDTH, D_MODEL), MIX_WIDTH ** -0.5),
        "od_w_in": nrm(ks[9], (N_ODD, D_MODEL, 2 * D_RNN), D_MODEL ** -0.5),
        "od_conv_w": nrm(ks[10], (N_ODD, RG_CONV, D_RNN), RG_CONV ** -0.5),
        "od_conv_b": nrm(ks[11], (N_ODD, D_RNN), 0.01),
        "od_w_a": nrm(ks[12], (N_ODD, RG_BLOCKS, RG_BLOCK_W, RG_BLOCK_W), RG_BLOCK_W ** -0.5),
        "od_b_a": nrm(ks[13], (N_ODD, D_RNN), 0.01),
        "od_w_x": nrm(ks[15], (N_ODD, RG_BLOCKS, RG_BLOCK_W, RG_BLOCK_W), RG_BLOCK_W ** -0.5),
        "od_b_x": nrm(ks[16], (N_ODD, D_RNN), 0.01),
        "od_lambda": lam,
        "od_w_out": nrm(ks[17], (N_ODD, D_RNN, D_MODEL), D_RNN ** -0.5),
        "ffn_w_up": nrm(ks[18], (DEPTH, D_MODEL, 2 * D_FF), D_MODEL ** -0.5),
        "ffn_conv_w": nrm(ks[19], (DEPTH, FFN_CONV, 2 * D_FF), FFN_CONV ** -0.5),
        "ffn_conv_b": nrm(ks[20], (DEPTH, 2 * D_FF), 0.01),
        "ffn_w_down": nrm(ks[21], (DEPTH, D_FF, D_MODEL), D_FF ** -0.5),
    }


def reference(x, mix_norm_g, ffn_norm_g, final_norm_g,
              ev_w_in, ev_w_gk2, ev_b_gk2, ev_gla_norm_g, ev_w_out,
              od_w_in, od_conv_w, od_conv_b, od_w_a, od_b_a, od_w_x, od_b_x, od_lambda, od_w_out,
              ffn_w_up, ffn_conv_w, ffn_conv_b, ffn_w_down):
    seq = x.shape[1]
    pos = jnp.arange(seq, dtype=jnp.int32)
    h = x
    for l in range(DEPTH):
        hn = rms_norm(h, mix_norm_g[l])
        j = l // 2
        if l % 2 == 0:
            mix = even_mixer(hn, ev_w_in[j], ev_w_gk2[j], ev_b_gk2[j], ev_gla_norm_g[j], ev_w_out[j], pos)
        else:
            mix = odd_mixer(hn, od_w_in[j], od_conv_w[j], od_conv_b[j], od_w_a[j], od_b_a[j],
                            od_w_x[j], od_b_x[j], od_lambda[j], od_w_out[j])
        h = h + mix
        h = h + conv_ffn(rms_norm(h, ffn_norm_g[l]), ffn_w_up[l], ffn_conv_w[l], ffn_conv_b[l], ffn_w_down[l])
    return rms_norm(h, final_norm_g)
```

```python
import functools

import jax
import jax.numpy as jnp
from jax import lax
from jax.experimental import pallas as pl
from jax.experimental.pallas import tpu as pltpu

F32 = jnp.float32
BF16 = jnp.bfloat16

A_HEADS = 8
A_HEAD_DIM = 64
A_WIDTH = A_HEADS * A_HEAD_DIM
MOBA_BLOCK = 256
MOBA_TOPK = 3
ROPE_THETA = 10000.0
B_HEADS = 8
B_HEAD_K = 32
B_HEAD_V = 64
B_KEY_WIDTH = B_HEADS * B_HEAD_K
B_VAL_WIDTH = B_HEADS * B_HEAD_V
GLA_GATE_RANK = 16
GLA_GATE_NORM = 16.0
RG_BLOCKS = 4
RG_CONV = 4
LRU_C = 8.0
FFN_CONV = 3
NORM_EPS = 1e-6
MASK_VALUE = -1e30

LANES = 128
SUBLANES = 8
VMEM_LIMIT = 56 * 1024 * 1024

GLA_CHUNK = 64
GLA_SUB = 16
MOBA_BLOCKS_PAD = LANES


def _params(sem):
    return pltpu.CompilerParams(dimension_semantics=sem, vmem_limit_bytes=VMEM_LIMIT)


def _rms(x, g):
    ms = jnp.mean(x * x, axis=-1, keepdims=True)
    return x * lax.rsqrt(ms + NORM_EPS) * g


def _gelu_tanh(x):
    return 0.5 * x * (1.0 + jnp.tanh(0.7978845608028654 * (x + 0.044715 * (x * x * x))))


def _log_sigmoid(z):
    return jnp.minimum(z, 0.0) - jnp.log1p(jnp.exp(-jnp.abs(z)))


def _resident(shape):
    nd = len(shape)
    return pl.BlockSpec(shape, lambda *_: (0,) * nd, pipeline_mode=pl.Buffered(1))


def _front0_kernel(x_ref, g_ref, w_ref, wgk_ref, bgk_ref, cos_ref, sa_ref, sb_ref,
                   q_ref, k_ref, v_ref, km_ref, bq_ref, bk_ref, bv_ref, gl_ref, og_ref, *, tm):
    hn = _rms(x_ref[0], g_ref[...]).astype(BF16)

    def proj(c0, n):
        return jnp.dot(hn, w_ref[:, c0:c0 + n], preferred_element_type=F32)

    cos, sa, sb = cos_ref[...], sa_ref[...], sb_ref[...]

    def rot(t):
        parts = []
        for c in range(t.shape[1] // LANES):
            tc = t[:, c * LANES:(c + 1) * LANES]
            parts.append(tc * cos + pltpu.roll(tc, LANES - 32, 1) * sa + pltpu.roll(tc, 32, 1) * sb)
        return jnp.concatenate(parts, axis=1)

    c = 0
    q_ref[0] = rot(proj(c, A_WIDTH)); c += A_WIDTH
    k = rot(proj(c, A_WIDTH)); c += A_WIDTH
    k_ref[0] = k.astype(BF16)
    nblk = tm // MOBA_BLOCK
    km_ref[...] = jnp.mean(k.reshape(nblk, MOBA_BLOCK, A_WIDTH), axis=1).reshape(nblk, 1, A_WIDTH)
    v_ref[0] = proj(c, A_WIDTH).astype(BF16); c += A_WIDTH
    bq_ref[0] = proj(c, B_KEY_WIDTH) * (B_HEAD_K ** -0.5); c += B_KEY_WIDTH
    bk_ref[0] = proj(c, B_KEY_WIDTH); c += B_KEY_WIDTH
    bv_ref[0] = proj(c, B_VAL_WIDTH); c += B_VAL_WIDTH
    og_ref[0] = proj(c, B_VAL_WIDTH); c += B_VAL_WIDTH
    gk = proj(c, LANES).astype(BF16)
    z = jnp.dot(gk, wgk_ref[...], preferred_element_type=F32) + bgk_ref[...]
    gl_ref[0] = _log_sigmoid(z) / GLA_GATE_NORM


def _front0(x, g, w_in, w_gk2, b_gk2):
    bsz, seq, d = x.shape
    tm = min(512, seq)
    aq, ak, av, bq, bk, bv, bgk, bog = jnp.split(
        w_in, [512, 1024, 1536, 1792, 2048, 2560, 2576], axis=1)
    gk_pad = LANES - GLA_GATE_RANK
    w = jnp.concatenate([aq, ak, av, bq, bk, bv, bog, jnp.pad(bgk, ((0, 0), (0, gk_pad)))],
                        axis=1).astype(BF16)
    wgk = jnp.pad(w_gk2, ((0, gk_pad), (0, 0))).astype(BF16)
    ncol = w.shape[1]

    half = A_HEAD_DIM // 2
    lane = jnp.arange(LANES)
    inv = ROPE_THETA ** (-jnp.arange(half, dtype=F32) / half)
    ang = jnp.arange(seq, dtype=jnp.int32).astype(F32)[:, None] * inv[None, :]
    cos_h, sin_h = jnp.cos(ang), jnp.sin(ang)
    fidx = (lane % A_HEAD_DIM) % half
    first = (lane % A_HEAD_DIM) < half
    cos_t = cos_h[:, fidx]
    sin_t = sin_h[:, fidx]
    sa = jnp.where(first[None, :], -sin_t, 0.0)
    sb = jnp.where(first[None, :], 0.0, sin_t)

    nb = seq // MOBA_BLOCK
    row = lambda n: pl.BlockSpec((1, tm, n), lambda b, i: (b, i, 0))
    tab = pl.BlockSpec((tm, LANES), lambda b, i: (i, 0))
    nt = seq // tm
    outs = pl.pallas_call(
        functools.partial(_front0_kernel, tm=tm),
        grid=(bsz, nt),
        in_specs=[row(d), _resident((1, d)), _resident((d, ncol)), _resident((LANES, B_KEY_WIDTH)),
                  _resident((1, B_KEY_WIDTH)), tab, tab, tab],
        out_specs=[row(A_WIDTH), row(A_WIDTH), row(A_WIDTH),
                   pl.BlockSpec((tm // MOBA_BLOCK, 1, A_WIDTH),
                                lambda b, i: (b * nt + i, 0, 0)),
                   row(B_KEY_WIDTH), row(B_KEY_WIDTH), row(B_VAL_WIDTH), row(B_KEY_WIDTH),
                   row(B_VAL_WIDTH)],
        out_shape=[jax.ShapeDtypeStruct((bsz, seq, A_WIDTH), F32),
                   jax.ShapeDtypeStruct((bsz, seq, A_WIDTH), BF16),
                   jax.ShapeDtypeStruct((bsz, seq, A_WIDTH), BF16),
                   jax.ShapeDtypeStruct((bsz * nb, 1, A_WIDTH), F32),
                   jax.ShapeDtypeStruct((bsz, seq, B_KEY_WIDTH), F32),
                   jax.ShapeDtypeStruct((bsz, seq, B_KEY_WIDTH), F32),
                   jax.ShapeDtypeStruct((bsz, seq, B_VAL_WIDTH), F32),
                   jax.ShapeDtypeStruct((bsz, seq, B_KEY_WIDTH), F32),
                   jax.ShapeDtypeStruct((bsz, seq, B_VAL_WIDTH), F32)],
        compiler_params=_params(("arbitrary", "arbitrary")),
        name="front0",
    )(x, g.reshape(1, d), w, wgk, b_gk2.reshape(1, B_KEY_WIDTH), cos_t, sa, sb)
    return outs


def _moba_kernel(q_ref, k_ref, v_ref, km_ref, o_ref, lhs_ref, acc_ref, m_ref, l_ref):
    qb = pl.program_id(2)
    blk = MOBA_BLOCK
    q = q_ref[0]
    km = km_ref[0]
    lane = lax.broadcasted_iota(jnp.int32, (1, LANES), 1)
    col = lax.broadcasted_iota(jnp.int32, (blk, MOBA_BLOCKS_PAD), 1)
    colf = col.astype(F32)
    scale = A_HEAD_DIM ** -0.5

    for h in range(2):
        qh = jnp.where(lane // A_HEAD_DIM == h, q, 0.0)
        gate = lax.dot_general(qh, km, (((1,), (1,)), ((), ())),
                               precision=lax.Precision.HIGHEST, preferred_element_type=F32)
        g = jnp.where(col < qb, gate, -jnp.inf)
        bias = jnp.full((blk, MOBA_BLOCKS_PAD), MASK_VALUE, F32)
        for _ in range(MOBA_TOPK):
            mx = jnp.max(g, axis=1, keepdims=True)
            first = jnp.min(jnp.where(g == mx, colf, float(MOBA_BLOCKS_PAD)), axis=1, keepdims=True)
            hit = colf == first
            bias = jnp.where(hit & (mx > -jnp.inf), 0.0, bias)
            g = jnp.where(hit, -jnp.inf, g)
        lhs_ref[h] = jnp.concatenate([(qh * scale).astype(BF16), bias.astype(BF16)], axis=1)

    own = pl.multiple_of(qb * blk, blk)
    k_own = k_ref[0, pl.ds(own, blk), :]
    v_own = v_ref[0, pl.ds(own, blk), :]
    rhs = jnp.concatenate([k_own, jnp.zeros((blk, MOBA_BLOCKS_PAD), BF16)], axis=1)
    causal = (lax.broadcasted_iota(jnp.int32, (blk, blk), 1)
              <= lax.broadcasted_iota(jnp.int32, (blk, blk), 0))
    for h in range(2):
        s = lax.dot_general(lhs_ref[h], rhs, (((1,), (1,)), ((), ())), preferred_element_type=F32)
        s = jnp.where(causal, s, MASK_VALUE)
        m = jnp.max(s, axis=1, keepdims=True)
        p = jnp.exp(s - m)
        m_ref[h] = m
        l_ref[h] = jnp.sum(p, axis=1, keepdims=True)
        acc_ref[h] = jnp.dot(p.astype(BF16), v_own, preferred_element_type=F32)

    def body(j, carry):
        off = pl.multiple_of(j * blk, blk)
        kj = k_ref[0, pl.ds(off, blk), :]
        vj = v_ref[0, pl.ds(off, blk), :]
        onehot = (lax.broadcasted_iota(jnp.int32, (blk, MOBA_BLOCKS_PAD), 1) == j).astype(BF16)
        rhs_j = jnp.concatenate([kj, onehot], axis=1)
        for h in range(2):
            s = lax.dot_general(lhs_ref[h], rhs_j, (((1,), (1,)), ((), ())),
                                preferred_element_type=F32)
            m_old = m_ref[h]
            m_new = jnp.maximum(m_old, jnp.max(s, axis=1, keepdims=True))
            alpha = jnp.exp(m_old - m_new)
            p = jnp.exp(s - m_new)
            m_ref[h] = m_new
            l_ref[h] = alpha * l_ref[h] + jnp.sum(p, axis=1, keepdims=True)
            acc_ref[h] = alpha * acc_ref[h] + jnp.dot(p.astype(BF16), vj, preferred_element_type=F32)
        return carry

    lax.fori_loop(0, qb, body, 0)

    o0 = acc_ref[0] / l_ref[0]
    o1 = acc_ref[1] / l_ref[1]
    o_ref[0] = jnp.where(lane < A_HEAD_DIM, o0, o1)


def _moba(q, k, v, kmean):
    bsz, seq, _ = q.shape
    nb = seq // MOBA_BLOCK
    assert nb <= MOBA_BLOCKS_PAD
    km = jnp.pad(kmean.reshape(bsz, nb, A_WIDTH), ((0, 0), (0, MOBA_BLOCKS_PAD - nb), (0, 0)))
    npair = A_WIDTH // LANES
    return pl.pallas_call(
        _moba_kernel,
        grid=(bsz, npair, nb),
        in_specs=[pl.BlockSpec((1, MOBA_BLOCK, LANES), lambda b, hp, i: (b, i, hp)),
                  pl.BlockSpec((1, seq, LANES), lambda b, hp, i: (b, 0, hp)),
                  pl.BlockSpec((1, seq, LANES), lambda b, hp, i: (b, 0, hp)),
                  pl.BlockSpec((1, MOBA_BLOCKS_PAD, LANES), lambda b, hp, i: (b, 0, hp))],
        out_specs=pl.BlockSpec((1, MOBA_BLOCK, LANES), lambda b, hp, i: (b, i, hp)),
        out_shape=jax.ShapeDtypeStruct((bsz, seq, A_WIDTH), F32),
        scratch_shapes=[pltpu.VMEM((2, MOBA_BLOCK, 2 * LANES), BF16),
                        pltpu.VMEM((2, MOBA_BLOCK, LANES), F32),
                        pltpu.VMEM((2, MOBA_BLOCK, 1), F32),
                        pltpu.VMEM((2, MOBA_BLOCK, 1), F32)],
        compiler_params=_params(("arbitrary", "arbitrary", "arbitrary")),
        name="moba",
    )(q, k, v, km)


def _gla_kernel(q_ref, k_ref, v_ref, g_ref, og_ref, ng_ref, o_ref, st_ref, *, tm):
    @pl.when(pl.program_id(1) == 0)
    def _():
        st_ref[...] = jnp.zeros_like(st_ref)

    q, k, v, g = q_ref[0], k_ref[0], v_ref[0], g_ref[0]
    kw, vw = B_KEY_WIDTH, B_VAL_WIDTH
    row = lax.broadcasted_iota(jnp.int32, (tm, 1), 0)
    r_sub = row % GLA_SUB
    r_chk = row % GLA_CHUNK

    def seg_cumsum(x, r, length):
        s = 1
        while s < length:
            x = x + jnp.where(r >= s, pltpu.roll(x, s, 0), 0.0)
            s *= 2
        return x

    c_sub = seg_cumsum(g, r_sub, GLA_SUB)
    b = seg_cumsum(g, r_chk, GLA_CHUNK)

    head_kv = (lax.broadcasted_iota(jnp.int32, (kw, vw), 0) // B_HEAD_K
               == lax.broadcasted_iota(jnp.int32, (kw, vw), 1) // B_HEAD_V).astype(BF16)
    head_vk = (lax.broadcasted_iota(jnp.int32, (vw, kw), 0) // B_HEAD_V
               == lax.broadcasted_iota(jnp.int32, (vw, kw), 1) // B_HEAD_K)
    head_vv = (lax.broadcasted_iota(jnp.int32, (vw, vw), 0) // B_HEAD_V
               == lax.broadcasted_iota(jnp.int32, (vw, vw), 1) // B_HEAD_V).astype(BF16)

    o_acc = jnp.zeros((tm, vw), F32)
    for d in range(GLA_SUB):
        if d == 0:
            w = q * k
            vv = v
        else:
            w = jnp.where(r_sub >= d,
                          q * pltpu.roll(k, d, 0) * jnp.exp(c_sub - pltpu.roll(c_sub, d, 0)), 0.0)
            vv = pltpu.roll(v, d, 0)
        o_acc = o_acc + jnp.dot(w.astype(BF16), head_kv, preferred_element_type=F32) * vv

    q_sub = q * jnp.exp(c_sub)
    q_chk = q * jnp.exp(b)

    nsub = GLA_CHUNK // GLA_SUB
    rr = lax.broadcasted_iota(jnp.int32, (GLA_CHUNK, 1), 0)
    lane_k = lax.broadcasted_iota(jnp.int32, (1, (nsub - 1) * kw), 1)
    lane_v = lax.broadcasted_iota(jnp.int32, (1, vw), 1)
    outs = []
    for c in range(tm // GLA_CHUNK):
        lo = c * GLA_CHUNK
        bc = b[lo:lo + GLA_CHUNK]
        kc = k[lo:lo + GLA_CHUNK]
        vc = v[lo:lo + GLA_CHUNK].astype(BF16)
        qs = q_sub[lo:lo + GLA_CHUNK]

        rhs_parts, lhs_parts = [], []
        for i in range(1, nsub):
            b_end = bc[i * GLA_SUB - 1:i * GLA_SUB, :]
            decay = jnp.exp(jnp.minimum(b_end - bc, 0.0))
            rhs_parts.append(jnp.where(rr < i * GLA_SUB, kc * decay, 0.0).astype(BF16))
            lhs_parts.append(jnp.where(rr // GLA_SUB == i, qs, 0.0).astype(BF16))
        rhs = jnp.concatenate(rhs_parts, axis=1)
        lhs_t = jnp.concatenate(lhs_parts, axis=1)
        lhs = jnp.concatenate(
            [jnp.where((lane_k % kw) // B_HEAD_K == h, lhs_t, jnp.zeros_like(lhs_t))
             for h in range(B_HEADS)], axis=0)
        attn = lax.dot_general(lhs, rhs, (((1,), (1,)), ((), ())), preferred_element_type=F32)
        res = jnp.dot(attn.astype(BF16), vc, preferred_element_type=F32)
        o_off = jnp.zeros((GLA_CHUNK, vw), F32)
        for h in range(B_HEADS):
            o_off = o_off + jnp.where(lane_v // B_HEAD_V == h,
                                      res[h * GLA_CHUNK:(h + 1) * GLA_CHUNK], 0.0)

        st = st_ref[...]
        o_int = lax.dot_general(q_chk[lo:lo + GLA_CHUNK].astype(BF16), st.astype(BF16),
                                (((1,), (1,)), ((), ())), preferred_element_type=F32)
        b_last = bc[GLA_CHUNK - 1:GLA_CHUNK, :]
        ks = (kc * jnp.exp(b_last - bc)).astype(BF16)
        upd = lax.dot_general(vc, ks, (((0,), (0,)), ((), ())), preferred_element_type=F32)
        st_ref[...] = st * jnp.exp(b_last) + jnp.where(head_vk, upd, 0.0)
        outs.append(o_off + o_int)

    o = o_acc + jnp.concatenate(outs, axis=0)

    sq = o * o
    sq_hi = sq.astype(BF16)
    sq_lo = (sq - sq_hi.astype(F32)).astype(BF16)
    ms = (jnp.dot(sq_hi, head_vv, preferred_element_type=F32)
          + jnp.dot(sq_lo, head_vv, preferred_element_type=F32)) * (1.0 / B_HEAD_V)
    og = og_ref[0]
    o_ref[0] = (o * lax.rsqrt(ms + NORM_EPS) * ng_ref[...]) * (og * jax.nn.sigmoid(og))


def _gla(bq, bk, bv, gl, og, norm_g):
    bsz, seq, _ = bq.shape
    tm = min(256, seq)
    row = lambda n: pl.BlockSpec((1, tm, n), lambda b, i: (b, i, 0))
    return pl.pallas_call(
        functools.partial(_gla_kernel, tm=tm),
        grid=(bsz, seq // tm),
        in_specs=[row(B_KEY_WIDTH), row(B_KEY_WIDTH), row(B_VAL_WIDTH), row(B_KEY_WIDTH),
                  row(B_VAL_WIDTH), _resident((1, B_VAL_WIDTH))],
        out_specs=row(B_VAL_WIDTH),
        out_shape=jax.ShapeDtypeStruct((bsz, seq, B_VAL_WIDTH), F32),
        scratch_shapes=[pltpu.VMEM((B_VAL_WIDTH, B_KEY_WIDTH), F32)],
        compiler_params=_params(("arbitrary", "arbitrary")),
        name="gla",
    )(bq, bk, bv, gl, og, norm_g.reshape(1, B_VAL_WIDTH))


def _outproj_kernel(x_ref, oa_ref, ob_ref, wa_ref, wb_ref, o_ref):
    o_ref[0] = (x_ref[0]
                + jnp.dot(oa_ref[0].astype(BF16), wa_ref[...], preferred_element_type=F32)
                + jnp.dot(ob_ref[0].astype(BF16), wb_ref[...], preferred_element_type=F32))


def _outproj(x, oa, ob, w_out):
    bsz, seq, d = x.shape
    tm = min(512, seq)
    row = lambda n: pl.BlockSpec((1, tm, n), lambda b, i: (b, i, 0))
    w = w_out.astype(BF16)
    return pl.pallas_call(
        _outproj_kernel,
        grid=(bsz, seq // tm),
        in_specs=[row(d), row(A_WIDTH), row(B_VAL_WIDTH),
                  _resident((A_WIDTH, d)), _resident((B_VAL_WIDTH, d))],
        out_specs=row(d),
        out_shape=jax.ShapeDtypeStruct((bsz, seq, d), F32),
        compiler_params=_params(("arbitrary", "arbitrary")),
        name="outproj",
    )(x, oa, ob, w[:A_WIDTH], w[A_WIDTH:])


def _ffn_kernel(x_ref, g_ref, wup_ref, cw_ref, cb_ref, wdn_ref, fg_ref, o_ref,
                halo_ref, ext_ref, *, tm, bn, dff, final_norm):
    @pl.when(pl.program_id(1) == 0)
    def _():
        halo_ref[...] = jnp.zeros_like(halo_ref)

    x = x_ref[0]
    hn = _rms(x, g_ref[...]).astype(BF16)
    acc = x
    for kb in range(dff // bn):
        halves = []
        for part in range(2):
            c0 = part * dff + kb * bn
            up = jnp.dot(hn, wup_ref[:, c0:c0 + bn], preferred_element_type=F32)
            ext_ref[part, 0:SUBLANES, :] = halo_ref[:, c0:c0 + bn]
            ext_ref[part, SUBLANES:, :] = up
            halo_ref[:, c0:c0 + bn] = up[tm - SUBLANES:, :]
            y = cb_ref[:, c0:c0 + bn]
            for i in range(FFN_CONV):
                off = SUBLANES - (FFN_CONV - 1) + i
                y = y + ext_ref[part, pl.ds(off, tm), :] * cw_ref[i:i + 1, c0:c0 + bn]
            halves.append(y)
        a = (halves[0] * _gelu_tanh(halves[1])).astype(BF16)
        acc = acc + jnp.dot(a, wdn_ref[kb * bn:(kb + 1) * bn, :], preferred_element_type=F32)
    if final_norm:
        acc = _rms(acc, fg_ref[...])
    o_ref[0] = acc


def _conv_ffn(x, g, w_up, conv_w, conv_b, w_down, final_g, final_norm):
    bsz, seq, d = x.shape
    dff = w_down.shape[0]
    tm = min(512, seq)
    bn = 512
    row = pl.BlockSpec((1, tm, d), lambda b, i: (b, i, 0))
    return pl.pallas_call(
        functools.partial(_ffn_kernel, tm=tm, bn=bn, dff=dff, final_norm=final_norm),
        grid=(bsz, seq // tm),
        in_specs=[row, _resident((1, d)), _resident((d, 2 * dff)), _resident((FFN_CONV, 2 * dff)),
                  _resident((1, 2 * dff)), _resident((dff, d)), _resident((1, d))],
        out_specs=row,
        out_shape=jax.ShapeDtypeStruct((bsz, seq, d), F32),
        scratch_shapes=[pltpu.VMEM((SUBLANES, 2 * dff), F32),
                        pltpu.VMEM((2, tm + SUBLANES, bn), F32)],
        compiler_params=_params(("arbitrary", "arbitrary")),
        name="conv_ffn_final" if final_norm else "conv_ffn",
    )(x, g.reshape(1, d), w_up.astype(BF16), conv_w, conv_b.reshape(1, 2 * dff),
      w_down.astype(BF16), final_g.reshape(1, d))


def _rglru_kernel(x_ref, g_ref, win_ref, cw_ref, cb_ref, wa_ref, ba_ref, wx_ref, bx_ref,
                  lam_ref, wout_ref, o_ref, ext_ref, ca_ref, cu_ref, h_ref, hlast_ref, *, tm, drnn):
    @pl.when(pl.program_id(1) == 0)
    def _():
        ext_ref[0:SUBLANES, :] = jnp.zeros((SUBLANES, drnn), F32)
        hlast_ref[...] = jnp.zeros_like(hlast_ref)

    x = x_ref[0]
    hn = _rms(x, g_ref[...]).astype(BF16)
    gate = jnp.dot(hn, win_ref[:, 0:drnn], preferred_element_type=F32)
    xr_raw = jnp.dot(hn, win_ref[:, drnn:2 * drnn], preferred_element_type=F32)

    ext_ref[SUBLANES:, :] = xr_raw
    xr = cb_ref[...]
    for i in range(RG_CONV):
        off = SUBLANES - (RG_CONV - 1) + i
        xr = xr + ext_ref[pl.ds(off, tm), :] * cw_ref[i:i + 1, :]
    ext_ref[0:SUBLANES, :] = xr_raw[tm - SUBLANES:, :]

    bw = drnn // RG_BLOCKS
    xb = xr.astype(BF16)
    ra, ri = [], []
    for n in range(RG_BLOCKS):
        xn = xb[:, n * bw:(n + 1) * bw]
        ra.append(jnp.dot(xn, wa_ref[n], preferred_element_type=F32))
        ri.append(jnp.dot(xn, wx_ref[n], preferred_element_type=F32))
    r = jax.nn.sigmoid(jnp.concatenate(ra, axis=1) + ba_ref[...])
    ig = jax.nn.sigmoid(jnp.concatenate(ri, axis=1) + bx_ref[...])
    log_a = r * lam_ref[...]
    a = jnp.exp(log_a)
    th = jnp.tanh(log_a)
    one_minus_a2 = -2.0 * th / (1.0 - th)
    u = jnp.sqrt(one_minus_a2) * (ig * xr)

    r8 = lax.broadcasted_iota(jnp.int32, (tm, 1), 0) % SUBLANES
    ca, cu = a, u
    s = 1
    while s < SUBLANES:
        keep = r8 >= s
        a_sh = jnp.where(keep, pltpu.roll(ca, s, 0), 1.0)
        u_sh = jnp.where(keep, pltpu.roll(cu, s, 0), 0.0)
        cu = cu + ca * u_sh
        ca = ca * a_sh
        s *= 2
    ca_ref[...] = ca
    cu_ref[...] = cu

    def group(gi, h_prev):
        lo = pl.multiple_of(gi * SUBLANES, SUBLANES)
        hg = cu_ref[pl.ds(lo, SUBLANES), :] + ca_ref[pl.ds(lo, SUBLANES), :] * h_prev
        h_ref[pl.ds(lo, SUBLANES), :] = hg
        return hg[SUBLANES - 1:SUBLANES, :]

    h_last = lax.fori_loop(0, tm // SUBLANES, group, hlast_ref[...])
    hlast_ref[...] = h_last

    y = (h_ref[...] * _gelu_tanh(gate)).astype(BF16)
    o_ref[0] = x + jnp.dot(y, wout_ref[...], preferred_element_type=F32)


def _rglru(x, g, w_in, conv_w, conv_b, w_a, b_a, w_x, b_x, lam, w_out):
    bsz, seq, d = x.shape
    drnn = w_out.shape[0]
    bw = drnn // RG_BLOCKS
    tm = min(256, seq)
    row = pl.BlockSpec((1, tm, d), lambda b, i: (b, i, 0))
    lam_c = (-LRU_C) * jax.nn.softplus(-lam.astype(F32))
    return pl.pallas_call(
        functools.partial(_rglru_kernel, tm=tm, drnn=drnn),
        grid=(bsz, seq // tm),
        in_specs=[row, _resident((1, d)), _resident((d, 2 * drnn)), _resident((RG_CONV, drnn)),
                  _resident((1, drnn)), _resident((RG_BLOCKS, bw, bw)), _resident((1, drnn)),
                  _resident((RG_BLOCKS, bw, bw)), _resident((1, drnn)), _resident((1, drnn)),
                  _resident((drnn, d))],
        out_specs=row,
        out_shape=jax.ShapeDtypeStruct((bsz, seq, d), F32),
        scratch_shapes=[pltpu.VMEM((tm + SUBLANES, drnn), F32),
                        pltpu.VMEM((tm, drnn), F32),
                        pltpu.VMEM((tm, drnn), F32),
                        pltpu.VMEM((tm, drnn), F32),
                        pltpu.VMEM((1, drnn), F32)],
        compiler_params=_params(("arbitrary", "arbitrary")),
        name="rglru",
    )(x, g.reshape(1, d), w_in.astype(BF16), conv_w, conv_b.reshape(1, drnn),
      w_a.astype(BF16), b_a.reshape(1, drnn), w_x.astype(BF16), b_x.reshape(1, drnn),
      lam_c.reshape(1, drnn), w_out.astype(BF16))


def kernel(x, mix_norm_g, ffn_norm_g, final_norm_g, ev_w_in, ev_w_gk2, ev_b_gk2, ev_gla_norm_g,
           ev_w_out, od_w_in, od_conv_w, od_conv_b, od_w_a, od_b_a, od_w_x, od_b_x, od_lambda,
           od_w_out, ffn_w_up, ffn_conv_w, ffn_conv_b, ffn_w_down):
    depth = mix_norm_g.shape[0]
    h = x
    for l in range(depth):
        j = l // 2
        if l % 2 == 0:
            q, k, v, kmean, bq, bk, bv, gl, og = _front0(h, mix_norm_g[l], ev_w_in[j], ev_w_gk2[j],
                                                       ev_b_gk2[j])
            o_a = _moba(q, k, v, kmean)
            o_b = _gla(bq, bk, bv, gl, og, ev_gla_norm_g[j])
            h = _outproj(h, o_a, o_b, ev_w_out[j])
        else:
            h = _rglru(h, mix_norm_g[l], od_w_in[j], od_conv_w[j], od_conv_b[j], od_w_a[j],
                       od_b_a[j], od_w_x[j], od_b_x[j], od_lambda[j], od_w_out[j])
        h = _conv_ffn(h, ffn_norm_g[l], ffn_w_up[l], ffn_conv_w[l], ffn_conv_b[l], ffn_w_down[l],
                      final_norm_g, final_norm=(l == depth - 1))
    return h
```

```python
import functools

import jax
import jax.numpy as jnp
from jax import lax
from jax.experimental import pallas as pl
from jax.experimental.pallas import tpu as pltpu

F32 = jnp.float32
BF16 = jnp.bfloat16

A_HEADS = 8
A_HEAD_DIM = 64
A_WIDTH = A_HEADS * A_HEAD_DIM
MOBA_BLOCK = 256
MOBA_TOPK = 3
ROPE_THETA = 10000.0
B_HEADS = 8
B_HEAD_K = 32
B_HEAD_V = 64
B_KEY_WIDTH = B_HEADS * B_HEAD_K
B_VAL_WIDTH = B_HEADS * B_HEAD_V
GLA_GATE_RANK = 16
GLA_GATE_NORM = 16.0
RG_BLOCKS = 4
RG_CONV = 4
LRU_C = 8.0
FFN_CONV = 3
NORM_EPS = 1e-6
MASK_VALUE = -1e30

LANES = 128
SUBLANES = 8
VMEM_LIMIT = 56 * 1024 * 1024

GLA_CHUNK = 64
GLA_SUB = 16
MOBA_BLOCKS_PAD = LANES
V_ROWS = 80


def _params(sem):
    return pltpu.CompilerParams(dimension_semantics=sem, vmem_limit_bytes=VMEM_LIMIT)


def _rms(x, g):
    ms = jnp.mean(x * x, axis=-1, keepdims=True)
    return x * lax.rsqrt(ms + NORM_EPS) * g


def _gelu_tanh(x):
    return 0.5 * x * (1.0 + jnp.tanh(0.7978845608028654 * (x + 0.044715 * (x * x * x))))


def _log_sigmoid(z):
    return jnp.minimum(z, 0.0) - jnp.log1p(jnp.exp(-jnp.abs(z)))


def _resident(shape):
    nd = len(shape)
    return pl.BlockSpec(shape, lambda *_: (0,) * nd, pipeline_mode=pl.Buffered(1))


def _front0_kernel(x_ref, g_ref, w_ref, wt_ref, wgk_ref, bgk_ref, cos_ref, sa_ref, sb_ref,
                   cost_ref, sat_ref, sbt_ref,
                   q_ref, k_ref, v_ref, km_ref, bq_ref, bk_ref, bv_ref, gl_ref, og_ref, *, tm):
    hn = _rms(x_ref[0], g_ref[...]).astype(BF16)

    def proj(c0, n):
        return jnp.dot(hn, w_ref[:, c0:c0 + n], preferred_element_type=F32)

    def proj_t(r0, n):
        return lax.dot_general(wt_ref[r0:r0 + n, :], hn, (((1,), (1,)), ((), ())),
                               preferred_element_type=F32)

    def rot(t, cos, sa, sb, axis):
        parts = []
        for c in range(t.shape[axis] // LANES):
            tc = lax.slice_in_dim(t, c * LANES, (c + 1) * LANES, axis=axis)
            parts.append(tc * cos + pltpu.roll(tc, LANES - 32, axis) * sa
                         + pltpu.roll(tc, 32, axis) * sb)
        return jnp.concatenate(parts, axis=axis)

    q_ref[0] = rot(proj_t(0, A_WIDTH), cost_ref[...], sat_ref[...], sbt_ref[...], 0)
    vt = proj_t(A_WIDTH, A_WIDTH)
    c = 0
    k = rot(proj(c, A_WIDTH), cos_ref[...], sa_ref[...], sb_ref[...], 1); c += A_WIDTH
    k_ref[0] = k.astype(BF16)
    nblk = tm // MOBA_BLOCK
    km_ref[...] = jnp.mean(k.reshape(nblk, MOBA_BLOCK, A_WIDTH), axis=1).reshape(nblk, 1, A_WIDTH)
    ones_rows = (lax.broadcasted_iota(jnp.int32, (V_ROWS - A_HEAD_DIM, MOBA_BLOCK), 0)
                 == 0).astype(BF16)
    for h in range(A_HEADS):
        for n in range(nblk):
            v_ref[0, h, n, 0:A_HEAD_DIM, :] = vt[h * A_HEAD_DIM:(h + 1) * A_HEAD_DIM,
                                                 n * MOBA_BLOCK:(n + 1) * MOBA_BLOCK].astype(BF16)
            v_ref[0, h, n, A_HEAD_DIM:, :] = ones_rows
    bq_ref[0] = proj(c, B_KEY_WIDTH) * (B_HEAD_K ** -0.5); c += B_KEY_WIDTH
    bk_ref[0] = proj(c, B_KEY_WIDTH); c += B_KEY_WIDTH
    bv_ref[0] = proj(c, B_VAL_WIDTH); c += B_VAL_WIDTH
    og_ref[0] = proj(c, B_VAL_WIDTH); c += B_VAL_WIDTH
    gk = proj(c, LANES).astype(BF16)
    z = jnp.dot(gk, wgk_ref[...], preferred_element_type=F32) + bgk_ref[...]
    gl_ref[0] = _log_sigmoid(z) / GLA_GATE_NORM


def _front0(x, g, w_in, w_gk2, b_gk2):
    bsz, seq, d = x.shape
    tm = min(512, seq)
    aq, ak, av, bq, bk, bv, bgk, bog = jnp.split(
        w_in, [512, 1024, 1536, 1792, 2048, 2560, 2576], axis=1)
    gk_pad = LANES - GLA_GATE_RANK
    w = jnp.concatenate([ak, bq, bk, bv, bog, jnp.pad(bgk, ((0, 0), (0, gk_pad)))],
                        axis=1).astype(BF16)
    wt = jnp.concatenate([aq, av], axis=1).T.astype(BF16)
    wgk = jnp.pad(w_gk2, ((0, gk_pad), (0, 0))).astype(BF16)
    ncol = w.shape[1]

    half = A_HEAD_DIM // 2
    lane = jnp.arange(LANES)
    inv = ROPE_THETA ** (-jnp.arange(half, dtype=F32) / half)
    ang = jnp.arange(seq, dtype=jnp.int32).astype(F32)[:, None] * inv[None, :]
    cos_h, sin_h = jnp.cos(ang), jnp.sin(ang)
    first = (lane % A_HEAD_DIM) < half
    cos_t = jnp.tile(cos_h, (1, LANES // half))
    sin_t = jnp.tile(sin_h, (1, LANES // half))
    sa = jnp.where(first[None, :], -sin_t, 0.0)
    sb = jnp.where(first[None, :], 0.0, sin_t)

    nb = seq // MOBA_BLOCK
    row = lambda n: pl.BlockSpec((1, tm, n), lambda b, i: (b, i, 0))
    tab = pl.BlockSpec((tm, LANES), lambda b, i: (i, 0))
    tab_t = pl.BlockSpec((LANES, tm), lambda b, i: (0, i))
    nt = seq // tm
    outs = pl.pallas_call(
        functools.partial(_front0_kernel, tm=tm),
        grid=(bsz, nt),
        in_specs=[row(d), _resident((1, d)), _resident((d, ncol)), _resident((2 * A_WIDTH, d)),
                  _resident((LANES, B_KEY_WIDTH)), _resident((1, B_KEY_WIDTH)),
                  tab, tab, tab, tab_t, tab_t, tab_t],
        out_specs=[pl.BlockSpec((1, A_WIDTH, tm), lambda b, i: (b, 0, i)), row(A_WIDTH),
                   pl.BlockSpec((1, A_HEADS, tm // MOBA_BLOCK, V_ROWS, MOBA_BLOCK),
                                lambda b, i: (b, 0, i, 0, 0)),
                   pl.BlockSpec((tm // MOBA_BLOCK, 1, A_WIDTH),
                                lambda b, i: (b * nt + i, 0, 0)),
                   row(B_KEY_WIDTH), row(B_KEY_WIDTH), row(B_VAL_WIDTH), row(B_KEY_WIDTH),
                   row(B_VAL_WIDTH)],
        out_shape=[jax.ShapeDtypeStruct((bsz, A_WIDTH, seq), F32),
                   jax.ShapeDtypeStruct((bsz, seq, A_WIDTH), BF16),
                   jax.ShapeDtypeStruct((bsz, A_HEADS, nb, V_ROWS, MOBA_BLOCK), BF16),
                   jax.ShapeDtypeStruct((bsz * nb, 1, A_WIDTH), F32),
                   jax.ShapeDtypeStruct((bsz, seq, B_KEY_WIDTH), F32),
                   jax.ShapeDtypeStruct((bsz, seq, B_KEY_WIDTH), F32),
                   jax.ShapeDtypeStruct((bsz, seq, B_VAL_WIDTH), F32),
                   jax.ShapeDtypeStruct((bsz, seq, B_KEY_WIDTH), F32),
                   jax.ShapeDtypeStruct((bsz, seq, B_VAL_WIDTH), F32)],
        compiler_params=_params(("arbitrary", "arbitrary")),
        name="front0",
    )(x, g.reshape(1, d), w, wt, wgk, b_gk2.reshape(1, B_KEY_WIDTH), cos_t, sa, sb,
      cos_t.T, sa.T, sb.T)
    return outs


def _moba_kernel(qt_ref, k_ref, vt_ref, km_ref, o_ref, rhs_ref, acc_ref, *, group):
    qb = pl.program_id(2)
    blk = MOBA_BLOCK
    qt = qt_ref[0]
    km = km_ref[0]
    feat = lax.broadcasted_iota(jnp.int32, (LANES, 1), 0)
    brow = lax.broadcasted_iota(jnp.int32, (MOBA_BLOCKS_PAD, blk), 0)
    browf = brow.astype(F32)
    scale = A_HEAD_DIM ** -0.5

    for h in range(2):
        qh = jnp.where(feat // A_HEAD_DIM == h, qt, 0.0)
        gate = jnp.dot(km, qh, precision=lax.Precision.HIGHEST, preferred_element_type=F32)
        g = jnp.where(brow < qb, gate, -jnp.inf)
        bias = jnp.full((MOBA_BLOCKS_PAD, blk), MASK_VALUE, F32)
        for _ in range(MOBA_TOPK):
            mx = jnp.max(g, axis=0, keepdims=True)
            first = jnp.min(jnp.where(g == mx, browf, float(MOBA_BLOCKS_PAD)), axis=0, keepdims=True)
            hit = browf == first
            bias = jnp.where(hit & (mx > -jnp.inf), 0.0, bias)
            g = jnp.where(hit, -jnp.inf, g)
        rhs_ref[:, h * blk:(h + 1) * blk] = jnp.concatenate(
            [(qh * scale).astype(BF16), bias.astype(BF16)], axis=0)

    own = pl.multiple_of(qb * blk, blk)
    lhs = jnp.concatenate([k_ref[0, pl.ds(own, blk), :], jnp.zeros((blk, MOBA_BLOCKS_PAD), BF16)],
                          axis=1)
    causal = (lax.broadcasted_iota(jnp.int32, (blk, 2 * blk), 0)
              <= lax.broadcasted_iota(jnp.int32, (blk, 2 * blk), 1) % blk)
    s = jnp.where(causal, jnp.dot(lhs, rhs_ref[...], preferred_element_type=F32), MASK_VALUE)
    m0 = jnp.max(s, axis=0, keepdims=True)
    p = jnp.exp(s - m0).astype(BF16)
    for h in range(2):
        acc_ref[h] = jnp.dot(vt_ref[0, h, qb], p[:, h * blk:(h + 1) * blk],
                             preferred_element_type=F32)

    keys = group * blk
    blk_of_row = lax.broadcasted_iota(jnp.int32, (keys, MOBA_BLOCKS_PAD), 0) // blk
    lane_blk = lax.broadcasted_iota(jnp.int32, (keys, MOBA_BLOCKS_PAD), 1)

    def body(i, m):
        j0 = i * group
        off = pl.multiple_of(j0 * blk, blk)
        onehot = (lane_blk == blk_of_row + j0).astype(BF16)
        lhs_j = jnp.concatenate([k_ref[0, pl.ds(off, keys), :], onehot], axis=1)
        s = jnp.dot(lhs_j, rhs_ref[...], preferred_element_type=F32)
        m_new = jnp.maximum(m, jnp.max(s, axis=0, keepdims=True))
        alpha = jnp.exp(m - m_new)
        p = jnp.exp(s - m_new).astype(BF16)
        for h in range(2):
            vt = jnp.concatenate([vt_ref[0, h, j0 + u] for u in range(group)], axis=1)
            acc_ref[h] = (alpha[:, h * blk:(h + 1) * blk] * acc_ref[h]
                          + jnp.dot(vt, p[:, h * blk:(h + 1) * blk], preferred_element_type=F32))
        return m_new

    lax.fori_loop(0, (qb + group - 1) // group, body, m0)

    outs = []
    for h in range(2):
        a = acc_ref[h]
        outs.append(a[0:A_HEAD_DIM] / a[A_HEAD_DIM:A_HEAD_DIM + 1])
    o_ref[0] = jnp.concatenate(outs, axis=0).T


def _moba(qt, k, vt, kmean):
    bsz, seq, _ = k.shape
    nb = seq // MOBA_BLOCK
    assert nb <= MOBA_BLOCKS_PAD
    km = jnp.pad(kmean.reshape(bsz, nb, A_WIDTH), ((0, 0), (0, MOBA_BLOCKS_PAD - nb), (0, 0)))
    npair = A_WIDTH // LANES
    group = 4 if nb % 4 == 0 else 1
    return pl.pallas_call(
        functools.partial(_moba_kernel, group=group),
        grid=(bsz, npair, nb),
        in_specs=[pl.BlockSpec((1, LANES, MOBA_BLOCK), lambda b, hp, i: (b, hp, i)),
                  pl.BlockSpec((1, seq, LANES), lambda b, hp, i: (b, 0, hp)),
                  pl.BlockSpec((1, 2, nb, V_ROWS, MOBA_BLOCK), lambda b, hp, i: (b, hp, 0, 0, 0)),
                  pl.BlockSpec((1, MOBA_BLOCKS_PAD, LANES), lambda b, hp, i: (b, 0, hp))],
        out_specs=pl.BlockSpec((1, MOBA_BLOCK, LANES), lambda b, hp, i: (b, i, hp)),
        out_shape=jax.ShapeDtypeStruct((bsz, seq, A_WIDTH), F32),
        scratch_shapes=[pltpu.VMEM((2 * LANES, 2 * MOBA_BLOCK), BF16),
                        pltpu.VMEM((2, V_ROWS, MOBA_BLOCK), F32)],
        compiler_params=_params(("arbitrary", "arbitrary", "arbitrary")),
        name="moba",
    )(qt, k, vt, km)


def _gla_kernel(q_ref, k_ref, v_ref, g_ref, og_ref, ng_ref, o_ref, st_ref, *, tm):
    @pl.when(pl.program_id(1) == 0)
    def _():
        st_ref[...] = jnp.zeros_like(st_ref)

    q, k, v, g = q_ref[0], k_ref[0], v_ref[0], g_ref[0]
    kw, vw = B_KEY_WIDTH, B_VAL_WIDTH
    row = lax.broadcasted_iota(jnp.int32, (tm, 1), 0)
    r_sub = row % GLA_SUB
    r_chk = row % GLA_CHUNK

    def seg_cumsum(x, r, length):
        s = 1
        while s < length:
            x = x + jnp.where(r >= s, pltpu.roll(x, s, 0), 0.0)
            s *= 2
        return x

    c_sub = seg_cumsum(g, r_sub, GLA_SUB)
    b = seg_cumsum(g, r_chk, GLA_CHUNK)

    head_kv = (lax.broadcasted_iota(jnp.int32, (kw, vw), 0) // B_HEAD_K
               == lax.broadcasted_iota(jnp.int32, (kw, vw), 1) // B_HEAD_V).astype(BF16)
    head_vk = (lax.broadcasted_iota(jnp.int32, (vw, kw), 0) // B_HEAD_V
               == lax.broadcasted_iota(jnp.int32, (vw, kw), 1) // B_HEAD_K)
    head_vv = (lax.broadcasted_iota(jnp.int32, (vw, vw), 0) // B_HEAD_V
               == lax.broadcasted_iota(jnp.int32, (vw, vw), 1) // B_HEAD_V).astype(BF16)

    o_acc = jnp.zeros((tm, vw), F32)
    for d in range(GLA_SUB):
        if d == 0:
            w = q * k
            vv = v
        else:
            w = jnp.where(r_sub >= d,
                          q * pltpu.roll(k, d, 0) * jnp.exp(c_sub - pltpu.roll(c_sub, d, 0)), 0.0)
            vv = pltpu.roll(v, d, 0)
        o_acc = o_acc + jnp.dot(w.astype(BF16), head_kv, preferred_element_type=F32) * vv

    q_sub = q * jnp.exp(c_sub)
    q_chk = q * jnp.exp(b)

    nsub = GLA_CHUNK // GLA_SUB
    rr = lax.broadcasted_iota(jnp.int32, (GLA_CHUNK, 1), 0)
    lane_k = lax.broadcasted_iota(jnp.int32, (1, (nsub - 1) * kw), 1)
    lane_v = lax.broadcasted_iota(jnp.int32, (1, vw), 1)
    outs = []
    for c in range(tm // GLA_CHUNK):
        lo = c * GLA_CHUNK
        bc = b[lo:lo + GLA_CHUNK]
        kc = k[lo:lo + GLA_CHUNK]
        vc = v[lo:lo + GLA_CHUNK].astype(BF16)
        qs = q_sub[lo:lo + GLA_CHUNK]

        rhs_parts, lhs_parts = [], []
        for i in range(1, nsub):
            b_end = bc[i * GLA_SUB - 1:i * GLA_SUB, :]
            decay = jnp.exp(jnp.minimum(b_end - bc, 0.0))
            rhs_parts.append(jnp.where(rr < i * GLA_SUB, kc * decay, 0.0).astype(BF16))
            lhs_parts.append(jnp.where(rr // GLA_SUB == i, qs, 0.0).astype(BF16))
        rhs = jnp.concatenate(rhs_parts, axis=1)
        lhs_t = jnp.concatenate(lhs_parts, axis=1)
        lhs = jnp.concatenate(
            [jnp.where((lane_k % kw) // B_HEAD_K == h, lhs_t, jnp.zeros_like(lhs_t))
             for h in range(B_HEADS)], axis=0)
        attn = lax.dot_general(lhs, rhs, (((1,), (1,)), ((), ())), preferred_element_type=F32)
        res = jnp.dot(attn.astype(BF16), vc, preferred_element_type=F32)
        o_off = jnp.zeros((GLA_CHUNK, vw), F32)
        for h in range(B_HEADS):
            o_off = o_off + jnp.where(lane_v // B_HEAD_V == h,
                                      res[h * GLA_CHUNK:(h + 1) * GLA_CHUNK], 0.0)

        st = st_ref[...]
        o_int = lax.dot_general(q_chk[lo:lo + GLA_CHUNK].astype(BF16), st.astype(BF16),
                                (((1,), (1,)), ((), ())), preferred_element_type=F32)
        b_last = bc[GLA_CHUNK - 1:GLA_CHUNK, :]
        ks = (kc * jnp.exp(b_last - bc)).astype(BF16)
        upd = lax.dot_general(vc, ks, (((0,), (0,)), ((), ())), preferred_element_type=F32)
        st_ref[...] = st * jnp.exp(b_last) + jnp.where(head_vk, upd, 0.0)
        outs.append(o_off + o_int)

    o = o_acc + jnp.concatenate(outs, axis=0)

    sq = o * o
    sq_hi = sq.astype(BF16)
    sq_lo = (sq - sq_hi.astype(F32)).astype(BF16)
    ms = (jnp.dot(sq_hi, head_vv, preferred_element_type=F32)
          + jnp.dot(sq_lo, head_vv, preferred_element_type=F32)) * (1.0 / B_HEAD_V)
    og = og_ref[0]
    o_ref[0] = (o * lax.rsqrt(ms + NORM_EPS) * ng_ref[...]) * (og * jax.nn.sigmoid(og))


def _gla(bq, bk, bv, gl, og, norm_g):
    bsz, seq, _ = bq.shape
    tm = min(256, seq)
    row = lambda n: pl.BlockSpec((1, tm, n), lambda b, i: (b, i, 0))
    return pl.pallas_call(
        functools.partial(_gla_kernel, tm=tm),
        grid=(bsz, seq // tm),
        in_specs=[row(B_KEY_WIDTH), row(B_KEY_WIDTH), row(B_VAL_WIDTH), row(B_KEY_WIDTH),
                  row(B_VAL_WIDTH), _resident((1, B_VAL_WIDTH))],
        out_specs=row(B_VAL_WIDTH),
        out_shape=jax.ShapeDtypeStruct((bsz, seq, B_VAL_WIDTH), F32),
        scratch_shapes=[pltpu.VMEM((B_VAL_WIDTH, B_KEY_WIDTH), F32)],
        compiler_params=_params(("arbitrary", "arbitrary")),
        name="gla",
    )(bq, bk, bv, gl, og, norm_g.reshape(1, B_VAL_WIDTH))


def _outproj_kernel(x_ref, oa_ref, ob_ref, wa_ref, wb_ref, o_ref):
    o_ref[0] = (x_ref[0]
                + jnp.dot(oa_ref[0].astype(BF16), wa_ref[...], preferred_element_type=F32)
                + jnp.dot(ob_ref[0].astype(BF16), wb_ref[...], preferred_element_type=F32))


def _outproj(x, oa, ob, w_out):
    bsz, seq, d = x.shape
    tm = min(512, seq)
    row = lambda n: pl.BlockSpec((1, tm, n), lambda b, i: (b, i, 0))
    w = w_out.astype(BF16)
    return pl.pallas_call(
        _outproj_kernel,
        grid=(bsz, seq // tm),
        in_specs=[row(d), row(A_WIDTH), row(B_VAL_WIDTH),
                  _resident((A_WIDTH, d)), _resident((B_VAL_WIDTH, d))],
        out_specs=row(d),
        out_shape=jax.ShapeDtypeStruct((bsz, seq, d), F32),
        compiler_params=_params(("arbitrary", "arbitrary")),
        name="outproj",
    )(x, oa, ob, w[:A_WIDTH], w[A_WIDTH:])


def _ffn_kernel(x_ref, g_ref, wup_ref, cw_ref, cb_ref, wdn_ref, fg_ref, o_ref,
                halo_ref, ext_ref, *, tm, bn, dff, final_norm):
    @pl.when(pl.program_id(1) == 0)
    def _():
        halo_ref[...] = jnp.zeros_like(halo_ref)

    x = x_ref[0]
    hn = _rms(x, g_ref[...]).astype(BF16)
    acc = x
    for kb in range(dff // bn):
        halves = []
        for part in range(2):
            c0 = part * dff + kb * bn
            up = jnp.dot(hn, wup_ref[:, c0:c0 + bn], preferred_element_type=F32)
            ext_ref[part, 0:SUBLANES, :] = halo_ref[:, c0:c0 + bn]
            ext_ref[part, SUBLANES:, :] = up
            halo_ref[:, c0:c0 + bn] = up[tm - SUBLANES:, :]
            y = cb_ref[:, c0:c0 + bn]
            for i in range(FFN_CONV):
                off = SUBLANES - (FFN_CONV - 1) + i
                y = y + ext_ref[part, pl.ds(off, tm), :] * cw_ref[i:i + 1, c0:c0 + bn]
            halves.append(y)
        a = (halves[0] * _gelu_tanh(halves[1])).astype(BF16)
        acc = acc + jnp.dot(a, wdn_ref[kb * bn:(kb + 1) * bn, :], preferred_element_type=F32)
    if final_norm:
        acc = _rms(acc, fg_ref[...])
    o_ref[0] = acc


def _conv_ffn(x, g, w_up, conv_w, conv_b, w_down, final_g, final_norm):
    bsz, seq, d = x.shape
    dff = w_down.shape[0]
    tm = min(512, seq)
    bn = 512
    row = pl.BlockSpec((1, tm, d), lambda b, i: (b, i, 0))
    return pl.pallas_call(
        functools.partial(_ffn_kernel, tm=tm, bn=bn, dff=dff, final_norm=final_norm),
        grid=(bsz, seq // tm),
        in_specs=[row, _resident((1, d)), _resident((d, 2 * dff)), _resident((FFN_CONV, 2 * dff)),
                  _resident((1, 2 * dff)), _resident((dff, d)), _resident((1, d))],
        out_specs=row,
        out_shape=jax.ShapeDtypeStruct((bsz, seq, d), F32),
        scratch_shapes=[pltpu.VMEM((SUBLANES, 2 * dff), F32),
                        pltpu.VMEM((2, tm + SUBLANES, bn), F32)],
        compiler_params=_params(("arbitrary", "arbitrary")),
        name="conv_ffn_final" if final_norm else "conv_ffn",
    )(x, g.reshape(1, d), w_up.astype(BF16), conv_w, conv_b.reshape(1, 2 * dff),
      w_down.astype(BF16), final_g.reshape(1, d))


def _rglru_kernel(x_ref, g_ref, win_ref, cw_ref, cb_ref, wa_ref, ba_ref, wx_ref, bx_ref,
                  lam_ref, wout_ref, o_ref, ext_ref, ca_ref, cu_ref, h_ref, hlast_ref, *, tm, drnn):
    @pl.when(pl.program_id(1) == 0)
    def _():
        ext_ref[0:SUBLANES, :] = jnp.zeros((SUBLANES, drnn), F32)
        hlast_ref[...] = jnp.zeros_like(hlast_ref)

    x = x_ref[0]
    hn = _rms(x, g_ref[...]).astype(BF16)
    gate = jnp.dot(hn, win_ref[:, 0:drnn], preferred_element_type=F32)
    xr_raw = jnp.dot(hn, win_ref[:, drnn:2 * drnn], preferred_element_type=F32)

    ext_ref[SUBLANES:, :] = xr_raw
    xr = cb_ref[...]
    for i in range(RG_CONV):
        off = SUBLANES - (RG_CONV - 1) + i
        xr = xr + ext_ref[pl.ds(off, tm), :] * cw_ref[i:i + 1, :]
    ext_ref[0:SUBLANES, :] = xr_raw[tm - SUBLANES:, :]

    bw = drnn // RG_BLOCKS
    xb = xr.astype(BF16)
    ra, ri = [], []
    for n in range(RG_BLOCKS):
        xn = xb[:, n * bw:(n + 1) * bw]
        ra.append(jnp.dot(xn, wa_ref[n], preferred_element_type=F32))
        ri.append(jnp.dot(xn, wx_ref[n], preferred_element_type=F32))
    r = jax.nn.sigmoid(jnp.concatenate(ra, axis=1) + ba_ref[...])
    ig = jax.nn.sigmoid(jnp.concatenate(ri, axis=1) + bx_ref[...])
    log_a = r * lam_ref[...]
    a = jnp.exp(log_a)
    th = jnp.tanh(log_a)
    one_minus_a2 = -2.0 * th / (1.0 - th)
    u = jnp.sqrt(one_minus_a2) * (ig * xr)

    r8 = lax.broadcasted_iota(jnp.int32, (tm, 1), 0) % SUBLANES
    ca, cu = a, u
    s = 1
    while s < SUBLANES:
        keep = r8 >= s
        a_sh = jnp.where(keep, pltpu.roll(ca, s, 0), 1.0)
        u_sh = jnp.where(keep, pltpu.roll(cu, s, 0), 0.0)
        cu = cu + ca * u_sh
        ca = ca * a_sh
        s *= 2
    ca_ref[...] = ca
    cu_ref[...] = cu

    def group(gi, h_prev):
        lo = pl.multiple_of(gi * SUBLANES, SUBLANES)
        hg = cu_ref[pl.ds(lo, SUBLANES), :] + ca_ref[pl.ds(lo, SUBLANES), :] * h_prev
        h_ref[pl.ds(lo, SUBLANES), :] = hg
        return hg[SUBLANES - 1:SUBLANES, :]

    h_last = lax.fori_loop(0, tm // SUBLANES, group, hlast_ref[...])
    hlast_ref[...] = h_last

    y = (h_ref[...] * _gelu_tanh(gate)).astype(BF16)
    o_ref[0] = x + jnp.dot(y, wout_ref[...], preferred_element_type=F32)


def _rglru(x, g, w_in, conv_w, conv_b, w_a, b_a, w_x, b_x, lam, w_out):
    bsz, seq, d = x.shape
    drnn = w_out.shape[0]
    bw = drnn // RG_BLOCKS
    tm = min(256, seq)
    row = pl.BlockSpec((1, tm, d), lambda b, i: (b, i, 0))
    lam_c = (-LRU_C) * jax.nn.softplus(-lam.astype(F32))
    return pl.pallas_call(
        functools.partial(_rglru_kernel, tm=tm, drnn=drnn),
        grid=(bsz, seq // tm),
        in_specs=[row, _resident((1, d)), _resident((d, 2 * drnn)), _resident((RG_CONV, drnn)),
                  _resident((1, drnn)), _resident((RG_BLOCKS, bw, bw)), _resident((1, drnn)),
                  _resident((RG_BLOCKS, bw, bw)), _resident((1, drnn)), _resident((1, drnn)),
                  _resident((drnn, d))],
        out_specs=row,
        out_shape=jax.ShapeDtypeStruct((bsz, seq, d), F32),
        scratch_shapes=[pltpu.VMEM((tm + SUBLANES, drnn), F32),
                        pltpu.VMEM((tm, drnn), F32),
                        pltpu.VMEM((tm, drnn), F32),
                        pltpu.VMEM((tm, drnn), F32),
                        pltpu.VMEM((1, drnn), F32)],
        compiler_params=_params(("arbitrary", "arbitrary")),
        name="rglru",
    )(x, g.reshape(1, d), w_in.astype(BF16), conv_w, conv_b.reshape(1, drnn),
      w_a.astype(BF16), b_a.reshape(1, drnn), w_x.astype(BF16), b_x.reshape(1, drnn),
      lam_c.reshape(1, drnn), w_out.astype(BF16))


def kernel(x, mix_norm_g, ffn_norm_g, final_norm_g, ev_w_in, ev_w_gk2, ev_b_gk2, ev_gla_norm_g,
           ev_w_out, od_w_in, od_conv_w, od_conv_b, od_w_a, od_b_a, od_w_x, od_b_x, od_lambda,
           od_w_out, ffn_w_up, ffn_conv_w, ffn_conv_b, ffn_w_down):
    depth = mix_norm_g.shape[0]
    h = x
    for l in range(depth):
        j = l // 2
        if l % 2 == 0:
            q, k, v, kmean, bq, bk, bv, gl, og = _front0(h, mix_norm_g[l], ev_w_in[j], ev_w_gk2[j],
                                                       ev_b_gk2[j])
            o_a = _moba(q, k, v, kmean)
            o_b = _gla(bq, bk, bv, gl, og, ev_gla_norm_g[j])
            h = _outproj(h, o_a, o_b, ev_w_out[j])
        else:
            h = _rglru(h, mix_norm_g[l], od_w_in[j], od_conv_w[j], od_conv_b[j], od_w_a[j],
                       od_b_a[j], od_w_x[j], od_b_x[j], od_lambda[j], od_w_out[j])
        h = _conv_ffn(h, ffn_norm_g[l], ffn_w_up[l], ffn_conv_w[l], ffn_conv_b[l], ffn_w_down[l],
                      final_norm_g, final_norm=(l == depth - 1))
    return h
```

```python
import functools

import jax
import jax.numpy as jnp
from jax import lax
from jax.experimental import pallas as pl
from jax.experimental.pallas import tpu as pltpu

F32 = jnp.float32
BF16 = jnp.bfloat16

A_HEADS = 8
A_HEAD_DIM = 64
A_WIDTH = A_HEADS * A_HEAD_DIM
MOBA_BLOCK = 256
MOBA_TOPK = 3
ROPE_THETA = 10000.0
B_HEADS = 8
B_HEAD_K = 32
B_HEAD_V = 64
B_KEY_WIDTH = B_HEADS * B_HEAD_K
B_VAL_WIDTH = B_HEADS * B_HEAD_V
GLA_GATE_RANK = 16
GLA_GATE_NORM = 16.0
RG_BLOCKS = 4
RG_CONV = 4
LRU_C = 8.0
FFN_CONV = 3
NORM_EPS = 1e-6
MASK_VALUE = -1e30

LANES = 128
SUBLANES = 8
VMEM_LIMIT = 56 * 1024 * 1024

GLA_CHUNK = 64
GLA_SUB = 16
MOBA_BLOCKS_PAD = LANES
V_ROWS = 80


def _params(sem):
    return pltpu.CompilerParams(dimension_semantics=sem, vmem_limit_bytes=VMEM_LIMIT)


def _rms(x, g):
    ms = jnp.mean(x * x, axis=-1, keepdims=True)
    return x * lax.rsqrt(ms + NORM_EPS) * g


def _gelu_tanh(x):
    return 0.5 * x * (1.0 + jnp.tanh(0.7978845608028654 * (x + 0.044715 * (x * x * x))))


def _log_sigmoid(z):
    return jnp.minimum(z, 0.0) - jnp.log1p(jnp.exp(-jnp.abs(z)))


def _resident(shape):
    nd = len(shape)
    return pl.BlockSpec(shape, lambda *_: (0,) * nd, pipeline_mode=pl.Buffered(1))


def _front0_kernel(x_ref, g_ref, w_ref, wt_ref, wgk_ref, bgk_ref, cos_ref, sa_ref, sb_ref,
                   cost_ref, sat_ref, sbt_ref,
                   q_ref, k_ref, v_ref, km_ref, bq_ref, bk_ref, bv_ref, gl_ref, og_ref, *, tm):
    hn = _rms(x_ref[0], g_ref[...]).astype(BF16)

    def proj(c0, n):
        return jnp.dot(hn, w_ref[:, c0:c0 + n], preferred_element_type=F32)

    def proj_t(r0, n):
        return lax.dot_general(wt_ref[r0:r0 + n, :], hn, (((1,), (1,)), ((), ())),
                               preferred_element_type=F32)

    def rot(t, cos, sa, sb, axis):
        parts = []
        for c in range(t.shape[axis] // LANES):
            tc = lax.slice_in_dim(t, c * LANES, (c + 1) * LANES, axis=axis)
            parts.append(tc * cos + pltpu.roll(tc, LANES - 32, axis) * sa
                         + pltpu.roll(tc, 32, axis) * sb)
        return jnp.concatenate(parts, axis=axis)

    q_ref[0] = rot(proj_t(0, A_WIDTH), cost_ref[...], sat_ref[...], sbt_ref[...], 0)
    vt = proj_t(A_WIDTH, A_WIDTH)
    c = 0
    k = rot(proj(c, A_WIDTH), cos_ref[...], sa_ref[...], sb_ref[...], 1); c += A_WIDTH
    k_ref[0] = k.astype(BF16)
    nblk = tm // MOBA_BLOCK
    km_ref[...] = jnp.mean(k.reshape(nblk, MOBA_BLOCK, A_WIDTH), axis=1).reshape(nblk, 1, A_WIDTH)
    ones_rows = (lax.broadcasted_iota(jnp.int32, (V_ROWS - A_HEAD_DIM, MOBA_BLOCK), 0)
                 == 0).astype(BF16)
    for h in range(A_HEADS):
        for n in range(nblk):
            v_ref[0, h, n, 0:A_HEAD_DIM, :] = vt[h * A_HEAD_DIM:(h + 1) * A_HEAD_DIM,
                                                 n * MOBA_BLOCK:(n + 1) * MOBA_BLOCK].astype(BF16)
            v_ref[0, h, n, A_HEAD_DIM:, :] = ones_rows
    bq_ref[0] = proj(c, B_KEY_WIDTH) * (B_HEAD_K ** -0.5); c += B_KEY_WIDTH
    bk_ref[0] = proj(c, B_KEY_WIDTH); c += B_KEY_WIDTH
    bv_ref[0] = proj(c, B_VAL_WIDTH); c += B_VAL_WIDTH
    og_ref[0] = proj(c, B_VAL_WIDTH); c += B_VAL_WIDTH
    gk = proj(c, LANES).astype(BF16)
    z = jnp.dot(gk, wgk_ref[...], preferred_element_type=F32) + bgk_ref[...]
    gl_ref[0] = _log_sigmoid(z) / GLA_GATE_NORM


def _front0(x, g, w_in, w_gk2, b_gk2):
    bsz, seq, d = x.shape
    tm = min(512, seq)
    aq, ak, av, bq, bk, bv, bgk, bog = jnp.split(
        w_in, [512, 1024, 1536, 1792, 2048, 2560, 2576], axis=1)
    gk_pad = LANES - GLA_GATE_RANK
    w = jnp.concatenate([ak, bq, bk, bv, bog, jnp.pad(bgk, ((0, 0), (0, gk_pad)))],
                        axis=1).astype(BF16)
    wt = jnp.concatenate([aq, av], axis=1).T.astype(BF16)
    wgk = jnp.pad(w_gk2, ((0, gk_pad), (0, 0))).astype(BF16)
    ncol = w.shape[1]

    half = A_HEAD_DIM // 2
    lane = jnp.arange(LANES)
    inv = ROPE_THETA ** (-jnp.arange(half, dtype=F32) / half)
    ang = jnp.arange(seq, dtype=jnp.int32).astype(F32)[:, None] * inv[None, :]
    cos_h, sin_h = jnp.cos(ang), jnp.sin(ang)
    first = (lane % A_HEAD_DIM) < half
    cos_t = jnp.tile(cos_h, (1, LANES // half))
    sin_t = jnp.tile(sin_h, (1, LANES // half))
    sa = jnp.where(first[None, :], -sin_t, 0.0)
    sb = jnp.where(first[None, :], 0.0, sin_t)

    nb = seq // MOBA_BLOCK
    row = lambda n: pl.BlockSpec((1, tm, n), lambda b, i: (b, i, 0))
    tab = pl.BlockSpec((tm, LANES), lambda b, i: (i, 0))
    tab_t = pl.BlockSpec((LANES, tm), lambda b, i: (0, i))
    nt = seq // tm
    outs = pl.pallas_call(
        functools.partial(_front0_kernel, tm=tm),
        grid=(bsz, nt),
        in_specs=[row(d), _resident((1, d)), _resident((d, ncol)), _resident((2 * A_WIDTH, d)),
                  _resident((LANES, B_KEY_WIDTH)), _resident((1, B_KEY_WIDTH)),
                  tab, tab, tab, tab_t, tab_t, tab_t],
        out_specs=[pl.BlockSpec((1, A_WIDTH, tm), lambda b, i: (b, 0, i)), row(A_WIDTH),
                   pl.BlockSpec((1, A_HEADS, tm // MOBA_BLOCK, V_ROWS, MOBA_BLOCK),
                                lambda b, i: (b, 0, i, 0, 0)),
                   pl.BlockSpec((tm // MOBA_BLOCK, 1, A_WIDTH),
                                lambda b, i: (b * nt + i, 0, 0)),
                   row(B_KEY_WIDTH), row(B_KEY_WIDTH), row(B_VAL_WIDTH), row(B_KEY_WIDTH),
                   row(B_VAL_WIDTH)],
        out_shape=[jax.ShapeDtypeStruct((bsz, A_WIDTH, seq), F32),
                   jax.ShapeDtypeStruct((bsz, seq, A_WIDTH), BF16),
                   jax.ShapeDtypeStruct((bsz, A_HEADS, nb, V_ROWS, MOBA_BLOCK), BF16),
                   jax.ShapeDtypeStruct((bsz * nb, 1, A_WIDTH), F32),
                   jax.ShapeDtypeStruct((bsz, seq, B_KEY_WIDTH), F32),
                   jax.ShapeDtypeStruct((bsz, seq, B_KEY_WIDTH), F32),
                   jax.ShapeDtypeStruct((bsz, seq, B_VAL_WIDTH), F32),
                   jax.ShapeDtypeStruct((bsz, seq, B_KEY_WIDTH), F32),
                   jax.ShapeDtypeStruct((bsz, seq, B_VAL_WIDTH), F32)],
        compiler_params=_params(("arbitrary", "arbitrary")),
        name="front0",
    )(x, g.reshape(1, d), w, wt, wgk, b_gk2.reshape(1, B_KEY_WIDTH), cos_t, sa, sb,
      cos_t.T, sa.T, sb.T)
    return outs


def _moba_kernel(qt_ref, k_ref, vt_ref, km_ref, o_ref, rhs_ref, acc_ref, sa_ref, sb_ref, *,
                 group, nb):
    qb = pl.program_id(2)
    blk = MOBA_BLOCK
    qt = qt_ref[0]
    km = km_ref[0]
    feat = lax.broadcasted_iota(jnp.int32, (LANES, 1), 0)
    brow = lax.broadcasted_iota(jnp.int32, (MOBA_BLOCKS_PAD, blk), 0)
    browf = brow.astype(F32)
    scale = A_HEAD_DIM ** -0.5

    for h in range(2):
        qh = jnp.where(feat // A_HEAD_DIM == h, qt, 0.0)
        gate = jnp.dot(km, qh, precision=lax.Precision.HIGHEST, preferred_element_type=F32)
        g = jnp.where(brow < qb, gate, -jnp.inf)
        bias = jnp.full((MOBA_BLOCKS_PAD, blk), MASK_VALUE, F32)
        for _ in range(MOBA_TOPK):
            mx = jnp.max(g, axis=0, keepdims=True)
            first = jnp.min(jnp.where(g == mx, browf, float(MOBA_BLOCKS_PAD)), axis=0, keepdims=True)
            hit = browf == first
            bias = jnp.where(hit & (mx > -jnp.inf), 0.0, bias)
            g = jnp.where(hit, -jnp.inf, g)
        rhs_ref[:, h * blk:(h + 1) * blk] = jnp.concatenate(
            [(qh * scale).astype(BF16), bias.astype(BF16)], axis=0)

    own = pl.multiple_of(qb * blk, blk)
    lhs = jnp.concatenate([k_ref[0, pl.ds(own, blk), :], jnp.zeros((blk, MOBA_BLOCKS_PAD), BF16)],
                          axis=1)
    causal = (lax.broadcasted_iota(jnp.int32, (blk, 2 * blk), 0)
              <= lax.broadcasted_iota(jnp.int32, (blk, 2 * blk), 1) % blk)
    s = jnp.where(causal, jnp.dot(lhs, rhs_ref[...], preferred_element_type=F32), MASK_VALUE)
    m0 = jnp.max(s, axis=0, keepdims=True)
    p = jnp.exp(s - m0).astype(BF16)
    for h in range(2):
        acc_ref[h] = jnp.dot(vt_ref[0, h, qb], p[:, h * blk:(h + 1) * blk],
                             preferred_element_type=F32)

    keys = group * blk
    blk_of_row = lax.broadcasted_iota(jnp.int32, (keys, MOBA_BLOCKS_PAD), 0) // blk
    lane_blk = lax.broadcasted_iota(jnp.int32, (keys, MOBA_BLOCKS_PAD), 1)

    def scores(g, s_ref):
        j0 = g * group
        off = pl.multiple_of(jnp.minimum(j0, nb - group) * blk, blk)
        onehot = (lane_blk == blk_of_row + j0).astype(BF16)
        lhs_j = jnp.concatenate([k_ref[0, pl.ds(off, keys), :], onehot], axis=1)
        s = jnp.dot(lhs_j, rhs_ref[...], preferred_element_type=F32)
        s_ref[...] = s
        return jnp.max(s, axis=0, keepdims=True)

    def accumulate(g, s_ref, m, s_max):
        j0 = g * group
        m_new = jnp.maximum(m, s_max)
        alpha = jnp.exp(m - m_new)
        p = jnp.exp(s_ref[...] - m_new).astype(BF16)
        for h in range(2):
            vt = jnp.concatenate([vt_ref[0, h, jnp.minimum(j0 + u, nb - 1)] for u in range(group)],
                                 axis=1)
            acc_ref[h] = (alpha[:, h * blk:(h + 1) * blk] * acc_ref[h]
                          + jnp.dot(vt, p[:, h * blk:(h + 1) * blk], preferred_element_type=F32))
        return m_new

    def body(i, carry):
        m, max_a = carry
        max_b = scores(2 * i + 1, sb_ref)
        m = accumulate(2 * i, sa_ref, m, max_a)
        max_a = scores(2 * i + 2, sa_ref)
        m = accumulate(2 * i + 1, sb_ref, m, max_b)
        return m, max_a

    n_groups = (qb + group - 1) // group
    lax.fori_loop(0, (n_groups + 1) // 2, body, (m0, scores(0, sa_ref)))

    outs = []
    for h in range(2):
        a = acc_ref[h]
        outs.append(a[0:A_HEAD_DIM] / a[A_HEAD_DIM:A_HEAD_DIM + 1])
    o_ref[0] = jnp.concatenate(outs, axis=0).T


def _moba(qt, k, vt, kmean):
    bsz, seq, _ = k.shape
    nb = seq // MOBA_BLOCK
    assert nb <= MOBA_BLOCKS_PAD
    km = jnp.pad(kmean.reshape(bsz, nb, A_WIDTH), ((0, 0), (0, MOBA_BLOCKS_PAD - nb), (0, 0)))
    npair = A_WIDTH // LANES
    group = 4 if nb % 4 == 0 else 1
    return pl.pallas_call(
        functools.partial(_moba_kernel, group=group, nb=nb),
        grid=(bsz, npair, nb),
        in_specs=[pl.BlockSpec((1, LANES, MOBA_BLOCK), lambda b, hp, i: (b, hp, i)),
                  pl.BlockSpec((1, seq, LANES), lambda b, hp, i: (b, 0, hp)),
                  pl.BlockSpec((1, 2, nb, V_ROWS, MOBA_BLOCK), lambda b, hp, i: (b, hp, 0, 0, 0)),
                  pl.BlockSpec((1, MOBA_BLOCKS_PAD, LANES), lambda b, hp, i: (b, 0, hp))],
        out_specs=pl.BlockSpec((1, MOBA_BLOCK, LANES), lambda b, hp, i: (b, i, hp)),
        out_shape=jax.ShapeDtypeStruct((bsz, seq, A_WIDTH), F32),
        scratch_shapes=[pltpu.VMEM((2 * LANES, 2 * MOBA_BLOCK), BF16),
                        pltpu.VMEM((2, V_ROWS, MOBA_BLOCK), F32),
                        pltpu.VMEM((group * MOBA_BLOCK, 2 * MOBA_BLOCK), F32),
                        pltpu.VMEM((group * MOBA_BLOCK, 2 * MOBA_BLOCK), F32)],
        compiler_params=_params(("arbitrary", "arbitrary", "arbitrary")),
        name="moba",
    )(qt, k, vt, km)


def _gla_kernel(q_ref, k_ref, v_ref, g_ref, og_ref, ng_ref, o_ref, st_ref, *, tm):
    @pl.when(pl.program_id(1) == 0)
    def _():
        st_ref[...] = jnp.zeros_like(st_ref)

    q, k, v, g = q_ref[0], k_ref[0], v_ref[0], g_ref[0]
    kw, vw = B_KEY_WIDTH, B_VAL_WIDTH
    row = lax.broadcasted_iota(jnp.int32, (tm, 1), 0)
    r_sub = row % GLA_SUB
    r_chk = row % GLA_CHUNK

    def seg_cumsum(x, r, length):
        s = 1
        while s < length:
            x = x + jnp.where(r >= s, pltpu.roll(x, s, 0), 0.0)
            s *= 2
        return x

    c_sub = seg_cumsum(g, r_sub, GLA_SUB)
    b = seg_cumsum(g, r_chk, GLA_CHUNK)

    head_kv = (lax.broadcasted_iota(jnp.int32, (kw, vw), 0) // B_HEAD_K
               == lax.broadcasted_iota(jnp.int32, (kw, vw), 1) // B_HEAD_V).astype(BF16)
    head_vk = (lax.broadcasted_iota(jnp.int32, (vw, kw), 0) // B_HEAD_V
               == lax.broadcasted_iota(jnp.int32, (vw, kw), 1) // B_HEAD_K)
    head_vv = (lax.broadcasted_iota(jnp.int32, (vw, vw), 0) // B_HEAD_V
               == lax.broadcasted_iota(jnp.int32, (vw, vw), 1) // B_HEAD_V).astype(BF16)

    o_acc = jnp.zeros((tm, vw), F32)
    for d in range(GLA_SUB):
        if d == 0:
            w = q * k
            vv = v
        else:
            w = jnp.where(r_sub >= d,
                          q * pltpu.roll(k, d, 0) * jnp.exp(c_sub - pltpu.roll(c_sub, d, 0)), 0.0)
            vv = pltpu.roll(v, d, 0)
        o_acc = o_acc + jnp.dot(w.astype(BF16), head_kv, preferred_element_type=F32) * vv

    q_sub = q * jnp.exp(c_sub)
    q_chk = q * jnp.exp(b)

    nsub = GLA_CHUNK // GLA_SUB
    rr = lax.broadcasted_iota(jnp.int32, (GLA_CHUNK, 1), 0)
    lane_k = lax.broadcasted_iota(jnp.int32, (1, (nsub - 1) * kw), 1)
    lane_v = lax.broadcasted_iota(jnp.int32, (1, vw), 1)
    outs = []
    for c in range(tm // GLA_CHUNK):
        lo = c * GLA_CHUNK
        bc = b[lo:lo + GLA_CHUNK]
        kc = k[lo:lo + GLA_CHUNK]
        vc = v[lo:lo + GLA_CHUNK].astype(BF16)
        qs = q_sub[lo:lo + GLA_CHUNK]

        rhs_parts, lhs_parts = [], []
        for i in range(1, nsub):
            b_end = bc[i * GLA_SUB - 1:i * GLA_SUB, :]
            decay = jnp.exp(jnp.minimum(b_end - bc, 0.0))
            rhs_parts.append(jnp.where(rr < i * GLA_SUB, kc * decay, 0.0).astype(BF16))
            lhs_parts.append(jnp.where(rr // GLA_SUB == i, qs, 0.0).astype(BF16))
        rhs = jnp.concatenate(rhs_parts, axis=1)
        lhs_t = jnp.concatenate(lhs_parts, axis=1)
        lhs = jnp.concatenate(
            [jnp.where((lane_k % kw) // B_HEAD_K == h, lhs_t, jnp.zeros_like(lhs_t))
             for h in range(B_HEADS)], axis=0)
        attn = lax.dot_general(lhs, rhs, (((1,), (1,)), ((), ())), preferred_element_type=F32)
        res = jnp.dot(attn.astype(BF16), vc, preferred_element_type=F32)
        o_off = jnp.zeros((GLA_CHUNK, vw), F32)
        for h in range(B_HEADS):
            o_off = o_off + jnp.where(lane_v // B_HEAD_V == h,
                                      res[h * GLA_CHUNK:(h + 1) * GLA_CHUNK], 0.0)

        st = st_ref[...]
        o_int = lax.dot_general(q_chk[lo:lo + GLA_CHUNK].astype(BF16), st.astype(BF16),
                                (((1,), (1,)), ((), ())), preferred_element_type=F32)
        b_last = bc[GLA_CHUNK - 1:GLA_CHUNK, :]
        ks = (kc * jnp.exp(b_last - bc)).astype(BF16)
        upd = lax.dot_general(vc, ks, (((0,), (0,)), ((), ())), preferred_element_type=F32)
        st_ref[...] = st * jnp.exp(b_last) + jnp.where(head_vk, upd, 0.0)
        outs.append(o_off + o_int)

    o = o_acc + jnp.concatenate(outs, axis=0)

    sq = o * o
    sq_hi = sq.astype(BF16)
    sq_lo = (sq - sq_hi.astype(F32)).astype(BF16)
    ms = (jnp.dot(sq_hi, head_vv, preferred_element_type=F32)
          + jnp.dot(sq_lo, head_vv, preferred_element_type=F32)) * (1.0 / B_HEAD_V)
    og = og_ref[0]
    o_ref[0] = (o * lax.rsqrt(ms + NORM_EPS) * ng_ref[...]) * (og * jax.nn.sigmoid(og))


def _gla(bq, bk, bv, gl, og, norm_g):
    bsz, seq, _ = bq.shape
    tm = min(256, seq)
    row = lambda n: pl.BlockSpec((1, tm, n), lambda b, i: (b, i, 0))
    return pl.pallas_call(
        functools.partial(_gla_kernel, tm=tm),
        grid=(bsz, seq // tm),
        in_specs=[row(B_KEY_WIDTH), row(B_KEY_WIDTH), row(B_VAL_WIDTH), row(B_KEY_WIDTH),
                  row(B_VAL_WIDTH), _resident((1, B_VAL_WIDTH))],
        out_specs=row(B_VAL_WIDTH),
        out_shape=jax.ShapeDtypeStruct((bsz, seq, B_VAL_WIDTH), F32),
        scratch_shapes=[pltpu.VMEM((B_VAL_WIDTH, B_KEY_WIDTH), F32)],
        compiler_params=_params(("arbitrary", "arbitrary")),
        name="gla",
    )(bq, bk, bv, gl, og, norm_g.reshape(1, B_VAL_WIDTH))


def _outproj_kernel(x_ref, oa_ref, ob_ref, wa_ref, wb_ref, o_ref):
    o_ref[0] = (x_ref[0]
                + jnp.dot(oa_ref[0].astype(BF16), wa_ref[...], preferred_element_type=F32)
                + jnp.dot(ob_ref[0].astype(BF16), wb_ref[...], preferred_element_type=F32))


def _outproj(x, oa, ob, w_out):
    bsz, seq, d = x.shape
    tm = min(512, seq)
    row = lambda n: pl.BlockSpec((1, tm, n), lambda b, i: (b, i, 0))
    w = w_out.astype(BF16)
    return pl.pallas_call(
        _outproj_kernel,
        grid=(bsz, seq // tm),
        in_specs=[row(d), row(A_WIDTH), row(B_VAL_WIDTH),
                  _resident((A_WIDTH, d)), _resident((B_VAL_WIDTH, d))],
        out_specs=row(d),
        out_shape=jax.ShapeDtypeStruct((bsz, seq, d), F32),
        compiler_params=_params(("arbitrary", "arbitrary")),
        name="outproj",
    )(x, oa, ob, w[:A_WIDTH], w[A_WIDTH:])


def _ffn_kernel(x_ref, g_ref, wup_ref, cw_ref, cb_ref, wdn_ref, fg_ref, o_ref,
                halo_ref, ext_ref, *, tm, bn, dff, final_norm):
    @pl.when(pl.program_id(1) == 0)
    def _():
        halo_ref[...] = jnp.zeros_like(halo_ref)

    x = x_ref[0]
    hn = _rms(x, g_ref[...]).astype(BF16)
    acc = x
    for kb in range(dff // bn):
        halves = []
        for part in range(2):
            c0 = part * dff + kb * bn
            up = jnp.dot(hn, wup_ref[:, c0:c0 + bn], preferred_element_type=F32)
            ext_ref[part, 0:SUBLANES, :] = halo_ref[:, c0:c0 + bn]
            ext_ref[part, SUBLANES:, :] = up
            halo_ref[:, c0:c0 + bn] = up[tm - SUBLANES:, :]
            y = cb_ref[:, c0:c0 + bn]
            for i in range(FFN_CONV):
                off = SUBLANES - (FFN_CONV - 1) + i
                y = y + ext_ref[part, pl.ds(off, tm), :] * cw_ref[i:i + 1, c0:c0 + bn]
            halves.append(y)
        a = (halves[0] * _gelu_tanh(halves[1])).astype(BF16)
        acc = acc + jnp.dot(a, wdn_ref[kb * bn:(kb + 1) * bn, :], preferred_element_type=F32)
    if final_norm:
        acc = _rms(acc, fg_ref[...])
    o_ref[0] = acc


def _conv_ffn(x, g, w_up, conv_w, conv_b, w_down, final_g, final_norm):
    bsz, seq, d = x.shape
    dff = w_down.shape[0]
    tm = min(512, seq)
    bn = 512
    row = pl.BlockSpec((1, tm, d), lambda b, i: (b, i, 0))
    return pl.pallas_call(
        functools.partial(_ffn_kernel, tm=tm, bn=bn, dff=dff, final_norm=final_norm),
        grid=(bsz, seq // tm),
        in_specs=[row, _resident((1, d)), _resident((d, 2 * dff)), _resident((FFN_CONV, 2 * dff)),
                  _resident((1, 2 * dff)), _resident((dff, d)), _resident((1, d))],
        out_specs=row,
        out_shape=jax.ShapeDtypeStruct((bsz, seq, d), F32),
        scratch_shapes=[pltpu.VMEM((SUBLANES, 2 * dff), F32),
                        pltpu.VMEM((2, tm + SUBLANES, bn), F32)],
        compiler_params=_params(("arbitrary", "arbitrary")),
        name="conv_ffn_final" if final_norm else "conv_ffn",
    )(x, g.reshape(1, d), w_up.astype(BF16), conv_w, conv_b.reshape(1, 2 * dff),
      w_down.astype(BF16), final_g.reshape(1, d))


def _rglru_kernel(x_ref, g_ref, win_ref, cw_ref, cb_ref, wa_ref, ba_ref, wx_ref, bx_ref,
                  lam_ref, wout_ref, o_ref, ext_ref, ca_ref, cu_ref, h_ref, hlast_ref, *, tm, drnn):
    @pl.when(pl.program_id(1) == 0)
    def _():
        ext_ref[0:SUBLANES, :] = jnp.zeros((SUBLANES, drnn), F32)
        hlast_ref[...] = jnp.zeros_like(hlast_ref)

    x = x_ref[0]
    hn = _rms(x, g_ref[...]).astype(BF16)
    gate = jnp.dot(hn, win_ref[:, 0:drnn], preferred_element_type=F32)
    xr_raw = jnp.dot(hn, win_ref[:, drnn:2 * drnn], preferred_element_type=F32)

    ext_ref[SUBLANES:, :] = xr_raw
    xr = cb_ref[...]
    for i in range(RG_CONV):
        off = SUBLANES - (RG_CONV - 1) + i
        xr = xr + ext_ref[pl.ds(off, tm), :] * cw_ref[i:i + 1, :]
    ext_ref[0:SUBLANES, :] = xr_raw[tm - SUBLANES:, :]

    bw = drnn // RG_BLOCKS
    xb = xr.astype(BF16)
    ra, ri = [], []
    for n in range(RG_BLOCKS):
        xn = xb[:, n * bw:(n + 1) * bw]
        ra.append(jnp.dot(xn, wa_ref[n], preferred_element_type=F32))
        ri.append(jnp.dot(xn, wx_ref[n], preferred_element_type=F32))
    r = jax.nn.sigmoid(jnp.concatenate(ra, axis=1) + ba_ref[...])
    ig = jax.nn.sigmoid(jnp.concatenate(ri, axis=1) + bx_ref[...])
    log_a = r * lam_ref[...]
    a = jnp.exp(log_a)
    th = jnp.tanh(log_a)
    one_minus_a2 = -2.0 * th / (1.0 - th)
    u = jnp.sqrt(one_minus_a2) * (ig * xr)

    r8 = lax.broadcasted_iota(jnp.int32, (tm, 1), 0) % SUBLANES
    ca, cu = a, u
    s = 1
    while s < SUBLANES:
        keep = r8 >= s
        a_sh = jnp.where(keep, pltpu.roll(ca, s, 0), 1.0)
        u_sh = jnp.where(keep, pltpu.roll(cu, s, 0), 0.0)
        cu = cu + ca * u_sh
        ca = ca * a_sh
        s *= 2
    ca_ref[...] = ca
    cu_ref[...] = cu

    def group(gi, h_prev):
        lo = pl.multiple_of(gi * SUBLANES, SUBLANES)
        hg = cu_ref[pl.ds(lo, SUBLANES), :] + ca_ref[pl.ds(lo, SUBLANES), :] * h_prev
        h_ref[pl.ds(lo, SUBLANES), :] = hg
        return hg[SUBLANES - 1:SUBLANES, :]

    h_last = lax.fori_loop(0, tm // SUBLANES, group, hlast_ref[...])
    hlast_ref[...] = h_last

    y = (h_ref[...] * _gelu_tanh(gate)).astype(BF16)
    o_ref[0] = x + jnp.dot(y, wout_ref[...], preferred_element_type=F32)


def _rglru(x, g, w_in, conv_w, conv_b, w_a, b_a, w_x, b_x, lam, w_out):
    bsz, seq, d = x.shape
    drnn = w_out.shape[0]
    bw = drnn // RG_BLOCKS
    tm = min(256, seq)
    row = pl.BlockSpec((1, tm, d), lambda b, i: (b, i, 0))
    lam_c = (-LRU_C) * jax.nn.softplus(-lam.astype(F32))
    return pl.pallas_call(
        functools.partial(_rglru_kernel, tm=tm, drnn=drnn),
        grid=(bsz, seq // tm),
        in_specs=[row, _resident((1, d)), _resident((d, 2 * drnn)), _resident((RG_CONV, drnn)),
                  _resident((1, drnn)), _resident((RG_BLOCKS, bw, bw)), _resident((1, drnn)),
                  _resident((RG_BLOCKS, bw, bw)), _resident((1, drnn)), _resident((1, drnn)),
                  _resident((drnn, d))],
        out_specs=row,
        out_shape=jax.ShapeDtypeStruct((bsz, seq, d), F32),
        scratch_shapes=[pltpu.VMEM((tm + SUBLANES, drnn), F32),
                        pltpu.VMEM((tm, drnn), F32),
                        pltpu.VMEM((tm, drnn), F32),
                        pltpu.VMEM((tm, drnn), F32),
                        pltpu.VMEM((1, drnn), F32)],
        compiler_params=_params(("arbitrary", "arbitrary")),
        name="rglru",
    )(x, g.reshape(1, d), w_in.astype(BF16), conv_w, conv_b.reshape(1, drnn),
      w_a.astype(BF16), b_a.reshape(1, drnn), w_x.astype(BF16), b_x.reshape(1, drnn),
      lam_c.reshape(1, drnn), w_out.astype(BF16))


def kernel(x, mix_norm_g, ffn_norm_g, final_norm_g, ev_w_in, ev_w_gk2, ev_b_gk2, ev_gla_norm_g,
           ev_w_out, od_w_in, od_conv_w, od_conv_b, od_w_a, od_b_a, od_w_x, od_b_x, od_lambda,
           od_w_out, ffn_w_up, ffn_conv_w, ffn_conv_b, ffn_w_down):
    depth = mix_norm_g.shape[0]
    h = x
    for l in range(depth):
        j = l // 2
        if l % 2 == 0:
            q, k, v, kmean, bq, bk, bv, gl, og = _front0(h, mix_norm_g[l], ev_w_in[j], ev_w_gk2[j],
                                                       ev_b_gk2[j])
            o_a = _moba(q, k, v, kmean)
            o_b = _gla(bq, bk, bv, gl, og, ev_gla_norm_g[j])
            h = _outproj(h, o_a, o_b, ev_w_out[j])
        else:
            h = _rglru(h, mix_norm_g[l], od_w_in[j], od_conv_w[j], od_conv_b[j], od_w_a[j],
                       od_b_a[j], od_w_x[j], od_b_x[j], od_lambda[j], od_w_out[j])
        h = _conv_ffn(h, ffn_norm_g[l], ffn_w_up[l], ffn_conv_w[l], ffn_conv_b[l], ffn_w_down[l],
                      final_norm_g, final_norm=(l == depth - 1))
    return h
```

```python
import functools

import jax
import jax.numpy as jnp
from jax import lax
from jax.experimental import pallas as pl
from jax.experimental.pallas import tpu as pltpu

F32 = jnp.float32
BF16 = jnp.bfloat16

A_HEADS = 8
A_HEAD_DIM = 64
A_WIDTH = A_HEADS * A_HEAD_DIM
MOBA_BLOCK = 256
MOBA_TOPK = 3
ROPE_THETA = 10000.0
B_HEADS = 8
B_HEAD_K = 32
B_HEAD_V = 64
B_KEY_WIDTH = B_HEADS * B_HEAD_K
B_VAL_WIDTH = B_HEADS * B_HEAD_V
GLA_GATE_RANK = 16
GLA_GATE_NORM = 16.0
RG_BLOCKS = 4
RG_CONV = 4
LRU_C = 8.0
FFN_CONV = 3
NORM_EPS = 1e-6
MASK_VALUE = -1e30

LANES = 128
SUBLANES = 8
BF16_ROWS = 16
LOG2_E = 1.4426950408889634
VMEM_LIMIT = 56 * 1024 * 1024

GLA_CHUNK = 64
GLA_SUB = 16
MOBA_BLOCKS_PAD = LANES
V_ROWS = 80


def _params(sem):
    return pltpu.CompilerParams(dimension_semantics=sem, vmem_limit_bytes=VMEM_LIMIT)


def _rms(x, g):
    ms = jnp.mean(x * x, axis=-1, keepdims=True)
    return x * lax.rsqrt(ms + NORM_EPS) * g


def _gelu_tanh(x):
    return 0.5 * x * (1.0 + jnp.tanh(0.7978845608028654 * (x + 0.044715 * (x * x * x))))


def _log_sigmoid(z):
    return jnp.minimum(z, 0.0) - jnp.log1p(jnp.exp(-jnp.abs(z)))


def _resident(shape):
    nd = len(shape)
    return pl.BlockSpec(shape, lambda *_: (0,) * nd, pipeline_mode=pl.Buffered(1))


def _front0_kernel(x_ref, g_ref, w_ref, wt_ref, wgk_ref, bgk_ref, cos_ref, sa_ref, sb_ref,
                   cost_ref, sat_ref, sbt_ref,
                   q_ref, k_ref, v_ref, km_ref, bq_ref, bk_ref, bv_ref, gl_ref, og_ref, *, tm):
    hn = _rms(x_ref[0], g_ref[...]).astype(BF16)

    def proj(c0, n):
        return jnp.dot(hn, w_ref[:, c0:c0 + n], preferred_element_type=F32)

    def proj_t(r0, n):
        return lax.dot_general(wt_ref[r0:r0 + n, :], hn, (((1,), (1,)), ((), ())),
                               preferred_element_type=F32)

    def rot(t, cos, sa, sb, axis):
        parts = []
        for c in range(t.shape[axis] // LANES):
            tc = lax.slice_in_dim(t, c * LANES, (c + 1) * LANES, axis=axis)
            parts.append(tc * cos + pltpu.roll(tc, LANES - 32, axis) * sa
                         + pltpu.roll(tc, 32, axis) * sb)
        return jnp.concatenate(parts, axis=axis)

    q_ref[0] = rot(proj_t(0, A_WIDTH), cost_ref[...], sat_ref[...], sbt_ref[...], 0)
    vt = proj_t(A_WIDTH, A_WIDTH)
    c = 0
    k = rot(proj(c, A_WIDTH), cos_ref[...], sa_ref[...], sb_ref[...], 1); c += A_WIDTH
    k_ref[0] = k.astype(BF16)
    nblk = tm // MOBA_BLOCK
    km_ref[...] = jnp.mean(k.reshape(nblk, MOBA_BLOCK, A_WIDTH), axis=1).reshape(nblk, 1, A_WIDTH)
    ones_rows = (lax.broadcasted_iota(jnp.int32, (V_ROWS - A_HEAD_DIM, MOBA_BLOCK), 0)
                 == 0).astype(BF16)
    for h in range(A_HEADS):
        for n in range(nblk):
            v_ref[0, h, n, 0:A_HEAD_DIM, :] = vt[h * A_HEAD_DIM:(h + 1) * A_HEAD_DIM,
                                                 n * MOBA_BLOCK:(n + 1) * MOBA_BLOCK].astype(BF16)
            v_ref[0, h, n, A_HEAD_DIM:, :] = ones_rows
    bq_ref[0] = proj(c, B_KEY_WIDTH) * (B_HEAD_K ** -0.5); c += B_KEY_WIDTH
    bk_ref[0] = proj(c, B_KEY_WIDTH); c += B_KEY_WIDTH
    bv_ref[0] = proj(c, B_VAL_WIDTH); c += B_VAL_WIDTH
    og_ref[0] = proj(c, B_VAL_WIDTH); c += B_VAL_WIDTH
    gk = proj(c, LANES).astype(BF16)
    z = jnp.dot(gk, wgk_ref[...], preferred_element_type=F32) + bgk_ref[...]
    gl_ref[0] = _log_sigmoid(z) / GLA_GATE_NORM


def _front0(x, g, w_in, w_gk2, b_gk2):
    bsz, seq, d = x.shape
    tm = min(512, seq)
    aq, ak, av, bq, bk, bv, bgk, bog = jnp.split(
        w_in, [512, 1024, 1536, 1792, 2048, 2560, 2576], axis=1)
    gk_pad = LANES - GLA_GATE_RANK
    w = jnp.concatenate([ak, bq, bk, bv, bog, jnp.pad(bgk, ((0, 0), (0, gk_pad)))],
                        axis=1).astype(BF16)
    wt = jnp.concatenate([aq, av], axis=1).T.astype(BF16)
    wgk = jnp.pad(w_gk2, ((0, gk_pad), (0, 0))).astype(BF16)
    ncol = w.shape[1]

    half = A_HEAD_DIM // 2
    lane = jnp.arange(LANES)
    inv = ROPE_THETA ** (-jnp.arange(half, dtype=F32) / half)
    ang = jnp.arange(seq, dtype=jnp.int32).astype(F32)[:, None] * inv[None, :]
    cos_h, sin_h = jnp.cos(ang), jnp.sin(ang)
    first = (lane % A_HEAD_DIM) < half
    cos_t = jnp.tile(cos_h, (1, LANES // half))
    sin_t = jnp.tile(sin_h, (1, LANES // half))
    sa = jnp.where(first[None, :], -sin_t, 0.0)
    sb = jnp.where(first[None, :], 0.0, sin_t)

    nb = seq // MOBA_BLOCK
    row = lambda n: pl.BlockSpec((1, tm, n), lambda b, i: (b, i, 0))
    tab = pl.BlockSpec((tm, LANES), lambda b, i: (i, 0))
    tab_t = pl.BlockSpec((LANES, tm), lambda b, i: (0, i))
    nt = seq // tm
    outs = pl.pallas_call(
        functools.partial(_front0_kernel, tm=tm),
        grid=(bsz, nt),
        in_specs=[row(d), _resident((1, d)), _resident((d, ncol)), _resident((2 * A_WIDTH, d)),
                  _resident((LANES, B_KEY_WIDTH)), _resident((1, B_KEY_WIDTH)),
                  tab, tab, tab, tab_t, tab_t, tab_t],
        out_specs=[pl.BlockSpec((1, A_WIDTH, tm), lambda b, i: (b, 0, i)), row(A_WIDTH),
                   pl.BlockSpec((1, A_HEADS, tm // MOBA_BLOCK, V_ROWS, MOBA_BLOCK),
                                lambda b, i: (b, 0, i, 0, 0)),
                   pl.BlockSpec((tm // MOBA_BLOCK, 1, A_WIDTH),
                                lambda b, i: (b * nt + i, 0, 0)),
                   row(B_KEY_WIDTH), row(B_KEY_WIDTH), row(B_VAL_WIDTH), row(B_KEY_WIDTH),
                   row(B_VAL_WIDTH)],
        out_shape=[jax.ShapeDtypeStruct((bsz, A_WIDTH, seq), F32),
                   jax.ShapeDtypeStruct((bsz, seq, A_WIDTH), BF16),
                   jax.ShapeDtypeStruct((bsz, A_HEADS, nb, V_ROWS, MOBA_BLOCK), BF16),
                   jax.ShapeDtypeStruct((bsz * nb, 1, A_WIDTH), F32),
                   jax.ShapeDtypeStruct((bsz, seq, B_KEY_WIDTH), F32),
                   jax.ShapeDtypeStruct((bsz, seq, B_KEY_WIDTH), F32),
                   jax.ShapeDtypeStruct((bsz, seq, B_VAL_WIDTH), F32),
                   jax.ShapeDtypeStruct((bsz, seq, B_KEY_WIDTH), F32),
                   jax.ShapeDtypeStruct((bsz, seq, B_VAL_WIDTH), F32)],
        compiler_params=_params(("arbitrary", "arbitrary")),
        name="front0",
    )(x, g.reshape(1, d), w, wt, wgk, b_gk2.reshape(1, B_KEY_WIDTH), cos_t, sa, sb,
      cos_t.T, sa.T, sb.T)
    return outs


def _moba_kernel(qt_ref, k_ref, vt_ref, km_ref, o_ref, rhs_ref, acc_ref, sa_ref, sb_ref, *,
                 group, nb):
    qb = pl.program_id(2)
    blk = MOBA_BLOCK
    qt = qt_ref[0]
    km = km_ref[0]
    feat = lax.broadcasted_iota(jnp.int32, (LANES, 1), 0)
    nbr = -(-nb // BF16_ROWS) * BF16_ROWS
    brow = lax.broadcasted_iota(jnp.int32, (nbr, blk), 0)
    browf = brow.astype(F32)
    scale = A_HEAD_DIM ** -0.5 * LOG2_E

    for h in range(2):
        qh = jnp.where(feat // A_HEAD_DIM == h, qt, 0.0)
        gate = jnp.dot(km[0:nbr], qh, precision=lax.Precision.HIGHEST, preferred_element_type=F32)
        g = jnp.where(brow < qb, gate, -jnp.inf)
        bias = jnp.full((nbr, blk), MASK_VALUE, F32)
        for _ in range(MOBA_TOPK):
            mx = jnp.max(g, axis=0, keepdims=True)
            first = jnp.min(jnp.where(g == mx, browf, float(nbr)), axis=0, keepdims=True)
            hit = browf == first
            bias = jnp.where(hit & (mx > -jnp.inf), 0.0, bias)
            g = jnp.where(hit, -jnp.inf, g)
        rhs_ref[:, h * blk:(h + 1) * blk] = jnp.concatenate(
            [(qh * scale).astype(BF16), bias.astype(BF16),
             jnp.full((MOBA_BLOCKS_PAD - nbr, blk), MASK_VALUE, BF16)], axis=0)

    own = pl.multiple_of(qb * blk, blk)
    lhs = jnp.concatenate([k_ref[0, pl.ds(own, blk), :], jnp.zeros((blk, MOBA_BLOCKS_PAD), BF16)],
                          axis=1)
    causal = (lax.broadcasted_iota(jnp.int32, (blk, 2 * blk), 0)
              <= lax.broadcasted_iota(jnp.int32, (blk, 2 * blk), 1) % blk)
    s = jnp.where(causal, jnp.dot(lhs, rhs_ref[...], preferred_element_type=F32), MASK_VALUE)
    m0 = jnp.max(s, axis=0, keepdims=True)
    p = jnp.exp2(s - m0).astype(BF16)
    for h in range(2):
        acc_ref[h] = jnp.dot(vt_ref[0, h, qb], p[:, h * blk:(h + 1) * blk],
                             preferred_element_type=F32)

    keys = group * blk
    blk_of_row = lax.broadcasted_iota(jnp.int32, (keys, MOBA_BLOCKS_PAD), 0) // blk
    lane_blk = lax.broadcasted_iota(jnp.int32, (keys, MOBA_BLOCKS_PAD), 1)

    def scores(g, s_ref):
        j0 = g * group
        off = pl.multiple_of(jnp.minimum(j0, nb - group) * blk, blk)
        onehot = (lane_blk == blk_of_row + j0).astype(BF16)
        lhs_j = jnp.concatenate([k_ref[0, pl.ds(off, keys), :], onehot], axis=1)
        s = jnp.dot(lhs_j, rhs_ref[...], preferred_element_type=F32)
        s_ref[...] = s
        return jnp.max(s, axis=0, keepdims=True)

    def accumulate(g, s_ref, m, s_max):
        j0 = g * group
        m_new = jnp.maximum(m, s_max)
        alpha = jnp.exp2(m - m_new)
        p = jnp.exp2(s_ref[...] - m_new).astype(BF16)
        for h in range(2):
            vt = jnp.concatenate([vt_ref[0, h, jnp.minimum(j0 + u, nb - 1)] for u in range(group)],
                                 axis=1)
            acc_ref[h] = (alpha[:, h * blk:(h + 1) * blk] * acc_ref[h]
                          + jnp.dot(vt, p[:, h * blk:(h + 1) * blk], preferred_element_type=F32))
        return m_new

    def body(i, carry):
        m, max_a = carry
        max_b = scores(2 * i + 1, sb_ref)
        m = accumulate(2 * i, sa_ref, m, max_a)
        max_a = scores(2 * i + 2, sa_ref)
        m = accumulate(2 * i + 1, sb_ref, m, max_b)
        return m, max_a

    n_groups = (qb + group - 1) // group
    lax.fori_loop(0, (n_groups + 1) // 2, body, (m0, scores(0, sa_ref)))

    outs = []
    for h in range(2):
        a = acc_ref[h]
        outs.append(a[0:A_HEAD_DIM] / a[A_HEAD_DIM:A_HEAD_DIM + 1])
    o_ref[0] = jnp.concatenate(outs, axis=0).T


def _moba(qt, k, vt, kmean):
    bsz, seq, _ = k.shape
    nb = seq // MOBA_BLOCK
    assert nb <= MOBA_BLOCKS_PAD
    km = jnp.pad(kmean.reshape(bsz, nb, A_WIDTH), ((0, 0), (0, MOBA_BLOCKS_PAD - nb), (0, 0)))
    npair = A_WIDTH // LANES
    group = 4 if nb % 4 == 0 else 1
    return pl.pallas_call(
        functools.partial(_moba_kernel, group=group, nb=nb),
        grid=(bsz, npair, nb),
        in_specs=[pl.BlockSpec((1, LANES, MOBA_BLOCK), lambda b, hp, i: (b, hp, i)),
                  pl.BlockSpec((1, seq, LANES), lambda b, hp, i: (b, 0, hp)),
                  pl.BlockSpec((1, 2, nb, V_ROWS, MOBA_BLOCK), lambda b, hp, i: (b, hp, 0, 0, 0)),
                  pl.BlockSpec((1, MOBA_BLOCKS_PAD, LANES), lambda b, hp, i: (b, 0, hp))],
        out_specs=pl.BlockSpec((1, MOBA_BLOCK, LANES), lambda b, hp, i: (b, i, hp)),
        out_shape=jax.ShapeDtypeStruct((bsz, seq, A_WIDTH), F32),
        scratch_shapes=[pltpu.VMEM((2 * LANES, 2 * MOBA_BLOCK), BF16),
                        pltpu.VMEM((2, V_ROWS, MOBA_BLOCK), F32),
                        pltpu.VMEM((group * MOBA_BLOCK, 2 * MOBA_BLOCK), F32),
                        pltpu.VMEM((group * MOBA_BLOCK, 2 * MOBA_BLOCK), F32)],
        compiler_params=_params(("arbitrary", "arbitrary", "arbitrary")),
        name="moba",
    )(qt, k, vt, km)


def _gla_kernel(q_ref, k_ref, v_ref, g_ref, og_ref, ng_ref, o_ref, st_ref, *, tm):
    @pl.when(pl.program_id(1) == 0)
    def _():
        st_ref[...] = jnp.zeros_like(st_ref)

    q, k, v, g = q_ref[0], k_ref[0], v_ref[0], g_ref[0]
    kw, vw = B_KEY_WIDTH, B_VAL_WIDTH
    row = lax.broadcasted_iota(jnp.int32, (tm, 1), 0)
    r_sub = row % GLA_SUB
    r_chk = row % GLA_CHUNK

    def seg_cumsum(x, r, length):
        s = 1
        while s < length:
            x = x + jnp.where(r >= s, pltpu.roll(x, s, 0), 0.0)
            s *= 2
        return x

    c_sub = seg_cumsum(g, r_sub, GLA_SUB)
    b = seg_cumsum(g, r_chk, GLA_CHUNK)

    head_kv = (lax.broadcasted_iota(jnp.int32, (kw, vw), 0) // B_HEAD_K
               == lax.broadcasted_iota(jnp.int32, (kw, vw), 1) // B_HEAD_V).astype(BF16)
    head_vk = (lax.broadcasted_iota(jnp.int32, (vw, kw), 0) // B_HEAD_V
               == lax.broadcasted_iota(jnp.int32, (vw, kw), 1) // B_HEAD_K)
    head_vv = (lax.broadcasted_iota(jnp.int32, (vw, vw), 0) // B_HEAD_V
               == lax.broadcasted_iota(jnp.int32, (vw, vw), 1) // B_HEAD_V).astype(BF16)

    o_acc = jnp.zeros((tm, vw), F32)
    for d in range(GLA_SUB):
        if d == 0:
            w = q * k
            vv = v
        else:
            w = jnp.where(r_sub >= d,
                          q * pltpu.roll(k, d, 0) * jnp.exp(c_sub - pltpu.roll(c_sub, d, 0)), 0.0)
            vv = pltpu.roll(v, d, 0)
        o_acc = o_acc + jnp.dot(w.astype(BF16), head_kv, preferred_element_type=F32) * vv

    q_sub = q * jnp.exp(c_sub)
    q_chk = q * jnp.exp(b)

    nsub = GLA_CHUNK // GLA_SUB
    rr = lax.broadcasted_iota(jnp.int32, (GLA_CHUNK, 1), 0)
    lane_k = lax.broadcasted_iota(jnp.int32, (1, (nsub - 1) * kw), 1)
    lane_v = lax.broadcasted_iota(jnp.int32, (1, vw), 1)
    outs = []
    for c in range(tm // GLA_CHUNK):
        lo = c * GLA_CHUNK
        bc = b[lo:lo + GLA_CHUNK]
        kc = k[lo:lo + GLA_CHUNK]
        vc = v[lo:lo + GLA_CHUNK].astype(BF16)
        qs = q_sub[lo:lo + GLA_CHUNK]

        rhs_parts, lhs_parts = [], []
        for i in range(1, nsub):
            b_end = bc[i * GLA_SUB - 1:i * GLA_SUB, :]
            decay = jnp.exp(jnp.minimum(b_end - bc, 0.0))
            rhs_parts.append(jnp.where(rr < i * GLA_SUB, kc * decay, 0.0).astype(BF16))
            lhs_parts.append(jnp.where(rr // GLA_SUB == i, qs, 0.0).astype(BF16))
        rhs = jnp.concatenate(rhs_parts, axis=1)
        lhs_t = jnp.concatenate(lhs_parts, axis=1)
        lhs = jnp.concatenate(
            [jnp.where((lane_k % kw) // B_HEAD_K == h, lhs_t, jnp.zeros_like(lhs_t))
             for h in range(B_HEADS)], axis=0)
        attn = lax.dot_general(lhs, rhs, (((1,), (1,)), ((), ())), preferred_element_type=F32)
        res = jnp.dot(attn.astype(BF16), vc, preferred_element_type=F32)
        o_off = jnp.zeros((GLA_CHUNK, vw), F32)
        for h in range(B_HEADS):
            o_off = o_off + jnp.where(lane_v // B_HEAD_V == h,
                                      res[h * GLA_CHUNK:(h + 1) * GLA_CHUNK], 0.0)

        st = st_ref[...]
        o_int = lax.dot_general(q_chk[lo:lo + GLA_CHUNK].astype(BF16), st.astype(BF16),
                                (((1,), (1,)), ((), ())), preferred_element_type=F32)
        b_last = bc[GLA_CHUNK - 1:GLA_CHUNK, :]
        ks = (kc * jnp.exp(b_last - bc)).astype(BF16)
        upd = lax.dot_general(vc, ks, (((0,), (0,)), ((), ())), preferred_element_type=F32)
        st_ref[...] = st * jnp.exp(b_last) + jnp.where(head_vk, upd, 0.0)
        outs.append(o_off + o_int)

    o = o_acc + jnp.concatenate(outs, axis=0)

    sq = o * o
    sq_hi = sq.astype(BF16)
    sq_lo = (sq - sq_hi.astype(F32)).astype(BF16)
    ms = (jnp.dot(sq_hi, head_vv, preferred_element_type=F32)
          + jnp.dot(sq_lo, head_vv, preferred_element_type=F32)) * (1.0 / B_HEAD_V)
    og = og_ref[0]
    o_ref[0] = (o * lax.rsqrt(ms + NORM_EPS) * ng_ref[...]) * (og * jax.nn.sigmoid(og))


def _gla(bq, bk, bv, gl, og, norm_g):
    bsz, seq, _ = bq.shape
    tm = min(256, seq)
    row = lambda n: pl.BlockSpec((1, tm, n), lambda b, i: (b, i, 0))
    return pl.pallas_call(
        functools.partial(_gla_kernel, tm=tm),
        grid=(bsz, seq // tm),
        in_specs=[row(B_KEY_WIDTH), row(B_KEY_WIDTH), row(B_VAL_WIDTH), row(B_KEY_WIDTH),
                  row(B_VAL_WIDTH), _resident((1, B_VAL_WIDTH))],
        out_specs=row(B_VAL_WIDTH),
        out_shape=jax.ShapeDtypeStruct((bsz, seq, B_VAL_WIDTH), F32),
        scratch_shapes=[pltpu.VMEM((B_VAL_WIDTH, B_KEY_WIDTH), F32)],
        compiler_params=_params(("arbitrary", "arbitrary")),
        name="gla",
    )(bq, bk, bv, gl, og, norm_g.reshape(1, B_VAL_WIDTH))


def _outproj_kernel(x_ref, oa_ref, ob_ref, wa_ref, wb_ref, o_ref):
    o_ref[0] = (x_ref[0]
                + jnp.dot(oa_ref[0].astype(BF16), wa_ref[...], preferred_element_type=F32)
                + jnp.dot(ob_ref[0].astype(BF16), wb_ref[...], preferred_element_type=F32))


def _outproj(x, oa, ob, w_out):
    bsz, seq, d = x.shape
    tm = min(512, seq)
    row = lambda n: pl.BlockSpec((1, tm, n), lambda b, i: (b, i, 0))
    w = w_out.astype(BF16)
    return pl.pallas_call(
        _outproj_kernel,
        grid=(bsz, seq // tm),
        in_specs=[row(d), row(A_WIDTH), row(B_VAL_WIDTH),
                  _resident((A_WIDTH, d)), _resident((B_VAL_WIDTH, d))],
        out_specs=row(d),
        out_shape=jax.ShapeDtypeStruct((bsz, seq, d), F32),
        compiler_params=_params(("arbitrary", "arbitrary")),
        name="outproj",
    )(x, oa, ob, w[:A_WIDTH], w[A_WIDTH:])


def _ffn_kernel(x_ref, g_ref, wup_ref, cw_ref, cb_ref, wdn_ref, fg_ref, o_ref,
                halo_ref, ext_ref, *, tm, bn, dff, final_norm):
    nkb = dff // bn

    @pl.when(pl.program_id(1) == 0)
    def _():
        halo_ref[...] = jnp.zeros_like(halo_ref)

    x = x_ref[0]
    hn = _rms(x, g_ref[...]).astype(BF16)

    def up_stage(kb):
        for part in range(2):
            c0 = part * dff + kb * bn
            up = jnp.dot(hn, wup_ref[:, c0:c0 + bn], preferred_element_type=F32)
            ext_ref[kb % 2, part, 0:SUBLANES, :] = halo_ref[:, c0:c0 + bn]
            ext_ref[kb % 2, part, SUBLANES:, :] = up
            halo_ref[:, c0:c0 + bn] = up[tm - SUBLANES:, :]

    def gate_stage(kb):
        halves = []
        for part in range(2):
            c0 = part * dff + kb * bn
            y = cb_ref[:, c0:c0 + bn]
            for i in range(FFN_CONV):
                off = SUBLANES - (FFN_CONV - 1) + i
                y = y + ext_ref[kb % 2, part, pl.ds(off, tm), :] * cw_ref[i:i + 1, c0:c0 + bn]
            halves.append(y)
        return (halves[0] * _gelu_tanh(halves[1])).astype(BF16)

    up_stage(0)
    acc = x
    for kb in range(nkb):
        if kb + 1 < nkb:
            up_stage(kb + 1)
        acc = acc + jnp.dot(gate_stage(kb), wdn_ref[kb * bn:(kb + 1) * bn, :],
                            preferred_element_type=F32)
    if final_norm:
        acc = _rms(acc, fg_ref[...])
    o_ref[0] = acc


def _conv_ffn(x, g, w_up, conv_w, conv_b, w_down, final_g, final_norm):
    bsz, seq, d = x.shape
    dff = w_down.shape[0]
    tm = min(512, seq)
    bn = 512
    row = pl.BlockSpec((1, tm, d), lambda b, i: (b, i, 0))
    return pl.pallas_call(
        functools.partial(_ffn_kernel, tm=tm, bn=bn, dff=dff, final_norm=final_norm),
        grid=(bsz, seq // tm),
        in_specs=[row, _resident((1, d)), _resident((d, 2 * dff)), _resident((FFN_CONV, 2 * dff)),
                  _resident((1, 2 * dff)), _resident((dff, d)), _resident((1, d))],
        out_specs=row,
        out_shape=jax.ShapeDtypeStruct((bsz, seq, d), F32),
        scratch_shapes=[pltpu.VMEM((SUBLANES, 2 * dff), F32),
                        pltpu.VMEM((2, 2, tm + SUBLANES, bn), F32)],
        compiler_params=_params(("arbitrary", "arbitrary")),
        name="conv_ffn_final" if final_norm else "conv_ffn",
    )(x, g.reshape(1, d), w_up.astype(BF16), conv_w, conv_b.reshape(1, 2 * dff),
      w_down.astype(BF16), final_g.reshape(1, d))


def _rglru_kernel(x_ref, g_ref, win_ref, cw_ref, cb_ref, wa_ref, ba_ref, wx_ref, bx_ref,
                  lam_ref, wout_ref, o_ref, ext_ref, ca_ref, cu_ref, h_ref, hlast_ref, *, tm, drnn):
    @pl.when(pl.program_id(1) == 0)
    def _():
        ext_ref[0:SUBLANES, :] = jnp.zeros((SUBLANES, drnn), F32)
        hlast_ref[...] = jnp.zeros_like(hlast_ref)

    x = x_ref[0]
    hn = _rms(x, g_ref[...]).astype(BF16)
    gate = jnp.dot(hn, win_ref[:, 0:drnn], preferred_element_type=F32)
    xr_raw = jnp.dot(hn, win_ref[:, drnn:2 * drnn], preferred_element_type=F32)

    ext_ref[SUBLANES:, :] = xr_raw
    xr = cb_ref[...]
    for i in range(RG_CONV):
        off = SUBLANES - (RG_CONV - 1) + i
        xr = xr + ext_ref[pl.ds(off, tm), :] * cw_ref[i:i + 1, :]
    ext_ref[0:SUBLANES, :] = xr_raw[tm - SUBLANES:, :]

    bw = drnn // RG_BLOCKS
    xb = xr.astype(BF16)
    ra, ri = [], []
    for n in range(RG_BLOCKS):
        xn = xb[:, n * bw:(n + 1) * bw]
        ra.append(jnp.dot(xn, wa_ref[n], preferred_element_type=F32))
        ri.append(jnp.dot(xn, wx_ref[n], preferred_element_type=F32))
    r = jax.nn.sigmoid(jnp.concatenate(ra, axis=1) + ba_ref[...])
    ig = jax.nn.sigmoid(jnp.concatenate(ri, axis=1) + bx_ref[...])
    log_a = r * lam_ref[...]
    a = jnp.exp(log_a)
    th = jnp.tanh(log_a)
    one_minus_a2 = -2.0 * th / (1.0 - th)
    u = jnp.sqrt(one_minus_a2) * (ig * xr)

    r8 = lax.broadcasted_iota(jnp.int32, (tm, 1), 0) % SUBLANES
    ca, cu = a, u
    s = 1
    while s < SUBLANES:
        keep = r8 >= s
        a_sh = jnp.where(keep, pltpu.roll(ca, s, 0), 1.0)
        u_sh = jnp.where(keep, pltpu.roll(cu, s, 0), 0.0)
        cu = cu + ca * u_sh
        ca = ca * a_sh
        s *= 2
    ca_ref[...] = ca
    cu_ref[...] = cu

    def group(gi, h_prev):
        lo = pl.multiple_of(gi * SUBLANES, SUBLANES)
        hg = cu_ref[pl.ds(lo, SUBLANES), :] + ca_ref[pl.ds(lo, SUBLANES), :] * h_prev
        h_ref[pl.ds(lo, SUBLANES), :] = hg
        return hg[SUBLANES - 1:SUBLANES, :]

    h_last = lax.fori_loop(0, tm // SUBLANES, group, hlast_ref[...])
    hlast_ref[...] = h_last

    y = (h_ref[...] * _gelu_tanh(gate)).astype(BF16)
    o_ref[0] = x + jnp.dot(y, wout_ref[...], preferred_element_type=F32)


def _rglru(x, g, w_in, conv_w, conv_b, w_a, b_a, w_x, b_x, lam, w_out):
    bsz, seq, d = x.shape
    drnn = w_out.shape[0]
    bw = drnn // RG_BLOCKS
    tm = min(256, seq)
    row = pl.BlockSpec((1, tm, d), lambda b, i: (b, i, 0))
    lam_c = (-LRU_C) * jax.nn.softplus(-lam.astype(F32))
    return pl.pallas_call(
        functools.partial(_rglru_kernel, tm=tm, drnn=drnn),
        grid=(bsz, seq // tm),
        in_specs=[row, _resident((1, d)), _resident((d, 2 * drnn)), _resident((RG_CONV, drnn)),
                  _resident((1, drnn)), _resident((RG_BLOCKS, bw, bw)), _resident((1, drnn)),
                  _resident((RG_BLOCKS, bw, bw)), _resident((1, drnn)), _resident((1, drnn)),
                  _resident((drnn, d))],
        out_specs=row,
        out_shape=jax.ShapeDtypeStruct((bsz, seq, d), F32),
        scratch_shapes=[pltpu.VMEM((tm + SUBLANES, drnn), F32),
                        pltpu.VMEM((tm, drnn), F32),
                        pltpu.VMEM((tm, drnn), F32),
                        pltpu.VMEM((tm, drnn), F32),
                        pltpu.VMEM((1, drnn), F32)],
        compiler_params=_params(("arbitrary", "arbitrary")),
        name="rglru",
    )(x, g.reshape(1, d), w_in.astype(BF16), conv_w, conv_b.reshape(1, drnn),
      w_a.astype(BF16), b_a.reshape(1, drnn), w_x.astype(BF16), b_x.reshape(1, drnn),
      lam_c.reshape(1, drnn), w_out.astype(BF16))


def kernel(x, mix_norm_g, ffn_norm_g, final_norm_g, ev_w_in, ev_w_gk2, ev_b_gk2, ev_gla_norm_g,
           ev_w_out, od_w_in, od_conv_w, od_conv_b, od_w_a, od_b_a, od_w_x, od_b_x, od_lambda,
           od_w_out, ffn_w_up, ffn_conv_w, ffn_conv_b, ffn_w_down):
    depth = mix_norm_g.shape[0]
    h = x
    for l in range(depth):
        j = l // 2
        if l % 2 == 0:
            q, k, v, kmean, bq, bk, bv, gl, og = _front0(h, mix_norm_g[l], ev_w_in[j], ev_w_gk2[j],
                                                       ev_b_gk2[j])
            o_a = _moba(q, k, v, kmean)
            o_b = _gla(bq, bk, bv, gl, og, ev_gla_norm_g[j])
            h = _outproj(h, o_a, o_b, ev_w_out[j])
        else:
            h = _rglru(h, mix_norm_g[l], od_w_in[j], od_conv_w[j], od_conv_b[j], od_w_a[j],
                       od_b_a[j], od_w_x[j], od_b_x[j], od_lambda[j], od_w_out[j])
        h = _conv_ffn(h, ffn_norm_g[l], ffn_w_up[l], ffn_conv_w[l], ffn_conv_b[l], ffn_w_down[l],
                      final_norm_g, final_norm=(l == depth - 1))
    return h
```

```python
import functools

import jax
import jax.numpy as jnp
from jax import lax
from jax.experimental import pallas as pl
from jax.experimental.pallas import tpu as pltpu

F32 = jnp.float32
BF16 = jnp.bfloat16

A_HEADS = 8
A_HEAD_DIM = 64
A_WIDTH = A_HEADS * A_HEAD_DIM
MOBA_BLOCK = 256
MOBA_TOPK = 3
ROPE_THETA = 10000.0
B_HEADS = 8
B_HEAD_K = 32
B_HEAD_V = 64
B_KEY_WIDTH = B_HEADS * B_HEAD_K
B_VAL_WIDTH = B_HEADS * B_HEAD_V
GLA_GATE_RANK = 16
GLA_GATE_NORM = 16.0
RG_BLOCKS = 4
RG_CONV = 4
LRU_C = 8.0
FFN_CONV = 3
NORM_EPS = 1e-6
MASK_VALUE = -1e30

LANES = 128
SUBLANES = 8
BF16_ROWS = 16
LOG2_E = 1.4426950408889634
VMEM_LIMIT = 56 * 1024 * 1024

GLA_CHUNK = 64
GLA_SUB = 16
MOBA_BLOCKS_PAD = LANES
V_ROWS = 80


def _params(sem):
    return pltpu.CompilerParams(dimension_semantics=sem, vmem_limit_bytes=VMEM_LIMIT)


def _rms(x, g):
    ms = jnp.mean(x * x, axis=-1, keepdims=True)
    return x * lax.rsqrt(ms + NORM_EPS) * g


def _gelu_tanh(x):
    return 0.5 * x * (1.0 + jnp.tanh(0.7978845608028654 * (x + 0.044715 * (x * x * x))))


def _log_sigmoid(z):
    return jnp.minimum(z, 0.0) - jnp.log1p(jnp.exp(-jnp.abs(z)))


def _resident(shape):
    nd = len(shape)
    return pl.BlockSpec(shape, lambda *_: (0,) * nd, pipeline_mode=pl.Buffered(1))


def _front0_kernel(x_ref, g_ref, w_ref, wt_ref, wgk_ref, bgk_ref, cos_ref, sa_ref, sb_ref,
                   cost_ref, sat_ref, sbt_ref,
                   q_ref, k_ref, v_ref, km_ref, bq_ref, bk_ref, bv_ref, gl_ref, og_ref, *, tm):
    hn = _rms(x_ref[0], g_ref[...]).astype(BF16)

    def proj(c0, n):
        return jnp.dot(hn, w_ref[:, c0:c0 + n], preferred_element_type=F32)

    def proj_t(r0, n):
        return lax.dot_general(wt_ref[r0:r0 + n, :], hn, (((1,), (1,)), ((), ())),
                               preferred_element_type=F32)

    def rot(t, cos, sa, sb, axis):
        parts = []
        for c in range(t.shape[axis] // LANES):
            tc = lax.slice_in_dim(t, c * LANES, (c + 1) * LANES, axis=axis)
            parts.append(tc * cos + pltpu.roll(tc, LANES - 32, axis) * sa
                         + pltpu.roll(tc, 32, axis) * sb)
        return jnp.concatenate(parts, axis=axis)

    q_ref[0] = rot(proj_t(0, A_WIDTH), cost_ref[...], sat_ref[...], sbt_ref[...], 0)
    vt = proj_t(A_WIDTH, A_WIDTH)
    c = 0
    k = rot(proj(c, A_WIDTH), cos_ref[...], sa_ref[...], sb_ref[...], 1); c += A_WIDTH
    k_ref[0] = k.astype(BF16)
    nblk = tm // MOBA_BLOCK
    km_ref[...] = jnp.mean(k.reshape(nblk, MOBA_BLOCK, A_WIDTH), axis=1).reshape(nblk, 1, A_WIDTH)
    ones_rows = (lax.broadcasted_iota(jnp.int32, (V_ROWS - A_HEAD_DIM, MOBA_BLOCK), 0)
                 == 0).astype(BF16)
    for h in range(A_HEADS):
        for n in range(nblk):
            v_ref[0, h, n, 0:A_HEAD_DIM, :] = vt[h * A_HEAD_DIM:(h + 1) * A_HEAD_DIM,
                                                 n * MOBA_BLOCK:(n + 1) * MOBA_BLOCK].astype(BF16)
            v_ref[0, h, n, A_HEAD_DIM:, :] = ones_rows
    bq_ref[0] = proj(c, B_KEY_WIDTH) * (B_HEAD_K ** -0.5); c += B_KEY_WIDTH
    bk_ref[0] = proj(c, B_KEY_WIDTH); c += B_KEY_WIDTH
    bv_ref[0] = proj(c, B_VAL_WIDTH); c += B_VAL_WIDTH
    og_ref[0] = proj(c, B_VAL_WIDTH); c += B_VAL_WIDTH
    gk = proj(c, LANES).astype(BF16)
    z = jnp.dot(gk, wgk_ref[...], preferred_element_type=F32) + bgk_ref[...]
    gl_ref[0] = _log_sigmoid(z) / GLA_GATE_NORM


def _front0(x, g, w_in, w_gk2, b_gk2):
    bsz, seq, d = x.shape
    tm = min(512, seq)
    aq, ak, av, bq, bk, bv, bgk, bog = jnp.split(
        w_in, [512, 1024, 1536, 1792, 2048, 2560, 2576], axis=1)
    gk_pad = LANES - GLA_GATE_RANK
    w = jnp.concatenate([ak, bq, bk, bv, bog, jnp.pad(bgk, ((0, 0), (0, gk_pad)))],
                        axis=1).astype(BF16)
    wt = jnp.concatenate([aq, av], axis=1).T.astype(BF16)
    wgk = jnp.pad(w_gk2, ((0, gk_pad), (0, 0))).astype(BF16)
    ncol = w.shape[1]

    half = A_HEAD_DIM // 2
    lane = jnp.arange(LANES)
    inv = ROPE_THETA ** (-jnp.arange(half, dtype=F32) / half)
    ang = jnp.arange(seq, dtype=jnp.int32).astype(F32)[:, None] * inv[None, :]
    cos_h, sin_h = jnp.cos(ang), jnp.sin(ang)
    first = (lane % A_HEAD_DIM) < half
    cos_t = jnp.tile(cos_h, (1, LANES // half))
    sin_t = jnp.tile(sin_h, (1, LANES // half))
    sa = jnp.where(first[None, :], -sin_t, 0.0)
    sb = jnp.where(first[None, :], 0.0, sin_t)

    nb = seq // MOBA_BLOCK
    row = lambda n: pl.BlockSpec((1, tm, n), lambda b, i: (b, i, 0))
    tab = pl.BlockSpec((tm, LANES), lambda b, i: (i, 0))
    tab_t = pl.BlockSpec((LANES, tm), lambda b, i: (0, i))
    nt = seq // tm
    outs = pl.pallas_call(
        functools.partial(_front0_kernel, tm=tm),
        grid=(bsz, nt),
        in_specs=[row(d), _resident((1, d)), _resident((d, ncol)), _resident((2 * A_WIDTH, d)),
                  _resident((LANES, B_KEY_WIDTH)), _resident((1, B_KEY_WIDTH)),
                  tab, tab, tab, tab_t, tab_t, tab_t],
        out_specs=[pl.BlockSpec((1, A_WIDTH, tm), lambda b, i: (b, 0, i)), row(A_WIDTH),
                   pl.BlockSpec((1, A_HEADS, tm // MOBA_BLOCK, V_ROWS, MOBA_BLOCK),
                                lambda b, i: (b, 0, i, 0, 0)),
                   pl.BlockSpec((tm // MOBA_BLOCK, 1, A_WIDTH),
                                lambda b, i: (b * nt + i, 0, 0)),
                   row(B_KEY_WIDTH), row(B_KEY_WIDTH), row(B_VAL_WIDTH), row(B_KEY_WIDTH),
                   row(B_VAL_WIDTH)],
        out_shape=[jax.ShapeDtypeStruct((bsz, A_WIDTH, seq), F32),
                   jax.ShapeDtypeStruct((bsz, seq, A_WIDTH), BF16),
                   jax.ShapeDtypeStruct((bsz, A_HEADS, nb, V_ROWS, MOBA_BLOCK), BF16),
                   jax.ShapeDtypeStruct((bsz * nb, 1, A_WIDTH), F32),
                   jax.ShapeDtypeStruct((bsz, seq, B_KEY_WIDTH), F32),
                   jax.ShapeDtypeStruct((bsz, seq, B_KEY_WIDTH), F32),
                   jax.ShapeDtypeStruct((bsz, seq, B_VAL_WIDTH), F32),
                   jax.ShapeDtypeStruct((bsz, seq, B_KEY_WIDTH), F32),
                   jax.ShapeDtypeStruct((bsz, seq, B_VAL_WIDTH), F32)],
        compiler_params=_params(("arbitrary", "arbitrary")),
        name="front0",
    )(x, g.reshape(1, d), w, wt, wgk, b_gk2.reshape(1, B_KEY_WIDTH), cos_t, sa, sb,
      cos_t.T, sa.T, sb.T)
    return outs


def _moba_kernel(qt_ref, k_ref, vt_ref, km_ref, o_ref, rhs_ref, acc_ref, sa_ref, sb_ref, *,
                 group, nb):
    qb = pl.program_id(2)
    blk = MOBA_BLOCK
    qt = qt_ref[0]
    km = km_ref[0]
    feat = lax.broadcasted_iota(jnp.int32, (LANES, 1), 0)
    nbr = -(-nb // BF16_ROWS) * BF16_ROWS
    brow = lax.broadcasted_iota(jnp.int32, (nbr, blk), 0)
    browf = brow.astype(F32)
    scale = A_HEAD_DIM ** -0.5 * LOG2_E

    for h in range(2):
        qh = jnp.where(feat // A_HEAD_DIM == h, qt, 0.0)
        gate = jnp.dot(km[0:nbr], qh, precision=lax.Precision.HIGHEST, preferred_element_type=F32)
        g = jnp.where(brow < qb, gate, -jnp.inf)
        bias = jnp.full((nbr, blk), MASK_VALUE, F32)
        for _ in range(MOBA_TOPK):
            mx = jnp.max(g, axis=0, keepdims=True)
            first = jnp.min(jnp.where(g == mx, browf, float(nbr)), axis=0, keepdims=True)
            hit = browf == first
            bias = jnp.where(hit & (mx > -jnp.inf), 0.0, bias)
            g = jnp.where(hit, -jnp.inf, g)
        rhs_ref[:, h * blk:(h + 1) * blk] = jnp.concatenate(
            [(qh * scale).astype(BF16), bias.astype(BF16),
             jnp.full((MOBA_BLOCKS_PAD - nbr, blk), MASK_VALUE, BF16)], axis=0)

    own = pl.multiple_of(qb * blk, blk)
    lhs = jnp.concatenate([k_ref[0, pl.ds(own, blk), :], jnp.zeros((blk, MOBA_BLOCKS_PAD), BF16)],
                          axis=1)
    causal = (lax.broadcasted_iota(jnp.int32, (blk, 2 * blk), 0)
              <= lax.broadcasted_iota(jnp.int32, (blk, 2 * blk), 1) % blk)
    s = jnp.where(causal, jnp.dot(lhs, rhs_ref[...], preferred_element_type=F32), MASK_VALUE)
    m0 = jnp.max(s, axis=0, keepdims=True)
    p = jnp.exp2(s - m0).astype(BF16)
    for h in range(2):
        acc_ref[h] = jnp.dot(vt_ref[0, h, qb], p[:, h * blk:(h + 1) * blk],
                             preferred_element_type=F32)

    keys = group * blk
    blk_of_row = lax.broadcasted_iota(jnp.int32, (keys, MOBA_BLOCKS_PAD), 0) // blk
    lane_blk = lax.broadcasted_iota(jnp.int32, (keys, MOBA_BLOCKS_PAD), 1)

    def scores(g, s_ref):
        j0 = g * group
        off = pl.multiple_of(jnp.minimum(j0, nb - group) * blk, blk)
        onehot = (lane_blk == blk_of_row + j0).astype(BF16)
        lhs_j = jnp.concatenate([k_ref[0, pl.ds(off, keys), :], onehot], axis=1)
        s = jnp.dot(lhs_j, rhs_ref[...], preferred_element_type=F32)
        s_ref[...] = s
        return jnp.max(s, axis=0, keepdims=True)

    def accumulate(g, s_ref, m, s_max):
        j0 = g * group
        m_new = jnp.maximum(m, s_max)
        alpha = jnp.exp2(m - m_new)
        p = jnp.exp2(s_ref[...] - m_new).astype(BF16)
        for h in range(2):
            vt = jnp.concatenate([vt_ref[0, h, jnp.minimum(j0 + u, nb - 1)] for u in range(group)],
                                 axis=1)
            acc_ref[h] = (alpha[:, h * blk:(h + 1) * blk] * acc_ref[h]
                          + jnp.dot(vt, p[:, h * blk:(h + 1) * blk], preferred_element_type=F32))
        return m_new

    def body(i, carry):
        m, max_a = carry
        max_b = scores(2 * i + 1, sb_ref)
        m = accumulate(2 * i, sa_ref, m, max_a)
        max_a = scores(2 * i + 2, sa_ref)
        m = accumulate(2 * i + 1, sb_ref, m, max_b)
        return m, max_a

    n_groups = (qb + group - 1) // group
    m, max_a = lax.fori_loop(0, n_groups // 2, body, (m0, scores(0, sa_ref)))

    @pl.when(n_groups % 2 == 1)
    def _():
        accumulate(n_groups - 1, sa_ref, m, max_a)

    outs = []
    for h in range(2):
        a = acc_ref[h]
        outs.append(a[0:A_HEAD_DIM] / a[A_HEAD_DIM:A_HEAD_DIM + 1])
    o_ref[0] = jnp.concatenate(outs, axis=0).T


def _moba(qt, k, vt, kmean):
    bsz, seq, _ = k.shape
    nb = seq // MOBA_BLOCK
    assert nb <= MOBA_BLOCKS_PAD
    km = jnp.pad(kmean.reshape(bsz, nb, A_WIDTH), ((0, 0), (0, MOBA_BLOCKS_PAD - nb), (0, 0)))
    npair = A_WIDTH // LANES
    group = 4 if nb % 4 == 0 else 1
    return pl.pallas_call(
        functools.partial(_moba_kernel, group=group, nb=nb),
        grid=(bsz, npair, nb),
        in_specs=[pl.BlockSpec((1, LANES, MOBA_BLOCK), lambda b, hp, i: (b, hp, i)),
                  pl.BlockSpec((1, seq, LANES), lambda b, hp, i: (b, 0, hp)),
                  pl.BlockSpec((1, 2, nb, V_ROWS, MOBA_BLOCK), lambda b, hp, i: (b, hp, 0, 0, 0)),
                  pl.BlockSpec((1, MOBA_BLOCKS_PAD, LANES), lambda b, hp, i: (b, 0, hp))],
        out_specs=pl.BlockSpec((1, MOBA_BLOCK, LANES), lambda b, hp, i: (b, i, hp)),
        out_shape=jax.ShapeDtypeStruct((bsz, seq, A_WIDTH), F32),
        scratch_shapes=[pltpu.VMEM((2 * LANES, 2 * MOBA_BLOCK), BF16),
                        pltpu.VMEM((2, V_ROWS, MOBA_BLOCK), F32),
                        pltpu.VMEM((group * MOBA_BLOCK, 2 * MOBA_BLOCK), F32),
                        pltpu.VMEM((group * MOBA_BLOCK, 2 * MOBA_BLOCK), F32)],
        compiler_params=_params(("arbitrary", "arbitrary", "arbitrary")),
        name="moba",
    )(qt, k, vt, km)


def _gla_kernel(q_ref, k_ref, v_ref, g_ref, og_ref, ng_ref, o_ref, st_ref, *, tm):
    @pl.when(pl.program_id(1) == 0)
    def _():
        st_ref[...] = jnp.zeros_like(st_ref)

    q, k, v, g = q_ref[0], k_ref[0], v_ref[0], g_ref[0]
    kw, vw = B_KEY_WIDTH, B_VAL_WIDTH
    row = lax.broadcasted_iota(jnp.int32, (tm, 1), 0)
    r_sub = row % GLA_SUB
    r_chk = row % GLA_CHUNK

    def seg_cumsum(x, r, length):
        s = 1
        while s < length:
            x = x + jnp.where(r >= s, pltpu.roll(x, s, 0), 0.0)
            s *= 2
        return x

    c_sub = seg_cumsum(g, r_sub, GLA_SUB)
    b = seg_cumsum(g, r_chk, GLA_CHUNK)

    head_kv = (lax.broadcasted_iota(jnp.int32, (kw, vw), 0) // B_HEAD_K
               == lax.broadcasted_iota(jnp.int32, (kw, vw), 1) // B_HEAD_V).astype(BF16)
    head_vk = (lax.broadcasted_iota(jnp.int32, (vw, kw), 0) // B_HEAD_V
               == lax.broadcasted_iota(jnp.int32, (vw, kw), 1) // B_HEAD_K)
    head_vv = (lax.broadcasted_iota(jnp.int32, (vw, vw), 0) // B_HEAD_V
               == lax.broadcasted_iota(jnp.int32, (vw, vw), 1) // B_HEAD_V).astype(BF16)

    o_acc = jnp.zeros((tm, vw), F32)
    for d in range(GLA_SUB):
        if d == 0:
            w = q * k
            vv = v
        else:
            w = jnp.where(r_sub >= d,
                          q * pltpu.roll(k, d, 0) * jnp.exp(c_sub - pltpu.roll(c_sub, d, 0)), 0.0)
            vv = pltpu.roll(v, d, 0)
        o_acc = o_acc + jnp.dot(w.astype(BF16), head_kv, preferred_element_type=F32) * vv

    q_sub = q * jnp.exp(c_sub)
    q_chk = q * jnp.exp(b)

    nsub = GLA_CHUNK // GLA_SUB
    rr = lax.broadcasted_iota(jnp.int32, (GLA_CHUNK, 1), 0)
    lane_k = lax.broadcasted_iota(jnp.int32, (1, (nsub - 1) * kw), 1)
    lane_v = lax.broadcasted_iota(jnp.int32, (1, vw), 1)
    outs = []
    for c in range(tm // GLA_CHUNK):
        lo = c * GLA_CHUNK
        bc = b[lo:lo + GLA_CHUNK]
        kc = k[lo:lo + GLA_CHUNK]
        vc = v[lo:lo + GLA_CHUNK].astype(BF16)
        qs = q_sub[lo:lo + GLA_CHUNK]

        rhs_parts, lhs_parts = [], []
        for i in range(1, nsub):
            b_end = bc[i * GLA_SUB - 1:i * GLA_SUB, :]
            decay = jnp.exp(jnp.minimum(b_end - bc, 0.0))
            rhs_parts.append(jnp.where(rr < i * GLA_SUB, kc * decay, 0.0).astype(BF16))
            lhs_parts.append(jnp.where(rr // GLA_SUB == i, qs, 0.0).astype(BF16))
        rhs = jnp.concatenate(rhs_parts, axis=1)
        lhs_t = jnp.concatenate(lhs_parts, axis=1)
        lhs = jnp.concatenate(
            [jnp.where((lane_k % kw) // B_HEAD_K == h, lhs_t, jnp.zeros_like(lhs_t))
             for h in range(B_HEADS)], axis=0)
        attn = lax.dot_general(lhs, rhs, (((1,), (1,)), ((), ())), preferred_element_type=F32)
        res = jnp.dot(attn.astype(BF16), vc, preferred_element_type=F32)
        o_off = jnp.zeros((GLA_CHUNK, vw), F32)
        for h in range(B_HEADS):
            o_off = o_off + jnp.where(lane_v // B_HEAD_V == h,
                                      res[h * GLA_CHUNK:(h + 1) * GLA_CHUNK], 0.0)

        st = st_ref[...]
        o_int = lax.dot_general(q_chk[lo:lo + GLA_CHUNK].astype(BF16), st.astype(BF16),
                                (((1,), (1,)), ((), ())), preferred_element_type=F32)
        b_last = bc[GLA_CHUNK - 1:GLA_CHUNK, :]
        ks = (kc * jnp.exp(b_last - bc)).astype(BF16)
        upd = lax.dot_general(vc, ks, (((0,), (0,)), ((), ())), preferred_element_type=F32)
        st_ref[...] = st * jnp.exp(b_last) + jnp.where(head_vk, upd, 0.0)
        outs.append(o_off + o_int)

    o = o_acc + jnp.concatenate(outs, axis=0)

    sq = o * o
    sq_hi = sq.astype(BF16)
    sq_lo = (sq - sq_hi.astype(F32)).astype(BF16)
    ms = (jnp.dot(sq_hi, head_vv, preferred_element_type=F32)
          + jnp.dot(sq_lo, head_vv, preferred_element_type=F32)) * (1.0 / B_HEAD_V)
    og = og_ref[0]
    o_ref[0] = (o * lax.rsqrt(ms + NORM_EPS) * ng_ref[...]) * (og * jax.nn.sigmoid(og))


def _gla(bq, bk, bv, gl, og, norm_g):
    bsz, seq, _ = bq.shape
    tm = min(256, seq)
    row = lambda n: pl.BlockSpec((1, tm, n), lambda b, i: (b, i, 0))
    return pl.pallas_call(
        functools.partial(_gla_kernel, tm=tm),
        grid=(bsz, seq // tm),
        in_specs=[row(B_KEY_WIDTH), row(B_KEY_WIDTH), row(B_VAL_WIDTH), row(B_KEY_WIDTH),
                  row(B_VAL_WIDTH), _resident((1, B_VAL_WIDTH))],
        out_specs=row(B_VAL_WIDTH),
        out_shape=jax.ShapeDtypeStruct((bsz, seq, B_VAL_WIDTH), F32),
        scratch_shapes=[pltpu.VMEM((B_VAL_WIDTH, B_KEY_WIDTH), F32)],
        compiler_params=_params(("arbitrary", "arbitrary")),
        name="gla",
    )(bq, bk, bv, gl, og, norm_g.reshape(1, B_VAL_WIDTH))


def _outproj_kernel(x_ref, oa_ref, ob_ref, wa_ref, wb_ref, o_ref):
    o_ref[0] = (x_ref[0]
                + jnp.dot(oa_ref[0].astype(BF16), wa_ref[...], preferred_element_type=F32)
                + jnp.dot(ob_ref[0].astype(BF16), wb_ref[...], preferred_element_type=F32))


def _outproj(x, oa, ob, w_out):
    bsz, seq, d = x.shape
    tm = min(512, seq)
    row = lambda n: pl.BlockSpec((1, tm, n), lambda b, i: (b, i, 0))
    w = w_out.astype(BF16)
    return pl.pallas_call(
        _outproj_kernel,
        grid=(bsz, seq // tm),
        in_specs=[row(d), row(A_WIDTH), row(B_VAL_WIDTH),
                  _resident((A_WIDTH, d)), _resident((B_VAL_WIDTH, d))],
        out_specs=row(d),
        out_shape=jax.ShapeDtypeStruct((bsz, seq, d), F32),
        compiler_params=_params(("arbitrary", "arbitrary")),
        name="outproj",
    )(x, oa, ob, w[:A_WIDTH], w[A_WIDTH:])


def _ffn_kernel(x_ref, g_ref, wup_ref, cw_ref, cb_ref, wdn_ref, fg_ref, o_ref,
                halo_ref, ext_ref, *, tm, bn, dff, final_norm):
    nkb = dff // bn

    @pl.when(pl.program_id(1) == 0)
    def _():
        halo_ref[...] = jnp.zeros_like(halo_ref)

    x = x_ref[0]
    hn = _rms(x, g_ref[...]).astype(BF16)

    def up_stage(kb):
        for part in range(2):
            c0 = part * dff + kb * bn
            up = jnp.dot(hn, wup_ref[:, c0:c0 + bn], preferred_element_type=F32)
            ext_ref[kb % 2, part, 0:SUBLANES, :] = halo_ref[:, c0:c0 + bn]
            ext_ref[kb % 2, part, SUBLANES:, :] = up
            halo_ref[:, c0:c0 + bn] = up[tm - SUBLANES:, :]

    def gate_stage(kb):
        halves = []
        for part in range(2):
            c0 = part * dff + kb * bn
            y = cb_ref[:, c0:c0 + bn]
            for i in range(FFN_CONV):
                off = SUBLANES - (FFN_CONV - 1) + i
                y = y + ext_ref[kb % 2, part, pl.ds(off, tm), :] * cw_ref[i:i + 1, c0:c0 + bn]
            halves.append(y)
        return (halves[0] * _gelu_tanh(halves[1])).astype(BF16)

    up_stage(0)
    acc = x
    for kb in range(nkb):
        if kb + 1 < nkb:
            up_stage(kb + 1)
        acc = acc + jnp.dot(gate_stage(kb), wdn_ref[kb * bn:(kb + 1) * bn, :],
                            preferred_element_type=F32)
    if final_norm:
        acc = _rms(acc, fg_ref[...])
    o_ref[0] = acc


def _conv_ffn(x, g, w_up, conv_w, conv_b, w_down, final_g, final_norm):
    bsz, seq, d = x.shape
    dff = w_down.shape[0]
    tm = min(512, seq)
    bn = 512
    row = pl.BlockSpec((1, tm, d), lambda b, i: (b, i, 0))
    return pl.pallas_call(
        functools.partial(_ffn_kernel, tm=tm, bn=bn, dff=dff, final_norm=final_norm),
        grid=(bsz, seq // tm),
        in_specs=[row, _resident((1, d)), _resident((d, 2 * dff)), _resident((FFN_CONV, 2 * dff)),
                  _resident((1, 2 * dff)), _resident((dff, d)), _resident((1, d))],
        out_specs=row,
        out_shape=jax.ShapeDtypeStruct((bsz, seq, d), F32),
        scratch_shapes=[pltpu.VMEM((SUBLANES, 2 * dff), F32),
                        pltpu.VMEM((2, 2, tm + SUBLANES, bn), F32)],
        compiler_params=_params(("arbitrary", "arbitrary")),
        name="conv_ffn_final" if final_norm else "conv_ffn",
    )(x, g.reshape(1, d), w_up.astype(BF16), conv_w, conv_b.reshape(1, 2 * dff),
      w_down.astype(BF16), final_g.reshape(1, d))


def _rglru_kernel(x_ref, g_ref, win_ref, cw_ref, cb_ref, wa_ref, ba_ref, wx_ref, bx_ref,
                  lam_ref, wout_ref, o_ref, ext_ref, ca_ref, cu_ref, h_ref, hlast_ref, *, tm, drnn):
    @pl.when(pl.program_id(1) == 0)
    def _():
        ext_ref[0:SUBLANES, :] = jnp.zeros((SUBLANES, drnn), F32)
        hlast_ref[...] = jnp.zeros_like(hlast_ref)

    x = x_ref[0]
    hn = _rms(x, g_ref[...]).astype(BF16)
    gate = jnp.dot(hn, win_ref[:, 0:drnn], preferred_element_type=F32)
    xr_raw = jnp.dot(hn, win_ref[:, drnn:2 * drnn], preferred_element_type=F32)

    ext_ref[SUBLANES:, :] = xr_raw
    xr = cb_ref[...]
    for i in range(RG_CONV):
        off = SUBLANES - (RG_CONV - 1) + i
        xr = xr + ext_ref[pl.ds(off, tm), :] * cw_ref[i:i + 1, :]
    ext_ref[0:SUBLANES, :] = xr_raw[tm - SUBLANES:, :]

    bw = drnn // RG_BLOCKS
    xb = xr.astype(BF16)
    ra, ri = [], []
    for n in range(RG_BLOCKS):
        xn = xb[:, n * bw:(n + 1) * bw]
        ra.append(jnp.dot(xn, wa_ref[n], preferred_element_type=F32))
        ri.append(jnp.dot(xn, wx_ref[n], preferred_element_type=F32))
    r = jax.nn.sigmoid(jnp.concatenate(ra, axis=1) + ba_ref[...])
    ig = jax.nn.sigmoid(jnp.concatenate(ri, axis=1) + bx_ref[...])
    log_a = r * lam_ref[...]
    a = jnp.exp(log_a)
    th = jnp.tanh(log_a)
    one_minus_a2 = -2.0 * th / (1.0 - th)
    u = jnp.sqrt(one_minus_a2) * (ig * xr)

    r8 = lax.broadcasted_iota(jnp.int32, (tm, 1), 0) % SUBLANES
    ca, cu = a, u
    s = 1
    while s < SUBLANES:
        keep = r8 >= s
        a_sh = jnp.where(keep, pltpu.roll(ca, s, 0), 1.0)
        u_sh = jnp.where(keep, pltpu.roll(cu, s, 0), 0.0)
        cu = cu + ca * u_sh
        ca = ca * a_sh
        s *= 2
    ca_ref[...] = ca
    cu_ref[...] = cu

    def group(gi, h_prev):
        lo = pl.multiple_of(gi * SUBLANES, SUBLANES)
        hg = cu_ref[pl.ds(lo, SUBLANES), :] + ca_ref[pl.ds(lo, SUBLANES), :] * h_prev
        h_ref[pl.ds(lo, SUBLANES), :] = hg
        return hg[SUBLANES - 1:SUBLANES, :]

    h_last = lax.fori_loop(0, tm // SUBLANES, group, hlast_ref[...])
    hlast_ref[...] = h_last

    y = (h_ref[...] * _gelu_tanh(gate)).astype(BF16)
    o_ref[0] = x + jnp.dot(y, wout_ref[...], preferred_element_type=F32)


def _rglru(x, g, w_in, conv_w, conv_b, w_a, b_a, w_x, b_x, lam, w_out):
    bsz, seq, d = x.shape
    drnn = w_out.shape[0]
    bw = drnn // RG_BLOCKS
    tm = min(256, seq)
    row = pl.BlockSpec((1, tm, d), lambda b, i: (b, i, 0))
    lam_c = (-LRU_C) * jax.nn.softplus(-lam.astype(F32))
    return pl.pallas_call(
        functools.partial(_rglru_kernel, tm=tm, drnn=drnn),
        grid=(bsz, seq // tm),
        in_specs=[row, _resident((1, d)), _resident((d, 2 * drnn)), _resident((RG_CONV, drnn)),
                  _resident((1, drnn)), _resident((RG_BLOCKS, bw, bw)), _resident((1, drnn)),
                  _resident((RG_BLOCKS, bw, bw)), _resident((1, drnn)), _resident((1, drnn)),
                  _resident((drnn, d))],
        out_specs=row,
        out_shape=jax.ShapeDtypeStruct((bsz, seq, d), F32),
        scratch_shapes=[pltpu.VMEM((tm + SUBLANES, drnn), F32),
                        pltpu.VMEM((tm, drnn), F32),
                        pltpu.VMEM((tm, drnn), F32),
                        pltpu.VMEM((tm, drnn), F32),
                        pltpu.VMEM((1, drnn), F32)],
        compiler_params=_params(("arbitrary", "arbitrary")),
        name="rglru",
    )(x, g.reshape(1, d), w_in.astype(BF16), conv_w, conv_b.reshape(1, drnn),
      w_a.astype(BF16), b_a.reshape(1, drnn), w_x.astype(BF16), b_x.reshape(1, drnn),
      lam_c.reshape(1, drnn), w_out.astype(BF16))


def kernel(x, mix_norm_g, ffn_norm_g, final_norm_g, ev_w_in, ev_w_gk2, ev_b_gk2, ev_gla_norm_g,
           ev_w_out, od_w_in, od_conv_w, od_conv_b, od_w_a, od_b_a, od_w_x, od_b_x, od_lambda,
           od_w_out, ffn_w_up, ffn_conv_w, ffn_conv_b, ffn_w_down):
    depth = mix_norm_g.shape[0]
    h = x
    for l in range(depth):
        j = l // 2
        if l % 2 == 0:
            q, k, v, kmean, bq, bk, bv, gl, og = _front0(h, mix_norm_g[l], ev_w_in[j], ev_w_gk2[j],
                                                       ev_b_gk2[j])
            o_a = _moba(q, k, v, kmean)
            o_b = _gla(bq, bk, bv, gl, og, ev_gla_norm_g[j])
            h = _outproj(h, o_a, o_b, ev_w_out[j])
        else:
            h = _rglru(h, mix_norm_g[l], od_w_in[j], od_conv_w[j], od_conv_b[j], od_w_a[j],
                       od_b_a[j], od_w_x[j], od_b_x[j], od_lambda[j], od_w_out[j])
        h = _conv_ffn(h, ffn_norm_g[l], ffn_w_up[l], ffn_conv_w[l], ffn_conv_b[l], ffn_w_down[l],
                      final_norm_g, final_norm=(l == depth - 1))
    return h
```

```python
import functools

import jax
import jax.numpy as jnp
from jax import lax
from jax.experimental import pallas as pl
from jax.experimental.pallas import tpu as pltpu

F32 = jnp.float32
BF16 = jnp.bfloat16

A_HEADS = 8
A_HEAD_DIM = 64
A_WIDTH = A_HEADS * A_HEAD_DIM
MOBA_BLOCK = 256
MOBA_TOPK = 3
ROPE_THETA = 10000.0
B_HEADS = 8
B_HEAD_K = 32
B_HEAD_V = 64
B_KEY_WIDTH = B_HEADS * B_HEAD_K
B_VAL_WIDTH = B_HEADS * B_HEAD_V
GLA_GATE_RANK = 16
GLA_GATE_NORM = 16.0
RG_BLOCKS = 4
RG_CONV = 4
LRU_C = 8.0
FFN_CONV = 3
NORM_EPS = 1e-6
MASK_VALUE = -1e30

LANES = 128
SUBLANES = 8
BF16_ROWS = 16
LOG2_E = 1.4426950408889634
VMEM_LIMIT = 56 * 1024 * 1024

GLA_CHUNK = 64
GLA_SUB = 16
MOBA_BLOCKS_PAD = LANES
V_ROWS = 80


def _params(sem):
    return pltpu.CompilerParams(dimension_semantics=sem, vmem_limit_bytes=VMEM_LIMIT)


def _rms(x, g):
    ms = jnp.mean(x * x, axis=-1, keepdims=True)
    return x * lax.rsqrt(ms + NORM_EPS) * g


def _gelu_tanh(x):
    return 0.5 * x * (1.0 + jnp.tanh(0.7978845608028654 * (x + 0.044715 * (x * x * x))))


def _log_sigmoid(z):
    return jnp.minimum(z, 0.0) - jnp.log1p(jnp.exp(-jnp.abs(z)))


def _resident(shape):
    nd = len(shape)
    return pl.BlockSpec(shape, lambda *_: (0,) * nd, pipeline_mode=pl.Buffered(1))


def _front0_kernel(x_ref, g_ref, w_ref, wt_ref, wgk_ref, bgk_ref, cos_ref, sa_ref, sb_ref,
                   cost_ref, sat_ref, sbt_ref,
                   q_ref, k_ref, v_ref, km_ref, bq_ref, bk_ref, bv_ref, gl_ref, og_ref, *, tm):
    hn = _rms(x_ref[0], g_ref[...]).astype(BF16)

    def proj(c0, n):
        return jnp.dot(hn, w_ref[:, c0:c0 + n], preferred_element_type=F32)

    def proj_t(r0, n):
        return lax.dot_general(wt_ref[r0:r0 + n, :], hn, (((1,), (1,)), ((), ())),
                               preferred_element_type=F32)

    def rot(t, cos, sa, sb, axis):
        parts = []
        for c in range(t.shape[axis] // LANES):
            tc = lax.slice_in_dim(t, c * LANES, (c + 1) * LANES, axis=axis)
            parts.append(tc * cos + pltpu.roll(tc, LANES - 32, axis) * sa
                         + pltpu.roll(tc, 32, axis) * sb)
        return jnp.concatenate(parts, axis=axis)

    q_ref[0] = rot(proj_t(0, A_WIDTH), cost_ref[...], sat_ref[...], sbt_ref[...], 0)
    vt = proj_t(A_WIDTH, A_WIDTH)
    c = 0
    k = rot(proj(c, A_WIDTH), cos_ref[...], sa_ref[...], sb_ref[...], 1); c += A_WIDTH
    k_ref[0] = k.astype(BF16)
    nblk = tm // MOBA_BLOCK
    km_ref[...] = jnp.mean(k.reshape(nblk, MOBA_BLOCK, A_WIDTH), axis=1).reshape(nblk, 1, A_WIDTH)
    ones_rows = (lax.broadcasted_iota(jnp.int32, (V_ROWS - A_HEAD_DIM, MOBA_BLOCK), 0)
                 == 0).astype(BF16)
    for h in range(A_HEADS):
        for n in range(nblk):
            v_ref[0, h, n, 0:A_HEAD_DIM, :] = vt[h * A_HEAD_DIM:(h + 1) * A_HEAD_DIM,
                                                 n * MOBA_BLOCK:(n + 1) * MOBA_BLOCK].astype(BF16)
            v_ref[0, h, n, A_HEAD_DIM:, :] = ones_rows
    bq_ref[0] = proj(c, B_KEY_WIDTH) * (B_HEAD_K ** -0.5); c += B_KEY_WIDTH
    bk_ref[0] = proj(c, B_KEY_WIDTH); c += B_KEY_WIDTH
    bv_ref[0] = proj(c, B_VAL_WIDTH); c += B_VAL_WIDTH
    og_ref[0] = proj(c, B_VAL_WIDTH); c += B_VAL_WIDTH
    gk = proj(c, LANES).astype(BF16)
    z = jnp.dot(gk, wgk_ref[...], preferred_element_type=F32) + bgk_ref[...]
    gl_ref[0] = _log_sigmoid(z) / GLA_GATE_NORM


def _front0(x, g, w_in, w_gk2, b_gk2):
    bsz, seq, d = x.shape
    tm = min(512, seq)
    aq, ak, av, bq, bk, bv, bgk, bog = jnp.split(
        w_in, [512, 1024, 1536, 1792, 2048, 2560, 2576], axis=1)
    gk_pad = LANES - GLA_GATE_RANK
    w = jnp.concatenate([ak, bq, bk, bv, bog, jnp.pad(bgk, ((0, 0), (0, gk_pad)))],
                        axis=1).astype(BF16)
    wt = jnp.concatenate([aq, av], axis=1).T.astype(BF16)
    wgk = jnp.pad(w_gk2, ((0, gk_pad), (0, 0))).astype(BF16)
    ncol = w.shape[1]

    half = A_HEAD_DIM // 2
    lane = jnp.arange(LANES)
    inv = ROPE_THETA ** (-jnp.arange(half, dtype=F32) / half)
    ang = jnp.arange(seq, dtype=jnp.int32).astype(F32)[:, None] * inv[None, :]
    cos_h, sin_h = jnp.cos(ang), jnp.sin(ang)
    first = (lane % A_HEAD_DIM) < half
    cos_t = jnp.tile(cos_h, (1, LANES // half))
    sin_t = jnp.tile(sin_h, (1, LANES // half))
    sa = jnp.where(first[None, :], -sin_t, 0.0)
    sb = jnp.where(first[None, :], 0.0, sin_t)

    nb = seq // MOBA_BLOCK
    row = lambda n: pl.BlockSpec((1, tm, n), lambda b, i: (b, i, 0))
    tab = pl.BlockSpec((tm, LANES), lambda b, i: (i, 0))
    tab_t = pl.BlockSpec((LANES, tm), lambda b, i: (0, i))
    nt = seq // tm
    outs = pl.pallas_call(
        functools.partial(_front0_kernel, tm=tm),
        grid=(bsz, nt),
        in_specs=[row(d), _resident((1, d)), _resident((d, ncol)), _resident((2 * A_WIDTH, d)),
                  _resident((LANES, B_KEY_WIDTH)), _resident((1, B_KEY_WIDTH)),
                  tab, tab, tab, tab_t, tab_t, tab_t],
        out_specs=[pl.BlockSpec((1, A_WIDTH, tm), lambda b, i: (b, 0, i)), row(A_WIDTH),
                   pl.BlockSpec((1, A_HEADS, tm // MOBA_BLOCK, V_ROWS, MOBA_BLOCK),
                                lambda b, i: (b, 0, i, 0, 0)),
                   pl.BlockSpec((tm // MOBA_BLOCK, 1, A_WIDTH),
                                lambda b, i: (b * nt + i, 0, 0)),
                   row(B_KEY_WIDTH), row(B_KEY_WIDTH), row(B_VAL_WIDTH), row(B_KEY_WIDTH),
                   row(B_VAL_WIDTH)],
        out_shape=[jax.ShapeDtypeStruct((bsz, A_WIDTH, seq), F32),
                   jax.ShapeDtypeStruct((bsz, seq, A_WIDTH), BF16),
                   jax.ShapeDtypeStruct((bsz, A_HEADS, nb, V_ROWS, MOBA_BLOCK), BF16),
                   jax.ShapeDtypeStruct((bsz * nb, 1, A_WIDTH), F32),
                   jax.ShapeDtypeStruct((bsz, seq, B_KEY_WIDTH), F32),
                   jax.ShapeDtypeStruct((bsz, seq, B_KEY_WIDTH), F32),
                   jax.ShapeDtypeStruct((bsz, seq, B_VAL_WIDTH), F32),
                   jax.ShapeDtypeStruct((bsz, seq, B_KEY_WIDTH), F32),
                   jax.ShapeDtypeStruct((bsz, seq, B_VAL_WIDTH), F32)],
        compiler_params=_params(("arbitrary", "arbitrary")),
        name="front0",
    )(x, g.reshape(1, d), w, wt, wgk, b_gk2.reshape(1, B_KEY_WIDTH), cos_t, sa, sb,
      cos_t.T, sa.T, sb.T)
    return outs


def _moba_kernel(qt_ref, k_ref, vt_ref, km_ref, o_ref, rhs_ref, acc_ref, sa_ref, sb_ref, *,
                 group, nb):
    qb = pl.program_id(2)
    blk = MOBA_BLOCK
    qt = qt_ref[0]
    km = km_ref[0]
    feat = lax.broadcasted_iota(jnp.int32, (LANES, 1), 0)
    nbr = -(-nb // BF16_ROWS) * BF16_ROWS
    brow = lax.broadcasted_iota(jnp.int32, (nbr, blk), 0)
    browf = brow.astype(F32)
    scale = A_HEAD_DIM ** -0.5 * LOG2_E

    for h in range(2):
        qh = jnp.where(feat // A_HEAD_DIM == h, qt, 0.0)
        gate = jnp.dot(km[0:nbr], qh, precision=lax.Precision.HIGHEST, preferred_element_type=F32)
        g = jnp.where(brow < qb, gate, -jnp.inf)
        bias = jnp.full((nbr, blk), MASK_VALUE, F32)
        for _ in range(MOBA_TOPK):
            mx = jnp.max(g, axis=0, keepdims=True)
            first = jnp.min(jnp.where(g == mx, browf, float(nbr)), axis=0, keepdims=True)
            hit = browf == first
            bias = jnp.where(hit & (mx > -jnp.inf), 0.0, bias)
            g = jnp.where(hit, -jnp.inf, g)
        rhs_ref[:, h * blk:(h + 1) * blk] = jnp.concatenate(
            [(qh * scale).astype(BF16), bias.astype(BF16),
             jnp.full((MOBA_BLOCKS_PAD - nbr, blk), MASK_VALUE, BF16)], axis=0)

    own = pl.multiple_of(qb * blk, blk)
    lhs = jnp.concatenate([k_ref[0, pl.ds(own, blk), :], jnp.zeros((blk, MOBA_BLOCKS_PAD), BF16)],
                          axis=1)
    causal = (lax.broadcasted_iota(jnp.int32, (blk, 2 * blk), 0)
              <= lax.broadcasted_iota(jnp.int32, (blk, 2 * blk), 1) % blk)
    s = jnp.where(causal, jnp.dot(lhs, rhs_ref[...], preferred_element_type=F32), MASK_VALUE)
    m0 = jnp.max(s, axis=0, keepdims=True)
    p = jnp.exp2(s - m0).astype(BF16)
    for h in range(2):
        acc_ref[h] = jnp.dot(vt_ref[0, h, qb], p[:, h * blk:(h + 1) * blk],
                             preferred_element_type=F32)

    keys = group * blk
    blk_of_row = lax.broadcasted_iota(jnp.int32, (keys, MOBA_BLOCKS_PAD), 0) // blk
    lane_blk = lax.broadcasted_iota(jnp.int32, (keys, MOBA_BLOCKS_PAD), 1)

    def scores(g, s_ref):
        j0 = g * group
        off = pl.multiple_of(jnp.minimum(j0, nb - group) * blk, blk)
        onehot = (lane_blk == blk_of_row + j0).astype(BF16)
        lhs_j = jnp.concatenate([k_ref[0, pl.ds(off, keys), :], onehot], axis=1)
        s = jnp.dot(lhs_j, rhs_ref[...], preferred_element_type=F32)
        s_ref[...] = s
        return jnp.max(s, axis=0, keepdims=True)

    def accumulate(g, s_ref, m, s_max):
        j0 = g * group
        m_new = jnp.maximum(m, s_max)
        alpha = jnp.exp2(m - m_new)
        p = jnp.exp2(s_ref[...] - m_new).astype(BF16)
        for h in range(2):
            vt = jnp.concatenate([vt_ref[0, h, jnp.minimum(j0 + u, nb - 1)] for u in range(group)],
                                 axis=1)
            acc_ref[h] = (alpha[:, h * blk:(h + 1) * blk] * acc_ref[h]
                          + jnp.dot(vt, p[:, h * blk:(h + 1) * blk], preferred_element_type=F32))
        return m_new

    def body(i, carry):
        m, max_a = carry
        max_b = scores(2 * i + 1, sb_ref)
        m = accumulate(2 * i, sa_ref, m, max_a)
        max_a = scores(2 * i + 2, sa_ref)
        m = accumulate(2 * i + 1, sb_ref, m, max_b)
        return m, max_a

    n_groups = (qb + group - 1) // group
    m, max_a = lax.fori_loop(0, n_groups // 2, body, (m0, scores(0, sa_ref)))

    @pl.when(n_groups % 2 == 1)
    def _():
        accumulate(n_groups - 1, sa_ref, m, max_a)

    outs = []
    for h in range(2):
        a = acc_ref[h]
        outs.append(a[0:A_HEAD_DIM] / a[A_HEAD_DIM:A_HEAD_DIM + 1])
    o_ref[0] = jnp.concatenate(outs, axis=0).T


def _moba(qt, k, vt, kmean):
    bsz, seq, _ = k.shape
    nb = seq // MOBA_BLOCK
    assert nb <= MOBA_BLOCKS_PAD
    km = jnp.pad(kmean.reshape(bsz, nb, A_WIDTH), ((0, 0), (0, MOBA_BLOCKS_PAD - nb), (0, 0)))
    npair = A_WIDTH // LANES
    group = 4 if nb % 4 == 0 else 1
    return pl.pallas_call(
        functools.partial(_moba_kernel, group=group, nb=nb),
        grid=(bsz, npair, nb),
        in_specs=[pl.BlockSpec((1, LANES, MOBA_BLOCK), lambda b, hp, i: (b, hp, i)),
                  pl.BlockSpec((1, seq, LANES), lambda b, hp, i: (b, 0, hp)),
                  pl.BlockSpec((1, 2, nb, V_ROWS, MOBA_BLOCK), lambda b, hp, i: (b, hp, 0, 0, 0)),
                  pl.BlockSpec((1, MOBA_BLOCKS_PAD, LANES), lambda b, hp, i: (b, 0, hp))],
        out_specs=pl.BlockSpec((1, MOBA_BLOCK, LANES), lambda b, hp, i: (b, i, hp)),
        out_shape=jax.ShapeDtypeStruct((bsz, seq, A_WIDTH), F32),
        scratch_shapes=[pltpu.VMEM((2 * LANES, 2 * MOBA_BLOCK), BF16),
                        pltpu.VMEM((2, V_ROWS, MOBA_BLOCK), F32),
                        pltpu.VMEM((group * MOBA_BLOCK, 2 * MOBA_BLOCK), F32),
                        pltpu.VMEM((group * MOBA_BLOCK, 2 * MOBA_BLOCK), F32)],
        compiler_params=_params(("arbitrary", "arbitrary", "arbitrary")),
        name="moba",
    )(qt, k, vt, km)


def _gla_kernel(q_ref, k_ref, v_ref, g_ref, og_ref, ng_ref, o_ref, st_ref, *, tm):
    @pl.when(pl.program_id(1) == 0)
    def _():
        st_ref[...] = jnp.zeros_like(st_ref)

    q, k, v, g = q_ref[0], k_ref[0], v_ref[0], g_ref[0]
    kw, vw = B_KEY_WIDTH, B_VAL_WIDTH
    row = lax.broadcasted_iota(jnp.int32, (tm, 1), 0)
    r_sub = row % GLA_SUB
    r_chk = row % GLA_CHUNK

    def seg_cumsum(x, r, length):
        s = 1
        while s < length:
            x = x + jnp.where(r >= s, pltpu.roll(x, s, 0), 0.0)
            s *= 2
        return x

    c_sub = seg_cumsum(g, r_sub, GLA_SUB)
    b = seg_cumsum(g, r_chk, GLA_CHUNK)

    head_kv = (lax.broadcasted_iota(jnp.int32, (kw, vw), 0) // B_HEAD_K
               == lax.broadcasted_iota(jnp.int32, (kw, vw), 1) // B_HEAD_V).astype(BF16)
    head_vk = (lax.broadcasted_iota(jnp.int32, (vw, kw), 0) // B_HEAD_V
               == lax.broadcasted_iota(jnp.int32, (vw, kw), 1) // B_HEAD_K)
    head_vv = (lax.broadcasted_iota(jnp.int32, (vw, vw), 0) // B_HEAD_V
               == lax.broadcasted_iota(jnp.int32, (vw, vw), 1) // B_HEAD_V).astype(BF16)

    o_acc = jnp.zeros((tm, vw), F32)
    for d in range(GLA_SUB):
        if d == 0:
            w = q * k
            vv = v
        else:
            w = jnp.where(r_sub >= d,
                          q * pltpu.roll(k, d, 0) * jnp.exp(c_sub - pltpu.roll(c_sub, d, 0)), 0.0)
            vv = pltpu.roll(v, d, 0)
        o_acc = o_acc + jnp.dot(w.astype(BF16), head_kv, preferred_element_type=F32) * vv

    q_sub = q * jnp.exp(c_sub)
    q_chk = q * jnp.exp(b)

    nsub = GLA_CHUNK // GLA_SUB
    rr = lax.broadcasted_iota(jnp.int32, (GLA_CHUNK, 1), 0)
    lane_k = lax.broadcasted_iota(jnp.int32, (1, (nsub - 1) * kw), 1)
    lane_v = lax.broadcasted_iota(jnp.int32, (1, vw), 1)
    outs = []
    for c in range(tm // GLA_CHUNK):
        lo = c * GLA_CHUNK
        bc = b[lo:lo + GLA_CHUNK]
        kc = k[lo:lo + GLA_CHUNK]
        vc = v[lo:lo + GLA_CHUNK].astype(BF16)
        qs = q_sub[lo:lo + GLA_CHUNK]

        rhs_parts, lhs_parts = [], []
        for i in range(1, nsub):
            b_end = bc[i * GLA_SUB - 1:i * GLA_SUB, :]
            decay = jnp.exp(jnp.minimum(b_end - bc, 0.0))
            rhs_parts.append(jnp.where(rr < i * GLA_SUB, kc * decay, 0.0).astype(BF16))
            lhs_parts.append(jnp.where(rr // GLA_SUB == i, qs, 0.0).astype(BF16))
        rhs = jnp.concatenate(rhs_parts, axis=1)
        lhs_t = jnp.concatenate(lhs_parts, axis=1)
        lhs = jnp.concatenate(
            [jnp.where((lane_k % kw) // B_HEAD_K == h, lhs_t, jnp.zeros_like(lhs_t))
             for h in range(B_HEADS)], axis=0)
        attn = lax.dot_general(lhs, rhs, (((1,), (1,)), ((), ())), preferred_element_type=F32)
        res = jnp.dot(attn.astype(BF16), vc, preferred_element_type=F32)
        o_off = jnp.zeros((GLA_CHUNK, vw), F32)
        for h in range(B_HEADS):
            o_off = o_off + jnp.where(lane_v // B_HEAD_V == h,
                                      res[h * GLA_CHUNK:(h + 1) * GLA_CHUNK], 0.0)

        st = st_ref[...]
        o_int = lax.dot_general(q_chk[lo:lo + GLA_CHUNK].astype(BF16), st.astype(BF16),
                                (((1,), (1,)), ((), ())), preferred_element_type=F32)
        b_last = bc[GLA_CHUNK - 1:GLA_CHUNK, :]
        ks = (kc * jnp.exp(b_last - bc)).astype(BF16)
        upd = lax.dot_general(vc, ks, (((0,), (0,)), ((), ())), preferred_element_type=F32)
        st_ref[...] = st * jnp.exp(b_last) + jnp.where(head_vk, upd, 0.0)
        outs.append(o_off + o_int)

    o = o_acc + jnp.concatenate(outs, axis=0)

    sq = o * o
    sq_hi = sq.astype(BF16)
    sq_lo = (sq - sq_hi.astype(F32)).astype(BF16)
    ms = (jnp.dot(sq_hi, head_vv, preferred_element_type=F32)
          + jnp.dot(sq_lo, head_vv, preferred_element_type=F32)) * (1.0 / B_HEAD_V)
    og = og_ref[0]
    o_ref[0] = (o * lax.rsqrt(ms + NORM_EPS) * ng_ref[...]) * (og * jax.nn.sigmoid(og))


def _gla(bq, bk, bv, gl, og, norm_g):
    bsz, seq, _ = bq.shape
    tm = min(256, seq)
    row = lambda n: pl.BlockSpec((1, tm, n), lambda b, i: (b, i, 0))
    return pl.pallas_call(
        functools.partial(_gla_kernel, tm=tm),
        grid=(bsz, seq // tm),
        in_specs=[row(B_KEY_WIDTH), row(B_KEY_WIDTH), row(B_VAL_WIDTH), row(B_KEY_WIDTH),
                  row(B_VAL_WIDTH), _resident((1, B_VAL_WIDTH))],
        out_specs=row(B_VAL_WIDTH),
        out_shape=jax.ShapeDtypeStruct((bsz, seq, B_VAL_WIDTH), F32),
        scratch_shapes=[pltpu.VMEM((B_VAL_WIDTH, B_KEY_WIDTH), F32)],
        compiler_params=_params(("arbitrary", "arbitrary")),
        name="gla",
    )(bq, bk, bv, gl, og, norm_g.reshape(1, B_VAL_WIDTH))


def _outproj_kernel(x_ref, oa_ref, ob_ref, wa_ref, wb_ref, o_ref):
    o_ref[0] = (x_ref[0]
                + jnp.dot(oa_ref[0].astype(BF16), wa_ref[...], preferred_element_type=F32)
                + jnp.dot(ob_ref[0].astype(BF16), wb_ref[...], preferred_element_type=F32))


def _outproj(x, oa, ob, w_out):
    bsz, seq, d = x.shape
    tm = min(512, seq)
    row = lambda n: pl.BlockSpec((1, tm, n), lambda b, i: (b, i, 0))
    w = w_out.astype(BF16)
    return pl.pallas_call(
        _outproj_kernel,
        grid=(bsz, seq // tm),
        in_specs=[row(d), row(A_WIDTH), row(B_VAL_WIDTH),
                  _resident((A_WIDTH, d)), _resident((B_VAL_WIDTH, d))],
        out_specs=row(d),
        out_shape=jax.ShapeDtypeStruct((bsz, seq, d), F32),
        compiler_params=_params(("arbitrary", "arbitrary")),
        name="outproj",
    )(x, oa, ob, w[:A_WIDTH], w[A_WIDTH:])


def _ffn_kernel(x_ref, g_ref, wup_ref, cw_ref, cb_ref, wdn_ref, fg_ref, o_ref,
                halo_ref, ext_ref, *, tm, bn, dff, final_norm):
    nkb = dff // bn

    @pl.when(pl.program_id(1) == 0)
    def _():
        halo_ref[...] = jnp.zeros_like(halo_ref)

    x = x_ref[0]
    hn = _rms(x, g_ref[...]).astype(BF16)

    def up_stage(kb):
        for part in range(2):
            c0 = part * dff + kb * bn
            up = jnp.dot(hn, wup_ref[:, c0:c0 + bn], preferred_element_type=F32)
            ext_ref[kb % 2, part, 0:SUBLANES, :] = halo_ref[:, c0:c0 + bn]
            ext_ref[kb % 2, part, SUBLANES:, :] = up
            halo_ref[:, c0:c0 + bn] = up[tm - SUBLANES:, :]

    def gate_stage(kb):
        halves = []
        for part in range(2):
            c0 = part * dff + kb * bn
            y = cb_ref[:, c0:c0 + bn]
            for i in range(FFN_CONV):
                off = SUBLANES - (FFN_CONV - 1) + i
                y = y + ext_ref[kb % 2, part, pl.ds(off, tm), :] * cw_ref[i:i + 1, c0:c0 + bn]
            halves.append(y)
        return (halves[0] * _gelu_tanh(halves[1])).astype(BF16)

    up_stage(0)
    acc = x
    for kb in range(nkb):
        if kb + 1 < nkb:
            up_stage(kb + 1)
        acc = acc + jnp.dot(gate_stage(kb), wdn_ref[kb * bn:(kb + 1) * bn, :],
                            preferred_element_type=F32)
    if final_norm:
        acc = _rms(acc, fg_ref[...])
    o_ref[0] = acc


def _conv_ffn(x, g, w_up, conv_w, conv_b, w_down, final_g, final_norm):
    bsz, seq, d = x.shape
    dff = w_down.shape[0]
    tm = min(512, seq)
    bn = 512
    row = pl.BlockSpec((1, tm, d), lambda b, i: (b, i, 0))
    return pl.pallas_call(
        functools.partial(_ffn_kernel, tm=tm, bn=bn, dff=dff, final_norm=final_norm),
        grid=(bsz, seq // tm),
        in_specs=[row, _resident((1, d)), _resident((d, 2 * dff)), _resident((FFN_CONV, 2 * dff)),
                  _resident((1, 2 * dff)), _resident((dff, d)), _resident((1, d))],
        out_specs=row,
        out_shape=jax.ShapeDtypeStruct((bsz, seq, d), F32),
        scratch_shapes=[pltpu.VMEM((SUBLANES, 2 * dff), F32),
                        pltpu.VMEM((2, 2, tm + SUBLANES, bn), F32)],
        compiler_params=_params(("arbitrary", "arbitrary")),
        name="conv_ffn_final" if final_norm else "conv_ffn",
    )(x, g.reshape(1, d), w_up.astype(BF16), conv_w, conv_b.reshape(1, 2 * dff),
      w_down.astype(BF16), final_g.reshape(1, d))


def _rglru_kernel(x_ref, g_ref, win_ref, cw_ref, cb_ref, wa_ref, ba_ref, wx_ref, bx_ref,
                  lam_ref, wout_ref, o_ref, ext_ref, halo_ref, hlast_ref, *, tm, drnn):
    seg = tm // SUBLANES
    shifts = RG_CONV - 1

    @pl.when(pl.program_id(1) == 0)
    def _():
        halo_ref[...] = jnp.zeros_like(halo_ref)
        hlast_ref[...] = jnp.zeros_like(hlast_ref)

    x = x_ref[0]
    prow = lax.broadcasted_iota(jnp.int32, (tm, tm), 0)
    pcol = lax.broadcasted_iota(jnp.int32, (tm, tm), 1)
    perm = (pcol == (prow % SUBLANES) * seg + prow // SUBLANES).astype(BF16)
    unperm = (prow == (pcol % SUBLANES) * seg + pcol // SUBLANES).astype(BF16)
    hn_t = _rms(x, g_ref[...]).astype(BF16)
    hn = jnp.dot(perm, hn_t, preferred_element_type=F32).astype(BF16)
    gate = jnp.dot(hn, win_ref[:, 0:drnn], preferred_element_type=F32)
    xr_raw = jnp.dot(hn, win_ref[:, drnn:2 * drnn], preferred_element_type=F32)

    first_sub = lax.broadcasted_iota(jnp.int32, (SUBLANES, 1), 0) == 0
    for r in range(shifts):
        lo = (seg - shifts + r) * SUBLANES
        cur = xr_raw[lo:lo + SUBLANES]
        ext_ref[r * SUBLANES:(r + 1) * SUBLANES, :] = jnp.where(
            first_sub, pltpu.roll(halo_ref[r], 1, 0), pltpu.roll(cur, 1, 0))
        halo_ref[r] = cur
    ext_ref[shifts * SUBLANES:, :] = xr_raw
    xr = cb_ref[...]
    for i in range(RG_CONV):
        xr = xr + ext_ref[i * SUBLANES:i * SUBLANES + tm, :] * cw_ref[i:i + 1, :]

    bw = drnn // RG_BLOCKS
    xb = xr.astype(BF16)
    ra, ri = [], []
    for n in range(RG_BLOCKS):
        xn = xb[:, n * bw:(n + 1) * bw]
        ra.append(jnp.dot(xn, wa_ref[n], preferred_element_type=F32))
        ri.append(jnp.dot(xn, wx_ref[n], preferred_element_type=F32))
    r = jax.nn.sigmoid(jnp.concatenate(ra, axis=1) + ba_ref[...])
    ig = jax.nn.sigmoid(jnp.concatenate(ri, axis=1) + bx_ref[...])
    log_a = r * lam_ref[...]
    a = jnp.exp(log_a)
    th = jnp.tanh(log_a)
    one_minus_a2 = -2.0 * th / (1.0 - th)
    u = jnp.sqrt(one_minus_a2) * (ig * xr)

    h_loc, a_cum = [u[0:SUBLANES]], [a[0:SUBLANES]]
    for j in range(1, seg):
        aj = a[j * SUBLANES:(j + 1) * SUBLANES]
        h_loc.append(aj * h_loc[-1] + u[j * SUBLANES:(j + 1) * SUBLANES])
        a_cum.append(aj * a_cum[-1])
    h_in = [hlast_ref[...]]
    for s in range(SUBLANES):
        h_in.append(h_loc[-1][s:s + 1] + a_cum[-1][s:s + 1] * h_in[-1])
    hlast_ref[...] = h_in[SUBLANES]
    h_seg = jnp.concatenate(h_in[:SUBLANES], axis=0)
    h = jnp.concatenate([h_loc[j] + a_cum[j] * h_seg for j in range(seg)], axis=0)

    y = (h * _gelu_tanh(gate)).astype(BF16)
    y_t = jnp.dot(unperm, y, preferred_element_type=F32).astype(BF16)
    o_ref[0] = x + jnp.dot(y_t, wout_ref[...], preferred_element_type=F32)


def _rglru(x, g, w_in, conv_w, conv_b, w_a, b_a, w_x, b_x, lam, w_out):
    bsz, seq, d = x.shape
    drnn = w_out.shape[0]
    bw = drnn // RG_BLOCKS
    tm = min(256, seq)
    row = pl.BlockSpec((1, tm, d), lambda b, i: (b, i, 0))
    lam_c = (-LRU_C) * jax.nn.softplus(-lam.astype(F32))
    return pl.pallas_call(
        functools.partial(_rglru_kernel, tm=tm, drnn=drnn),
        grid=(bsz, seq // tm),
        in_specs=[row, _resident((1, d)), _resident((d, 2 * drnn)), _resident((RG_CONV, drnn)),
                  _resident((1, drnn)), _resident((RG_BLOCKS, bw, bw)), _resident((1, drnn)),
                  _resident((RG_BLOCKS, bw, bw)), _resident((1, drnn)), _resident((1, drnn)),
                  _resident((drnn, d))],
        out_specs=row,
        out_shape=jax.ShapeDtypeStruct((bsz, seq, d), F32),
        scratch_shapes=[pltpu.VMEM((tm + (RG_CONV - 1) * SUBLANES, drnn), F32),
                        pltpu.VMEM((RG_CONV - 1, SUBLANES, drnn), F32),
                        pltpu.VMEM((1, drnn), F32)],
        compiler_params=_params(("arbitrary", "arbitrary")),
        name="rglru",
    )(x, g.reshape(1, d), w_in.astype(BF16), conv_w, conv_b.reshape(1, drnn),
      w_a.astype(BF16), b_a.reshape(1, drnn), w_x.astype(BF16), b_x.reshape(1, drnn),
      lam_c.reshape(1, drnn), w_out.astype(BF16))


def kernel(x, mix_norm_g, ffn_norm_g, final_norm_g, ev_w_in, ev_w_gk2, ev_b_gk2, ev_gla_norm_g,
           ev_w_out, od_w_in, od_conv_w, od_conv_b, od_w_a, od_b_a, od_w_x, od_b_x, od_lambda,
           od_w_out, ffn_w_up, ffn_conv_w, ffn_conv_b, ffn_w_down):
    depth = mix_norm_g.shape[0]
    h = x
    for l in range(depth):
        j = l // 2
        if l % 2 == 0:
            q, k, v, kmean, bq, bk, bv, gl, og = _front0(h, mix_norm_g[l], ev_w_in[j], ev_w_gk2[j],
                                                       ev_b_gk2[j])
            o_a = _moba(q, k, v, kmean)
            o_b = _gla(bq, bk, bv, gl, og, ev_gla_norm_g[j])
            h = _outproj(h, o_a, o_b, ev_w_out[j])
        else:
            h = _rglru(h, mix_norm_g[l], od_w_in[j], od_conv_w[j], od_conv_b[j], od_w_a[j],
                       od_b_a[j], od_w_x[j], od_b_x[j], od_lambda[j], od_w_out[j])
        h = _conv_ffn(h, ffn_norm_g[l], ffn_w_up[l], ffn_conv_w[l], ffn_conv_b[l], ffn_w_down[l],
                      final_norm_g, final_norm=(l == depth - 1))
    return h
```

```python
import functools

import jax
import jax.numpy as jnp
from jax import lax
from jax.experimental import pallas as pl
from jax.experimental.pallas import tpu as pltpu

F32 = jnp.float32
BF16 = jnp.bfloat16

A_HEADS = 8
A_HEAD_DIM = 64
A_WIDTH = A_HEADS * A_HEAD_DIM
MOBA_BLOCK = 256
MOBA_TOPK = 3
ROPE_THETA = 10000.0
B_HEADS = 8
B_HEAD_K = 32
B_HEAD_V = 64
B_KEY_WIDTH = B_HEADS * B_HEAD_K
B_VAL_WIDTH = B_HEADS * B_HEAD_V
GLA_GATE_RANK = 16
GLA_GATE_NORM = 16.0
RG_BLOCKS = 4
RG_CONV = 4
LRU_C = 8.0
FFN_CONV = 3
NORM_EPS = 1e-6
MASK_VALUE = -1e30

LANES = 128
SUBLANES = 8
BF16_ROWS = 16
LOG2_E = 1.4426950408889634
VMEM_LIMIT = 56 * 1024 * 1024

GLA_CHUNK = 64
GLA_SUB = 16
MOBA_BLOCKS_PAD = LANES
V_ROWS = 80


def _params(sem):
    return pltpu.CompilerParams(dimension_semantics=sem, vmem_limit_bytes=VMEM_LIMIT)


def _rms(x, g):
    ms = jnp.mean(x * x, axis=-1, keepdims=True)
    return x * lax.rsqrt(ms + NORM_EPS) * g


def _gelu_tanh(x):
    return 0.5 * x * (1.0 + jnp.tanh(0.7978845608028654 * (x + 0.044715 * (x * x * x))))


def _log_sigmoid(z):
    return jnp.minimum(z, 0.0) - jnp.log1p(jnp.exp(-jnp.abs(z)))


def _resident(shape):
    nd = len(shape)
    return pl.BlockSpec(shape, lambda *_: (0,) * nd, pipeline_mode=pl.Buffered(1))


def _front0_kernel(x_ref, g_ref, w_ref, wt_ref, wgk_ref, bgk_ref, cos_ref, sa_ref, sb_ref,
                   cost_ref, sat_ref, sbt_ref,
                   q_ref, k_ref, v_ref, km_ref, bq_ref, bk_ref, bv_ref, gl_ref, og_ref, *, tm):
    hn = _rms(x_ref[0], g_ref[...]).astype(BF16)

    def proj(c0, n):
        return jnp.dot(hn, w_ref[:, c0:c0 + n], preferred_element_type=F32)

    def proj_t(r0, n):
        return lax.dot_general(wt_ref[r0:r0 + n, :], hn, (((1,), (1,)), ((), ())),
                               preferred_element_type=F32)

    def rot(t, cos, sa, sb, axis):
        parts = []
        for c in range(t.shape[axis] // LANES):
            tc = lax.slice_in_dim(t, c * LANES, (c + 1) * LANES, axis=axis)
            parts.append(tc * cos + pltpu.roll(tc, LANES - 32, axis) * sa
                         + pltpu.roll(tc, 32, axis) * sb)
        return jnp.concatenate(parts, axis=axis)

    q_ref[0] = rot(proj_t(0, A_WIDTH), cost_ref[...], sat_ref[...], sbt_ref[...], 0)
    vt = proj_t(A_WIDTH, A_WIDTH)
    c = 0
    k = rot(proj(c, A_WIDTH), cos_ref[...], sa_ref[...], sb_ref[...], 1); c += A_WIDTH
    k_ref[0] = k.astype(BF16)
    nblk = tm // MOBA_BLOCK
    km_ref[...] = jnp.mean(k.reshape(nblk, MOBA_BLOCK, A_WIDTH), axis=1).reshape(nblk, 1, A_WIDTH)
    ones_rows = (lax.broadcasted_iota(jnp.int32, (V_ROWS - A_HEAD_DIM, MOBA_BLOCK), 0)
                 == 0).astype(BF16)
    for h in range(A_HEADS):
        for n in range(nblk):
            v_ref[0, h, n, 0:A_HEAD_DIM, :] = vt[h * A_HEAD_DIM:(h + 1) * A_HEAD_DIM,
                                                 n * MOBA_BLOCK:(n + 1) * MOBA_BLOCK].astype(BF16)
            v_ref[0, h, n, A_HEAD_DIM:, :] = ones_rows
    bq_ref[0] = proj(c, B_KEY_WIDTH) * (B_HEAD_K ** -0.5); c += B_KEY_WIDTH
    bk_ref[0] = proj(c, B_KEY_WIDTH); c += B_KEY_WIDTH
    bv_ref[0] = proj(c, B_VAL_WIDTH); c += B_VAL_WIDTH
    og_ref[0] = proj(c, B_VAL_WIDTH); c += B_VAL_WIDTH
    gk = proj(c, LANES).astype(BF16)
    z = jnp.dot(gk, wgk_ref[...], preferred_element_type=F32) + bgk_ref[...]
    gl_ref[0] = _log_sigmoid(z) / GLA_GATE_NORM


def _front0(x, g, w_in, w_gk2, b_gk2):
    bsz, seq, d = x.shape
    tm = min(512, seq)
    aq, ak, av, bq, bk, bv, bgk, bog = jnp.split(
        w_in, [512, 1024, 1536, 1792, 2048, 2560, 2576], axis=1)
    gk_pad = LANES - GLA_GATE_RANK
    w = jnp.concatenate([ak, bq, bk, bv, bog, jnp.pad(bgk, ((0, 0), (0, gk_pad)))],
                        axis=1).astype(BF16)
    wt = jnp.concatenate([aq, av], axis=1).T.astype(BF16)
    wgk = jnp.pad(w_gk2, ((0, gk_pad), (0, 0))).astype(BF16)
    ncol = w.shape[1]

    half = A_HEAD_DIM // 2
    lane = jnp.arange(LANES)
    inv = ROPE_THETA ** (-jnp.arange(half, dtype=F32) / half)
    ang = jnp.arange(seq, dtype=jnp.int32).astype(F32)[:, None] * inv[None, :]
    cos_h, sin_h = jnp.cos(ang), jnp.sin(ang)
    first = (lane % A_HEAD_DIM) < half
    cos_t = jnp.tile(cos_h, (1, LANES // half))
    sin_t = jnp.tile(sin_h, (1, LANES // half))
    sa = jnp.where(first[None, :], -sin_t, 0.0)
    sb = jnp.where(first[None, :], 0.0, sin_t)

    nb = seq // MOBA_BLOCK
    row = lambda n: pl.BlockSpec((1, tm, n), lambda b, i: (b, i, 0))
    tab = pl.BlockSpec((tm, LANES), lambda b, i: (i, 0))
    tab_t = pl.BlockSpec((LANES, tm), lambda b, i: (0, i))
    nt = seq // tm
    outs = pl.pallas_call(
        functools.partial(_front0_kernel, tm=tm),
        grid=(bsz, nt),
        in_specs=[row(d), _resident((1, d)), _resident((d, ncol)), _resident((2 * A_WIDTH, d)),
                  _resident((LANES, B_KEY_WIDTH)), _resident((1, B_KEY_WIDTH)),
                  tab, tab, tab, tab_t, tab_t, tab_t],
        out_specs=[pl.BlockSpec((1, A_WIDTH, tm), lambda b, i: (b, 0, i)), row(A_WIDTH),
                   pl.BlockSpec((1, A_HEADS, tm // MOBA_BLOCK, V_ROWS, MOBA_BLOCK),
                                lambda b, i: (b, 0, i, 0, 0)),
                   pl.BlockSpec((tm // MOBA_BLOCK, 1, A_WIDTH),
                                lambda b, i: (b * nt + i, 0, 0)),
                   row(B_KEY_WIDTH), row(B_KEY_WIDTH), row(B_VAL_WIDTH), row(B_KEY_WIDTH),
                   row(B_VAL_WIDTH)],
        out_shape=[jax.ShapeDtypeStruct((bsz, A_WIDTH, seq), F32),
                   jax.ShapeDtypeStruct((bsz, seq, A_WIDTH), BF16),
                   jax.ShapeDtypeStruct((bsz, A_HEADS, nb, V_ROWS, MOBA_BLOCK), BF16),
                   jax.ShapeDtypeStruct((bsz * nb, 1, A_WIDTH), F32),
                   jax.ShapeDtypeStruct((bsz, seq, B_KEY_WIDTH), F32),
                   jax.ShapeDtypeStruct((bsz, seq, B_KEY_WIDTH), F32),
                   jax.ShapeDtypeStruct((bsz, seq, B_VAL_WIDTH), F32),
                   jax.ShapeDtypeStruct((bsz, seq, B_KEY_WIDTH), F32),
                   jax.ShapeDtypeStruct((bsz, seq, B_VAL_WIDTH), F32)],
        compiler_params=_params(("arbitrary", "arbitrary")),
        name="front0",
    )(x, g.reshape(1, d), w, wt, wgk, b_gk2.reshape(1, B_KEY_WIDTH), cos_t, sa, sb,
      cos_t.T, sa.T, sb.T)
    return outs


def _moba_kernel(qt_ref, k_ref, vt_ref, km_ref, o_ref, rhs_ref, acc_ref, sa_ref, sb_ref, *,
                 group, nb):
    qb = pl.program_id(2)
    blk = MOBA_BLOCK
    qt = qt_ref[0]
    km = km_ref[0]
    feat = lax.broadcasted_iota(jnp.int32, (LANES, 1), 0)
    nbr = -(-nb // BF16_ROWS) * BF16_ROWS
    brow = lax.broadcasted_iota(jnp.int32, (nbr, blk), 0)
    browf = brow.astype(F32)
    scale = A_HEAD_DIM ** -0.5 * LOG2_E

    for h in range(2):
        qh = jnp.where(feat // A_HEAD_DIM == h, qt, 0.0)
        gate = jnp.dot(km[0:nbr], qh, precision=lax.Precision.HIGHEST, preferred_element_type=F32)
        g = jnp.where(brow < qb, gate, -jnp.inf)
        bias = jnp.full((nbr, blk), MASK_VALUE, F32)
        for _ in range(MOBA_TOPK):
            mx = jnp.max(g, axis=0, keepdims=True)
            first = jnp.min(jnp.where(g == mx, browf, float(nbr)), axis=0, keepdims=True)
            hit = browf == first
            bias = jnp.where(hit & (mx > -jnp.inf), 0.0, bias)
            g = jnp.where(hit, -jnp.inf, g)
        rhs_ref[:, h * blk:(h + 1) * blk] = jnp.concatenate(
            [(qh * scale).astype(BF16), bias.astype(BF16),
             jnp.full((MOBA_BLOCKS_PAD - nbr, blk), MASK_VALUE, BF16)], axis=0)

    own = pl.multiple_of(qb * blk, blk)
    lhs = jnp.concatenate([k_ref[0, pl.ds(own, blk), :], jnp.zeros((blk, MOBA_BLOCKS_PAD), BF16)],
                          axis=1)
    causal = (lax.broadcasted_iota(jnp.int32, (blk, 2 * blk), 0)
              <= lax.broadcasted_iota(jnp.int32, (blk, 2 * blk), 1) % blk)
    s = jnp.where(causal, jnp.dot(lhs, rhs_ref[...], preferred_element_type=F32), MASK_VALUE)
    m0 = jnp.max(s, axis=0, keepdims=True)
    p = jnp.exp2(s - m0).astype(BF16)
    for h in range(2):
        acc_ref[h] = jnp.dot(vt_ref[0, h, qb], p[:, h * blk:(h + 1) * blk],
                             preferred_element_type=F32)

    keys = group * blk
    blk_of_row = lax.broadcasted_iota(jnp.int32, (keys, MOBA_BLOCKS_PAD), 0) // blk
    lane_blk = lax.broadcasted_iota(jnp.int32, (keys, MOBA_BLOCKS_PAD), 1)

    def scores(g, s_ref):
        j0 = g * group
        off = pl.multiple_of(jnp.minimum(j0, nb - group) * blk, blk)
        onehot = (lane_blk == blk_of_row + j0).astype(BF16)
        lhs_j = jnp.concatenate([k_ref[0, pl.ds(off, keys), :], onehot], axis=1)
        s = jnp.dot(lhs_j, rhs_ref[...], preferred_element_type=F32)
        s_ref[...] = s
        return jnp.max(s, axis=0, keepdims=True)

    def accumulate(g, s_ref, m, s_max):
        j0 = g * group
        m_new = jnp.maximum(m, s_max)
        alpha = jnp.exp2(m - m_new)
        p = jnp.exp2(s_ref[...] - m_new).astype(BF16)
        for h in range(2):
            vt = jnp.concatenate([vt_ref[0, h, jnp.minimum(j0 + u, nb - 1)] for u in range(group)],
                                 axis=1)
            acc_ref[h] = (alpha[:, h * blk:(h + 1) * blk] * acc_ref[h]
                          + jnp.dot(vt, p[:, h * blk:(h + 1) * blk], preferred_element_type=F32))
        return m_new

    def body(i, carry):
        m, max_a = carry
        max_b = scores(2 * i + 1, sb_ref)
        m = accumulate(2 * i, sa_ref, m, max_a)
        max_a = scores(2 * i + 2, sa_ref)
        m = accumulate(2 * i + 1, sb_ref, m, max_b)
        return m, max_a

    n_groups = (qb + group - 1) // group
    m, max_a = lax.fori_loop(0, n_groups // 2, body, (m0, scores(0, sa_ref)))

    @pl.when(n_groups % 2 == 1)
    def _():
        accumulate(n_groups - 1, sa_ref, m, max_a)

    outs = []
    for h in range(2):
        a = acc_ref[h]
        outs.append(a[0:A_HEAD_DIM] / a[A_HEAD_DIM:A_HEAD_DIM + 1])
    o_ref[0] = jnp.concatenate(outs, axis=0).T


def _moba(qt, k, vt, kmean):
    bsz, seq, _ = k.shape
    nb = seq // MOBA_BLOCK
    assert nb <= MOBA_BLOCKS_PAD
    km = jnp.pad(kmean.reshape(bsz, nb, A_WIDTH), ((0, 0), (0, MOBA_BLOCKS_PAD - nb), (0, 0)))
    npair = A_WIDTH // LANES
    group = 4 if nb % 4 == 0 else 1
    return pl.pallas_call(
        functools.partial(_moba_kernel, group=group, nb=nb),
        grid=(bsz, npair, nb),
        in_specs=[pl.BlockSpec((1, LANES, MOBA_BLOCK), lambda b, hp, i: (b, hp, i)),
                  pl.BlockSpec((1, seq, LANES), lambda b, hp, i: (b, 0, hp)),
                  pl.BlockSpec((1, 2, nb, V_ROWS, MOBA_BLOCK), lambda b, hp, i: (b, hp, 0, 0, 0)),
                  pl.BlockSpec((1, MOBA_BLOCKS_PAD, LANES), lambda b, hp, i: (b, 0, hp))],
        out_specs=pl.BlockSpec((1, MOBA_BLOCK, LANES), lambda b, hp, i: (b, i, hp)),
        out_shape=jax.ShapeDtypeStruct((bsz, seq, A_WIDTH), F32),
        scratch_shapes=[pltpu.VMEM((2 * LANES, 2 * MOBA_BLOCK), BF16),
                        pltpu.VMEM((2, V_ROWS, MOBA_BLOCK), F32),
                        pltpu.VMEM((group * MOBA_BLOCK, 2 * MOBA_BLOCK), F32),
                        pltpu.VMEM((group * MOBA_BLOCK, 2 * MOBA_BLOCK), F32)],
        compiler_params=_params(("arbitrary", "arbitrary", "arbitrary")),
        name="moba",
    )(qt, k, vt, km)


def _gla_kernel(q_ref, k_ref, v_ref, g_ref, og_ref, ng_ref, o_ref, st_ref, *, tm):
    @pl.when(pl.program_id(1) == 0)
    def _():
        st_ref[...] = jnp.zeros_like(st_ref)

    q, k, v, g = q_ref[0], k_ref[0], v_ref[0], g_ref[0]
    kw, vw = B_KEY_WIDTH, B_VAL_WIDTH
    row = lax.broadcasted_iota(jnp.int32, (tm, 1), 0)
    r_sub = row % GLA_SUB
    r_chk = row % GLA_CHUNK

    def seg_cumsum(x, r, length):
        s = 1
        while s < length:
            x = x + jnp.where(r >= s, pltpu.roll(x, s, 0), 0.0)
            s *= 2
        return x

    c_sub = seg_cumsum(g, r_sub, GLA_SUB)
    b = seg_cumsum(g, r_chk, GLA_CHUNK)

    head_kv = (lax.broadcasted_iota(jnp.int32, (kw, vw), 0) // B_HEAD_K
               == lax.broadcasted_iota(jnp.int32, (kw, vw), 1) // B_HEAD_V).astype(BF16)
    head_vk = (lax.broadcasted_iota(jnp.int32, (vw, kw), 0) // B_HEAD_V
               == lax.broadcasted_iota(jnp.int32, (vw, kw), 1) // B_HEAD_K)
    head_vv = (lax.broadcasted_iota(jnp.int32, (vw, vw), 0) // B_HEAD_V
               == lax.broadcasted_iota(jnp.int32, (vw, vw), 1) // B_HEAD_V).astype(BF16)

    o_acc = jnp.zeros((tm, vw), F32)
    for d in range(GLA_SUB):
        if d == 0:
            w = q * k
            vv = v
        else:
            w = jnp.where(r_sub >= d,
                          q * pltpu.roll(k, d, 0) * jnp.exp(c_sub - pltpu.roll(c_sub, d, 0)), 0.0)
            vv = pltpu.roll(v, d, 0)
        o_acc = o_acc + jnp.dot(w.astype(BF16), head_kv, preferred_element_type=F32) * vv

    q_sub = q * jnp.exp(c_sub)
    q_chk = q * jnp.exp(b)

    nsub = GLA_CHUNK // GLA_SUB
    rr = lax.broadcasted_iota(jnp.int32, (GLA_CHUNK, 1), 0)
    lane_k = lax.broadcasted_iota(jnp.int32, (1, (nsub - 1) * kw), 1)
    lane_v = lax.broadcasted_iota(jnp.int32, (1, vw), 1)
    outs = []
    for c in range(tm // GLA_CHUNK):
        lo = c * GLA_CHUNK
        bc = b[lo:lo + GLA_CHUNK]
        kc = k[lo:lo + GLA_CHUNK]
        vc = v[lo:lo + GLA_CHUNK].astype(BF16)
        qs = q_sub[lo:lo + GLA_CHUNK]

        rhs_parts, lhs_parts = [], []
        for i in range(1, nsub):
            b_end = bc[i * GLA_SUB - 1:i * GLA_SUB, :]
            decay = jnp.exp(jnp.minimum(b_end - bc, 0.0))
            rhs_parts.append(jnp.where(rr < i * GLA_SUB, kc * decay, 0.0).astype(BF16))
            lhs_parts.append(jnp.where(rr // GLA_SUB == i, qs, 0.0).astype(BF16))
        rhs = jnp.concatenate(rhs_parts, axis=1)
        lhs_t = jnp.concatenate(lhs_parts, axis=1)
        lhs = jnp.concatenate(
            [jnp.where((lane_k % kw) // B_HEAD_K == h, lhs_t, jnp.zeros_like(lhs_t))
             for h in range(B_HEADS)], axis=0)
        attn = lax.dot_general(lhs, rhs, (((1,), (1,)), ((), ())), preferred_element_type=F32)
        res = jnp.dot(attn.astype(BF16), vc, preferred_element_type=F32)
        o_off = jnp.zeros((GLA_CHUNK, vw), F32)
        for h in range(B_HEADS):
            o_off = o_off + jnp.where(lane_v // B_HEAD_V == h,
                                      res[h * GLA_CHUNK:(h + 1) * GLA_CHUNK], 0.0)

        st = st_ref[...]
        o_int = lax.dot_general(q_chk[lo:lo + GLA_CHUNK].astype(BF16), st.astype(BF16),
                                (((1,), (1,)), ((), ())), preferred_element_type=F32)
        b_last = bc[GLA_CHUNK - 1:GLA_CHUNK, :]
        ks = (kc * jnp.exp(b_last - bc)).astype(BF16)
        upd = lax.dot_general(vc, ks, (((0,), (0,)), ((), ())), preferred_element_type=F32)
        st_ref[...] = st * jnp.exp(b_last) + jnp.where(head_vk, upd, 0.0)
        outs.append(o_off + o_int)

    o = o_acc + jnp.concatenate(outs, axis=0)

    sq = o * o
    sq_hi = sq.astype(BF16)
    sq_lo = (sq - sq_hi.astype(F32)).astype(BF16)
    ms = (jnp.dot(sq_hi, head_vv, preferred_element_type=F32)
          + jnp.dot(sq_lo, head_vv, preferred_element_type=F32)) * (1.0 / B_HEAD_V)
    og = og_ref[0]
    o_ref[0] = (o * lax.rsqrt(ms + NORM_EPS) * ng_ref[...]) * (og * jax.nn.sigmoid(og))


def _gla(bq, bk, bv, gl, og, norm_g):
    bsz, seq, _ = bq.shape
    tm = min(256, seq)
    row = lambda n: pl.BlockSpec((1, tm, n), lambda b, i: (b, i, 0))
    return pl.pallas_call(
        functools.partial(_gla_kernel, tm=tm),
        grid=(bsz, seq // tm),
        in_specs=[row(B_KEY_WIDTH), row(B_KEY_WIDTH), row(B_VAL_WIDTH), row(B_KEY_WIDTH),
                  row(B_VAL_WIDTH), _resident((1, B_VAL_WIDTH))],
        out_specs=row(B_VAL_WIDTH),
        out_shape=jax.ShapeDtypeStruct((bsz, seq, B_VAL_WIDTH), F32),
        scratch_shapes=[pltpu.VMEM((B_VAL_WIDTH, B_KEY_WIDTH), F32)],
        compiler_params=_params(("arbitrary", "arbitrary")),
        name="gla",
    )(bq, bk, bv, gl, og, norm_g.reshape(1, B_VAL_WIDTH))


def _ffn_kernel(*refs, tm, bn, dff, final_norm, mixer):
    if mixer:
        x_ref, oa_ref, ob_ref, wa_ref, wb_ref = refs[:5]
        refs = refs[5:]
    else:
        x_ref, refs = refs[0], refs[1:]
    g_ref, wup_ref, cw_ref, cb_ref, wdn_ref, fg_ref, o_ref, halo_ref, ext_ref = refs
    nkb = dff // bn

    @pl.when(pl.program_id(1) == 0)
    def _():
        halo_ref[...] = jnp.zeros_like(halo_ref)

    x = x_ref[0]
    if mixer:
        o_ref[0] = (jnp.dot(oa_ref[0].astype(BF16), wa_ref[...], preferred_element_type=F32)
                    + jnp.dot(ob_ref[0].astype(BF16), wb_ref[...], preferred_element_type=F32))
        x = x + o_ref[0]
    hn = _rms(x, g_ref[...]).astype(BF16)

    def up_stage(kb):
        for part in range(2):
            c0 = part * dff + kb * bn
            up = jnp.dot(hn, wup_ref[:, c0:c0 + bn], preferred_element_type=F32)
            ext_ref[kb % 2, part, 0:SUBLANES, :] = halo_ref[:, c0:c0 + bn]
            ext_ref[kb % 2, part, SUBLANES:, :] = up
            halo_ref[:, c0:c0 + bn] = up[tm - SUBLANES:, :]

    def gate_stage(kb):
        halves = []
        for part in range(2):
            c0 = part * dff + kb * bn
            y = cb_ref[:, c0:c0 + bn]
            for i in range(FFN_CONV):
                off = SUBLANES - (FFN_CONV - 1) + i
                y = y + ext_ref[kb % 2, part, pl.ds(off, tm), :] * cw_ref[i:i + 1, c0:c0 + bn]
            halves.append(y)
        return (halves[0] * _gelu_tanh(halves[1])).astype(BF16)

    up_stage(0)
    acc = x
    for kb in range(nkb):
        if kb + 1 < nkb:
            up_stage(kb + 1)
        acc = acc + jnp.dot(gate_stage(kb), wdn_ref[kb * bn:(kb + 1) * bn, :],
                            preferred_element_type=F32)
    if final_norm:
        acc = _rms(acc, fg_ref[...])
    o_ref[0] = acc


def _conv_ffn(x, g, w_up, conv_w, conv_b, w_down, final_g, final_norm, mixer=None):
    bsz, seq, d = x.shape
    dff = w_down.shape[0]
    tm = min(512, seq)
    bn = 512
    rows = lambda n: pl.BlockSpec((1, tm, n), lambda b, i: (b, i, 0))
    row = rows(d)
    lead_specs, lead_args = [row], [x]
    if mixer is not None:
        o_a, o_b, w_out = mixer
        w = w_out.astype(BF16)
        lead_specs += [rows(A_WIDTH), rows(B_VAL_WIDTH),
                       _resident((A_WIDTH, d)), _resident((B_VAL_WIDTH, d))]
        lead_args += [o_a, o_b, w[:A_WIDTH], w[A_WIDTH:]]
    return pl.pallas_call(
        functools.partial(_ffn_kernel, tm=tm, bn=bn, dff=dff, final_norm=final_norm,
                          mixer=mixer is not None),
        grid=(bsz, seq // tm),
        in_specs=lead_specs + [_resident((1, d)), _resident((d, 2 * dff)),
                               _resident((FFN_CONV, 2 * dff)), _resident((1, 2 * dff)),
                               _resident((dff, d)), _resident((1, d))],
        out_specs=row,
        out_shape=jax.ShapeDtypeStruct((bsz, seq, d), F32),
        scratch_shapes=[pltpu.VMEM((SUBLANES, 2 * dff), F32),
                        pltpu.VMEM((2, 2, tm + SUBLANES, bn), F32)],
        compiler_params=_params(("arbitrary", "arbitrary")),
        name="conv_ffn_final" if final_norm else "conv_ffn",
    )(*lead_args, g.reshape(1, d), w_up.astype(BF16), conv_w, conv_b.reshape(1, 2 * dff),
      w_down.astype(BF16), final_g.reshape(1, d))


def _rglru_kernel(x_ref, g_ref, win_ref, cw_ref, cb_ref, wa_ref, ba_ref, wx_ref, bx_ref,
                  lam_ref, wout_ref, o_ref, ext_ref, halo_ref, hlast_ref, *, tm, drnn):
    seg = tm // SUBLANES
    shifts = RG_CONV - 1

    @pl.when(pl.program_id(1) == 0)
    def _():
        halo_ref[...] = jnp.zeros_like(halo_ref)
        hlast_ref[...] = jnp.zeros_like(hlast_ref)

    x = x_ref[0]
    prow = lax.broadcasted_iota(jnp.int32, (tm, tm), 0)
    pcol = lax.broadcasted_iota(jnp.int32, (tm, tm), 1)
    perm = (pcol == (prow % SUBLANES) * seg + prow // SUBLANES).astype(BF16)
    unperm = (prow == (pcol % SUBLANES) * seg + pcol // SUBLANES).astype(BF16)
    hn_t = _rms(x, g_ref[...]).astype(BF16)
    hn = jnp.dot(perm, hn_t, preferred_element_type=F32).astype(BF16)
    gate = jnp.dot(hn, win_ref[:, 0:drnn], preferred_element_type=F32)
    xr_raw = jnp.dot(hn, win_ref[:, drnn:2 * drnn], preferred_element_type=F32)

    first_sub = lax.broadcasted_iota(jnp.int32, (SUBLANES, 1), 0) == 0
    for r in range(shifts):
        lo = (seg - shifts + r) * SUBLANES
        cur = xr_raw[lo:lo + SUBLANES]
        ext_ref[r * SUBLANES:(r + 1) * SUBLANES, :] = jnp.where(
            first_sub, pltpu.roll(halo_ref[r], 1, 0), pltpu.roll(cur, 1, 0))
        halo_ref[r] = cur
    ext_ref[shifts * SUBLANES:, :] = xr_raw
    xr = cb_ref[...]
    for i in range(RG_CONV):
        xr = xr + ext_ref[i * SUBLANES:i * SUBLANES + tm, :] * cw_ref[i:i + 1, :]

    bw = drnn // RG_BLOCKS
    xb = xr.astype(BF16)
    ra, ri = [], []
    for n in range(RG_BLOCKS):
        xn = xb[:, n * bw:(n + 1) * bw]
        ra.append(jnp.dot(xn, wa_ref[n], preferred_element_type=F32))
        ri.append(jnp.dot(xn, wx_ref[n], preferred_element_type=F32))
    r = jax.nn.sigmoid(jnp.concatenate(ra, axis=1) + ba_ref[...])
    ig = jax.nn.sigmoid(jnp.concatenate(ri, axis=1) + bx_ref[...])
    log_a = r * lam_ref[...]
    a = jnp.exp(log_a)
    th = jnp.tanh(log_a)
    one_minus_a2 = -2.0 * th / (1.0 - th)
    u = jnp.sqrt(one_minus_a2) * (ig * xr)

    h_loc, a_cum = [u[0:SUBLANES]], [a[0:SUBLANES]]
    for j in range(1, seg):
        aj = a[j * SUBLANES:(j + 1) * SUBLANES]
        h_loc.append(aj * h_loc[-1] + u[j * SUBLANES:(j + 1) * SUBLANES])
        a_cum.append(aj * a_cum[-1])
    h_in = [hlast_ref[...]]
    for s in range(SUBLANES):
        h_in.append(h_loc[-1][s:s + 1] + a_cum[-1][s:s + 1] * h_in[-1])
    hlast_ref[...] = h_in[SUBLANES]
    h_seg = jnp.concatenate(h_in[:SUBLANES], axis=0)
    h = jnp.concatenate([h_loc[j] + a_cum[j] * h_seg for j in range(seg)], axis=0)

    y = (h * _gelu_tanh(gate)).astype(BF16)
    y_t = jnp.dot(unperm, y, preferred_element_type=F32).astype(BF16)
    o_ref[0] = x + jnp.dot(y_t, wout_ref[...], preferred_element_type=F32)


def _rglru(x, g, w_in, conv_w, conv_b, w_a, b_a, w_x, b_x, lam, w_out):
    bsz, seq, d = x.shape
    drnn = w_out.shape[0]
    bw = drnn // RG_BLOCKS
    tm = min(256, seq)
    row = pl.BlockSpec((1, tm, d), lambda b, i: (b, i, 0))
    lam_c = (-LRU_C) * jax.nn.softplus(-lam.astype(F32))
    return pl.pallas_call(
        functools.partial(_rglru_kernel, tm=tm, drnn=drnn),
        grid=(bsz, seq // tm),
        in_specs=[row, _resident((1, d)), _resident((d, 2 * drnn)), _resident((RG_CONV, drnn)),
                  _resident((1, drnn)), _resident((RG_BLOCKS, bw, bw)), _resident((1, drnn)),
                  _resident((RG_BLOCKS, bw, bw)), _resident((1, drnn)), _resident((1, drnn)),
                  _resident((drnn, d))],
        out_specs=row,
        out_shape=jax.ShapeDtypeStruct((bsz, seq, d), F32),
        scratch_shapes=[pltpu.VMEM((tm + (RG_CONV - 1) * SUBLANES, drnn), F32),
                        pltpu.VMEM((RG_CONV - 1, SUBLANES, drnn), F32),
                        pltpu.VMEM((1, drnn), F32)],
        compiler_params=_params(("arbitrary", "arbitrary")),
        name="rglru",
    )(x, g.reshape(1, d), w_in.astype(BF16), conv_w, conv_b.reshape(1, drnn),
      w_a.astype(BF16), b_a.reshape(1, drnn), w_x.astype(BF16), b_x.reshape(1, drnn),
      lam_c.reshape(1, drnn), w_out.astype(BF16))


def kernel(x, mix_norm_g, ffn_norm_g, final_norm_g, ev_w_in, ev_w_gk2, ev_b_gk2, ev_gla_norm_g,
           ev_w_out, od_w_in, od_conv_w, od_conv_b, od_w_a, od_b_a, od_w_x, od_b_x, od_lambda,
           od_w_out, ffn_w_up, ffn_conv_w, ffn_conv_b, ffn_w_down):
    depth = mix_norm_g.shape[0]
    h = x
    for l in range(depth):
        j = l // 2
        mixer = None
        if l % 2 == 0:
            q, k, v, kmean, bq, bk, bv, gl, og = _front0(h, mix_norm_g[l], ev_w_in[j], ev_w_gk2[j],
                                                       ev_b_gk2[j])
            o_a = _moba(q, k, v, kmean)
            o_b = _gla(bq, bk, bv, gl, og, ev_gla_norm_g[j])
            mixer = (o_a, o_b, ev_w_out[j])
        else:
            h = _rglru(h, mix_norm_g[l], od_w_in[j], od_conv_w[j], od_conv_b[j], od_w_a[j],
                       od_b_a[j], od_w_x[j], od_b_x[j], od_lambda[j], od_w_out[j])
        h = _conv_ffn(h, ffn_norm_g[l], ffn_w_up[l], ffn_conv_w[l], ffn_conv_b[l], ffn_w_down[l],
                      final_norm_g, final_norm=(l == depth - 1), mixer=mixer)
    return h
```

```python
import functools

import jax
import jax.numpy as jnp
from jax import lax
from jax.experimental import pallas as pl
from jax.experimental.pallas import tpu as pltpu

F32 = jnp.float32
BF16 = jnp.bfloat16

A_HEADS = 8
A_HEAD_DIM = 64
A_WIDTH = A_HEADS * A_HEAD_DIM
MOBA_BLOCK = 256
MOBA_TOPK = 3
ROPE_THETA = 10000.0
B_HEADS = 8
B_HEAD_K = 32
B_HEAD_V = 64
B_KEY_WIDTH = B_HEADS * B_HEAD_K
B_VAL_WIDTH = B_HEADS * B_HEAD_V
GLA_GATE_RANK = 16
GLA_GATE_NORM = 16.0
RG_BLOCKS = 4
RG_CONV = 4
LRU_C = 8.0
FFN_CONV = 3
NORM_EPS = 1e-6
MASK_VALUE = -1e30

LANES = 128
SUBLANES = 8
BF16_ROWS = 16
LOG2_E = 1.4426950408889634
VMEM_LIMIT = 56 * 1024 * 1024

GLA_CHUNK = 64
GLA_SUB = 16
MOBA_BLOCKS_PAD = LANES
V_ROWS = 80


def _params(sem):
    return pltpu.CompilerParams(dimension_semantics=sem, vmem_limit_bytes=VMEM_LIMIT)


def _rms(x, g):
    ms = jnp.mean(x * x, axis=-1, keepdims=True)
    return x * lax.rsqrt(ms + NORM_EPS) * g


def _gelu_tanh(x):
    return 0.5 * x * (1.0 + jnp.tanh(0.7978845608028654 * (x + 0.044715 * (x * x * x))))


def _log_sigmoid(z):
    return jnp.minimum(z, 0.0) - jnp.log1p(jnp.exp(-jnp.abs(z)))


def _resident(shape):
    nd = len(shape)
    return pl.BlockSpec(shape, lambda *_: (0,) * nd, pipeline_mode=pl.Buffered(1))


def _front0_kernel(x_ref, g_ref, w_ref, wt_ref, wgk_ref, bgk_ref, cos_ref, sa_ref, sb_ref,
                   cost_ref, sat_ref, sbt_ref,
                   q_ref, k_ref, v_ref, km_ref, bq_ref, bk_ref, bv_ref, gl_ref, og_ref, *, tm):
    hn = _rms(x_ref[0], g_ref[...]).astype(BF16)

    def proj(c0, n):
        return jnp.dot(hn, w_ref[:, c0:c0 + n], preferred_element_type=F32)

    def proj_t(r0, n):
        return lax.dot_general(wt_ref[r0:r0 + n, :], hn, (((1,), (1,)), ((), ())),
                               preferred_element_type=F32)

    def rot(t, cos, sa, sb, axis):
        parts = []
        for c in range(t.shape[axis] // LANES):
            tc = lax.slice_in_dim(t, c * LANES, (c + 1) * LANES, axis=axis)
            parts.append(tc * cos + pltpu.roll(tc, LANES - 32, axis) * sa
                         + pltpu.roll(tc, 32, axis) * sb)
        return jnp.concatenate(parts, axis=axis)

    q_ref[0] = rot(proj_t(0, A_WIDTH), cost_ref[...], sat_ref[...], sbt_ref[...], 0)
    vt = proj_t(A_WIDTH, A_WIDTH)
    c = 0
    k = rot(proj(c, A_WIDTH), cos_ref[...], sa_ref[...], sb_ref[...], 1); c += A_WIDTH
    k_ref[0] = k.astype(BF16)
    nblk = tm // MOBA_BLOCK
    km_ref[...] = jnp.mean(k.reshape(nblk, MOBA_BLOCK, A_WIDTH), axis=1).reshape(nblk, 1, A_WIDTH)
    ones_rows = (lax.broadcasted_iota(jnp.int32, (V_ROWS - A_HEAD_DIM, MOBA_BLOCK), 0)
                 == 0).astype(BF16)
    for h in range(A_HEADS):
        for n in range(nblk):
            v_ref[0, h, n, 0:A_HEAD_DIM, :] = vt[h * A_HEAD_DIM:(h + 1) * A_HEAD_DIM,
                                                 n * MOBA_BLOCK:(n + 1) * MOBA_BLOCK].astype(BF16)
            v_ref[0, h, n, A_HEAD_DIM:, :] = ones_rows
    bq_ref[0] = proj(c, B_KEY_WIDTH) * (B_HEAD_K ** -0.5); c += B_KEY_WIDTH
    bk_ref[0] = proj(c, B_KEY_WIDTH); c += B_KEY_WIDTH
    bv_ref[0] = proj(c, B_VAL_WIDTH); c += B_VAL_WIDTH
    og_ref[0] = proj(c, B_VAL_WIDTH); c += B_VAL_WIDTH
    gk = proj(c, LANES).astype(BF16)
    z = jnp.dot(gk, wgk_ref[...], preferred_element_type=F32) + bgk_ref[...]
    gl_ref[0] = _log_sigmoid(z) / GLA_GATE_NORM


def _front0(x, g, w_in, w_gk2, b_gk2):
    bsz, seq, d = x.shape
    tm = min(512, seq)
    aq, ak, av, bq, bk, bv, bgk, bog = jnp.split(
        w_in, [512, 1024, 1536, 1792, 2048, 2560, 2576], axis=1)
    gk_pad = LANES - GLA_GATE_RANK
    w = jnp.concatenate([ak, bq, bk, bv, bog, jnp.pad(bgk, ((0, 0), (0, gk_pad)))],
                        axis=1).astype(BF16)
    wt = jnp.concatenate([aq, av], axis=1).T.astype(BF16)
    wgk = jnp.pad(w_gk2, ((0, gk_pad), (0, 0))).astype(BF16)
    ncol = w.shape[1]

    half = A_HEAD_DIM // 2
    lane = jnp.arange(LANES)
    inv = ROPE_THETA ** (-jnp.arange(half, dtype=F32) / half)
    ang = jnp.arange(seq, dtype=jnp.int32).astype(F32)[:, None] * inv[None, :]
    cos_h, sin_h = jnp.cos(ang), jnp.sin(ang)
    first = (lane % A_HEAD_DIM) < half
    cos_t = jnp.tile(cos_h, (1, LANES // half))
    sin_t = jnp.tile(sin_h, (1, LANES // half))
    sa = jnp.where(first[None, :], -sin_t, 0.0)
    sb = jnp.where(first[None, :], 0.0, sin_t)

    nb = seq // MOBA_BLOCK
    row = lambda n: pl.BlockSpec((1, tm, n), lambda b, i: (b, i, 0))
    tab = pl.BlockSpec((tm, LANES), lambda b, i: (i, 0))
    tab_t = pl.BlockSpec((LANES, tm), lambda b, i: (0, i))
    nt = seq // tm
    outs = pl.pallas_call(
        functools.partial(_front0_kernel, tm=tm),
        grid=(bsz, nt),
        in_specs=[row(d), _resident((1, d)), _resident((d, ncol)), _resident((2 * A_WIDTH, d)),
                  _resident((LANES, B_KEY_WIDTH)), _resident((1, B_KEY_WIDTH)),
                  tab, tab, tab, tab_t, tab_t, tab_t],
        out_specs=[pl.BlockSpec((1, A_WIDTH, tm), lambda b, i: (b, 0, i)), row(A_WIDTH),
                   pl.BlockSpec((1, A_HEADS, tm // MOBA_BLOCK, V_ROWS, MOBA_BLOCK),
                                lambda b, i: (b, 0, i, 0, 0)),
                   pl.BlockSpec((tm // MOBA_BLOCK, 1, A_WIDTH),
                                lambda b, i: (b * nt + i, 0, 0)),
                   row(B_KEY_WIDTH), row(B_KEY_WIDTH), row(B_VAL_WIDTH), row(B_KEY_WIDTH),
                   row(B_VAL_WIDTH)],
        out_shape=[jax.ShapeDtypeStruct((bsz, A_WIDTH, seq), F32),
                   jax.ShapeDtypeStruct((bsz, seq, A_WIDTH), BF16),
                   jax.ShapeDtypeStruct((bsz, A_HEADS, nb, V_ROWS, MOBA_BLOCK), BF16),
                   jax.ShapeDtypeStruct((bsz * nb, 1, A_WIDTH), F32),
                   jax.ShapeDtypeStruct((bsz, seq, B_KEY_WIDTH), F32),
                   jax.ShapeDtypeStruct((bsz, seq, B_KEY_WIDTH), F32),
                   jax.ShapeDtypeStruct((bsz, seq, B_VAL_WIDTH), F32),
                   jax.ShapeDtypeStruct((bsz, seq, B_KEY_WIDTH), F32),
                   jax.ShapeDtypeStruct((bsz, seq, B_VAL_WIDTH), F32)],
        compiler_params=_params(("arbitrary", "arbitrary")),
        name="front0",
    )(x, g.reshape(1, d), w, wt, wgk, b_gk2.reshape(1, B_KEY_WIDTH), cos_t, sa, sb,
      cos_t.T, sa.T, sb.T)
    return outs


def _moba_kernel(qt_ref, k_ref, vt_ref, km_ref, o_ref, rhs_ref, acc_ref, sa_ref, sb_ref, *,
                 group, nb):
    qb = pl.program_id(2)
    blk = MOBA_BLOCK
    qt = qt_ref[0]
    km = km_ref[0]
    feat = lax.broadcasted_iota(jnp.int32, (LANES, 1), 0)
    nbr = -(-nb // BF16_ROWS) * BF16_ROWS
    brow = lax.broadcasted_iota(jnp.int32, (nbr, blk), 0)
    browf = brow.astype(F32)
    scale = A_HEAD_DIM ** -0.5 * LOG2_E

    for h in range(2):
        qh = jnp.where(feat // A_HEAD_DIM == h, qt, 0.0)
        gate = jnp.dot(km[0:nbr], qh, precision=lax.Precision.HIGHEST, preferred_element_type=F32)
        g = jnp.where(brow < qb, gate, -jnp.inf)
        bias = jnp.full((nbr, blk), MASK_VALUE, F32)
        for _ in range(MOBA_TOPK):
            mx = jnp.max(g, axis=0, keepdims=True)
            first = jnp.min(jnp.where(g == mx, browf, float(nbr)), axis=0, keepdims=True)
            hit = browf == first
            bias = jnp.where(hit & (mx > -jnp.inf), 0.0, bias)
            g = jnp.where(hit, -jnp.inf, g)
        rhs_ref[:, h * blk:(h + 1) * blk] = jnp.concatenate(
            [(qh * scale).astype(BF16), bias.astype(BF16),
             jnp.full((MOBA_BLOCKS_PAD - nbr, blk), MASK_VALUE, BF16)], axis=0)

    own = pl.multiple_of(qb * blk, blk)
    lhs = jnp.concatenate([k_ref[0, pl.ds(own, blk), :], jnp.zeros((blk, MOBA_BLOCKS_PAD), BF16)],
                          axis=1)
    causal = (lax.broadcasted_iota(jnp.int32, (blk, 2 * blk), 0)
              <= lax.broadcasted_iota(jnp.int32, (blk, 2 * blk), 1) % blk)
    s = jnp.where(causal, jnp.dot(lhs, rhs_ref[...], preferred_element_type=F32), MASK_VALUE)
    m0 = jnp.max(s, axis=0, keepdims=True)
    p = jnp.exp2(s - m0).astype(BF16)
    for h in range(2):
        acc_ref[h] = jnp.dot(vt_ref[0, h, qb], p[:, h * blk:(h + 1) * blk],
                             preferred_element_type=F32)

    keys = group * blk
    blk_of_row = lax.broadcasted_iota(jnp.int32, (keys, MOBA_BLOCKS_PAD), 0) // blk
    lane_blk = lax.broadcasted_iota(jnp.int32, (keys, MOBA_BLOCKS_PAD), 1)

    def scores(g, s_ref):
        j0 = g * group
        off = pl.multiple_of(jnp.minimum(j0, nb - group) * blk, blk)
        onehot = (lane_blk == blk_of_row + j0).astype(BF16)
        lhs_j = jnp.concatenate([k_ref[0, pl.ds(off, keys), :], onehot], axis=1)
        s = jnp.dot(lhs_j, rhs_ref[...], preferred_element_type=F32)
        s_ref[...] = s
        return jnp.max(s, axis=0, keepdims=True)

    def accumulate(g, s_ref, m, s_max):
        j0 = g * group
        m_new = jnp.maximum(m, s_max)
        alpha = jnp.exp2(m - m_new)
        p = jnp.exp2(s_ref[...] - m_new).astype(BF16)
        for h in range(2):
            vt = jnp.concatenate([vt_ref[0, h, jnp.minimum(j0 + u, nb - 1)] for u in range(group)],
                                 axis=1)
            acc_ref[h] = (alpha[:, h * blk:(h + 1) * blk] * acc_ref[h]
                          + jnp.dot(vt, p[:, h * blk:(h + 1) * blk], preferred_element_type=F32))
        return m_new

    def body(i, carry):
        m, max_a = carry
        max_b = scores(2 * i + 1, sb_ref)
        m = accumulate(2 * i, sa_ref, m, max_a)
        max_a = scores(2 * i + 2, sa_ref)
        m = accumulate(2 * i + 1, sb_ref, m, max_b)
        return m, max_a

    n_groups = (qb + group - 1) // group
    m, max_a = lax.fori_loop(0, n_groups // 2, body, (m0, scores(0, sa_ref)))

    @pl.when(n_groups % 2 == 1)
    def _():
        accumulate(n_groups - 1, sa_ref, m, max_a)

    outs = []
    for h in range(2):
        a = acc_ref[h]
        outs.append(a[0:A_HEAD_DIM] / a[A_HEAD_DIM:A_HEAD_DIM + 1])
    o_ref[0] = jnp.concatenate(outs, axis=0).T


def _moba(qt, k, vt, kmean):
    bsz, seq, _ = k.shape
    nb = seq // MOBA_BLOCK
    assert nb <= MOBA_BLOCKS_PAD
    km = jnp.pad(kmean.reshape(bsz, nb, A_WIDTH), ((0, 0), (0, MOBA_BLOCKS_PAD - nb), (0, 0)))
    npair = A_WIDTH // LANES
    group = 4 if nb % 4 == 0 else 1
    return pl.pallas_call(
        functools.partial(_moba_kernel, group=group, nb=nb),
        grid=(bsz, npair, nb),
        in_specs=[pl.BlockSpec((1, LANES, MOBA_BLOCK), lambda b, hp, i: (b, hp, i)),
                  pl.BlockSpec((1, seq, LANES), lambda b, hp, i: (b, 0, hp)),
                  pl.BlockSpec((1, 2, nb, V_ROWS, MOBA_BLOCK), lambda b, hp, i: (b, hp, 0, 0, 0)),
                  pl.BlockSpec((1, MOBA_BLOCKS_PAD, LANES), lambda b, hp, i: (b, 0, hp))],
        out_specs=pl.BlockSpec((1, MOBA_BLOCK, LANES), lambda b, hp, i: (b, i, hp)),
        out_shape=jax.ShapeDtypeStruct((bsz, seq, A_WIDTH), F32),
        scratch_shapes=[pltpu.VMEM((2 * LANES, 2 * MOBA_BLOCK), BF16),
                        pltpu.VMEM((2, V_ROWS, MOBA_BLOCK), F32),
                        pltpu.VMEM((group * MOBA_BLOCK, 2 * MOBA_BLOCK), F32),
                        pltpu.VMEM((group * MOBA_BLOCK, 2 * MOBA_BLOCK), F32)],
        compiler_params=_params(("arbitrary", "arbitrary", "arbitrary")),
        name="moba",
    )(qt, k, vt, km)


def _gla_kernel(q_ref, k_ref, v_ref, g_ref, og_ref, ng_ref, o_ref, st_ref, *, tm):
    @pl.when(pl.program_id(1) == 0)
    def _():
        st_ref[...] = jnp.zeros_like(st_ref)

    q, k, v, g = q_ref[0], k_ref[0], v_ref[0], g_ref[0]
    kw, vw = B_KEY_WIDTH, B_VAL_WIDTH
    row = lax.broadcasted_iota(jnp.int32, (tm, 1), 0)
    r_sub = row % GLA_SUB
    r_chk = row % GLA_CHUNK

    def seg_cumsum(x, r, length):
        s = 1
        while s < length:
            x = x + jnp.where(r >= s, pltpu.roll(x, s, 0), 0.0)
            s *= 2
        return x

    c_sub = seg_cumsum(g, r_sub, GLA_SUB)
    b = seg_cumsum(g, r_chk, GLA_CHUNK)

    head_kv = (lax.broadcasted_iota(jnp.int32, (kw, vw), 0) // B_HEAD_K
               == lax.broadcasted_iota(jnp.int32, (kw, vw), 1) // B_HEAD_V).astype(BF16)
    head_vk = (lax.broadcasted_iota(jnp.int32, (vw, kw), 0) // B_HEAD_V
               == lax.broadcasted_iota(jnp.int32, (vw, kw), 1) // B_HEAD_K)
    head_vv = (lax.broadcasted_iota(jnp.int32, (vw, vw), 0) // B_HEAD_V
               == lax.broadcasted_iota(jnp.int32, (vw, vw), 1) // B_HEAD_V).astype(BF16)

    o_acc = jnp.zeros((tm, vw), F32)
    for d in range(GLA_SUB):
        if d == 0:
            w = q * k
            vv = v
        else:
            w = jnp.where(r_sub >= d,
                          q * pltpu.roll(k, d, 0) * jnp.exp(c_sub - pltpu.roll(c_sub, d, 0)), 0.0)
            vv = pltpu.roll(v, d, 0)
        o_acc = o_acc + jnp.dot(w.astype(BF16), head_kv, preferred_element_type=F32) * vv

    q_sub = q * jnp.exp(c_sub)
    q_chk = q * jnp.exp(b)

    nsub = GLA_CHUNK // GLA_SUB
    rr = lax.broadcasted_iota(jnp.int32, (GLA_CHUNK, 1), 0)
    lane_k = lax.broadcasted_iota(jnp.int32, (1, (nsub - 1) * kw), 1)
    lane_v = lax.broadcasted_iota(jnp.int32, (1, vw), 1)
    outs = []
    for c in range(tm // GLA_CHUNK):
        lo = c * GLA_CHUNK
        bc = b[lo:lo + GLA_CHUNK]
        kc = k[lo:lo + GLA_CHUNK]
        vc = v[lo:lo + GLA_CHUNK].astype(BF16)
        qs = q_sub[lo:lo + GLA_CHUNK]

        rhs_parts, lhs_parts = [], []
        for i in range(1, nsub):
            b_end = bc[i * GLA_SUB - 1:i * GLA_SUB, :]
            decay = jnp.exp(jnp.minimum(b_end - bc, 0.0))
            rhs_parts.append(jnp.where(rr < i * GLA_SUB, kc * decay, 0.0).astype(BF16))
            lhs_parts.append(jnp.where(rr // GLA_SUB == i, qs, 0.0).astype(BF16))
        rhs = jnp.concatenate(rhs_parts, axis=1)
        lhs_t = jnp.concatenate(lhs_parts, axis=1)
        lhs = jnp.concatenate(
            [jnp.where((lane_k % kw) // B_HEAD_K == h, lhs_t, jnp.zeros_like(lhs_t))
             for h in range(B_HEADS)], axis=0)
        attn = lax.dot_general(lhs, rhs, (((1,), (1,)), ((), ())), preferred_element_type=F32)
        res = jnp.dot(attn.astype(BF16), vc, preferred_element_type=F32)
        o_off = jnp.zeros((GLA_CHUNK, vw), F32)
        for h in range(B_HEADS):
            o_off = o_off + jnp.where(lane_v // B_HEAD_V == h,
                                      res[h * GLA_CHUNK:(h + 1) * GLA_CHUNK], 0.0)

        st = st_ref[...]
        o_int = lax.dot_general(q_chk[lo:lo + GLA_CHUNK].astype(BF16), st.astype(BF16),
                                (((1,), (1,)), ((), ())), preferred_element_type=F32)
        b_last = bc[GLA_CHUNK - 1:GLA_CHUNK, :]
        ks = (kc * jnp.exp(b_last - bc)).astype(BF16)
        upd = lax.dot_general(vc, ks, (((0,), (0,)), ((), ())), preferred_element_type=F32)
        st_ref[...] = st * jnp.exp(b_last) + jnp.where(head_vk, upd, 0.0)
        outs.append(o_off + o_int)

    o = o_acc + jnp.concatenate(outs, axis=0)

    sq = o * o
    sq_hi = sq.astype(BF16)
    sq_lo = (sq - sq_hi.astype(F32)).astype(BF16)
    ms = (jnp.dot(sq_hi, head_vv, preferred_element_type=F32)
          + jnp.dot(sq_lo, head_vv, preferred_element_type=F32)) * (1.0 / B_HEAD_V)
    og = og_ref[0]
    o_ref[0] = (o * lax.rsqrt(ms + NORM_EPS) * ng_ref[...]) * (og * jax.nn.sigmoid(og))


def _gla(bq, bk, bv, gl, og, norm_g):
    bsz, seq, _ = bq.shape
    tm = min(256, seq)
    row = lambda n: pl.BlockSpec((1, tm, n), lambda b, i: (b, i, 0))
    return pl.pallas_call(
        functools.partial(_gla_kernel, tm=tm),
        grid=(bsz, seq // tm),
        in_specs=[row(B_KEY_WIDTH), row(B_KEY_WIDTH), row(B_VAL_WIDTH), row(B_KEY_WIDTH),
                  row(B_VAL_WIDTH), _resident((1, B_VAL_WIDTH))],
        out_specs=row(B_VAL_WIDTH),
        out_shape=jax.ShapeDtypeStruct((bsz, seq, B_VAL_WIDTH), F32),
        scratch_shapes=[pltpu.VMEM((B_VAL_WIDTH, B_KEY_WIDTH), F32)],
        compiler_params=_params(("arbitrary", "arbitrary")),
        name="gla",
    )(bq, bk, bv, gl, og, norm_g.reshape(1, B_VAL_WIDTH))


def _ffn_kernel(*refs, tm, bn, dff, final_norm, mixer):
    if mixer:
        x_ref, oa_ref, ob_ref, wa_ref, wb_ref = refs[:5]
        refs = refs[5:]
    else:
        x_ref, refs = refs[0], refs[1:]
    g_ref, wup_ref, cw_ref, cb_ref, wdn_ref, fg_ref, o_ref, halo_ref, ext_ref = refs
    nkb = dff // bn

    @pl.when(pl.program_id(1) == 0)
    def _():
        halo_ref[...] = jnp.zeros_like(halo_ref)

    x = x_ref[0]
    if mixer:
        o_ref[0] = (jnp.dot(oa_ref[0].astype(BF16), wa_ref[...], preferred_element_type=F32)
                    + jnp.dot(ob_ref[0].astype(BF16), wb_ref[...], preferred_element_type=F32))
        x = x + o_ref[0]
    hn = _rms(x, g_ref[...]).astype(BF16)

    def up_stage(kb):
        for part in range(2):
            c0 = part * dff + kb * bn
            up = jnp.dot(hn, wup_ref[:, c0:c0 + bn], preferred_element_type=F32)
            ext_ref[kb % 2, part, 0:SUBLANES, :] = halo_ref[:, c0:c0 + bn]
            ext_ref[kb % 2, part, SUBLANES:, :] = up
            halo_ref[:, c0:c0 + bn] = up[tm - SUBLANES:, :]

    def gate_stage(kb):
        halves = []
        for part in range(2):
            c0 = part * dff + kb * bn
            y = cb_ref[:, c0:c0 + bn]
            for i in range(FFN_CONV):
                off = SUBLANES - (FFN_CONV - 1) + i
                y = y + ext_ref[kb % 2, part, pl.ds(off, tm), :] * cw_ref[i:i + 1, c0:c0 + bn]
            halves.append(y)
        return (halves[0] * _gelu_tanh(halves[1])).astype(BF16)

    up_stage(0)
    acc = x
    for kb in range(nkb):
        if kb + 1 < nkb:
            up_stage(kb + 1)
        acc = acc + jnp.dot(gate_stage(kb), wdn_ref[kb * bn:(kb + 1) * bn, :],
                            preferred_element_type=F32)
    if final_norm:
        acc = _rms(acc, fg_ref[...])
    o_ref[0] = acc


def _conv_ffn(x, g, w_up, conv_w, conv_b, w_down, final_g, final_norm, mixer=None):
    bsz, seq, d = x.shape
    dff = w_down.shape[0]
    tm = min(512, seq)
    bn = 512
    rows = lambda n: pl.BlockSpec((1, tm, n), lambda b, i: (b, i, 0))
    row = rows(d)
    lead_specs, lead_args = [row], [x]
    if mixer is not None:
        o_a, o_b, w_out = mixer
        w = w_out.astype(BF16)
        lead_specs += [rows(A_WIDTH), rows(B_VAL_WIDTH),
                       _resident((A_WIDTH, d)), _resident((B_VAL_WIDTH, d))]
        lead_args += [o_a, o_b, w[:A_WIDTH], w[A_WIDTH:]]
    return pl.pallas_call(
        functools.partial(_ffn_kernel, tm=tm, bn=bn, dff=dff, final_norm=final_norm,
                          mixer=mixer is not None),
        grid=(bsz, seq // tm),
        in_specs=lead_specs + [_resident((1, d)), _resident((d, 2 * dff)),
                               _resident((FFN_CONV, 2 * dff)), _resident((1, 2 * dff)),
                               _resident((dff, d)), _resident((1, d))],
        out_specs=row,
        out_shape=jax.ShapeDtypeStruct((bsz, seq, d), F32),
        scratch_shapes=[pltpu.VMEM((SUBLANES, 2 * dff), F32),
                        pltpu.VMEM((2, 2, tm + SUBLANES, bn), F32)],
        compiler_params=_params(("arbitrary", "arbitrary")),
        name="conv_ffn_final" if final_norm else "conv_ffn",
    )(*lead_args, g.reshape(1, d), w_up.astype(BF16), conv_w, conv_b.reshape(1, 2 * dff),
      w_down.astype(BF16), final_g.reshape(1, d))


def _rglru_kernel(x_ref, g_ref, win_ref, cw_ref, cb_ref, wa_ref, ba_ref, wx_ref, bx_ref,
                  lam_ref, wout_ref, o_ref, ext_ref, halo_ref, hlast_ref, *, tm, drnn, nseq):
    seg = tm // SUBLANES
    shifts = RG_CONV - 1
    bw = drnn // RG_BLOCKS
    seqs = range(nseq)

    @pl.when(pl.program_id(1) == 0)
    def _():
        halo_ref[...] = jnp.zeros_like(halo_ref)
        hlast_ref[...] = jnp.zeros_like(hlast_ref)

    prow = lax.broadcasted_iota(jnp.int32, (tm, tm), 0)
    pcol = lax.broadcasted_iota(jnp.int32, (tm, tm), 1)
    perm = (pcol == (prow % SUBLANES) * seg + prow // SUBLANES).astype(BF16)
    unperm = (prow == (pcol % SUBLANES) * seg + pcol // SUBLANES).astype(BF16)
    first_sub = lax.broadcasted_iota(jnp.int32, (SUBLANES, 1), 0) == 0

    xs = [x_ref[b] for b in seqs]
    hn = [jnp.dot(perm, _rms(xs[b], g_ref[...]).astype(BF16),
                  preferred_element_type=F32).astype(BF16) for b in seqs]
    gate = [jnp.dot(hn[b], win_ref[:, 0:drnn], preferred_element_type=F32) for b in seqs]
    xr_raw = [jnp.dot(hn[b], win_ref[:, drnn:2 * drnn], preferred_element_type=F32) for b in seqs]

    xr = []
    for b in seqs:
        for r in range(shifts):
            lo = (seg - shifts + r) * SUBLANES
            cur = xr_raw[b][lo:lo + SUBLANES]
            ext_ref[b, r * SUBLANES:(r + 1) * SUBLANES, :] = jnp.where(
                first_sub, pltpu.roll(halo_ref[b, r], 1, 0), pltpu.roll(cur, 1, 0))
            halo_ref[b, r] = cur
        ext_ref[b, shifts * SUBLANES:, :] = xr_raw[b]
        acc = cb_ref[...]
        for i in range(RG_CONV):
            acc = acc + ext_ref[b, i * SUBLANES:i * SUBLANES + tm, :] * cw_ref[i:i + 1, :]
        xr.append(acc)

    ys = []
    for b in seqs:
        xb = xr[b].astype(BF16)
        ra, ri = [], []
        for n in range(RG_BLOCKS):
            xn = xb[:, n * bw:(n + 1) * bw]
            ra.append(jnp.dot(xn, wa_ref[n], preferred_element_type=F32))
            ri.append(jnp.dot(xn, wx_ref[n], preferred_element_type=F32))
        r = jax.nn.sigmoid(jnp.concatenate(ra, axis=1) + ba_ref[...])
        ig = jax.nn.sigmoid(jnp.concatenate(ri, axis=1) + bx_ref[...])
        log_a = r * lam_ref[...]
        a = jnp.exp(log_a)
        th = jnp.tanh(log_a)
        one_minus_a2 = -2.0 * th / (1.0 - th)
        u = jnp.sqrt(one_minus_a2) * (ig * xr[b])

        h_loc, a_cum = [u[0:SUBLANES]], [a[0:SUBLANES]]
        for j in range(1, seg):
            aj = a[j * SUBLANES:(j + 1) * SUBLANES]
            h_loc.append(aj * h_loc[-1] + u[j * SUBLANES:(j + 1) * SUBLANES])
            a_cum.append(aj * a_cum[-1])
        h_in = [hlast_ref[b]]
        for s in range(SUBLANES):
            h_in.append(h_loc[-1][s:s + 1] + a_cum[-1][s:s + 1] * h_in[-1])
        hlast_ref[b] = h_in[SUBLANES]
        h_seg = jnp.concatenate(h_in[:SUBLANES], axis=0)
        h = jnp.concatenate([h_loc[j] + a_cum[j] * h_seg for j in range(seg)], axis=0)
        ys.append((h * _gelu_tanh(gate[b])).astype(BF16))

    for b in seqs:
        y_t = jnp.dot(unperm, ys[b], preferred_element_type=F32).astype(BF16)
        o_ref[b] = xs[b] + jnp.dot(y_t, wout_ref[...], preferred_element_type=F32)


def _rglru(x, g, w_in, conv_w, conv_b, w_a, b_a, w_x, b_x, lam, w_out):
    bsz, seq, d = x.shape
    drnn = w_out.shape[0]
    bw = drnn // RG_BLOCKS
    tm = min(256, seq)
    nseq = 2 if bsz % 2 == 0 else 1
    row = pl.BlockSpec((nseq, tm, d), lambda b, i: (b, i, 0))
    lam_c = (-LRU_C) * jax.nn.softplus(-lam.astype(F32))
    return pl.pallas_call(
        functools.partial(_rglru_kernel, tm=tm, drnn=drnn, nseq=nseq),
        grid=(bsz // nseq, seq // tm),
        in_specs=[row, _resident((1, d)), _resident((d, 2 * drnn)), _resident((RG_CONV, drnn)),
                  _resident((1, drnn)), _resident((RG_BLOCKS, bw, bw)), _resident((1, drnn)),
                  _resident((RG_BLOCKS, bw, bw)), _resident((1, drnn)), _resident((1, drnn)),
                  _resident((drnn, d))],
        out_specs=row,
        out_shape=jax.ShapeDtypeStruct((bsz, seq, d), F32),
        scratch_shapes=[pltpu.VMEM((nseq, tm + (RG_CONV - 1) * SUBLANES, drnn), F32),
                        pltpu.VMEM((nseq, RG_CONV - 1, SUBLANES, drnn), F32),
                        pltpu.VMEM((nseq, 1, drnn), F32)],
        compiler_params=_params(("arbitrary", "arbitrary")),
        name="rglru",
    )(x, g.reshape(1, d), w_in.astype(BF16), conv_w, conv_b.reshape(1, drnn),
      w_a.astype(BF16), b_a.reshape(1, drnn), w_x.astype(BF16), b_x.reshape(1, drnn),
      lam_c.reshape(1, drnn), w_out.astype(BF16))


def kernel(x, mix_norm_g, ffn_norm_g, final_norm_g, ev_w_in, ev_w_gk2, ev_b_gk2, ev_gla_norm_g,
           ev_w_out, od_w_in, od_conv_w, od_conv_b, od_w_a, od_b_a, od_w_x, od_b_x, od_lambda,
           od_w_out, ffn_w_up, ffn_conv_w, ffn_conv_b, ffn_w_down):
    depth = mix_norm_g.shape[0]
    h = x
    for l in range(depth):
        j = l // 2
        mixer = None
        if l % 2 == 0:
            q, k, v, kmean, bq, bk, bv, gl, og = _front0(h, mix_norm_g[l], ev_w_in[j], ev_w_gk2[j],
                                                       ev_b_gk2[j])
            o_a = _moba(q, k, v, kmean)
            o_b = _gla(bq, bk, bv, gl, og, ev_gla_norm_g[j])
            mixer = (o_a, o_b, ev_w_out[j])
        else:
            h = _rglru(h, mix_norm_g[l], od_w_in[j], od_conv_w[j], od_conv_b[j], od_w_a[j],
                       od_b_a[j], od_w_x[j], od_b_x[j], od_lambda[j], od_w_out[j])
        h = _conv_ffn(h, ffn_norm_g[l], ffn_w_up[l], ffn_conv_w[l], ffn_conv_b[l], ffn_w_down[l],
                      final_norm_g, final_norm=(l == depth - 1), mixer=mixer)
    return h
```

```python
import functools

import jax
import jax.numpy as jnp
from jax import lax
from jax.experimental import pallas as pl
from jax.experimental.pallas import tpu as pltpu

F32 = jnp.float32
BF16 = jnp.bfloat16

A_HEADS = 8
A_HEAD_DIM = 64
A_WIDTH = A_HEADS * A_HEAD_DIM
MOBA_BLOCK = 256
MOBA_TOPK = 3
ROPE_THETA = 10000.0
B_HEADS = 8
B_HEAD_K = 32
B_HEAD_V = 64
B_KEY_WIDTH = B_HEADS * B_HEAD_K
B_VAL_WIDTH = B_HEADS * B_HEAD_V
GLA_GATE_RANK = 16
GLA_GATE_NORM = 16.0
RG_BLOCKS = 4
RG_CONV = 4
LRU_C = 8.0
FFN_CONV = 3
NORM_EPS = 1e-6
MASK_VALUE = -1e30

LANES = 128
SUBLANES = 8
BF16_ROWS = 16
LOG2_E = 1.4426950408889634
VMEM_LIMIT = 56 * 1024 * 1024

GLA_CHUNK = 64
GLA_SUB = 16
MOBA_BLOCKS_PAD = LANES
V_ROWS = 80


def _params(sem):
    return pltpu.CompilerParams(dimension_semantics=sem, vmem_limit_bytes=VMEM_LIMIT)


def _rms(x, g):
    ms = jnp.mean(x * x, axis=-1, keepdims=True)
    return x * lax.rsqrt(ms + NORM_EPS) * g


def _gelu_tanh(x):
    return 0.5 * x * (1.0 + jnp.tanh(0.7978845608028654 * (x + 0.044715 * (x * x * x))))


def _log_sigmoid(z):
    return jnp.minimum(z, 0.0) - jnp.log1p(jnp.exp(-jnp.abs(z)))


def _resident(shape):
    nd = len(shape)
    return pl.BlockSpec(shape, lambda *_: (0,) * nd, pipeline_mode=pl.Buffered(1))


def _front0_kernel(x_ref, g_ref, w_ref, wt_ref, wgk_ref, bgk_ref, cos_ref, sa_ref, sb_ref,
                   cost_ref, sat_ref, sbt_ref,
                   q_ref, k_ref, v_ref, km_ref, bq_ref, bk_ref, bv_ref, gl_ref, og_ref, *, tm):
    hn = _rms(x_ref[0], g_ref[...]).astype(BF16)

    def proj(c0, n):
        return jnp.dot(hn, w_ref[:, c0:c0 + n], preferred_element_type=F32)

    def proj_t(r0, n):
        return lax.dot_general(wt_ref[r0:r0 + n, :], hn, (((1,), (1,)), ((), ())),
                               preferred_element_type=F32)

    def rot(t, cos, sa, sb, axis):
        parts = []
        for c in range(t.shape[axis] // LANES):
            tc = lax.slice_in_dim(t, c * LANES, (c + 1) * LANES, axis=axis)
            parts.append(tc * cos + pltpu.roll(tc, LANES - 32, axis) * sa
                         + pltpu.roll(tc, 32, axis) * sb)
        return jnp.concatenate(parts, axis=axis)

    q_ref[0] = rot(proj_t(0, A_WIDTH), cost_ref[...], sat_ref[...], sbt_ref[...], 0)
    vt = proj_t(A_WIDTH, A_WIDTH)
    c = 0
    k = rot(proj(c, A_WIDTH), cos_ref[...], sa_ref[...], sb_ref[...], 1); c += A_WIDTH
    k_ref[0] = k.astype(BF16)
    nblk = tm // MOBA_BLOCK
    km_ref[...] = jnp.mean(k.reshape(nblk, MOBA_BLOCK, A_WIDTH), axis=1).reshape(nblk, 1, A_WIDTH)
    ones_rows = (lax.broadcasted_iota(jnp.int32, (V_ROWS - A_HEAD_DIM, MOBA_BLOCK), 0)
                 == 0).astype(BF16)
    for h in range(A_HEADS):
        for n in range(nblk):
            v_ref[0, h, n, 0:A_HEAD_DIM, :] = vt[h * A_HEAD_DIM:(h + 1) * A_HEAD_DIM,
                                                 n * MOBA_BLOCK:(n + 1) * MOBA_BLOCK].astype(BF16)
            v_ref[0, h, n, A_HEAD_DIM:, :] = ones_rows
    bq_ref[0] = proj(c, B_KEY_WIDTH) * (B_HEAD_K ** -0.5); c += B_KEY_WIDTH
    bk_ref[0] = proj(c, B_KEY_WIDTH); c += B_KEY_WIDTH
    bv_ref[0] = proj(c, B_VAL_WIDTH); c += B_VAL_WIDTH
    og_ref[0] = proj(c, B_VAL_WIDTH); c += B_VAL_WIDTH
    gk = proj(c, LANES).astype(BF16)
    z = jnp.dot(gk, wgk_ref[...], preferred_element_type=F32) + bgk_ref[...]
    gl_ref[0] = _log_sigmoid(z) / GLA_GATE_NORM


def _front0(x, g, w_in, w_gk2, b_gk2):
    bsz, seq, d = x.shape
    tm = min(512, seq)
    aq, ak, av, bq, bk, bv, bgk, bog = jnp.split(
        w_in, [512, 1024, 1536, 1792, 2048, 2560, 2576], axis=1)
    gk_pad = LANES - GLA_GATE_RANK
    w = jnp.concatenate([ak, bq, bk, bv, bog, jnp.pad(bgk, ((0, 0), (0, gk_pad)))],
                        axis=1).astype(BF16)
    wt = jnp.concatenate([aq, av], axis=1).T.astype(BF16)
    wgk = jnp.pad(w_gk2, ((0, gk_pad), (0, 0))).astype(BF16)
    ncol = w.shape[1]

    half = A_HEAD_DIM // 2
    lane = jnp.arange(LANES)
    inv = ROPE_THETA ** (-jnp.arange(half, dtype=F32) / half)
    ang = jnp.arange(seq, dtype=jnp.int32).astype(F32)[:, None] * inv[None, :]
    cos_h, sin_h = jnp.cos(ang), jnp.sin(ang)
    first = (lane % A_HEAD_DIM) < half
    cos_t = jnp.tile(cos_h, (1, LANES // half))
    sin_t = jnp.tile(sin_h, (1, LANES // half))
    sa = jnp.where(first[None, :], -sin_t, 0.0)
    sb = jnp.where(first[None, :], 0.0, sin_t)

    nb = seq // MOBA_BLOCK
    row = lambda n: pl.BlockSpec((1, tm, n), lambda b, i: (b, i, 0))
    tab = pl.BlockSpec((tm, LANES), lambda b, i: (i, 0))
    tab_t = pl.BlockSpec((LANES, tm), lambda b, i: (0, i))
    nt = seq // tm
    outs = pl.pallas_call(
        functools.partial(_front0_kernel, tm=tm),
        grid=(bsz, nt),
        in_specs=[row(d), _resident((1, d)), _resident((d, ncol)), _resident((2 * A_WIDTH, d)),
                  _resident((LANES, B_KEY_WIDTH)), _resident((1, B_KEY_WIDTH)),
                  tab, tab, tab, tab_t, tab_t, tab_t],
        out_specs=[pl.BlockSpec((1, A_WIDTH, tm), lambda b, i: (b, 0, i)), row(A_WIDTH),
                   pl.BlockSpec((1, A_HEADS, tm // MOBA_BLOCK, V_ROWS, MOBA_BLOCK),
                                lambda b, i: (b, 0, i, 0, 0)),
                   pl.BlockSpec((tm // MOBA_BLOCK, 1, A_WIDTH),
                                lambda b, i: (b * nt + i, 0, 0)),
                   row(B_KEY_WIDTH), row(B_KEY_WIDTH), row(B_VAL_WIDTH), row(B_KEY_WIDTH),
                   row(B_VAL_WIDTH)],
        out_shape=[jax.ShapeDtypeStruct((bsz, A_WIDTH, seq), F32),
                   jax.ShapeDtypeStruct((bsz, seq, A_WIDTH), BF16),
                   jax.ShapeDtypeStruct((bsz, A_HEADS, nb, V_ROWS, MOBA_BLOCK), BF16),
                   jax.ShapeDtypeStruct((bsz * nb, 1, A_WIDTH), F32),
                   jax.ShapeDtypeStruct((bsz, seq, B_KEY_WIDTH), F32),
                   jax.ShapeDtypeStruct((bsz, seq, B_KEY_WIDTH), F32),
                   jax.ShapeDtypeStruct((bsz, seq, B_VAL_WIDTH), F32),
                   jax.ShapeDtypeStruct((bsz, seq, B_KEY_WIDTH), F32),
                   jax.ShapeDtypeStruct((bsz, seq, B_VAL_WIDTH), F32)],
        compiler_params=_params(("arbitrary", "arbitrary")),
        name="front0",
    )(x, g.reshape(1, d), w, wt, wgk, b_gk2.reshape(1, B_KEY_WIDTH), cos_t, sa, sb,
      cos_t.T, sa.T, sb.T)
    return outs


def _moba_kernel(qt_ref, k_ref, vt_ref, km_ref, o_ref, rhs_ref, acc_ref, sa_ref, sb_ref, *,
                 group, nb):
    qb = pl.program_id(2)
    blk = MOBA_BLOCK
    qt = qt_ref[0]
    km = km_ref[0]
    feat = lax.broadcasted_iota(jnp.int32, (LANES, 1), 0)
    nbr = -(-nb // BF16_ROWS) * BF16_ROWS
    brow = lax.broadcasted_iota(jnp.int32, (nbr, 2 * blk), 0)
    browf = brow.astype(F32)
    scale = A_HEAD_DIM ** -0.5 * LOG2_E

    q2 = jnp.concatenate([jnp.where(feat // A_HEAD_DIM == h, qt, 0.0) for h in range(2)], axis=1)

    def split(a):
        hi = a.astype(BF16)
        return hi, (a - hi.astype(F32)).astype(BF16)
    km_hi, km_lo = split(km[0:nbr])
    q_hi, q_lo = split(q2)
    gate = jnp.dot(jnp.concatenate([km_hi, km_hi, km_lo], axis=1),
                   jnp.concatenate([q_hi, q_lo, q_hi], axis=0), preferred_element_type=F32)
    g = jnp.where(brow < qb, gate, -jnp.inf)
    bias = jnp.full((nbr, 2 * blk), MASK_VALUE, F32)
    for _ in range(MOBA_TOPK):
        mx = jnp.max(g, axis=0, keepdims=True)
        first = jnp.min(jnp.where(g == mx, browf, float(nbr)), axis=0, keepdims=True)
        hit = browf == first
        bias = jnp.where(hit & (mx > -jnp.inf), 0.0, bias)
        g = jnp.where(hit, -jnp.inf, g)
    rhs_ref[...] = jnp.concatenate(
        [(q2 * scale).astype(BF16), bias.astype(BF16),
         jnp.full((MOBA_BLOCKS_PAD - nbr, 2 * blk), MASK_VALUE, BF16)], axis=0)

    keys = group * blk
    blk_of_row = lax.broadcasted_iota(jnp.int32, (keys, MOBA_BLOCKS_PAD), 0) // blk
    lane_blk = lax.broadcasted_iota(jnp.int32, (keys, MOBA_BLOCKS_PAD), 1)

    def scores(g, s_ref):
        j0 = g * group
        off = pl.multiple_of(jnp.minimum(j0, nb - group) * blk, blk)
        onehot = (lane_blk == blk_of_row + j0).astype(BF16)
        lhs_j = jnp.concatenate([k_ref[0, pl.ds(off, keys), :], onehot], axis=1)
        s = jnp.dot(lhs_j, rhs_ref[...], preferred_element_type=F32)
        s_ref[...] = s
        return jnp.max(s, axis=0, keepdims=True)

    def accumulate(g, s_ref, m, s_max):
        j0 = g * group
        m_new = jnp.maximum(m, s_max)
        alpha = jnp.exp2(m - m_new)
        p = jnp.exp2(s_ref[...] - m_new).astype(BF16)
        for h in range(2):
            vt = jnp.concatenate([vt_ref[0, h, jnp.minimum(j0 + u, nb - 1)] for u in range(group)],
                                 axis=1)
            acc_ref[h] = (alpha[:, h * blk:(h + 1) * blk] * acc_ref[h]
                          + jnp.dot(vt, p[:, h * blk:(h + 1) * blk], preferred_element_type=F32))
        return m_new

    def body(i, carry):
        m, max_a = carry
        max_b = scores(2 * i + 1, sb_ref)
        m = accumulate(2 * i, sa_ref, m, max_a)
        max_a = scores(2 * i + 2, sa_ref)
        m = accumulate(2 * i + 1, sb_ref, m, max_b)
        return m, max_a

    max_0 = scores(0, sa_ref)

    own = pl.multiple_of(qb * blk, blk)
    lhs = jnp.concatenate([k_ref[0, pl.ds(own, blk), :], jnp.zeros((blk, MOBA_BLOCKS_PAD), BF16)],
                          axis=1)
    causal = (lax.broadcasted_iota(jnp.int32, (blk, 2 * blk), 0)
              <= lax.broadcasted_iota(jnp.int32, (blk, 2 * blk), 1) % blk)
    s = jnp.where(causal, jnp.dot(lhs, rhs_ref[...], preferred_element_type=F32), MASK_VALUE)
    m0 = jnp.max(s, axis=0, keepdims=True)
    p = jnp.exp2(s - m0).astype(BF16)
    for h in range(2):
        acc_ref[h] = jnp.dot(vt_ref[0, h, qb], p[:, h * blk:(h + 1) * blk],
                             preferred_element_type=F32)

    n_groups = (qb + group - 1) // group
    m, max_a = lax.fori_loop(0, n_groups // 2, body, (m0, max_0))

    @pl.when(n_groups % 2 == 1)
    def _():
        accumulate(n_groups - 1, sa_ref, m, max_a)

    outs = []
    for h in range(2):
        a = acc_ref[h]
        outs.append(a[0:A_HEAD_DIM] / a[A_HEAD_DIM:A_HEAD_DIM + 1])
    o_ref[0] = jnp.concatenate(outs, axis=0).T


def _moba(qt, k, vt, kmean):
    bsz, seq, _ = k.shape
    nb = seq // MOBA_BLOCK
    assert nb <= MOBA_BLOCKS_PAD
    km = jnp.pad(kmean.reshape(bsz, nb, A_WIDTH), ((0, 0), (0, MOBA_BLOCKS_PAD - nb), (0, 0)))
    npair = A_WIDTH // LANES
    group = 4 if nb % 4 == 0 else 1
    return pl.pallas_call(
        functools.partial(_moba_kernel, group=group, nb=nb),
        grid=(bsz, npair, nb),
        in_specs=[pl.BlockSpec((1, LANES, MOBA_BLOCK), lambda b, hp, i: (b, hp, i)),
                  pl.BlockSpec((1, seq, LANES), lambda b, hp, i: (b, 0, hp)),
                  pl.BlockSpec((1, 2, nb, V_ROWS, MOBA_BLOCK), lambda b, hp, i: (b, hp, 0, 0, 0)),
                  pl.BlockSpec((1, MOBA_BLOCKS_PAD, LANES), lambda b, hp, i: (b, 0, hp))],
        out_specs=pl.BlockSpec((1, MOBA_BLOCK, LANES), lambda b, hp, i: (b, i, hp)),
        out_shape=jax.ShapeDtypeStruct((bsz, seq, A_WIDTH), F32),
        scratch_shapes=[pltpu.VMEM((2 * LANES, 2 * MOBA_BLOCK), BF16),
                        pltpu.VMEM((2, V_ROWS, MOBA_BLOCK), F32),
                        pltpu.VMEM((group * MOBA_BLOCK, 2 * MOBA_BLOCK), F32),
                        pltpu.VMEM((group * MOBA_BLOCK, 2 * MOBA_BLOCK), F32)],
        compiler_params=_params(("arbitrary", "arbitrary", "arbitrary")),
        name="moba",
    )(qt, k, vt, km)


def _gla_kernel(q_ref, k_ref, v_ref, g_ref, og_ref, ng_ref, o_ref, st_ref, *, tm):
    @pl.when(pl.program_id(1) == 0)
    def _():
        st_ref[...] = jnp.zeros_like(st_ref)

    q, k, v, g = q_ref[0], k_ref[0], v_ref[0], g_ref[0]
    kw, vw = B_KEY_WIDTH, B_VAL_WIDTH
    row = lax.broadcasted_iota(jnp.int32, (tm, 1), 0)
    r_sub = row % GLA_SUB
    r_chk = row % GLA_CHUNK

    def seg_cumsum(x, r, length):
        s = 1
        while s < length:
            x = x + jnp.where(r >= s, pltpu.roll(x, s, 0), 0.0)
            s *= 2
        return x

    c_sub = seg_cumsum(g, r_sub, GLA_SUB)
    b = seg_cumsum(g, r_chk, GLA_CHUNK)

    head_kv = (lax.broadcasted_iota(jnp.int32, (kw, vw), 0) // B_HEAD_K
               == lax.broadcasted_iota(jnp.int32, (kw, vw), 1) // B_HEAD_V).astype(BF16)
    head_vk = (lax.broadcasted_iota(jnp.int32, (vw, kw), 0) // B_HEAD_V
               == lax.broadcasted_iota(jnp.int32, (vw, kw), 1) // B_HEAD_K)
    head_vv = (lax.broadcasted_iota(jnp.int32, (vw, vw), 0) // B_HEAD_V
               == lax.broadcasted_iota(jnp.int32, (vw, vw), 1) // B_HEAD_V).astype(BF16)

    o_acc = jnp.zeros((tm, vw), F32)
    for d in range(GLA_SUB):
        if d == 0:
            w = q * k
            vv = v
        else:
            w = jnp.where(r_sub >= d,
                          q * pltpu.roll(k, d, 0) * jnp.exp(c_sub - pltpu.roll(c_sub, d, 0)), 0.0)
            vv = pltpu.roll(v, d, 0)
        o_acc = o_acc + jnp.dot(w.astype(BF16), head_kv, preferred_element_type=F32) * vv

    q_sub = q * jnp.exp(c_sub)
    q_chk = q * jnp.exp(b)

    nsub = GLA_CHUNK // GLA_SUB
    rr = lax.broadcasted_iota(jnp.int32, (GLA_CHUNK, 1), 0)
    lane_k = lax.broadcasted_iota(jnp.int32, (1, (nsub - 1) * kw), 1)
    lane_v = lax.broadcasted_iota(jnp.int32, (1, vw), 1)
    outs = []
    for c in range(tm // GLA_CHUNK):
        lo = c * GLA_CHUNK
        bc = b[lo:lo + GLA_CHUNK]
        kc = k[lo:lo + GLA_CHUNK]
        vc = v[lo:lo + GLA_CHUNK].astype(BF16)
        qs = q_sub[lo:lo + GLA_CHUNK]

        rhs_parts, lhs_parts = [], []
        for i in range(1, nsub):
            b_end = bc[i * GLA_SUB - 1:i * GLA_SUB, :]
            decay = jnp.exp(jnp.minimum(b_end - bc, 0.0))
            rhs_parts.append(jnp.where(rr < i * GLA_SUB, kc * decay, 0.0).astype(BF16))
            lhs_parts.append(jnp.where(rr // GLA_SUB == i, qs, 0.0).astype(BF16))
        rhs = jnp.concatenate(rhs_parts, axis=1)
        lhs_t = jnp.concatenate(lhs_parts, axis=1)
        lhs = jnp.concatenate(
            [jnp.where((lane_k % kw) // B_HEAD_K == h, lhs_t, jnp.zeros_like(lhs_t))
             for h in range(B_HEADS)], axis=0)
        attn = lax.dot_general(lhs, rhs, (((1,), (1,)), ((), ())), preferred_element_type=F32)
        res = jnp.dot(attn.astype(BF16), vc, preferred_element_type=F32)
        o_off = jnp.zeros((GLA_CHUNK, vw), F32)
        for h in range(B_HEADS):
            o_off = o_off + jnp.where(lane_v // B_HEAD_V == h,
                                      res[h * GLA_CHUNK:(h + 1) * GLA_CHUNK], 0.0)

        st = st_ref[...]
        o_int = lax.dot_general(q_chk[lo:lo + GLA_CHUNK].astype(BF16), st.astype(BF16),
                                (((1,), (1,)), ((), ())), preferred_element_type=F32)
        b_last = bc[GLA_CHUNK - 1:GLA_CHUNK, :]
        ks = (kc * jnp.exp(b_last - bc)).astype(BF16)
        upd = lax.dot_general(vc, ks, (((0,), (0,)), ((), ())), preferred_element_type=F32)
        st_ref[...] = st * jnp.exp(b_last) + jnp.where(head_vk, upd, 0.0)
        outs.append(o_off + o_int)

    o = o_acc + jnp.concatenate(outs, axis=0)

    sq = o * o
    sq_hi = sq.astype(BF16)
    sq_lo = (sq - sq_hi.astype(F32)).astype(BF16)
    ms = (jnp.dot(sq_hi, head_vv, preferred_element_type=F32)
          + jnp.dot(sq_lo, head_vv, preferred_element_type=F32)) * (1.0 / B_HEAD_V)
    og = og_ref[0]
    o_ref[0] = (o * lax.rsqrt(ms + NORM_EPS) * ng_ref[...]) * (og * jax.nn.sigmoid(og))


def _gla(bq, bk, bv, gl, og, norm_g):
    bsz, seq, _ = bq.shape
    tm = min(256, seq)
    row = lambda n: pl.BlockSpec((1, tm, n), lambda b, i: (b, i, 0))
    return pl.pallas_call(
        functools.partial(_gla_kernel, tm=tm),
        grid=(bsz, seq // tm),
        in_specs=[row(B_KEY_WIDTH), row(B_KEY_WIDTH), row(B_VAL_WIDTH), row(B_KEY_WIDTH),
                  row(B_VAL_WIDTH), _resident((1, B_VAL_WIDTH))],
        out_specs=row(B_VAL_WIDTH),
        out_shape=jax.ShapeDtypeStruct((bsz, seq, B_VAL_WIDTH), F32),
        scratch_shapes=[pltpu.VMEM((B_VAL_WIDTH, B_KEY_WIDTH), F32)],
        compiler_params=_params(("arbitrary", "arbitrary")),
        name="gla",
    )(bq, bk, bv, gl, og, norm_g.reshape(1, B_VAL_WIDTH))


def _ffn_kernel(*refs, tm, bn, dff, final_norm, mixer):
    if mixer:
        x_ref, oa_ref, ob_ref, wa_ref, wb_ref = refs[:5]
        refs = refs[5:]
    else:
        x_ref, refs = refs[0], refs[1:]
    g_ref, wup_ref, cw_ref, cb_ref, wdn_ref, fg_ref, o_ref, halo_ref, ext_ref = refs
    nkb = dff // bn

    @pl.when(pl.program_id(1) == 0)
    def _():
        halo_ref[...] = jnp.zeros_like(halo_ref)

    x = x_ref[0]
    if mixer:
        o_ref[0] = (jnp.dot(oa_ref[0].astype(BF16), wa_ref[...], preferred_element_type=F32)
                    + jnp.dot(ob_ref[0].astype(BF16), wb_ref[...], preferred_element_type=F32))
        x = x + o_ref[0]
    hn = _rms(x, g_ref[...]).astype(BF16)

    def up_stage(kb):
        for part in range(2):
            c0 = part * dff + kb * bn
            up = jnp.dot(hn, wup_ref[:, c0:c0 + bn], preferred_element_type=F32)
            ext_ref[kb % 2, part, 0:SUBLANES, :] = halo_ref[:, c0:c0 + bn]
            ext_ref[kb % 2, part, SUBLANES:, :] = up
            halo_ref[:, c0:c0 + bn] = up[tm - SUBLANES:, :]

    def gate_stage(kb):
        halves = []
        for part in range(2):
            c0 = part * dff + kb * bn
            y = cb_ref[:, c0:c0 + bn]
            for i in range(FFN_CONV):
                off = SUBLANES - (FFN_CONV - 1) + i
                y = y + ext_ref[kb % 2, part, pl.ds(off, tm), :] * cw_ref[i:i + 1, c0:c0 + bn]
            halves.append(y)
        return (halves[0] * _gelu_tanh(halves[1])).astype(BF16)

    up_stage(0)
    acc = x
    for kb in range(nkb):
        if kb + 1 < nkb:
            up_stage(kb + 1)
        acc = acc + jnp.dot(gate_stage(kb), wdn_ref[kb * bn:(kb + 1) * bn, :],
                            preferred_element_type=F32)
    if final_norm:
        acc = _rms(acc, fg_ref[...])
    o_ref[0] = acc


def _conv_ffn(x, g, w_up, conv_w, conv_b, w_down, final_g, final_norm, mixer=None):
    bsz, seq, d = x.shape
    dff = w_down.shape[0]
    tm = min(512, seq)
    bn = 512
    rows = lambda n: pl.BlockSpec((1, tm, n), lambda b, i: (b, i, 0))
    row = rows(d)
    lead_specs, lead_args = [row], [x]
    if mixer is not None:
        o_a, o_b, w_out = mixer
        w = w_out.astype(BF16)
        lead_specs += [rows(A_WIDTH), rows(B_VAL_WIDTH),
                       _resident((A_WIDTH, d)), _resident((B_VAL_WIDTH, d))]
        lead_args += [o_a, o_b, w[:A_WIDTH], w[A_WIDTH:]]
    return pl.pallas_call(
        functools.partial(_ffn_kernel, tm=tm, bn=bn, dff=dff, final_norm=final_norm,
                          mixer=mixer is not None),
        grid=(bsz, seq // tm),
        in_specs=lead_specs + [_resident((1, d)), _resident((d, 2 * dff)),
                               _resident((FFN_CONV, 2 * dff)), _resident((1, 2 * dff)),
                               _resident((dff, d)), _resident((1, d))],
        out_specs=row,
        out_shape=jax.ShapeDtypeStruct((bsz, seq, d), F32),
        scratch_shapes=[pltpu.VMEM((SUBLANES, 2 * dff), F32),
                        pltpu.VMEM((2, 2, tm + SUBLANES, bn), F32)],
        compiler_params=_params(("arbitrary", "arbitrary")),
        name="conv_ffn_final" if final_norm else "conv_ffn",
    )(*lead_args, g.reshape(1, d), w_up.astype(BF16), conv_w, conv_b.reshape(1, 2 * dff),
      w_down.astype(BF16), final_g.reshape(1, d))


def _rglru_kernel(x_ref, g_ref, win_ref, cw_ref, cb_ref, wa_ref, ba_ref, wx_ref, bx_ref,
                  lam_ref, wout_ref, o_ref, ext_ref, halo_ref, hlast_ref, *, tm, drnn, nseq):
    seg = tm // SUBLANES
    shifts = RG_CONV - 1
    bw = drnn // RG_BLOCKS
    seqs = range(nseq)

    @pl.when(pl.program_id(1) == 0)
    def _():
        halo_ref[...] = jnp.zeros_like(halo_ref)
        hlast_ref[...] = jnp.zeros_like(hlast_ref)

    prow = lax.broadcasted_iota(jnp.int32, (tm, tm), 0)
    pcol = lax.broadcasted_iota(jnp.int32, (tm, tm), 1)
    perm = (pcol == (prow % SUBLANES) * seg + prow // SUBLANES).astype(BF16)
    unperm = (prow == (pcol % SUBLANES) * seg + pcol // SUBLANES).astype(BF16)
    first_sub = lax.broadcasted_iota(jnp.int32, (SUBLANES, 1), 0) == 0

    xs = [x_ref[b] for b in seqs]
    hn = [jnp.dot(perm, _rms(xs[b], g_ref[...]).astype(BF16),
                  preferred_element_type=F32).astype(BF16) for b in seqs]
    gate = [jnp.dot(hn[b], win_ref[:, 0:drnn], preferred_element_type=F32) for b in seqs]
    xr_raw = [jnp.dot(hn[b], win_ref[:, drnn:2 * drnn], preferred_element_type=F32) for b in seqs]

    xr = []
    for b in seqs:
        for r in range(shifts):
            lo = (seg - shifts + r) * SUBLANES
            cur = xr_raw[b][lo:lo + SUBLANES]
            ext_ref[b, r * SUBLANES:(r + 1) * SUBLANES, :] = jnp.where(
                first_sub, pltpu.roll(halo_ref[b, r], 1, 0), pltpu.roll(cur, 1, 0))
            halo_ref[b, r] = cur
        ext_ref[b, shifts * SUBLANES:, :] = xr_raw[b]
        acc = cb_ref[...]
        for i in range(RG_CONV):
            acc = acc + ext_ref[b, i * SUBLANES:i * SUBLANES + tm, :] * cw_ref[i:i + 1, :]
        xr.append(acc)

    ys = []
    for b in seqs:
        xb = xr[b].astype(BF16)
        ra, ri = [], []
        for n in range(RG_BLOCKS):
            xn = xb[:, n * bw:(n + 1) * bw]
            ra.append(jnp.dot(xn, wa_ref[n], preferred_element_type=F32))
            ri.append(jnp.dot(xn, wx_ref[n], preferred_element_type=F32))
        r = jax.nn.sigmoid(jnp.concatenate(ra, axis=1) + ba_ref[...])
        ig = jax.nn.sigmoid(jnp.concatenate(ri, axis=1) + bx_ref[...])
        log_a = r * lam_ref[...]
        a = jnp.exp(log_a)
        th = jnp.tanh(log_a)
        one_minus_a2 = -2.0 * th / (1.0 - th)
        u = jnp.sqrt(one_minus_a2) * (ig * xr[b])

        h_loc, a_cum = [u[0:SUBLANES]], [a[0:SUBLANES]]
        for j in range(1, seg):
            aj = a[j * SUBLANES:(j + 1) * SUBLANES]
            h_loc.append(aj * h_loc[-1] + u[j * SUBLANES:(j + 1) * SUBLANES])
            a_cum.append(aj * a_cum[-1])
        h_in = [hlast_ref[b]]
        for s in range(SUBLANES):
            h_in.append(h_loc[-1][s:s + 1] + a_cum[-1][s:s + 1] * h_in[-1])
        hlast_ref[b] = h_in[SUBLANES]
        h_seg = jnp.concatenate(h_in[:SUBLANES], axis=0)
        h = jnp.concatenate([h_loc[j] + a_cum[j] * h_seg for j in range(seg)], axis=0)
        ys.append((h * _gelu_tanh(gate[b])).astype(BF16))

    for b in seqs:
        y_t = jnp.dot(unperm, ys[b], preferred_element_type=F32).astype(BF16)
        o_ref[b] = xs[b] + jnp.dot(y_t, wout_ref[...], preferred_element_type=F32)


def _rglru(x, g, w_in, conv_w, conv_b, w_a, b_a, w_x, b_x, lam, w_out):
    bsz, seq, d = x.shape
    drnn = w_out.shape[0]
    bw = drnn // RG_BLOCKS
    tm = min(256, seq)
    nseq = 2 if bsz % 2 == 0 else 1
    row = pl.BlockSpec((nseq, tm, d), lambda b, i: (b, i, 0))
    lam_c = (-LRU_C) * jax.nn.softplus(-lam.astype(F32))
    return pl.pallas_call(
        functools.partial(_rglru_kernel, tm=tm, drnn=drnn, nseq=nseq),
        grid=(bsz // nseq, seq // tm),
        in_specs=[row, _resident((1, d)), _resident((d, 2 * drnn)), _resident((RG_CONV, drnn)),
                  _resident((1, drnn)), _resident((RG_BLOCKS, bw, bw)), _resident((1, drnn)),
                  _resident((RG_BLOCKS, bw, bw)), _resident((1, drnn)), _resident((1, drnn)),
                  _resident((drnn, d))],
        out_specs=row,
        out_shape=jax.ShapeDtypeStruct((bsz, seq, d), F32),
        scratch_shapes=[pltpu.VMEM((nseq, tm + (RG_CONV - 1) * SUBLANES, drnn), F32),
                        pltpu.VMEM((nseq, RG_CONV - 1, SUBLANES, drnn), F32),
                        pltpu.VMEM((nseq, 1, drnn), F32)],
        compiler_params=_params(("arbitrary", "arbitrary")),
        name="rglru",
    )(x, g.reshape(1, d), w_in.astype(BF16), conv_w, conv_b.reshape(1, drnn),
      w_a.astype(BF16), b_a.reshape(1, drnn), w_x.astype(BF16), b_x.reshape(1, drnn),
      lam_c.reshape(1, drnn), w_out.astype(BF16))


def kernel(x, mix_norm_g, ffn_norm_g, final_norm_g, ev_w_in, ev_w_gk2, ev_b_gk2, ev_gla_norm_g,
           ev_w_out, od_w_in, od_conv_w, od_conv_b, od_w_a, od_b_a, od_w_x, od_b_x, od_lambda,
           od_w_out, ffn_w_up, ffn_conv_w, ffn_conv_b, ffn_w_down):
    depth = mix_norm_g.shape[0]
    h = x
    for l in range(depth):
        j = l // 2
        mixer = None
        if l % 2 == 0:
            q, k, v, kmean, bq, bk, bv, gl, og = _front0(h, mix_norm_g[l], ev_w_in[j], ev_w_gk2[j],
                                                       ev_b_gk2[j])
            o_a = _moba(q, k, v, kmean)
            o_b = _gla(bq, bk, bv, gl, og, ev_gla_norm_g[j])
            mixer = (o_a, o_b, ev_w_out[j])
        else:
            h = _rglru(h, mix_norm_g[l], od_w_in[j], od_conv_w[j], od_conv_b[j], od_w_a[j],
                       od_b_a[j], od_w_x[j], od_b_x[j], od_lambda[j], od_w_out[j])
        h = _conv_ffn(h, ffn_norm_g[l], ffn_w_up[l], ffn_conv_w[l], ffn_conv_b[l], ffn_w_down[l],
                      final_norm_g, final_norm=(l == depth - 1), mixer=mixer)
    return h
```

```python
import functools

import jax
import jax.numpy as jnp
from jax import lax
from jax.experimental import pallas as pl
from jax.experimental.pallas import tpu as pltpu

F32 = jnp.float32
BF16 = jnp.bfloat16

A_HEADS = 8
A_HEAD_DIM = 64
A_WIDTH = A_HEADS * A_HEAD_DIM
MOBA_BLOCK = 256
MOBA_TOPK = 3
ROPE_THETA = 10000.0
B_HEADS = 8
B_HEAD_K = 32
B_HEAD_V = 64
B_KEY_WIDTH = B_HEADS * B_HEAD_K
B_VAL_WIDTH = B_HEADS * B_HEAD_V
GLA_GATE_RANK = 16
GLA_GATE_NORM = 16.0
RG_BLOCKS = 4
RG_CONV = 4
LRU_C = 8.0
FFN_CONV = 3
NORM_EPS = 1e-6
MASK_VALUE = -1e30

LANES = 128
SUBLANES = 8
BF16_ROWS = 16
LOG2_E = 1.4426950408889634
VMEM_LIMIT = 56 * 1024 * 1024

GLA_CHUNK = 64
GLA_SUB = 16
MOBA_BLOCKS_PAD = LANES
V_ROWS = 80


def _params(sem):
    return pltpu.CompilerParams(dimension_semantics=sem, vmem_limit_bytes=VMEM_LIMIT)


def _rms(x, g):
    ms = jnp.mean(x * x, axis=-1, keepdims=True)
    return x * lax.rsqrt(ms + NORM_EPS) * g


def _gelu_tanh(x):
    return 0.5 * x * (1.0 + jnp.tanh(0.7978845608028654 * (x + 0.044715 * (x * x * x))))


def _log_sigmoid(z):
    return jnp.minimum(z, 0.0) - jnp.log1p(jnp.exp(-jnp.abs(z)))


def _moba_gate_rows(nb):
    assert nb <= MOBA_BLOCKS_PAD
    return -(-nb // BF16_ROWS) * BF16_ROWS


def _resident(shape):
    nd = len(shape)
    return pl.BlockSpec(shape, lambda *_: (0,) * nd, pipeline_mode=pl.Buffered(1))


def _front0_kernel(x_ref, g_ref, w_ref, wt_ref, wgk_ref, bgk_ref, cos_ref, sa_ref, sb_ref,
                   cost_ref, sat_ref, sbt_ref,
                   q_ref, k_ref, v_ref, bias_ref, bq_ref, bk_ref, bv_ref, gl_ref, og_ref,
                   km_ref, *, tm):
    @pl.when(pl.program_id(1) == 0)
    def _():
        km_ref[...] = jnp.zeros_like(km_ref)

    hn = _rms(x_ref[0], g_ref[...]).astype(BF16)

    def proj(c0, n):
        return jnp.dot(hn, w_ref[:, c0:c0 + n], preferred_element_type=F32)

    def proj_t(r0, n):
        return lax.dot_general(wt_ref[r0:r0 + n, :], hn, (((1,), (1,)), ((), ())),
                               preferred_element_type=F32)

    def rot(t, cos, sa, sb, axis):
        parts = []
        for c in range(t.shape[axis] // LANES):
            tc = lax.slice_in_dim(t, c * LANES, (c + 1) * LANES, axis=axis)
            parts.append(tc * cos + pltpu.roll(tc, LANES - 32, axis) * sa
                         + pltpu.roll(tc, 32, axis) * sb)
        return jnp.concatenate(parts, axis=axis)

    qt = rot(proj_t(0, A_WIDTH), cost_ref[...], sat_ref[...], sbt_ref[...], 0)
    q_ref[0] = (qt * (A_HEAD_DIM ** -0.5 * LOG2_E)).astype(BF16)
    vt = proj_t(A_WIDTH, A_WIDTH)
    c = 0
    k = rot(proj(c, A_WIDTH), cos_ref[...], sa_ref[...], sb_ref[...], 1); c += A_WIDTH
    k_ref[0] = k.astype(BF16)
    nblk = tm // MOBA_BLOCK
    kmean = jnp.mean(k.reshape(nblk, MOBA_BLOCK, A_WIDTH), axis=1)
    blk0 = pl.program_id(1) * nblk
    for n in range(nblk):
        km_ref[pl.ds(blk0 + n, 1), :] = kmean[n:n + 1]

    nbr = km_ref.shape[0]
    feat = lax.broadcasted_iota(jnp.int32, (LANES, 1), 0)
    brow = lax.broadcasted_iota(jnp.int32, (nbr, 2 * tm), 0)
    browf = brow.astype(F32)
    q_blk = blk0 + (lax.broadcasted_iota(jnp.int32, (1, 2 * tm), 1) % tm) // MOBA_BLOCK

    def split(a):
        hi = a.astype(BF16)
        return hi, (a - hi.astype(F32)).astype(BF16)

    def gate_pair(hp):
        q_pair = qt[hp * LANES:(hp + 1) * LANES]
        q2 = jnp.concatenate([jnp.where(feat // A_HEAD_DIM == h, q_pair, 0.0) for h in range(2)],
                             axis=1)
        km_hi, km_lo = split(km_ref[:, hp * LANES:(hp + 1) * LANES])
        q_hi, q_lo = split(q2)
        gate = jnp.dot(jnp.concatenate([km_hi, km_hi, km_lo], axis=1),
                       jnp.concatenate([q_hi, q_lo, q_hi], axis=0), preferred_element_type=F32)
        g = jnp.where(brow < q_blk, gate, -jnp.inf)
        bias = jnp.full((nbr, 2 * tm), MASK_VALUE, F32)
        for _ in range(MOBA_TOPK):
            mx = jnp.max(g, axis=0, keepdims=True)
            first = jnp.min(jnp.where(g == mx, browf, float(nbr)), axis=0, keepdims=True)
            hit = browf == first
            bias = jnp.where(hit & (mx > -jnp.inf), 0.0, bias)
            g = jnp.where(hit, -jnp.inf, g)
        for h in range(2):
            bias_ref[0, 2 * hp + h] = bias[:, h * tm:(h + 1) * tm].astype(BF16)

    ones_rows = (lax.broadcasted_iota(jnp.int32, (V_ROWS - A_HEAD_DIM, MOBA_BLOCK), 0)
                 == 0).astype(BF16)
    for h in range(A_HEADS):
        for n in range(nblk):
            v_ref[0, h, n, 0:A_HEAD_DIM, :] = vt[h * A_HEAD_DIM:(h + 1) * A_HEAD_DIM,
                                                 n * MOBA_BLOCK:(n + 1) * MOBA_BLOCK].astype(BF16)
            v_ref[0, h, n, A_HEAD_DIM:, :] = ones_rows
    gate_pair(0)
    bq_ref[0] = proj(c, B_KEY_WIDTH) * (B_HEAD_K ** -0.5); c += B_KEY_WIDTH
    bk_ref[0] = proj(c, B_KEY_WIDTH); c += B_KEY_WIDTH
    gate_pair(1)
    bv_ref[0] = proj(c, B_VAL_WIDTH); c += B_VAL_WIDTH
    gate_pair(2)
    og_ref[0] = proj(c, B_VAL_WIDTH); c += B_VAL_WIDTH
    gate_pair(3)
    gk = proj(c, LANES).astype(BF16)
    z = jnp.dot(gk, wgk_ref[...], preferred_element_type=F32) + bgk_ref[...]
    gl_ref[0] = _log_sigmoid(z) / GLA_GATE_NORM


def _front0(x, g, w_in, w_gk2, b_gk2):
    bsz, seq, d = x.shape
    tm = min(512, seq)
    aq, ak, av, bq, bk, bv, bgk, bog = jnp.split(
        w_in, [512, 1024, 1536, 1792, 2048, 2560, 2576], axis=1)
    gk_pad = LANES - GLA_GATE_RANK
    w = jnp.concatenate([ak, bq, bk, bv, bog, jnp.pad(bgk, ((0, 0), (0, gk_pad)))],
                        axis=1).astype(BF16)
    wt = jnp.concatenate([aq, av], axis=1).T.astype(BF16)
    wgk = jnp.pad(w_gk2, ((0, gk_pad), (0, 0))).astype(BF16)
    ncol = w.shape[1]

    half = A_HEAD_DIM // 2
    lane = jnp.arange(LANES)
    inv = ROPE_THETA ** (-jnp.arange(half, dtype=F32) / half)
    ang = jnp.arange(seq, dtype=jnp.int32).astype(F32)[:, None] * inv[None, :]
    cos_h, sin_h = jnp.cos(ang), jnp.sin(ang)
    first = (lane % A_HEAD_DIM) < half
    cos_t = jnp.tile(cos_h, (1, LANES // half))
    sin_t = jnp.tile(sin_h, (1, LANES // half))
    sa = jnp.where(first[None, :], -sin_t, 0.0)
    sb = jnp.where(first[None, :], 0.0, sin_t)

    nb = seq // MOBA_BLOCK
    nbr = _moba_gate_rows(nb)
    row = lambda n: pl.BlockSpec((1, tm, n), lambda b, i: (b, i, 0))
    tab = pl.BlockSpec((tm, LANES), lambda b, i: (i, 0))
    tab_t = pl.BlockSpec((LANES, tm), lambda b, i: (0, i))
    nt = seq // tm
    outs = pl.pallas_call(
        functools.partial(_front0_kernel, tm=tm),
        grid=(bsz, nt),
        in_specs=[row(d), _resident((1, d)), _resident((d, ncol)), _resident((2 * A_WIDTH, d)),
                  _resident((LANES, B_KEY_WIDTH)), _resident((1, B_KEY_WIDTH)),
                  tab, tab, tab, tab_t, tab_t, tab_t],
        out_specs=[pl.BlockSpec((1, A_WIDTH, tm), lambda b, i: (b, 0, i)), row(A_WIDTH),
                   pl.BlockSpec((1, A_HEADS, tm // MOBA_BLOCK, V_ROWS, MOBA_BLOCK),
                                lambda b, i: (b, 0, i, 0, 0)),
                   pl.BlockSpec((1, A_HEADS, nbr, tm), lambda b, i: (b, 0, 0, i)),
                   row(B_KEY_WIDTH), row(B_KEY_WIDTH), row(B_VAL_WIDTH), row(B_KEY_WIDTH),
                   row(B_VAL_WIDTH)],
        out_shape=[jax.ShapeDtypeStruct((bsz, A_WIDTH, seq), BF16),
                   jax.ShapeDtypeStruct((bsz, seq, A_WIDTH), BF16),
                   jax.ShapeDtypeStruct((bsz, A_HEADS, nb, V_ROWS, MOBA_BLOCK), BF16),
                   jax.ShapeDtypeStruct((bsz, A_HEADS, nbr, seq), BF16),
                   jax.ShapeDtypeStruct((bsz, seq, B_KEY_WIDTH), F32),
                   jax.ShapeDtypeStruct((bsz, seq, B_KEY_WIDTH), F32),
                   jax.ShapeDtypeStruct((bsz, seq, B_VAL_WIDTH), F32),
                   jax.ShapeDtypeStruct((bsz, seq, B_KEY_WIDTH), F32),
                   jax.ShapeDtypeStruct((bsz, seq, B_VAL_WIDTH), F32)],
        scratch_shapes=[pltpu.VMEM((nbr, A_WIDTH), F32)],
        compiler_params=_params(("arbitrary", "arbitrary")),
        name="front0",
    )(x, g.reshape(1, d), w, wt, wgk, b_gk2.reshape(1, B_KEY_WIDTH), cos_t, sa, sb,
      cos_t.T, sa.T, sb.T)
    return outs


def _moba_kernel(qt_ref, k_ref, vt_ref, bias_ref, o_ref, rhs_ref, acc_ref, sa_ref, sb_ref, *,
                 group, nb):
    qb = pl.program_id(2)
    blk = MOBA_BLOCK
    qt = qt_ref[0]
    feat = lax.broadcasted_iota(jnp.int32, (LANES, 1), 0)
    nbr = bias_ref.shape[2]
    for h in range(2):
        rhs_ref[:, h * blk:(h + 1) * blk] = jnp.concatenate(
            [jnp.where(feat // A_HEAD_DIM == h, qt, jnp.zeros_like(qt)), bias_ref[0, h],
             jnp.full((MOBA_BLOCKS_PAD - nbr, blk), MASK_VALUE, BF16)], axis=0)

    keys = group * blk
    blk_of_row = lax.broadcasted_iota(jnp.int32, (keys, MOBA_BLOCKS_PAD), 0) // blk
    lane_blk = lax.broadcasted_iota(jnp.int32, (keys, MOBA_BLOCKS_PAD), 1)

    def scores(g, s_ref):
        j0 = g * group
        off = pl.multiple_of(jnp.minimum(j0, nb - group) * blk, blk)
        onehot = (lane_blk == blk_of_row + j0).astype(BF16)
        lhs_j = jnp.concatenate([k_ref[0, pl.ds(off, keys), :], onehot], axis=1)
        s = jnp.dot(lhs_j, rhs_ref[...], preferred_element_type=F32)
        s_ref[...] = s
        return jnp.max(s, axis=0, keepdims=True)

    def accumulate(g, s_ref, m, s_max):
        j0 = g * group
        m_new = jnp.maximum(m, s_max)
        alpha = jnp.exp2(m - m_new)
        p = jnp.exp2(s_ref[...] - m_new).astype(BF16)
        for h in range(2):
            vt = jnp.concatenate([vt_ref[0, h, jnp.minimum(j0 + u, nb - 1)] for u in range(group)],
                                 axis=1)
            acc_ref[h] = (alpha[:, h * blk:(h + 1) * blk] * acc_ref[h]
                          + jnp.dot(vt, p[:, h * blk:(h + 1) * blk], preferred_element_type=F32))
        return m_new

    def body(i, carry):
        m, max_a = carry
        max_b = scores(2 * i + 1, sb_ref)
        m = accumulate(2 * i, sa_ref, m, max_a)
        max_a = scores(2 * i + 2, sa_ref)
        m = accumulate(2 * i + 1, sb_ref, m, max_b)
        return m, max_a

    max_0 = scores(0, sa_ref)

    own = pl.multiple_of(qb * blk, blk)
    lhs = jnp.concatenate([k_ref[0, pl.ds(own, blk), :], jnp.zeros((blk, MOBA_BLOCKS_PAD), BF16)],
                          axis=1)
    causal = (lax.broadcasted_iota(jnp.int32, (blk, 2 * blk), 0)
              <= lax.broadcasted_iota(jnp.int32, (blk, 2 * blk), 1) % blk)
    s = jnp.where(causal, jnp.dot(lhs, rhs_ref[...], preferred_element_type=F32), MASK_VALUE)
    m0 = jnp.max(s, axis=0, keepdims=True)
    p = jnp.exp2(s - m0).astype(BF16)
    for h in range(2):
        acc_ref[h] = jnp.dot(vt_ref[0, h, qb], p[:, h * blk:(h + 1) * blk],
                             preferred_element_type=F32)

    n_groups = (qb + group - 1) // group
    m, max_a = lax.fori_loop(0, n_groups // 2, body, (m0, max_0))

    @pl.when(n_groups % 2 == 1)
    def _():
        accumulate(n_groups - 1, sa_ref, m, max_a)

    outs = []
    for h in range(2):
        a = acc_ref[h]
        outs.append(a[0:A_HEAD_DIM] / a[A_HEAD_DIM:A_HEAD_DIM + 1])
    o_ref[0] = jnp.concatenate(outs, axis=0).T


def _moba(qt, k, vt, bias):
    bsz, seq, _ = k.shape
    nb = seq // MOBA_BLOCK
    nbr = _moba_gate_rows(nb)
    npair = A_WIDTH // LANES
    group = 4 if nb % 4 == 0 else 1
    return pl.pallas_call(
        functools.partial(_moba_kernel, group=group, nb=nb),
        grid=(bsz, npair, nb),
        in_specs=[pl.BlockSpec((1, LANES, MOBA_BLOCK), lambda b, hp, i: (b, hp, i)),
                  pl.BlockSpec((1, seq, LANES), lambda b, hp, i: (b, 0, hp)),
                  pl.BlockSpec((1, 2, nb, V_ROWS, MOBA_BLOCK), lambda b, hp, i: (b, hp, 0, 0, 0)),
                  pl.BlockSpec((1, 2, nbr, MOBA_BLOCK), lambda b, hp, i: (b, hp, 0, i))],
        out_specs=pl.BlockSpec((1, MOBA_BLOCK, LANES), lambda b, hp, i: (b, i, hp)),
        out_shape=jax.ShapeDtypeStruct((bsz, seq, A_WIDTH), F32),
        scratch_shapes=[pltpu.VMEM((2 * LANES, 2 * MOBA_BLOCK), BF16),
                        pltpu.VMEM((2, V_ROWS, MOBA_BLOCK), F32),
                        pltpu.VMEM((group * MOBA_BLOCK, 2 * MOBA_BLOCK), F32),
                        pltpu.VMEM((group * MOBA_BLOCK, 2 * MOBA_BLOCK), F32)],
        compiler_params=_params(("arbitrary", "arbitrary", "arbitrary")),
        name="moba",
    )(qt, k, vt, bias)


def _gla_kernel(q_ref, k_ref, v_ref, g_ref, og_ref, ng_ref, o_ref, st_ref, *, tm):
    @pl.when(pl.program_id(1) == 0)
    def _():
        st_ref[...] = jnp.zeros_like(st_ref)

    q, k, v, g = q_ref[0], k_ref[0], v_ref[0], g_ref[0]
    kw, vw = B_KEY_WIDTH, B_VAL_WIDTH
    row = lax.broadcasted_iota(jnp.int32, (tm, 1), 0)
    r_sub = row % GLA_SUB
    r_chk = row % GLA_CHUNK

    def seg_cumsum(x, r, length):
        s = 1
        while s < length:
            x = x + jnp.where(r >= s, pltpu.roll(x, s, 0), 0.0)
            s *= 2
        return x

    c_sub = seg_cumsum(g, r_sub, GLA_SUB)
    b = seg_cumsum(g, r_chk, GLA_CHUNK)

    head_kv = (lax.broadcasted_iota(jnp.int32, (kw, vw), 0) // B_HEAD_K
               == lax.broadcasted_iota(jnp.int32, (kw, vw), 1) // B_HEAD_V).astype(BF16)
    head_vk = (lax.broadcasted_iota(jnp.int32, (vw, kw), 0) // B_HEAD_V
               == lax.broadcasted_iota(jnp.int32, (vw, kw), 1) // B_HEAD_K)
    head_vv = (lax.broadcasted_iota(jnp.int32, (vw, vw), 0) // B_HEAD_V
               == lax.broadcasted_iota(jnp.int32, (vw, vw), 1) // B_HEAD_V).astype(BF16)

    o_acc = jnp.zeros((tm, vw), F32)
    for d in range(GLA_SUB):
        if d == 0:
            w = q * k
            vv = v
        else:
            w = jnp.where(r_sub >= d,
                          q * pltpu.roll(k, d, 0) * jnp.exp(c_sub - pltpu.roll(c_sub, d, 0)), 0.0)
            vv = pltpu.roll(v, d, 0)
        o_acc = o_acc + jnp.dot(w.astype(BF16), head_kv, preferred_element_type=F32) * vv

    q_sub = q * jnp.exp(c_sub)
    q_chk = q * jnp.exp(b)

    nsub = GLA_CHUNK // GLA_SUB
    rr = lax.broadcasted_iota(jnp.int32, (GLA_CHUNK, 1), 0)
    lane_k = lax.broadcasted_iota(jnp.int32, (1, (nsub - 1) * kw), 1)
    lane_v = lax.broadcasted_iota(jnp.int32, (1, vw), 1)
    outs = []
    for c in range(tm // GLA_CHUNK):
        lo = c * GLA_CHUNK
        bc = b[lo:lo + GLA_CHUNK]
        kc = k[lo:lo + GLA_CHUNK]
        vc = v[lo:lo + GLA_CHUNK].astype(BF16)
        qs = q_sub[lo:lo + GLA_CHUNK]

        rhs_parts, lhs_parts = [], []
        for i in range(1, nsub):
            b_end = bc[i * GLA_SUB - 1:i * GLA_SUB, :]
            decay = jnp.exp(jnp.minimum(b_end - bc, 0.0))
            rhs_parts.append(jnp.where(rr < i * GLA_SUB, kc * decay, 0.0).astype(BF16))
            lhs_parts.append(jnp.where(rr // GLA_SUB == i, qs, 0.0).astype(BF16))
        rhs = jnp.concatenate(rhs_parts, axis=1)
        lhs_t = jnp.concatenate(lhs_parts, axis=1)
        lhs = jnp.concatenate(
            [jnp.where((lane_k % kw) // B_HEAD_K == h, lhs_t, jnp.zeros_like(lhs_t))
             for h in range(B_HEADS)], axis=0)
        attn = lax.dot_general(lhs, rhs, (((1,), (1,)), ((), ())), preferred_element_type=F32)
        res = jnp.dot(attn.astype(BF16), vc, preferred_element_type=F32)
        o_off = jnp.zeros((GLA_CHUNK, vw), F32)
        for h in range(B_HEADS):
            o_off = o_off + jnp.where(lane_v // B_HEAD_V == h,
                                      res[h * GLA_CHUNK:(h + 1) * GLA_CHUNK], 0.0)

        st = st_ref[...]
        o_int = lax.dot_general(q_chk[lo:lo + GLA_CHUNK].astype(BF16), st.astype(BF16),
                                (((1,), (1,)), ((), ())), preferred_element_type=F32)
        b_last = bc[GLA_CHUNK - 1:GLA_CHUNK, :]
        ks = (kc * jnp.exp(b_last - bc)).astype(BF16)
        upd = lax.dot_general(vc, ks, (((0,), (0,)), ((), ())), preferred_element_type=F32)
        st_ref[...] = st * jnp.exp(b_last) + jnp.where(head_vk, upd, 0.0)
        outs.append(o_off + o_int)

    o = o_acc + jnp.concatenate(outs, axis=0)

    sq = o * o
    sq_hi = sq.astype(BF16)
    sq_lo = (sq - sq_hi.astype(F32)).astype(BF16)
    ms = (jnp.dot(sq_hi, head_vv, preferred_element_type=F32)
          + jnp.dot(sq_lo, head_vv, preferred_element_type=F32)) * (1.0 / B_HEAD_V)
    og = og_ref[0]
    o_ref[0] = (o * lax.rsqrt(ms + NORM_EPS) * ng_ref[...]) * (og * jax.nn.sigmoid(og))


def _gla(bq, bk, bv, gl, og, norm_g):
    bsz, seq, _ = bq.shape
    tm = min(256, seq)
    row = lambda n: pl.BlockSpec((1, tm, n), lambda b, i: (b, i, 0))
    return pl.pallas_call(
        functools.partial(_gla_kernel, tm=tm),
        grid=(bsz, seq // tm),
        in_specs=[row(B_KEY_WIDTH), row(B_KEY_WIDTH), row(B_VAL_WIDTH), row(B_KEY_WIDTH),
                  row(B_VAL_WIDTH), _resident((1, B_VAL_WIDTH))],
        out_specs=row(B_VAL_WIDTH),
        out_shape=jax.ShapeDtypeStruct((bsz, seq, B_VAL_WIDTH), F32),
        scratch_shapes=[pltpu.VMEM((B_VAL_WIDTH, B_KEY_WIDTH), F32)],
        compiler_params=_params(("arbitrary", "arbitrary")),
        name="gla",
    )(bq, bk, bv, gl, og, norm_g.reshape(1, B_VAL_WIDTH))


def _ffn_kernel(*refs, tm, bn, dff, final_norm, mixer):
    if mixer:
        x_ref, oa_ref, ob_ref, wa_ref, wb_ref = refs[:5]
        refs = refs[5:]
    else:
        x_ref, refs = refs[0], refs[1:]
    g_ref, wup_ref, cw_ref, cb_ref, wdn_ref, fg_ref, o_ref, halo_ref, ext_ref = refs
    nkb = dff // bn

    @pl.when(pl.program_id(1) == 0)
    def _():
        halo_ref[...] = jnp.zeros_like(halo_ref)

    x = x_ref[0]
    if mixer:
        o_ref[0] = (jnp.dot(oa_ref[0].astype(BF16), wa_ref[...], preferred_element_type=F32)
                    + jnp.dot(ob_ref[0].astype(BF16), wb_ref[...], preferred_element_type=F32))
        x = x + o_ref[0]
    hn = _rms(x, g_ref[...]).astype(BF16)

    def up_stage(kb):
        for part in range(2):
            c0 = part * dff + kb * bn
            up = jnp.dot(hn, wup_ref[:, c0:c0 + bn], preferred_element_type=F32)
            ext_ref[kb % 2, part, 0:SUBLANES, :] = halo_ref[:, c0:c0 + bn]
            ext_ref[kb % 2, part, SUBLANES:, :] = up
            halo_ref[:, c0:c0 + bn] = up[tm - SUBLANES:, :]

    def gate_stage(kb):
        halves = []
        for part in range(2):
            c0 = part * dff + kb * bn
            y = cb_ref[:, c0:c0 + bn]
            for i in range(FFN_CONV):
                off = SUBLANES - (FFN_CONV - 1) + i
                y = y + ext_ref[kb % 2, part, pl.ds(off, tm), :] * cw_ref[i:i + 1, c0:c0 + bn]
            halves.append(y)
        return (halves[0] * _gelu_tanh(halves[1])).astype(BF16)

    up_stage(0)
    acc = x
    for kb in range(nkb):
        if kb + 1 < nkb:
            up_stage(kb + 1)
        acc = acc + jnp.dot(gate_stage(kb), wdn_ref[kb * bn:(kb + 1) * bn, :],
                            preferred_element_type=F32)
    if final_norm:
        acc = _rms(acc, fg_ref[...])
    o_ref[0] = acc


def _conv_ffn(x, g, w_up, conv_w, conv_b, w_down, final_g, final_norm, mixer=None):
    bsz, seq, d = x.shape
    dff = w_down.shape[0]
    tm = min(512, seq)
    bn = 512
    rows = lambda n: pl.BlockSpec((1, tm, n), lambda b, i: (b, i, 0))
    row = rows(d)
    lead_specs, lead_args = [row], [x]
    if mixer is not None:
        o_a, o_b, w_out = mixer
        w = w_out.astype(BF16)
        lead_specs += [rows(A_WIDTH), rows(B_VAL_WIDTH),
                       _resident((A_WIDTH, d)), _resident((B_VAL_WIDTH, d))]
        lead_args += [o_a, o_b, w[:A_WIDTH], w[A_WIDTH:]]
    return pl.pallas_call(
        functools.partial(_ffn_kernel, tm=tm, bn=bn, dff=dff, final_norm=final_norm,
                          mixer=mixer is not None),
        grid=(bsz, seq // tm),
        in_specs=lead_specs + [_resident((1, d)), _resident((d, 2 * dff)),
                               _resident((FFN_CONV, 2 * dff)), _resident((1, 2 * dff)),
                               _resident((dff, d)), _resident((1, d))],
        out_specs=row,
        out_shape=jax.ShapeDtypeStruct((bsz, seq, d), F32),
        scratch_shapes=[pltpu.VMEM((SUBLANES, 2 * dff), F32),
                        pltpu.VMEM((2, 2, tm + SUBLANES, bn), F32)],
        compiler_params=_params(("arbitrary", "arbitrary")),
        name="conv_ffn_final" if final_norm else "conv_ffn",
    )(*lead_args, g.reshape(1, d), w_up.astype(BF16), conv_w, conv_b.reshape(1, 2 * dff),
      w_down.astype(BF16), final_g.reshape(1, d))


def _rglru_kernel(x_ref, g_ref, win_ref, cw_ref, cb_ref, wa_ref, ba_ref, wx_ref, bx_ref,
                  lam_ref, wout_ref, o_ref, ext_ref, halo_ref, hlast_ref, *, tm, drnn, nseq):
    seg = tm // SUBLANES
    shifts = RG_CONV - 1
    bw = drnn // RG_BLOCKS
    seqs = range(nseq)

    @pl.when(pl.program_id(1) == 0)
    def _():
        halo_ref[...] = jnp.zeros_like(halo_ref)
        hlast_ref[...] = jnp.zeros_like(hlast_ref)

    prow = lax.broadcasted_iota(jnp.int32, (tm, tm), 0)
    pcol = lax.broadcasted_iota(jnp.int32, (tm, tm), 1)
    perm = (pcol == (prow % SUBLANES) * seg + prow // SUBLANES).astype(BF16)
    unperm = (prow == (pcol % SUBLANES) * seg + pcol // SUBLANES).astype(BF16)
    first_sub = lax.broadcasted_iota(jnp.int32, (SUBLANES, 1), 0) == 0

    xs = [x_ref[b] for b in seqs]
    hn = [jnp.dot(perm, _rms(xs[b], g_ref[...]).astype(BF16),
                  preferred_element_type=F32).astype(BF16) for b in seqs]
    gate = [jnp.dot(hn[b], win_ref[:, 0:drnn], preferred_element_type=F32) for b in seqs]
    xr_raw = [jnp.dot(hn[b], win_ref[:, drnn:2 * drnn], preferred_element_type=F32) for b in seqs]

    xr = []
    for b in seqs:
        for r in range(shifts):
            lo = (seg - shifts + r) * SUBLANES
            cur = xr_raw[b][lo:lo + SUBLANES]
            ext_ref[b, r * SUBLANES:(r + 1) * SUBLANES, :] = jnp.where(
                first_sub, pltpu.roll(halo_ref[b, r], 1, 0), pltpu.roll(cur, 1, 0))
            halo_ref[b, r] = cur
        ext_ref[b, shifts * SUBLANES:, :] = xr_raw[b]
        acc = cb_ref[...]
        for i in range(RG_CONV):
            acc = acc + ext_ref[b, i * SUBLANES:i * SUBLANES + tm, :] * cw_ref[i:i + 1, :]
        xr.append(acc)

    ys = []
    for b in seqs:
        xb = xr[b].astype(BF16)
        ra, ri = [], []
        for n in range(RG_BLOCKS):
            xn = xb[:, n * bw:(n + 1) * bw]
            ra.append(jnp.dot(xn, wa_ref[n], preferred_element_type=F32))
            ri.append(jnp.dot(xn, wx_ref[n], preferred_element_type=F32))
        r = jax.nn.sigmoid(jnp.concatenate(ra, axis=1) + ba_ref[...])
        ig = jax.nn.sigmoid(jnp.concatenate(ri, axis=1) + bx_ref[...])
        log_a = r * lam_ref[...]
        a = jnp.exp(log_a)
        th = jnp.tanh(log_a)
        one_minus_a2 = -2.0 * th / (1.0 - th)
        u = jnp.sqrt(one_minus_a2) * (ig * xr[b])

        h_loc, a_cum = [u[0:SUBLANES]], [a[0:SUBLANES]]
        for j in range(1, seg):
            aj = a[j * SUBLANES:(j + 1) * SUBLANES]
            h_loc.append(aj * h_loc[-1] + u[j * SUBLANES:(j + 1) * SUBLANES])
            a_cum.append(aj * a_cum[-1])
        h_in = [hlast_ref[b]]
        for s in range(SUBLANES):
            h_in.append(h_loc[-1][s:s + 1] + a_cum[-1][s:s + 1] * h_in[-1])
        hlast_ref[b] = h_in[SUBLANES]
        h_seg = jnp.concatenate(h_in[:SUBLANES], axis=0)
        h = jnp.concatenate([h_loc[j] + a_cum[j] * h_seg for j in range(seg)], axis=0)
        ys.append((h * _gelu_tanh(gate[b])).astype(BF16))

    for b in seqs:
        y_t = jnp.dot(unperm, ys[b], preferred_element_type=F32).astype(BF16)
        o_ref[b] = xs[b] + jnp.dot(y_t, wout_ref[...], preferred_element_type=F32)


def _rglru(x, g, w_in, conv_w, conv_b, w_a, b_a, w_x, b_x, lam, w_out):
    bsz, seq, d = x.shape
    drnn = w_out.shape[0]
    bw = drnn // RG_BLOCKS
    tm = min(256, seq)
    nseq = 2 if bsz % 2 == 0 else 1
    row = pl.BlockSpec((nseq, tm, d), lambda b, i: (b, i, 0))
    lam_c = (-LRU_C) * jax.nn.softplus(-lam.astype(F32))
    return pl.pallas_call(
        functools.partial(_rglru_kernel, tm=tm, drnn=drnn, nseq=nseq),
        grid=(bsz // nseq, seq // tm),
        in_specs=[row, _resident((1, d)), _resident((d, 2 * drnn)), _resident((RG_CONV, drnn)),
                  _resident((1, drnn)), _resident((RG_BLOCKS, bw, bw)), _resident((1, drnn)),
                  _resident((RG_BLOCKS, bw, bw)), _resident((1, drnn)), _resident((1, drnn)),
                  _resident((drnn, d))],
        out_specs=row,
        out_shape=jax.ShapeDtypeStruct((bsz, seq, d), F32),
        scratch_shapes=[pltpu.VMEM((nseq, tm + (RG_CONV - 1) * SUBLANES, drnn), F32),
                        pltpu.VMEM((nseq, RG_CONV - 1, SUBLANES, drnn), F32),
                        pltpu.VMEM((nseq, 1, drnn), F32)],
        compiler_params=_params(("arbitrary", "arbitrary")),
        name="rglru",
    )(x, g.reshape(1, d), w_in.astype(BF16), conv_w, conv_b.reshape(1, drnn),
      w_a.astype(BF16), b_a.reshape(1, drnn), w_x.astype(BF16), b_x.reshape(1, drnn),
      lam_c.reshape(1, drnn), w_out.astype(BF16))


def kernel(x, mix_norm_g, ffn_norm_g, final_norm_g, ev_w_in, ev_w_gk2, ev_b_gk2, ev_gla_norm_g,
           ev_w_out, od_w_in, od_conv_w, od_conv_b, od_w_a, od_b_a, od_w_x, od_b_x, od_lambda,
           od_w_out, ffn_w_up, ffn_conv_w, ffn_conv_b, ffn_w_down):
    depth = mix_norm_g.shape[0]
    h = x
    for l in range(depth):
        j = l // 2
        mixer = None
        if l % 2 == 0:
            q, k, v, bias, bq, bk, bv, gl, og = _front0(h, mix_norm_g[l], ev_w_in[j], ev_w_gk2[j],
                                                      ev_b_gk2[j])
            o_a = _moba(q, k, v, bias)
            o_b = _gla(bq, bk, bv, gl, og, ev_gla_norm_g[j])
            mixer = (o_a, o_b, ev_w_out[j])
        else:
            h = _rglru(h, mix_norm_g[l], od_w_in[j], od_conv_w[j], od_conv_b[j], od_w_a[j],
                       od_b_a[j], od_w_x[j], od_b_x[j], od_lambda[j], od_w_out[j])
        h = _conv_ffn(h, ffn_norm_g[l], ffn_w_up[l], ffn_conv_w[l], ffn_conv_b[l], ffn_w_down[l],
                      final_norm_g, final_norm=(l == depth - 1), mixer=mixer)
    return h
```

```python
import functools

import jax
import jax.numpy as jnp
from jax import lax
from jax.experimental import pallas as pl
from jax.experimental.pallas import tpu as pltpu

F32 = jnp.float32
BF16 = jnp.bfloat16

A_HEADS = 8
A_HEAD_DIM = 64
A_WIDTH = A_HEADS * A_HEAD_DIM
MOBA_BLOCK = 256
MOBA_TOPK = 3
ROPE_THETA = 10000.0
B_HEADS = 8
B_HEAD_K = 32
B_HEAD_V = 64
B_KEY_WIDTH = B_HEADS * B_HEAD_K
B_VAL_WIDTH = B_HEADS * B_HEAD_V
GLA_GATE_RANK = 16
GLA_GATE_NORM = 16.0
RG_BLOCKS = 4
RG_CONV = 4
LRU_C = 8.0
FFN_CONV = 3
NORM_EPS = 1e-6
MASK_VALUE = -1e30

LANES = 128
SUBLANES = 8
BF16_ROWS = 16
LOG2_E = 1.4426950408889634
VMEM_LIMIT = 56 * 1024 * 1024

GLA_CHUNK = 64
GLA_SUB = 16
MOBA_BLOCKS_PAD = LANES
V_ROWS = A_HEAD_DIM + BF16_ROWS


def _params(sem):
    return pltpu.CompilerParams(dimension_semantics=sem, vmem_limit_bytes=VMEM_LIMIT)


def _rms(x, g):
    ms = jnp.mean(x * x, axis=-1, keepdims=True)
    return x * lax.rsqrt(ms + NORM_EPS) * g


def _gelu_tanh(x):
    return 0.5 * x * (1.0 + jnp.tanh(0.7978845608028654 * (x + 0.044715 * (x * x * x))))


def _log_sigmoid(z):
    return jnp.minimum(z, 0.0) - jnp.log1p(jnp.exp(-jnp.abs(z)))


def _moba_gate_rows(nb):
    assert nb <= MOBA_BLOCKS_PAD
    return -(-nb // BF16_ROWS) * BF16_ROWS


def _resident(shape):
    nd = len(shape)
    return pl.BlockSpec(shape, lambda *_: (0,) * nd, pipeline_mode=pl.Buffered(1))


def _front0_kernel(x_ref, g_ref, w_ref, wt_ref, wgk_ref, bgk_ref, cos_ref, sa_ref, sb_ref,
                   cost_ref, sat_ref, sbt_ref,
                   q_ref, k_ref, v_ref, bias_ref, bq_ref, bk_ref, bv_ref, gl_ref, og_ref,
                   km_ref, *, tm):
    @pl.when(pl.program_id(1) == 0)
    def _():
        km_ref[...] = jnp.zeros_like(km_ref)

    hn = _rms(x_ref[0], g_ref[...]).astype(BF16)

    def proj(c0, n):
        return jnp.dot(hn, w_ref[:, c0:c0 + n], preferred_element_type=F32)

    def proj_t(r0, n):
        return lax.dot_general(wt_ref[r0:r0 + n, :], hn, (((1,), (1,)), ((), ())),
                               preferred_element_type=F32)

    def rot(t, cos, sa, sb, axis):
        parts = []
        for c in range(t.shape[axis] // LANES):
            tc = lax.slice_in_dim(t, c * LANES, (c + 1) * LANES, axis=axis)
            parts.append(tc * cos + pltpu.roll(tc, LANES - 32, axis) * sa
                         + pltpu.roll(tc, 32, axis) * sb)
        return jnp.concatenate(parts, axis=axis)

    qt = rot(proj_t(0, A_WIDTH), cost_ref[...], sat_ref[...], sbt_ref[...], 0)
    q_ref[0] = (qt * (A_HEAD_DIM ** -0.5 * LOG2_E)).astype(BF16)
    vt = proj_t(A_WIDTH, A_WIDTH)
    c = 0
    k = rot(proj(c, A_WIDTH), cos_ref[...], sa_ref[...], sb_ref[...], 1); c += A_WIDTH
    k_ref[0] = k.astype(BF16)
    nblk = tm // MOBA_BLOCK
    kmean = jnp.mean(k.reshape(nblk, MOBA_BLOCK, A_WIDTH), axis=1)
    blk0 = pl.program_id(1) * nblk
    for n in range(nblk):
        km_ref[pl.ds(blk0 + n, 1), :] = kmean[n:n + 1]

    nbr = km_ref.shape[0]
    feat = lax.broadcasted_iota(jnp.int32, (LANES, 1), 0)
    brow = lax.broadcasted_iota(jnp.int32, (nbr, 2 * tm), 0)
    browf = brow.astype(F32)
    q_blk = blk0 + (lax.broadcasted_iota(jnp.int32, (1, 2 * tm), 1) % tm) // MOBA_BLOCK

    def split(a):
        hi = a.astype(BF16)
        return hi, (a - hi.astype(F32)).astype(BF16)

    def gate_pair(hp):
        q_pair = qt[hp * LANES:(hp + 1) * LANES]
        q2 = jnp.concatenate([jnp.where(feat // A_HEAD_DIM == h, q_pair, 0.0) for h in range(2)],
                             axis=1)
        km_hi, km_lo = split(km_ref[:, hp * LANES:(hp + 1) * LANES])
        q_hi, q_lo = split(q2)
        gate = jnp.dot(jnp.concatenate([km_hi, km_hi, km_lo], axis=1),
                       jnp.concatenate([q_hi, q_lo, q_hi], axis=0), preferred_element_type=F32)
        g = jnp.where(brow < q_blk, gate, -jnp.inf)
        bias = jnp.full((nbr, 2 * tm), MASK_VALUE, F32)
        for _ in range(MOBA_TOPK):
            mx = jnp.max(g, axis=0, keepdims=True)
            first = jnp.min(jnp.where(g == mx, browf, float(nbr)), axis=0, keepdims=True)
            hit = browf == first
            bias = jnp.where(hit & (mx > -jnp.inf), 0.0, bias)
            g = jnp.where(hit, -jnp.inf, g)
        for h in range(2):
            bias_ref[0, 2 * hp + h] = bias[:, h * tm:(h + 1) * tm].astype(BF16)

    ones_rows = (lax.broadcasted_iota(jnp.int32, (V_ROWS - A_HEAD_DIM, MOBA_BLOCK), 0)
                 == 0).astype(BF16)
    for h in range(A_HEADS):
        for n in range(nblk):
            v_ref[0, h, n, 0:A_HEAD_DIM, :] = vt[h * A_HEAD_DIM:(h + 1) * A_HEAD_DIM,
                                                 n * MOBA_BLOCK:(n + 1) * MOBA_BLOCK].astype(BF16)
            v_ref[0, h, n, A_HEAD_DIM:, :] = ones_rows
    gate_pair(0)
    bq_ref[0] = proj(c, B_KEY_WIDTH) * (B_HEAD_K ** -0.5); c += B_KEY_WIDTH
    bk_ref[0] = proj(c, B_KEY_WIDTH); c += B_KEY_WIDTH
    gate_pair(1)
    bv_ref[0] = proj(c, B_VAL_WIDTH); c += B_VAL_WIDTH
    gate_pair(2)
    og_ref[0] = proj(c, B_VAL_WIDTH); c += B_VAL_WIDTH
    gate_pair(3)
    gk = proj(c, LANES).astype(BF16)
    z = jnp.dot(gk, wgk_ref[...], preferred_element_type=F32) + bgk_ref[...]
    gl_ref[0] = _log_sigmoid(z) / GLA_GATE_NORM


def _front0(x, g, w_in, w_gk2, b_gk2):
    bsz, seq, d = x.shape
    tm = min(512, seq)
    widths = [A_WIDTH, A_WIDTH, A_WIDTH, B_KEY_WIDTH, B_KEY_WIDTH, B_VAL_WIDTH, GLA_GATE_RANK]
    aq, ak, av, bq, bk, bv, bgk, bog = jnp.split(
        w_in, [sum(widths[:n + 1]) for n in range(len(widths))], axis=1)
    gk_pad = LANES - GLA_GATE_RANK
    w = jnp.concatenate([ak, bq, bk, bv, bog, jnp.pad(bgk, ((0, 0), (0, gk_pad)))],
                        axis=1).astype(BF16)
    wt = jnp.concatenate([aq, av], axis=1).T.astype(BF16)
    wgk = jnp.pad(w_gk2, ((0, gk_pad), (0, 0))).astype(BF16)
    ncol = w.shape[1]

    half = A_HEAD_DIM // 2
    inv = ROPE_THETA ** (-jnp.arange(half, dtype=F32) / half)
    pos = jnp.arange(seq, dtype=jnp.int32).astype(F32)
    first = (jnp.arange(LANES) % A_HEAD_DIM) < half

    def tables(feature_major):
        if feature_major:
            ang, reps, sel = inv[:, None] * pos[None, :], (LANES // half, 1), first[:, None]
        else:
            ang, reps, sel = pos[:, None] * inv[None, :], (1, LANES // half), first[None, :]
        cos, sin = jnp.tile(jnp.cos(ang), reps), jnp.tile(jnp.sin(ang), reps)
        return cos, jnp.where(sel, -sin, 0.0), jnp.where(sel, 0.0, sin)

    nb = seq // MOBA_BLOCK
    nbr = _moba_gate_rows(nb)
    row = lambda n: pl.BlockSpec((1, tm, n), lambda b, i: (b, i, 0))
    tab = pl.BlockSpec((tm, LANES), lambda b, i: (i, 0))
    tab_t = pl.BlockSpec((LANES, tm), lambda b, i: (0, i))
    nt = seq // tm
    outs = pl.pallas_call(
        functools.partial(_front0_kernel, tm=tm),
        grid=(bsz, nt),
        in_specs=[row(d), _resident((1, d)), _resident((d, ncol)), _resident((2 * A_WIDTH, d)),
                  _resident((LANES, B_KEY_WIDTH)), _resident((1, B_KEY_WIDTH)),
                  tab, tab, tab, tab_t, tab_t, tab_t],
        out_specs=[pl.BlockSpec((1, A_WIDTH, tm), lambda b, i: (b, 0, i)), row(A_WIDTH),
                   pl.BlockSpec((1, A_HEADS, tm // MOBA_BLOCK, V_ROWS, MOBA_BLOCK),
                                lambda b, i: (b, 0, i, 0, 0)),
                   pl.BlockSpec((1, A_HEADS, nbr, tm), lambda b, i: (b, 0, 0, i)),
                   row(B_KEY_WIDTH), row(B_KEY_WIDTH), row(B_VAL_WIDTH), row(B_KEY_WIDTH),
                   row(B_VAL_WIDTH)],
        out_shape=[jax.ShapeDtypeStruct((bsz, A_WIDTH, seq), BF16),
                   jax.ShapeDtypeStruct((bsz, seq, A_WIDTH), BF16),
                   jax.ShapeDtypeStruct((bsz, A_HEADS, nb, V_ROWS, MOBA_BLOCK), BF16),
                   jax.ShapeDtypeStruct((bsz, A_HEADS, nbr, seq), BF16),
                   jax.ShapeDtypeStruct((bsz, seq, B_KEY_WIDTH), F32),
                   jax.ShapeDtypeStruct((bsz, seq, B_KEY_WIDTH), F32),
                   jax.ShapeDtypeStruct((bsz, seq, B_VAL_WIDTH), F32),
                   jax.ShapeDtypeStruct((bsz, seq, B_KEY_WIDTH), F32),
                   jax.ShapeDtypeStruct((bsz, seq, B_VAL_WIDTH), F32)],
        scratch_shapes=[pltpu.VMEM((nbr, A_WIDTH), F32)],
        compiler_params=_params(("arbitrary", "arbitrary")),
        name="front0",
    )(x, g.reshape(1, d), w, wt, wgk, b_gk2.reshape(1, B_KEY_WIDTH), *tables(False), *tables(True))
    return outs


def _moba_kernel(qt_ref, k_ref, vt_ref, bias_ref, o_ref, rhs_ref, acc_ref, sa_ref, sb_ref, *,
                 group, nb):
    qb = pl.program_id(2)
    blk = MOBA_BLOCK
    qt = qt_ref[0]
    feat = lax.broadcasted_iota(jnp.int32, (LANES, 1), 0)
    nbr = bias_ref.shape[2]
    for h in range(2):
        rhs_ref[:, h * blk:(h + 1) * blk] = jnp.concatenate(
            [jnp.where(feat // A_HEAD_DIM == h, qt, jnp.zeros_like(qt)), bias_ref[0, h],
             jnp.full((MOBA_BLOCKS_PAD - nbr, blk), MASK_VALUE, BF16)], axis=0)

    keys = group * blk
    blk_of_row = lax.broadcasted_iota(jnp.int32, (keys, MOBA_BLOCKS_PAD), 0) // blk
    lane_blk = lax.broadcasted_iota(jnp.int32, (keys, MOBA_BLOCKS_PAD), 1)

    def scores(g, s_ref):
        j0 = g * group
        off = pl.multiple_of(jnp.minimum(j0, nb - group) * blk, blk)
        onehot = (lane_blk == blk_of_row + j0).astype(BF16)
        lhs_j = jnp.concatenate([k_ref[0, pl.ds(off, keys), :], onehot], axis=1)
        s = jnp.dot(lhs_j, rhs_ref[...], preferred_element_type=F32)
        s_ref[...] = s
        return jnp.max(s, axis=0, keepdims=True)

    def accumulate(g, s_ref, m, s_max):
        j0 = g * group
        m_new = jnp.maximum(m, s_max)
        alpha = jnp.exp2(m - m_new)
        p = jnp.exp2(s_ref[...] - m_new).astype(BF16)
        for h in range(2):
            vt = jnp.concatenate([vt_ref[0, h, jnp.minimum(j0 + u, nb - 1)] for u in range(group)],
                                 axis=1)
            acc_ref[h] = (alpha[:, h * blk:(h + 1) * blk] * acc_ref[h]
                          + jnp.dot(vt, p[:, h * blk:(h + 1) * blk], preferred_element_type=F32))
        return m_new

    def body(i, carry):
        m, max_a = carry
        max_b = scores(2 * i + 1, sb_ref)
        m = accumulate(2 * i, sa_ref, m, max_a)
        max_a = scores(2 * i + 2, sa_ref)
        m = accumulate(2 * i + 1, sb_ref, m, max_b)
        return m, max_a

    max_0 = scores(0, sa_ref)

    own = pl.multiple_of(qb * blk, blk)
    lhs = jnp.concatenate([k_ref[0, pl.ds(own, blk), :], jnp.zeros((blk, MOBA_BLOCKS_PAD), BF16)],
                          axis=1)
    causal = (lax.broadcasted_iota(jnp.int32, (blk, 2 * blk), 0)
              <= lax.broadcasted_iota(jnp.int32, (blk, 2 * blk), 1) % blk)
    s = jnp.where(causal, jnp.dot(lhs, rhs_ref[...], preferred_element_type=F32), MASK_VALUE)
    m0 = jnp.max(s, axis=0, keepdims=True)
    p = jnp.exp2(s - m0).astype(BF16)
    for h in range(2):
        acc_ref[h] = jnp.dot(vt_ref[0, h, qb], p[:, h * blk:(h + 1) * blk],
                             preferred_element_type=F32)

    n_groups = (qb + group - 1) // group
    m, max_a = lax.fori_loop(0, n_groups // 2, body, (m0, max_0))

    @pl.when(n_groups % 2 == 1)
    def _():
        accumulate(n_groups - 1, sa_ref, m, max_a)

    outs = []
    for h in range(2):
        a = acc_ref[h]
        outs.append(a[0:A_HEAD_DIM] / a[A_HEAD_DIM:A_HEAD_DIM + 1])
    o_ref[0] = jnp.concatenate(outs, axis=0).T


def _moba(qt, k, vt, bias):
    bsz, seq, _ = k.shape
    nb = seq // MOBA_BLOCK
    nbr = _moba_gate_rows(nb)
    npair = A_WIDTH // LANES
    group = 4 if nb % 4 == 0 else 1
    return pl.pallas_call(
        functools.partial(_moba_kernel, group=group, nb=nb),
        grid=(bsz, npair, nb),
        in_specs=[pl.BlockSpec((1, LANES, MOBA_BLOCK), lambda b, hp, i: (b, hp, i)),
                  pl.BlockSpec((1, seq, LANES), lambda b, hp, i: (b, 0, hp)),
                  pl.BlockSpec((1, 2, nb, V_ROWS, MOBA_BLOCK), lambda b, hp, i: (b, hp, 0, 0, 0)),
                  pl.BlockSpec((1, 2, nbr, MOBA_BLOCK), lambda b, hp, i: (b, hp, 0, i))],
        out_specs=pl.BlockSpec((1, MOBA_BLOCK, LANES), lambda b, hp, i: (b, i, hp)),
        out_shape=jax.ShapeDtypeStruct((bsz, seq, A_WIDTH), F32),
        scratch_shapes=[pltpu.VMEM((2 * LANES, 2 * MOBA_BLOCK), BF16),
                        pltpu.VMEM((2, V_ROWS, MOBA_BLOCK), F32),
                        pltpu.VMEM((group * MOBA_BLOCK, 2 * MOBA_BLOCK), F32),
                        pltpu.VMEM((group * MOBA_BLOCK, 2 * MOBA_BLOCK), F32)],
        compiler_params=_params(("arbitrary", "arbitrary", "arbitrary")),
        name="moba",
    )(qt, k, vt, bias)


def _gla_kernel(q_ref, k_ref, v_ref, g_ref, og_ref, ng_ref, o_ref, st_ref, *, tm):
    @pl.when(pl.program_id(1) == 0)
    def _():
        st_ref[...] = jnp.zeros_like(st_ref)

    q, k, v, g = q_ref[0], k_ref[0], v_ref[0], g_ref[0]
    kw, vw = B_KEY_WIDTH, B_VAL_WIDTH
    row = lax.broadcasted_iota(jnp.int32, (tm, 1), 0)
    r_sub = row % GLA_SUB
    r_chk = row % GLA_CHUNK

    def seg_cumsum(x, r, length):
        s = 1
        while s < length:
            x = x + jnp.where(r >= s, pltpu.roll(x, s, 0), 0.0)
            s *= 2
        return x

    c_sub = seg_cumsum(g, r_sub, GLA_SUB)
    b = seg_cumsum(g, r_chk, GLA_CHUNK)

    head_kv = (lax.broadcasted_iota(jnp.int32, (kw, vw), 0) // B_HEAD_K
               == lax.broadcasted_iota(jnp.int32, (kw, vw), 1) // B_HEAD_V).astype(BF16)
    head_vk = (lax.broadcasted_iota(jnp.int32, (vw, kw), 0) // B_HEAD_V
               == lax.broadcasted_iota(jnp.int32, (vw, kw), 1) // B_HEAD_K)
    head_vv = (lax.broadcasted_iota(jnp.int32, (vw, vw), 0) // B_HEAD_V
               == lax.broadcasted_iota(jnp.int32, (vw, vw), 1) // B_HEAD_V).astype(BF16)

    o_acc = jnp.zeros((tm, vw), F32)
    for d in range(GLA_SUB):
        if d == 0:
            w = q * k
            vv = v
        else:
            w = jnp.where(r_sub >= d,
                          q * pltpu.roll(k, d, 0) * jnp.exp(c_sub - pltpu.roll(c_sub, d, 0)), 0.0)
            vv = pltpu.roll(v, d, 0)
        o_acc = o_acc + jnp.dot(w.astype(BF16), head_kv, preferred_element_type=F32) * vv

    q_sub = q * jnp.exp(c_sub)
    q_chk = q * jnp.exp(b)

    nsub = GLA_CHUNK // GLA_SUB
    rr = lax.broadcasted_iota(jnp.int32, (GLA_CHUNK, 1), 0)
    lane_k = lax.broadcasted_iota(jnp.int32, (1, (nsub - 1) * kw), 1)
    lane_v = lax.broadcasted_iota(jnp.int32, (1, vw), 1)
    outs = []
    for c in range(tm // GLA_CHUNK):
        lo = c * GLA_CHUNK
        bc = b[lo:lo + GLA_CHUNK]
        kc = k[lo:lo + GLA_CHUNK]
        vc = v[lo:lo + GLA_CHUNK].astype(BF16)
        qs = q_sub[lo:lo + GLA_CHUNK]

        rhs_parts, lhs_parts = [], []
        for i in range(1, nsub):
            b_end = bc[i * GLA_SUB - 1:i * GLA_SUB, :]
            decay = jnp.exp(jnp.minimum(b_end - bc, 0.0))
            rhs_parts.append(jnp.where(rr < i * GLA_SUB, kc * decay, 0.0).astype(BF16))
            lhs_parts.append(jnp.where(rr // GLA_SUB == i, qs, 0.0).astype(BF16))
        rhs = jnp.concatenate(rhs_parts, axis=1)
        lhs_t = jnp.concatenate(lhs_parts, axis=1)
        lhs = jnp.concatenate(
            [jnp.where((lane_k % kw) // B_HEAD_K == h, lhs_t, jnp.zeros_like(lhs_t))
             for h in range(B_HEADS)], axis=0)
        attn = lax.dot_general(lhs, rhs, (((1,), (1,)), ((), ())), preferred_element_type=F32)
        res = jnp.dot(attn.astype(BF16), vc, preferred_element_type=F32)
        o_off = jnp.zeros((GLA_CHUNK, vw), F32)
        for h in range(B_HEADS):
            o_off = o_off + jnp.where(lane_v // B_HEAD_V == h,
                                      res[h * GLA_CHUNK:(h + 1) * GLA_CHUNK], 0.0)

        st = st_ref[...]
        o_int = lax.dot_general(q_chk[lo:lo + GLA_CHUNK].astype(BF16), st.astype(BF16),
                                (((1,), (1,)), ((), ())), preferred_element_type=F32)
        b_last = bc[GLA_CHUNK - 1:GLA_CHUNK, :]
        ks = (kc * jnp.exp(b_last - bc)).astype(BF16)
        upd = lax.dot_general(vc, ks, (((0,), (0,)), ((), ())), preferred_element_type=F32)
        st_ref[...] = st * jnp.exp(b_last) + jnp.where(head_vk, upd, 0.0)
        outs.append(o_off + o_int)

    o = o_acc + jnp.concatenate(outs, axis=0)

    sq = o * o
    sq_hi = sq.astype(BF16)
    sq_lo = (sq - sq_hi.astype(F32)).astype(BF16)
    ms = (jnp.dot(sq_hi, head_vv, preferred_element_type=F32)
          + jnp.dot(sq_lo, head_vv, preferred_element_type=F32)) * (1.0 / B_HEAD_V)
    og = og_ref[0]
    o_ref[0] = (o * lax.rsqrt(ms + NORM_EPS) * ng_ref[...]) * (og * jax.nn.sigmoid(og))


def _gla(bq, bk, bv, gl, og, norm_g):
    bsz, seq, _ = bq.shape
    tm = min(256, seq)
    row = lambda n: pl.BlockSpec((1, tm, n), lambda b, i: (b, i, 0))
    return pl.pallas_call(
        functools.partial(_gla_kernel, tm=tm),
        grid=(bsz, seq // tm),
        in_specs=[row(B_KEY_WIDTH), row(B_KEY_WIDTH), row(B_VAL_WIDTH), row(B_KEY_WIDTH),
                  row(B_VAL_WIDTH), _resident((1, B_VAL_WIDTH))],
        out_specs=row(B_VAL_WIDTH),
        out_shape=jax.ShapeDtypeStruct((bsz, seq, B_VAL_WIDTH), F32),
        scratch_shapes=[pltpu.VMEM((B_VAL_WIDTH, B_KEY_WIDTH), F32)],
        compiler_params=_params(("arbitrary", "arbitrary")),
        name="gla",
    )(bq, bk, bv, gl, og, norm_g.reshape(1, B_VAL_WIDTH))


def _ffn_kernel(*refs, tm, bn, dff, final_norm, mixer):
    if mixer:
        x_ref, oa_ref, ob_ref, wa_ref, wb_ref = refs[:5]
        refs = refs[5:]
    else:
        x_ref, refs = refs[0], refs[1:]
    g_ref, wup_ref, cw_ref, cb_ref, wdn_ref, fg_ref, o_ref, halo_ref, ext_ref = refs
    nkb = dff // bn

    @pl.when(pl.program_id(1) == 0)
    def _():
        halo_ref[...] = jnp.zeros_like(halo_ref)

    x = x_ref[0]
    if mixer:
        o_ref[0] = (jnp.dot(oa_ref[0].astype(BF16), wa_ref[...], preferred_element_type=F32)
                    + jnp.dot(ob_ref[0].astype(BF16), wb_ref[...], preferred_element_type=F32))
        x = x + o_ref[0]
    hn = _rms(x, g_ref[...]).astype(BF16)

    def up_stage(kb):
        for part in range(2):
            c0 = part * dff + kb * bn
            up = jnp.dot(hn, wup_ref[:, c0:c0 + bn], preferred_element_type=F32)
            ext_ref[kb % 2, part, 0:SUBLANES, :] = halo_ref[:, c0:c0 + bn]
            ext_ref[kb % 2, part, SUBLANES:, :] = up
            halo_ref[:, c0:c0 + bn] = up[tm - SUBLANES:, :]

    def gate_stage(kb):
        halves = []
        for part in range(2):
            c0 = part * dff + kb * bn
            y = cb_ref[:, c0:c0 + bn]
            for i in range(FFN_CONV):
                off = SUBLANES - (FFN_CONV - 1) + i
                y = y + ext_ref[kb % 2, part, pl.ds(off, tm), :] * cw_ref[i:i + 1, c0:c0 + bn]
            halves.append(y)
        return (halves[0] * _gelu_tanh(halves[1])).astype(BF16)

    up_stage(0)
    if nkb > 1:
        up_stage(1)
    a = gate_stage(0)
    acc = x
    for kb in range(nkb):
        if kb + 2 < nkb:
            up_stage(kb + 2)
        a_next = gate_stage(kb + 1) if kb + 1 < nkb else None
        acc = acc + jnp.dot(a, wdn_ref[kb * bn:(kb + 1) * bn, :], preferred_element_type=F32)
        a = a_next
    if final_norm:
        acc = _rms(acc, fg_ref[...])
    o_ref[0] = acc


def _conv_ffn(x, g, w_up, conv_w, conv_b, w_down, final_g, final_norm, mixer=None):
    bsz, seq, d = x.shape
    dff = w_down.shape[0]
    tm = min(512, seq)
    bn = 512
    rows = lambda n: pl.BlockSpec((1, tm, n), lambda b, i: (b, i, 0))
    row = rows(d)
    lead_specs, lead_args = [row], [x]
    if mixer is not None:
        o_a, o_b, w_out = mixer
        w = w_out.astype(BF16)
        lead_specs += [rows(A_WIDTH), rows(B_VAL_WIDTH),
                       _resident((A_WIDTH, d)), _resident((B_VAL_WIDTH, d))]
        lead_args += [o_a, o_b, w[:A_WIDTH], w[A_WIDTH:]]
    return pl.pallas_call(
        functools.partial(_ffn_kernel, tm=tm, bn=bn, dff=dff, final_norm=final_norm,
                          mixer=mixer is not None),
        grid=(bsz, seq // tm),
        in_specs=lead_specs + [_resident((1, d)), _resident((d, 2 * dff)),
                               _resident((FFN_CONV, 2 * dff)), _resident((1, 2 * dff)),
                               _resident((dff, d)), _resident((1, d))],
        out_specs=row,
        out_shape=jax.ShapeDtypeStruct((bsz, seq, d), F32),
        scratch_shapes=[pltpu.VMEM((SUBLANES, 2 * dff), F32),
                        pltpu.VMEM((2, 2, tm + SUBLANES, bn), F32)],
        compiler_params=_params(("arbitrary", "arbitrary")),
        name="conv_ffn_final" if final_norm else "conv_ffn",
    )(*lead_args, g.reshape(1, d), w_up.astype(BF16), conv_w, conv_b.reshape(1, 2 * dff),
      w_down.astype(BF16), final_g.reshape(1, d))


def _rglru_kernel(x_ref, g_ref, win_ref, cw_ref, cb_ref, wa_ref, ba_ref, wx_ref, bx_ref,
                  lam_ref, wout_ref, o_ref, ext_ref, halo_ref, hlast_ref, *, tm, drnn, nseq):
    seg = tm // SUBLANES
    shifts = RG_CONV - 1
    bw = drnn // RG_BLOCKS
    seqs = range(nseq)

    @pl.when(pl.program_id(1) == 0)
    def _():
        halo_ref[...] = jnp.zeros_like(halo_ref)
        hlast_ref[...] = jnp.zeros_like(hlast_ref)

    prow = lax.broadcasted_iota(jnp.int32, (tm, tm), 0)
    pcol = lax.broadcasted_iota(jnp.int32, (tm, tm), 1)
    perm = (pcol == (prow % SUBLANES) * seg + prow // SUBLANES).astype(BF16)
    unperm = (prow == (pcol % SUBLANES) * seg + pcol // SUBLANES).astype(BF16)
    first_sub = lax.broadcasted_iota(jnp.int32, (SUBLANES, 1), 0) == 0

    xs = [x_ref[b] for b in seqs]
    hn = [jnp.dot(perm, _rms(xs[b], g_ref[...]).astype(BF16),
                  preferred_element_type=F32).astype(BF16) for b in seqs]
    gate = [jnp.dot(hn[b], win_ref[:, 0:drnn], preferred_element_type=F32) for b in seqs]
    xr_raw = [jnp.dot(hn[b], win_ref[:, drnn:2 * drnn], preferred_element_type=F32) for b in seqs]

    xr = []
    for b in seqs:
        for r in range(shifts):
            lo = (seg - shifts + r) * SUBLANES
            cur = xr_raw[b][lo:lo + SUBLANES]
            ext_ref[b, r * SUBLANES:(r + 1) * SUBLANES, :] = jnp.where(
                first_sub, pltpu.roll(halo_ref[b, r], 1, 0), pltpu.roll(cur, 1, 0))
            halo_ref[b, r] = cur
        ext_ref[b, shifts * SUBLANES:, :] = xr_raw[b]
        acc = cb_ref[...]
        for i in range(RG_CONV):
            acc = acc + ext_ref[b, i * SUBLANES:i * SUBLANES + tm, :] * cw_ref[i:i + 1, :]
        xr.append(acc)

    ys = []
    for b in seqs:
        xb = xr[b].astype(BF16)
        ra, ri = [], []
        for n in range(RG_BLOCKS):
            xn = xb[:, n * bw:(n + 1) * bw]
            ra.append(jnp.dot(xn, wa_ref[n], preferred_element_type=F32))
            ri.append(jnp.dot(xn, wx_ref[n], preferred_element_type=F32))
        r = jax.nn.sigmoid(jnp.concatenate(ra, axis=1) + ba_ref[...])
        ig = jax.nn.sigmoid(jnp.concatenate(ri, axis=1) + bx_ref[...])
        log_a = r * lam_ref[...]
        a = jnp.exp(log_a)
        th = jnp.tanh(log_a)
        one_minus_a2 = -2.0 * th / (1.0 - th)
        u = jnp.sqrt(one_minus_a2) * (ig * xr[b])

        h_loc, a_cum = [u[0:SUBLANES]], [a[0:SUBLANES]]
        for j in range(1, seg):
            aj = a[j * SUBLANES:(j + 1) * SUBLANES]
            h_loc.append(aj * h_loc[-1] + u[j * SUBLANES:(j + 1) * SUBLANES])
            a_cum.append(aj * a_cum[-1])
        h_in = [hlast_ref[b]]
        for s in range(SUBLANES):
            h_in.append(h_loc[-1][s:s + 1] + a_cum[-1][s:s + 1] * h_in[-1])
        hlast_ref[b] = h_in[SUBLANES]
        h_seg = jnp.concatenate(h_in[:SUBLANES], axis=0)
        h = jnp.concatenate([h_loc[j] + a_cum[j] * h_seg for j in range(seg)], axis=0)
        ys.append((h * _gelu_tanh(gate[b])).astype(BF16))

    for b in seqs:
        y_t = jnp.dot(unperm, ys[b], preferred_element_type=F32).astype(BF16)
        o_ref[b] = xs[b] + jnp.dot(y_t, wout_ref[...], preferred_element_type=F32)


def _rglru(x, g, w_in, conv_w, conv_b, w_a, b_a, w_x, b_x, lam, w_out):
    bsz, seq, d = x.shape
    drnn = w_out.shape[0]
    bw = drnn // RG_BLOCKS
    tm = min(256, seq)
    nseq = 2 if bsz % 2 == 0 else 1
    row = pl.BlockSpec((nseq, tm, d), lambda b, i: (b, i, 0))
    lam_c = (-LRU_C) * jax.nn.softplus(-lam.astype(F32))
    return pl.pallas_call(
        functools.partial(_rglru_kernel, tm=tm, drnn=drnn, nseq=nseq),
        grid=(bsz // nseq, seq // tm),
        in_specs=[row, _resident((1, d)), _resident((d, 2 * drnn)), _resident((RG_CONV, drnn)),
                  _resident((1, drnn)), _resident((RG_BLOCKS, bw, bw)), _resident((1, drnn)),
                  _resident((RG_BLOCKS, bw, bw)), _resident((1, drnn)), _resident((1, drnn)),
                  _resident((drnn, d))],
        out_specs=row,
        out_shape=jax.ShapeDtypeStruct((bsz, seq, d), F32),
        scratch_shapes=[pltpu.VMEM((nseq, tm + (RG_CONV - 1) * SUBLANES, drnn), F32),
                        pltpu.VMEM((nseq, RG_CONV - 1, SUBLANES, drnn), F32),
                        pltpu.VMEM((nseq, 1, drnn), F32)],
        compiler_params=_params(("arbitrary", "arbitrary")),
        name="rglru",
    )(x, g.reshape(1, d), w_in.astype(BF16), conv_w, conv_b.reshape(1, drnn),
      w_a.astype(BF16), b_a.reshape(1, drnn), w_x.astype(BF16), b_x.reshape(1, drnn),
      lam_c.reshape(1, drnn), w_out.astype(BF16))


def kernel(x, mix_norm_g, ffn_norm_g, final_norm_g, ev_w_in, ev_w_gk2, ev_b_gk2, ev_gla_norm_g,
           ev_w_out, od_w_in, od_conv_w, od_conv_b, od_w_a, od_b_a, od_w_x, od_b_x, od_lambda,
           od_w_out, ffn_w_up, ffn_conv_w, ffn_conv_b, ffn_w_down):
    depth = mix_norm_g.shape[0]
    h = x
    for l in range(depth):
        j = l // 2
        mixer = None
        if l % 2 == 0:
            q, k, v, bias, bq, bk, bv, gl, og = _front0(h, mix_norm_g[l], ev_w_in[j], ev_w_gk2[j],
                                                      ev_b_gk2[j])
            o_a = _moba(q, k, v, bias)
            o_b = _gla(bq, bk, bv, gl, og, ev_gla_norm_g[j])
            mixer = (o_a, o_b, ev_w_out[j])
        else:
            h = _rglru(h, mix_norm_g[l], od_w_in[j], od_conv_w[j], od_conv_b[j], od_w_a[j],
                       od_b_a[j], od_w_x[j], od_b_x[j], od_lambda[j], od_w_out[j])
        h = _conv_ffn(h, ffn_norm_g[l], ffn_w_up[l], ffn_conv_w[l], ffn_conv_b[l], ffn_w_down[l],
                      final_norm_g, final_norm=(l == depth - 1), mixer=mixer)
    return h
```

```python
import functools

import jax
import jax.numpy as jnp
from jax import lax
from jax.experimental import pallas as pl
from jax.experimental.pallas import tpu as pltpu

F32 = jnp.float32
BF16 = jnp.bfloat16

A_HEADS = 8
A_HEAD_DIM = 64
A_WIDTH = A_HEADS * A_HEAD_DIM
MOBA_BLOCK = 256
MOBA_TOPK = 3
ROPE_THETA = 10000.0
B_HEADS = 8
B_HEAD_K = 32
B_HEAD_V = 64
B_KEY_WIDTH = B_HEADS * B_HEAD_K
B_VAL_WIDTH = B_HEADS * B_HEAD_V
GLA_GATE_RANK = 16
GLA_GATE_NORM = 16.0
RG_BLOCKS = 4
RG_CONV = 4
LRU_C = 8.0
FFN_CONV = 3
NORM_EPS = 1e-6
MASK_VALUE = -1e30

LANES = 128
SUBLANES = 8
BF16_ROWS = 16
LOG2_E = 1.4426950408889634
VMEM_LIMIT = 56 * 1024 * 1024

GLA_CHUNK = 64
GLA_SUB = 16
MOBA_BLOCKS_PAD = LANES
V_ROWS = A_HEAD_DIM + BF16_ROWS


def _params(sem):
    return pltpu.CompilerParams(dimension_semantics=sem, vmem_limit_bytes=VMEM_LIMIT)


def _rms(x, g):
    ms = jnp.mean(x * x, axis=-1, keepdims=True)
    return x * lax.rsqrt(ms + NORM_EPS) * g


def _gelu_tanh(x):
    return 0.5 * x * (1.0 + jnp.tanh(0.7978845608028654 * (x + 0.044715 * (x * x * x))))


def _log_sigmoid(z):
    return jnp.minimum(z, 0.0) - jnp.log1p(jnp.exp(-jnp.abs(z)))


def _moba_gate_rows(nb):
    assert nb <= MOBA_BLOCKS_PAD
    return -(-nb // BF16_ROWS) * BF16_ROWS


def _resident(shape):
    nd = len(shape)
    return pl.BlockSpec(shape, lambda *_: (0,) * nd, pipeline_mode=pl.Buffered(1))


def _front0_kernel(x_ref, g_ref, w_ref, wt_ref, wgk_ref, bgk_ref, cos_ref, sa_ref, sb_ref,
                   cost_ref, sat_ref, sbt_ref,
                   q_ref, k_ref, v_ref, bias_ref, bq_ref, bk_ref, bv_ref, gl_ref, og_ref,
                   km_ref, *, tm):
    @pl.when(pl.program_id(1) == 0)
    def _():
        km_ref[...] = jnp.zeros_like(km_ref)

    hn = _rms(x_ref[0], g_ref[...]).astype(BF16)

    def proj(c0, n):
        return jnp.dot(hn, w_ref[:, c0:c0 + n], preferred_element_type=F32)

    def proj_t(r0, n):
        return lax.dot_general(wt_ref[r0:r0 + n, :], hn, (((1,), (1,)), ((), ())),
                               preferred_element_type=F32)

    def rot(t, cos, sa, sb, axis):
        parts = []
        for c in range(t.shape[axis] // LANES):
            tc = lax.slice_in_dim(t, c * LANES, (c + 1) * LANES, axis=axis)
            parts.append(tc * cos + pltpu.roll(tc, LANES - 32, axis) * sa
                         + pltpu.roll(tc, 32, axis) * sb)
        return jnp.concatenate(parts, axis=axis)

    qt = rot(proj_t(0, A_WIDTH), cost_ref[...], sat_ref[...], sbt_ref[...], 0)
    q_ref[0] = (qt * (A_HEAD_DIM ** -0.5 * LOG2_E)).astype(BF16)
    vt = proj_t(A_WIDTH, A_WIDTH)
    c = 0
    k = rot(proj(c, A_WIDTH), cos_ref[...], sa_ref[...], sb_ref[...], 1); c += A_WIDTH
    k_ref[0] = k.astype(BF16)
    nblk = tm // MOBA_BLOCK
    kmean = jnp.mean(k.reshape(nblk, MOBA_BLOCK, A_WIDTH), axis=1)
    blk0 = pl.program_id(1) * nblk
    for n in range(nblk):
        km_ref[pl.ds(blk0 + n, 1), :] = kmean[n:n + 1]

    nbr = km_ref.shape[0]
    feat = lax.broadcasted_iota(jnp.int32, (LANES, 1), 0)
    brow = lax.broadcasted_iota(jnp.int32, (nbr, 2 * tm), 0)
    browf = brow.astype(F32)
    q_blk = blk0 + (lax.broadcasted_iota(jnp.int32, (1, 2 * tm), 1) % tm) // MOBA_BLOCK

    def split(a):
        hi = a.astype(BF16)
        return hi, (a - hi.astype(F32)).astype(BF16)

    def gate_pair(hp):
        q_pair = qt[hp * LANES:(hp + 1) * LANES]
        q2 = jnp.concatenate([jnp.where(feat // A_HEAD_DIM == h, q_pair, 0.0) for h in range(2)],
                             axis=1)
        km_hi, km_lo = split(km_ref[:, hp * LANES:(hp + 1) * LANES])
        q_hi, q_lo = split(q2)
        gate = jnp.dot(jnp.concatenate([km_hi, km_hi, km_lo], axis=1),
                       jnp.concatenate([q_hi, q_lo, q_hi], axis=0), preferred_element_type=F32)
        g = jnp.where(brow < q_blk, gate, -jnp.inf)
        bias = jnp.full((nbr, 2 * tm), MASK_VALUE, F32)
        for _ in range(MOBA_TOPK):
            mx = jnp.max(g, axis=0, keepdims=True)
            first = jnp.min(jnp.where(g == mx, browf, float(nbr)), axis=0, keepdims=True)
            hit = browf == first
            bias = jnp.where(hit & (mx > -jnp.inf), 0.0, bias)
            g = jnp.where(hit, -jnp.inf, g)
        for h in range(2):
            bias_ref[0, 2 * hp + h] = bias[:, h * tm:(h + 1) * tm].astype(BF16)

    ones_rows = (lax.broadcasted_iota(jnp.int32, (V_ROWS - A_HEAD_DIM, MOBA_BLOCK), 0)
                 == 0).astype(BF16)
    for h in range(A_HEADS):
        for n in range(nblk):
            v_ref[0, h, n, 0:A_HEAD_DIM, :] = vt[h * A_HEAD_DIM:(h + 1) * A_HEAD_DIM,
                                                 n * MOBA_BLOCK:(n + 1) * MOBA_BLOCK].astype(BF16)
            v_ref[0, h, n, A_HEAD_DIM:, :] = ones_rows
    gate_pair(0)
    bq_ref[0] = proj(c, B_KEY_WIDTH) * (B_HEAD_K ** -0.5); c += B_KEY_WIDTH
    bk_ref[0] = proj(c, B_KEY_WIDTH); c += B_KEY_WIDTH
    gate_pair(1)
    bv_ref[0] = proj(c, B_VAL_WIDTH); c += B_VAL_WIDTH
    gate_pair(2)
    og_ref[0] = proj(c, B_VAL_WIDTH); c += B_VAL_WIDTH
    gate_pair(3)
    gk = proj(c, LANES).astype(BF16)
    z = jnp.dot(gk, wgk_ref[...], preferred_element_type=F32) + bgk_ref[...]
    gl_ref[0] = _log_sigmoid(z) / GLA_GATE_NORM


def _front0(x, g, w_in, w_gk2, b_gk2):
    bsz, seq, d = x.shape
    tm = min(512, seq)
    widths = [A_WIDTH, A_WIDTH, A_WIDTH, B_KEY_WIDTH, B_KEY_WIDTH, B_VAL_WIDTH, GLA_GATE_RANK]
    aq, ak, av, bq, bk, bv, bgk, bog = jnp.split(
        w_in, [sum(widths[:n + 1]) for n in range(len(widths))], axis=1)
    gk_pad = LANES - GLA_GATE_RANK
    w = jnp.concatenate([ak, bq, bk, bv, bog, jnp.pad(bgk, ((0, 0), (0, gk_pad)))],
                        axis=1).astype(BF16)
    wt = jnp.concatenate([aq, av], axis=1).T.astype(BF16)
    wgk = jnp.pad(w_gk2, ((0, gk_pad), (0, 0))).astype(BF16)
    ncol = w.shape[1]

    half = A_HEAD_DIM // 2
    inv = ROPE_THETA ** (-jnp.arange(half, dtype=F32) / half)
    pos = jnp.arange(seq, dtype=jnp.int32).astype(F32)
    first = (jnp.arange(LANES) % A_HEAD_DIM) < half

    def tables(feature_major):
        if feature_major:
            ang, reps, sel = inv[:, None] * pos[None, :], (LANES // half, 1), first[:, None]
        else:
            ang, reps, sel = pos[:, None] * inv[None, :], (1, LANES // half), first[None, :]
        cos, sin = jnp.tile(jnp.cos(ang), reps), jnp.tile(jnp.sin(ang), reps)
        return cos, jnp.where(sel, -sin, 0.0), jnp.where(sel, 0.0, sin)

    nb = seq // MOBA_BLOCK
    nbr = _moba_gate_rows(nb)
    row = lambda n: pl.BlockSpec((1, tm, n), lambda b, i: (b, i, 0))
    tab = pl.BlockSpec((tm, LANES), lambda b, i: (i, 0))
    tab_t = pl.BlockSpec((LANES, tm), lambda b, i: (0, i))
    nt = seq // tm
    outs = pl.pallas_call(
        functools.partial(_front0_kernel, tm=tm),
        grid=(bsz, nt),
        in_specs=[row(d), _resident((1, d)), _resident((d, ncol)), _resident((2 * A_WIDTH, d)),
                  _resident((LANES, B_KEY_WIDTH)), _resident((1, B_KEY_WIDTH)),
                  tab, tab, tab, tab_t, tab_t, tab_t],
        out_specs=[pl.BlockSpec((1, A_WIDTH, tm), lambda b, i: (b, 0, i)), row(A_WIDTH),
                   pl.BlockSpec((1, A_HEADS, tm // MOBA_BLOCK, V_ROWS, MOBA_BLOCK),
                                lambda b, i: (b, 0, i, 0, 0)),
                   pl.BlockSpec((1, A_HEADS, nbr, tm), lambda b, i: (b, 0, 0, i)),
                   row(B_KEY_WIDTH), row(B_KEY_WIDTH), row(B_VAL_WIDTH), row(B_KEY_WIDTH),
                   row(B_VAL_WIDTH)],
        out_shape=[jax.ShapeDtypeStruct((bsz, A_WIDTH, seq), BF16),
                   jax.ShapeDtypeStruct((bsz, seq, A_WIDTH), BF16),
                   jax.ShapeDtypeStruct((bsz, A_HEADS, nb, V_ROWS, MOBA_BLOCK), BF16),
                   jax.ShapeDtypeStruct((bsz, A_HEADS, nbr, seq), BF16),
                   jax.ShapeDtypeStruct((bsz, seq, B_KEY_WIDTH), F32),
                   jax.ShapeDtypeStruct((bsz, seq, B_KEY_WIDTH), F32),
                   jax.ShapeDtypeStruct((bsz, seq, B_VAL_WIDTH), F32),
                   jax.ShapeDtypeStruct((bsz, seq, B_KEY_WIDTH), F32),
                   jax.ShapeDtypeStruct((bsz, seq, B_VAL_WIDTH), F32)],
        scratch_shapes=[pltpu.VMEM((nbr, A_WIDTH), F32)],
        compiler_params=_params(("arbitrary", "arbitrary")),
        name="front0",
    )(x, g.reshape(1, d), w, wt, wgk, b_gk2.reshape(1, B_KEY_WIDTH), *tables(False), *tables(True))
    return outs


def _moba_kernel(qt_ref, k_ref, vt_ref, bias_ref, o_ref, rhs_ref, acc_ref, sa_ref, sb_ref, *,
                 group, nb):
    qb = pl.program_id(2)
    blk = MOBA_BLOCK
    qt = qt_ref[0]
    feat = lax.broadcasted_iota(jnp.int32, (LANES, 1), 0)
    nbr = bias_ref.shape[2]
    for h in range(2):
        rhs_ref[:, h * blk:(h + 1) * blk] = jnp.concatenate(
            [jnp.where(feat // A_HEAD_DIM == h, qt, jnp.zeros_like(qt)), bias_ref[0, h],
             jnp.full((MOBA_BLOCKS_PAD - nbr, blk), MASK_VALUE, BF16)], axis=0)

    keys = group * blk
    blk_of_row = lax.broadcasted_iota(jnp.int32, (keys, MOBA_BLOCKS_PAD), 0) // blk
    lane_blk = lax.broadcasted_iota(jnp.int32, (keys, MOBA_BLOCKS_PAD), 1)

    def scores(g, s_ref):
        j0 = g * group
        off = pl.multiple_of(jnp.minimum(j0, nb - group) * blk, blk)
        onehot = (lane_blk == blk_of_row + j0).astype(BF16)
        lhs_j = jnp.concatenate([k_ref[0, pl.ds(off, keys), :], onehot], axis=1)
        s = jnp.dot(lhs_j, rhs_ref[...], preferred_element_type=F32)
        s_ref[...] = s
        return jnp.max(s, axis=0, keepdims=True)

    def accumulate(g, s_ref, m, s_max):
        j0 = g * group
        m_new = jnp.maximum(m, s_max)
        alpha = jnp.exp2(m - m_new)
        p = jnp.exp2(s_ref[...] - m_new).astype(BF16)
        for h in range(2):
            vt = jnp.concatenate([vt_ref[0, h, jnp.minimum(j0 + u, nb - 1)] for u in range(group)],
                                 axis=1)
            acc_ref[h] = (alpha[:, h * blk:(h + 1) * blk] * acc_ref[h]
                          + jnp.dot(vt, p[:, h * blk:(h + 1) * blk], preferred_element_type=F32))
        return m_new

    def body(i, carry):
        m, max_a = carry
        max_b = scores(2 * i + 1, sb_ref)
        m = accumulate(2 * i, sa_ref, m, max_a)
        max_a = scores(2 * i + 2, sa_ref)
        m = accumulate(2 * i + 1, sb_ref, m, max_b)
        return m, max_a

    max_0 = scores(0, sa_ref)

    own = pl.multiple_of(qb * blk, blk)
    lhs = jnp.concatenate([k_ref[0, pl.ds(own, blk), :], jnp.zeros((blk, MOBA_BLOCKS_PAD), BF16)],
                          axis=1)
    causal = (lax.broadcasted_iota(jnp.int32, (blk, 2 * blk), 0)
              <= lax.broadcasted_iota(jnp.int32, (blk, 2 * blk), 1) % blk)
    s = jnp.where(causal, jnp.dot(lhs, rhs_ref[...], preferred_element_type=F32), MASK_VALUE)
    m0 = jnp.max(s, axis=0, keepdims=True)
    p = jnp.exp2(s - m0).astype(BF16)
    for h in range(2):
        acc_ref[h] = jnp.dot(vt_ref[0, h, qb], p[:, h * blk:(h + 1) * blk],
                             preferred_element_type=F32)

    n_groups = (qb + group - 1) // group
    m, max_a = lax.fori_loop(0, n_groups // 2, body, (m0, max_0))

    @pl.when(n_groups % 2 == 1)
    def _():
        accumulate(n_groups - 1, sa_ref, m, max_a)

    outs = []
    for h in range(2):
        a = acc_ref[h]
        outs.append(a[0:A_HEAD_DIM] / a[A_HEAD_DIM:A_HEAD_DIM + 1])
    o_ref[0] = jnp.concatenate(outs, axis=0).T


def _moba(qt, k, vt, bias):
    bsz, seq, _ = k.shape
    nb = seq // MOBA_BLOCK
    nbr = _moba_gate_rows(nb)
    npair = A_WIDTH // LANES
    group = 4 if nb % 4 == 0 else 1
    return pl.pallas_call(
        functools.partial(_moba_kernel, group=group, nb=nb),
        grid=(bsz, npair, nb),
        in_specs=[pl.BlockSpec((1, LANES, MOBA_BLOCK), lambda b, hp, i: (b, hp, i)),
                  pl.BlockSpec((1, seq, LANES), lambda b, hp, i: (b, 0, hp)),
                  pl.BlockSpec((1, 2, nb, V_ROWS, MOBA_BLOCK), lambda b, hp, i: (b, hp, 0, 0, 0)),
                  pl.BlockSpec((1, 2, nbr, MOBA_BLOCK), lambda b, hp, i: (b, hp, 0, i))],
        out_specs=pl.BlockSpec((1, MOBA_BLOCK, LANES), lambda b, hp, i: (b, i, hp)),
        out_shape=jax.ShapeDtypeStruct((bsz, seq, A_WIDTH), F32),
        scratch_shapes=[pltpu.VMEM((2 * LANES, 2 * MOBA_BLOCK), BF16),
                        pltpu.VMEM((2, V_ROWS, MOBA_BLOCK), F32),
                        pltpu.VMEM((group * MOBA_BLOCK, 2 * MOBA_BLOCK), F32),
                        pltpu.VMEM((group * MOBA_BLOCK, 2 * MOBA_BLOCK), F32)],
        compiler_params=_params(("arbitrary", "arbitrary", "arbitrary")),
        name="moba",
    )(qt, k, vt, bias)


def _gla_kernel(q_ref, k_ref, v_ref, g_ref, og_ref, ng_ref, o_ref, st_ref, *, tm):
    @pl.when(pl.program_id(1) == 0)
    def _():
        st_ref[...] = jnp.zeros_like(st_ref)

    q, k, v, g = q_ref[0], k_ref[0], v_ref[0], g_ref[0]
    kw, vw = B_KEY_WIDTH, B_VAL_WIDTH
    row = lax.broadcasted_iota(jnp.int32, (tm, 1), 0)
    r_sub = row % GLA_SUB
    r_chk = row % GLA_CHUNK

    def seg_cumsum(x, r, length):
        s = 1
        while s < length:
            x = x + jnp.where(r >= s, pltpu.roll(x, s, 0), 0.0)
            s *= 2
        return x

    c_sub = seg_cumsum(g, r_sub, GLA_SUB)
    b = seg_cumsum(g, r_chk, GLA_CHUNK)

    head_kv = (lax.broadcasted_iota(jnp.int32, (kw, vw), 0) // B_HEAD_K
               == lax.broadcasted_iota(jnp.int32, (kw, vw), 1) // B_HEAD_V).astype(BF16)
    head_vk = (lax.broadcasted_iota(jnp.int32, (vw, kw), 0) // B_HEAD_V
               == lax.broadcasted_iota(jnp.int32, (vw, kw), 1) // B_HEAD_K)
    head_vv = (lax.broadcasted_iota(jnp.int32, (vw, vw), 0) // B_HEAD_V
               == lax.broadcasted_iota(jnp.int32, (vw, vw), 1) // B_HEAD_V).astype(BF16)

    o_acc = jnp.zeros((tm, vw), F32)
    for d in range(GLA_SUB):
        if d == 0:
            w = q * k
            vv = v
        else:
            w = jnp.where(r_sub >= d,
                          q * pltpu.roll(k, d, 0) * jnp.exp(c_sub - pltpu.roll(c_sub, d, 0)), 0.0)
            vv = pltpu.roll(v, d, 0)
        o_acc = o_acc + jnp.dot(w.astype(BF16), head_kv, preferred_element_type=F32) * vv

    q_sub = q * jnp.exp(c_sub)
    q_chk = q * jnp.exp(b)

    nsub = GLA_CHUNK // GLA_SUB
    rr = lax.broadcasted_iota(jnp.int32, (GLA_CHUNK, 1), 0)
    lane_k = lax.broadcasted_iota(jnp.int32, (1, (nsub - 1) * kw), 1)
    lane_v = lax.broadcasted_iota(jnp.int32, (1, vw), 1)
    outs = []
    for c in range(tm // GLA_CHUNK):
        lo = c * GLA_CHUNK
        bc = b[lo:lo + GLA_CHUNK]
        kc = k[lo:lo + GLA_CHUNK]
        vc = v[lo:lo + GLA_CHUNK].astype(BF16)
        qs = q_sub[lo:lo + GLA_CHUNK]

        rhs_parts, lhs_parts = [], []
        for i in range(1, nsub):
            b_end = bc[i * GLA_SUB - 1:i * GLA_SUB, :]
            decay = jnp.exp(jnp.minimum(b_end - bc, 0.0))
            rhs_parts.append(jnp.where(rr < i * GLA_SUB, kc * decay, 0.0).astype(BF16))
            lhs_parts.append(jnp.where(rr // GLA_SUB == i, qs, 0.0).astype(BF16))
        rhs = jnp.concatenate(rhs_parts, axis=1)
        lhs_t = jnp.concatenate(lhs_parts, axis=1)
        lhs = jnp.concatenate(
            [jnp.where((lane_k % kw) // B_HEAD_K == h, lhs_t, jnp.zeros_like(lhs_t))
             for h in range(B_HEADS)], axis=0)
        attn = lax.dot_general(lhs, rhs, (((1,), (1,)), ((), ())), preferred_element_type=F32)
        res = jnp.dot(attn.astype(BF16), vc, preferred_element_type=F32)
        o_off = jnp.zeros((GLA_CHUNK, vw), F32)
        for h in range(B_HEADS):
            o_off = o_off + jnp.where(lane_v // B_HEAD_V == h,
                                      res[h * GLA_CHUNK:(h + 1) * GLA_CHUNK], 0.0)

        st = st_ref[...]
        o_int = lax.dot_general(q_chk[lo:lo + GLA_CHUNK].astype(BF16), st.astype(BF16),
                                (((1,), (1,)), ((), ())), preferred_element_type=F32)
        b_last = bc[GLA_CHUNK - 1:GLA_CHUNK, :]
        ks = (kc * jnp.exp(b_last - bc)).astype(BF16)
        upd = lax.dot_general(vc, ks, (((0,), (0,)), ((), ())), preferred_element_type=F32)
        st_ref[...] = st * jnp.exp(b_last) + jnp.where(head_vk, upd, 0.0)
        outs.append(o_off + o_int)

    o = o_acc + jnp.concatenate(outs, axis=0)

    sq = o * o
    sq_hi = sq.astype(BF16)
    sq_lo = (sq - sq_hi.astype(F32)).astype(BF16)
    ms = (jnp.dot(sq_hi, head_vv, preferred_element_type=F32)
          + jnp.dot(sq_lo, head_vv, preferred_element_type=F32)) * (1.0 / B_HEAD_V)
    og = og_ref[0]
    o_ref[0] = (o * lax.rsqrt(ms + NORM_EPS) * ng_ref[...]) * (og * jax.nn.sigmoid(og))


def _gla(bq, bk, bv, gl, og, norm_g):
    bsz, seq, _ = bq.shape
    tm = min(256, seq)
    row = lambda n: pl.BlockSpec((1, tm, n), lambda b, i: (b, i, 0))
    return pl.pallas_call(
        functools.partial(_gla_kernel, tm=tm),
        grid=(bsz, seq // tm),
        in_specs=[row(B_KEY_WIDTH), row(B_KEY_WIDTH), row(B_VAL_WIDTH), row(B_KEY_WIDTH),
                  row(B_VAL_WIDTH), _resident((1, B_VAL_WIDTH))],
        out_specs=row(B_VAL_WIDTH),
        out_shape=jax.ShapeDtypeStruct((bsz, seq, B_VAL_WIDTH), F32),
        scratch_shapes=[pltpu.VMEM((B_VAL_WIDTH, B_KEY_WIDTH), F32)],
        compiler_params=_params(("arbitrary", "arbitrary")),
        name="gla",
    )(bq, bk, bv, gl, og, norm_g.reshape(1, B_VAL_WIDTH))


def _ffn_kernel(*refs, tm, bn, dff, final_norm, mixer):
    if mixer:
        x_ref, oa_ref, ob_ref, wa_ref, wb_ref = refs[:5]
        refs = refs[5:]
    else:
        x_ref, refs = refs[0], refs[1:]
    g_ref, wup_ref, cw_ref, cb_ref, wdn_ref, fg_ref, o_ref, halo_ref, ext_ref = refs
    nkb = dff // bn

    @pl.when(pl.program_id(1) == 0)
    def _():
        halo_ref[...] = jnp.zeros_like(halo_ref)

    x = x_ref[0]
    if mixer:
        o_ref[0] = (jnp.dot(oa_ref[0].astype(BF16), wa_ref[...], preferred_element_type=F32)
                    + jnp.dot(ob_ref[0].astype(BF16), wb_ref[...], preferred_element_type=F32))
        x = x + o_ref[0]
    hn = _rms(x, g_ref[...]).astype(BF16)

    def up_stage(kb):
        for part in range(2):
            c0 = part * dff + kb * bn
            up = jnp.dot(hn, wup_ref[:, c0:c0 + bn], preferred_element_type=F32)
            ext_ref[kb % 2, part, 0:SUBLANES, :] = halo_ref[:, c0:c0 + bn]
            ext_ref[kb % 2, part, SUBLANES:, :] = up
            halo_ref[:, c0:c0 + bn] = up[tm - SUBLANES:, :]

    def gate_stage(kb):
        halves = []
        for part in range(2):
            c0 = part * dff + kb * bn
            y = cb_ref[:, c0:c0 + bn]
            for i in range(FFN_CONV):
                off = SUBLANES - (FFN_CONV - 1) + i
                y = y + ext_ref[kb % 2, part, pl.ds(off, tm), :] * cw_ref[i:i + 1, c0:c0 + bn]
            halves.append(y)
        return (halves[0] * _gelu_tanh(halves[1])).astype(BF16)

    up_stage(0)
    if nkb > 1:
        up_stage(1)
    a = gate_stage(0)
    acc = x
    for kb in range(nkb):
        if kb + 2 < nkb:
            up_stage(kb + 2)
        a_next = gate_stage(kb + 1) if kb + 1 < nkb else None
        acc = acc + jnp.dot(a, wdn_ref[kb * bn:(kb + 1) * bn, :], preferred_element_type=F32)
        a = a_next
    if final_norm:
        acc = _rms(acc, fg_ref[...])
    o_ref[0] = acc


def _conv_ffn(x, g, w_up, conv_w, conv_b, w_down, final_g, final_norm, mixer=None):
    bsz, seq, d = x.shape
    dff = w_down.shape[0]
    tm = min(512, seq)
    bn = 1536
    rows = lambda n: pl.BlockSpec((1, tm, n), lambda b, i: (b, i, 0))
    row = rows(d)
    lead_specs, lead_args = [row], [x]
    if mixer is not None:
        o_a, o_b, w_out = mixer
        w = w_out.astype(BF16)
        lead_specs += [rows(A_WIDTH), rows(B_VAL_WIDTH),
                       _resident((A_WIDTH, d)), _resident((B_VAL_WIDTH, d))]
        lead_args += [o_a, o_b, w[:A_WIDTH], w[A_WIDTH:]]
    return pl.pallas_call(
        functools.partial(_ffn_kernel, tm=tm, bn=bn, dff=dff, final_norm=final_norm,
                          mixer=mixer is not None),
        grid=(bsz, seq // tm),
        in_specs=lead_specs + [_resident((1, d)), _resident((d, 2 * dff)),
                               _resident((FFN_CONV, 2 * dff)), _resident((1, 2 * dff)),
                               _resident((dff, d)), _resident((1, d))],
        out_specs=row,
        out_shape=jax.ShapeDtypeStruct((bsz, seq, d), F32),
        scratch_shapes=[pltpu.VMEM((SUBLANES, 2 * dff), F32),
                        pltpu.VMEM((2, 2, tm + SUBLANES, bn), F32)],
        compiler_params=_params(("arbitrary", "arbitrary")),
        name="conv_ffn_final" if final_norm else "conv_ffn",
    )(*lead_args, g.reshape(1, d), w_up.astype(BF16), conv_w, conv_b.reshape(1, 2 * dff),
      w_down.astype(BF16), final_g.reshape(1, d))


def _rglru_kernel(x_ref, g_ref, win_ref, cw_ref, cb_ref, wa_ref, ba_ref, wx_ref, bx_ref,
                  lam_ref, wout_ref, o_ref, ext_ref, halo_ref, hlast_ref, *, tm, drnn, nseq):
    seg = tm // SUBLANES
    shifts = RG_CONV - 1
    bw = drnn // RG_BLOCKS
    seqs = range(nseq)

    @pl.when(pl.program_id(1) == 0)
    def _():
        halo_ref[...] = jnp.zeros_like(halo_ref)
        hlast_ref[...] = jnp.zeros_like(hlast_ref)

    prow = lax.broadcasted_iota(jnp.int32, (tm, tm), 0)
    pcol = lax.broadcasted_iota(jnp.int32, (tm, tm), 1)
    perm = (pcol == (prow % SUBLANES) * seg + prow // SUBLANES).astype(BF16)
    unperm = (prow == (pcol % SUBLANES) * seg + pcol // SUBLANES).astype(BF16)
    first_sub = lax.broadcasted_iota(jnp.int32, (SUBLANES, 1), 0) == 0

    xs = [x_ref[b] for b in seqs]
    hn = [jnp.dot(perm, _rms(xs[b], g_ref[...]).astype(BF16),
                  preferred_element_type=F32).astype(BF16) for b in seqs]
    gate = [jnp.dot(hn[b], win_ref[:, 0:drnn], preferred_element_type=F32) for b in seqs]
    xr_raw = [jnp.dot(hn[b], win_ref[:, drnn:2 * drnn], preferred_element_type=F32) for b in seqs]

    xr = []
    for b in seqs:
        for r in range(shifts):
            lo = (seg - shifts + r) * SUBLANES
            cur = xr_raw[b][lo:lo + SUBLANES]
            ext_ref[b, r * SUBLANES:(r + 1) * SUBLANES, :] = jnp.where(
                first_sub, pltpu.roll(halo_ref[b, r], 1, 0), pltpu.roll(cur, 1, 0))
            halo_ref[b, r] = cur
        ext_ref[b, shifts * SUBLANES:, :] = xr_raw[b]
        acc = cb_ref[...]
        for i in range(RG_CONV):
            acc = acc + ext_ref[b, i * SUBLANES:i * SUBLANES + tm, :] * cw_ref[i:i + 1, :]
        xr.append(acc)

    ys = []
    for b in seqs:
        xb = xr[b].astype(BF16)
        ra, ri = [], []
        for n in range(RG_BLOCKS):
            xn = xb[:, n * bw:(n + 1) * bw]
            ra.append(jnp.dot(xn, wa_ref[n], preferred_element_type=F32))
            ri.append(jnp.dot(xn, wx_ref[n], preferred_element_type=F32))
        r = jax.nn.sigmoid(jnp.concatenate(ra, axis=1) + ba_ref[...])
        ig = jax.nn.sigmoid(jnp.concatenate(ri, axis=1) + bx_ref[...])
        log_a = r * lam_ref[...]
        a = jnp.exp(log_a)
        th = jnp.tanh(log_a)
        one_minus_a2 = -2.0 * th / (1.0 - th)
        u = jnp.sqrt(one_minus_a2) * (ig * xr[b])

        h_loc, a_cum = [u[0:SUBLANES]], [a[0:SUBLANES]]
        for j in range(1, seg):
            aj = a[j * SUBLANES:(j + 1) * SUBLANES]
            h_loc.append(aj * h_loc[-1] + u[j * SUBLANES:(j + 1) * SUBLANES])
            a_cum.append(aj * a_cum[-1])
        h_in = [hlast_ref[b]]
        for s in range(SUBLANES):
            h_in.append(h_loc[-1][s:s + 1] + a_cum[-1][s:s + 1] * h_in[-1])
        hlast_ref[b] = h_in[SUBLANES]
        h_seg = jnp.concatenate(h_in[:SUBLANES], axis=0)
        h = jnp.concatenate([h_loc[j] + a_cum[j] * h_seg for j in range(seg)], axis=0)
        ys.append((h * _gelu_tanh(gate[b])).astype(BF16))

    for b in seqs:
        y_t = jnp.dot(unperm, ys[b], preferred_element_type=F32).astype(BF16)
        o_ref[b] = xs[b] + jnp.dot(y_t, wout_ref[...], preferred_element_type=F32)


def _rglru(x, g, w_in, conv_w, conv_b, w_a, b_a, w_x, b_x, lam, w_out):
    bsz, seq, d = x.shape
    drnn = w_out.shape[0]
    bw = drnn // RG_BLOCKS
    tm = min(256, seq)
    nseq = 2 if bsz % 2 == 0 else 1
    row = pl.BlockSpec((nseq, tm, d), lambda b, i: (b, i, 0))
    lam_c = (-LRU_C) * jax.nn.softplus(-lam.astype(F32))
    return pl.pallas_call(
        functools.partial(_rglru_kernel, tm=tm, drnn=drnn, nseq=nseq),
        grid=(bsz // nseq, seq // tm),
        in_specs=[row, _resident((1, d)), _resident((d, 2 * drnn)), _resident((RG_CONV, drnn)),
                  _resident((1, drnn)), _resident((RG_BLOCKS, bw, bw)), _resident((1, drnn)),
                  _resident((RG_BLOCKS, bw, bw)), _resident((1, drnn)), _resident((1, drnn)),
                  _resident((drnn, d))],
        out_specs=row,
        out_shape=jax.ShapeDtypeStruct((bsz, seq, d), F32),
        scratch_shapes=[pltpu.VMEM((nseq, tm + (RG_CONV - 1) * SUBLANES, drnn), F32),
                        pltpu.VMEM((nseq, RG_CONV - 1, SUBLANES, drnn), F32),
                        pltpu.VMEM((nseq, 1, drnn), F32)],
        compiler_params=_params(("arbitrary", "arbitrary")),
        name="rglru",
    )(x, g.reshape(1, d), w_in.astype(BF16), conv_w, conv_b.reshape(1, drnn),
      w_a.astype(BF16), b_a.reshape(1, drnn), w_x.astype(BF16), b_x.reshape(1, drnn),
      lam_c.reshape(1, drnn), w_out.astype(BF16))


def kernel(x, mix_norm_g, ffn_norm_g, final_norm_g, ev_w_in, ev_w_gk2, ev_b_gk2, ev_gla_norm_g,
           ev_w_out, od_w_in, od_conv_w, od_conv_b, od_w_a, od_b_a, od_w_x, od_b_x, od_lambda,
           od_w_out, ffn_w_up, ffn_conv_w, ffn_conv_b, ffn_w_down):
    depth = mix_norm_g.shape[0]
    h = x
    for l in range(depth):
        j = l // 2
        mixer = None
        if l % 2 == 0:
            q, k, v, bias, bq, bk, bv, gl, og = _front0(h, mix_norm_g[l], ev_w_in[j], ev_w_gk2[j],
                                                      ev_b_gk2[j])
            o_a = _moba(q, k, v, bias)
            o_b = _gla(bq, bk, bv, gl, og, ev_gla_norm_g[j])
            mixer = (o_a, o_b, ev_w_out[j])
        else:
            h = _rglru(h, mix_norm_g[l], od_w_in[j], od_conv_w[j], od_conv_b[j], od_w_a[j],
                       od_b_a[j], od_w_x[j], od_b_x[j], od_lambda[j], od_w_out[j])
        h = _conv_ffn(h, ffn_norm_g[l], ffn_w_up[l], ffn_conv_w[l], ffn_conv_b[l], ffn_w_down[l],
                      final_norm_g, final_norm=(l == depth - 1), mixer=mixer)
    return h
```

```python
import functools

import jax
import jax.numpy as jnp
from jax import lax
from jax.experimental import pallas as pl
from jax.experimental.pallas import tpu as pltpu

F32 = jnp.float32
BF16 = jnp.bfloat16

A_HEADS = 8
A_HEAD_DIM = 64
A_WIDTH = A_HEADS * A_HEAD_DIM
MOBA_BLOCK = 256
MOBA_TOPK = 3
ROPE_THETA = 10000.0
B_HEADS = 8
B_HEAD_K = 32
B_HEAD_V = 64
B_KEY_WIDTH = B_HEADS * B_HEAD_K
B_VAL_WIDTH = B_HEADS * B_HEAD_V
GLA_GATE_RANK = 16
GLA_GATE_NORM = 16.0
RG_BLOCKS = 4
RG_CONV = 4
LRU_C = 8.0
FFN_CONV = 3
NORM_EPS = 1e-6
MASK_VALUE = -1e30

LANES = 128
SUBLANES = 8
BF16_ROWS = 16
LOG2_E = 1.4426950408889634
VMEM_LIMIT = 56 * 1024 * 1024

GLA_CHUNK = 64
GLA_SUB = 16
MOBA_BLOCKS_PAD = LANES
V_ROWS = A_HEAD_DIM + BF16_ROWS


def _params(sem):
    return pltpu.CompilerParams(dimension_semantics=sem, vmem_limit_bytes=VMEM_LIMIT)


def _rms(x, g):
    ms = jnp.mean(x * x, axis=-1, keepdims=True)
    return x * lax.rsqrt(ms + NORM_EPS) * g


def _gelu_tanh(x):
    return 0.5 * x * (1.0 + jnp.tanh(0.7978845608028654 * (x + 0.044715 * (x * x * x))))


def _log_sigmoid(z):
    return jnp.minimum(z, 0.0) - jnp.log1p(jnp.exp(-jnp.abs(z)))


def _moba_gate_rows(nb):
    assert nb <= MOBA_BLOCKS_PAD
    return -(-nb // BF16_ROWS) * BF16_ROWS


def _resident(shape):
    nd = len(shape)
    return pl.BlockSpec(shape, lambda *_: (0,) * nd, pipeline_mode=pl.Buffered(1))


def _front0_kernel(x_ref, g_ref, w_ref, wt_ref, wgk_ref, bgk_ref, cos_ref, sa_ref, sb_ref,
                   cost_ref, sat_ref, sbt_ref,
                   q_ref, k_ref, v_ref, bias_ref, bq_ref, bk_ref, bv_ref, gl_ref, og_ref,
                   km_ref, *, tm):
    @pl.when(pl.program_id(1) == 0)
    def _():
        km_ref[...] = jnp.zeros_like(km_ref)

    hn = _rms(x_ref[0], g_ref[...]).astype(BF16)

    def proj(c0, n):
        return jnp.dot(hn, w_ref[:, c0:c0 + n], preferred_element_type=F32)

    def proj_t(r0, n):
        return lax.dot_general(wt_ref[r0:r0 + n, :], hn, (((1,), (1,)), ((), ())),
                               preferred_element_type=F32)

    def rot(t, cos, sa, sb, axis):
        parts = []
        for c in range(t.shape[axis] // LANES):
            tc = lax.slice_in_dim(t, c * LANES, (c + 1) * LANES, axis=axis)
            parts.append(tc * cos + pltpu.roll(tc, LANES - 32, axis) * sa
                         + pltpu.roll(tc, 32, axis) * sb)
        return jnp.concatenate(parts, axis=axis)

    qt = rot(proj_t(0, A_WIDTH), cost_ref[...], sat_ref[...], sbt_ref[...], 0)
    q_ref[0] = (qt * (A_HEAD_DIM ** -0.5 * LOG2_E)).astype(BF16)
    vt = proj_t(A_WIDTH, A_WIDTH)
    c = 0
    k = rot(proj(c, A_WIDTH), cos_ref[...], sa_ref[...], sb_ref[...], 1); c += A_WIDTH
    k_ref[0] = k.astype(BF16)
    nblk = tm // MOBA_BLOCK
    kmean = jnp.mean(k.reshape(nblk, MOBA_BLOCK, A_WIDTH), axis=1)
    blk0 = pl.program_id(1) * nblk
    for n in range(nblk):
        km_ref[pl.ds(blk0 + n, 1), :] = kmean[n:n + 1]

    nbr = km_ref.shape[0]
    feat = lax.broadcasted_iota(jnp.int32, (LANES, 1), 0)
    brow = lax.broadcasted_iota(jnp.int32, (nbr, 2 * tm), 0)
    browf = brow.astype(F32)
    q_blk = blk0 + (lax.broadcasted_iota(jnp.int32, (1, 2 * tm), 1) % tm) // MOBA_BLOCK

    def split(a):
        hi = a.astype(BF16)
        return hi, (a - hi.astype(F32)).astype(BF16)

    def gate_pair(hp):
        q_pair = qt[hp * LANES:(hp + 1) * LANES]
        q2 = jnp.concatenate([jnp.where(feat // A_HEAD_DIM == h, q_pair, 0.0) for h in range(2)],
                             axis=1)
        km_hi, km_lo = split(km_ref[:, hp * LANES:(hp + 1) * LANES])
        q_hi, q_lo = split(q2)
        gate = jnp.dot(jnp.concatenate([km_hi, km_hi, km_lo], axis=1),
                       jnp.concatenate([q_hi, q_lo, q_hi], axis=0), preferred_element_type=F32)
        g = jnp.where(brow < q_blk, gate, -jnp.inf)
        bias = jnp.full((nbr, 2 * tm), MASK_VALUE, F32)
        for _ in range(MOBA_TOPK):
            mx = jnp.max(g, axis=0, keepdims=True)
            first = jnp.min(jnp.where(g == mx, browf, float(nbr)), axis=0, keepdims=True)
            hit = browf == first
            bias = jnp.where(hit & (mx > -jnp.inf), 0.0, bias)
            g = jnp.where(hit, -jnp.inf, g)
        for h in range(2):
            bias_ref[0, 2 * hp + h] = bias[:, h * tm:(h + 1) * tm].astype(BF16)

    ones_rows = (lax.broadcasted_iota(jnp.int32, (V_ROWS - A_HEAD_DIM, MOBA_BLOCK), 0)
                 == 0).astype(BF16)
    for h in range(A_HEADS):
        for n in range(nblk):
            v_ref[0, h, n, 0:A_HEAD_DIM, :] = vt[h * A_HEAD_DIM:(h + 1) * A_HEAD_DIM,
                                                 n * MOBA_BLOCK:(n + 1) * MOBA_BLOCK].astype(BF16)
            v_ref[0, h, n, A_HEAD_DIM:, :] = ones_rows
    gate_pair(0)
    bq_ref[0] = proj(c, B_KEY_WIDTH) * (B_HEAD_K ** -0.5); c += B_KEY_WIDTH
    bk_ref[0] = proj(c, B_KEY_WIDTH); c += B_KEY_WIDTH
    gate_pair(1)
    bv_ref[0] = proj(c, B_VAL_WIDTH); c += B_VAL_WIDTH
    gate_pair(2)
    og_ref[0] = proj(c, B_VAL_WIDTH); c += B_VAL_WIDTH
    gate_pair(3)
    gk = proj(c, LANES).astype(BF16)
    z = jnp.dot(gk, wgk_ref[...], preferred_element_type=F32) + bgk_ref[...]
    gl_ref[0] = _log_sigmoid(z) / GLA_GATE_NORM


def _front0(x, g, w_in, w_gk2, b_gk2):
    bsz, seq, d = x.shape
    tm = min(512, seq)
    widths = [A_WIDTH, A_WIDTH, A_WIDTH, B_KEY_WIDTH, B_KEY_WIDTH, B_VAL_WIDTH, GLA_GATE_RANK]
    aq, ak, av, bq, bk, bv, bgk, bog = jnp.split(
        w_in, [sum(widths[:n + 1]) for n in range(len(widths))], axis=1)
    gk_pad = LANES - GLA_GATE_RANK
    w = jnp.concatenate([ak, bq, bk, bv, bog, jnp.pad(bgk, ((0, 0), (0, gk_pad)))],
                        axis=1).astype(BF16)
    wt = jnp.concatenate([aq, av], axis=1).T.astype(BF16)
    wgk = jnp.pad(w_gk2, ((0, gk_pad), (0, 0))).astype(BF16)
    ncol = w.shape[1]

    half = A_HEAD_DIM // 2
    inv = ROPE_THETA ** (-jnp.arange(half, dtype=F32) / half)
    pos = jnp.arange(seq, dtype=jnp.int32).astype(F32)
    first = (jnp.arange(LANES) % A_HEAD_DIM) < half

    def tables(feature_major):
        if feature_major:
            ang, reps, sel = inv[:, None] * pos[None, :], (LANES // half, 1), first[:, None]
        else:
            ang, reps, sel = pos[:, None] * inv[None, :], (1, LANES // half), first[None, :]
        cos, sin = jnp.tile(jnp.cos(ang), reps), jnp.tile(jnp.sin(ang), reps)
        return cos, jnp.where(sel, -sin, 0.0), jnp.where(sel, 0.0, sin)

    nb = seq // MOBA_BLOCK
    nbr = _moba_gate_rows(nb)
    row = lambda n: pl.BlockSpec((1, tm, n), lambda b, i: (b, i, 0))
    tab = pl.BlockSpec((tm, LANES), lambda b, i: (i, 0))
    tab_t = pl.BlockSpec((LANES, tm), lambda b, i: (0, i))
    nt = seq // tm
    outs = pl.pallas_call(
        functools.partial(_front0_kernel, tm=tm),
        grid=(bsz, nt),
        in_specs=[row(d), _resident((1, d)), _resident((d, ncol)), _resident((2 * A_WIDTH, d)),
                  _resident((LANES, B_KEY_WIDTH)), _resident((1, B_KEY_WIDTH)),
                  tab, tab, tab, tab_t, tab_t, tab_t],
        out_specs=[pl.BlockSpec((1, A_WIDTH, tm), lambda b, i: (b, 0, i)), row(A_WIDTH),
                   pl.BlockSpec((1, A_HEADS, tm // MOBA_BLOCK, V_ROWS, MOBA_BLOCK),
                                lambda b, i: (b, 0, i, 0, 0)),
                   pl.BlockSpec((1, A_HEADS, nbr, tm), lambda b, i: (b, 0, 0, i)),
                   row(B_KEY_WIDTH), row(B_KEY_WIDTH), row(B_VAL_WIDTH), row(B_KEY_WIDTH),
                   row(B_VAL_WIDTH)],
        out_shape=[jax.ShapeDtypeStruct((bsz, A_WIDTH, seq), BF16),
                   jax.ShapeDtypeStruct((bsz, seq, A_WIDTH), BF16),
                   jax.ShapeDtypeStruct((bsz, A_HEADS, nb, V_ROWS, MOBA_BLOCK), BF16),
                   jax.ShapeDtypeStruct((bsz, A_HEADS, nbr, seq), BF16),
                   jax.ShapeDtypeStruct((bsz, seq, B_KEY_WIDTH), F32),
                   jax.ShapeDtypeStruct((bsz, seq, B_KEY_WIDTH), F32),
                   jax.ShapeDtypeStruct((bsz, seq, B_VAL_WIDTH), F32),
                   jax.ShapeDtypeStruct((bsz, seq, B_KEY_WIDTH), F32),
                   jax.ShapeDtypeStruct((bsz, seq, B_VAL_WIDTH), F32)],
        scratch_shapes=[pltpu.VMEM((nbr, A_WIDTH), F32)],
        compiler_params=_params(("arbitrary", "arbitrary")),
        name="front0",
    )(x, g.reshape(1, d), w, wt, wgk, b_gk2.reshape(1, B_KEY_WIDTH), *tables(False), *tables(True))
    return outs


def _moba_kernel(qt_ref, k_ref, vt_ref, bias_ref, o_ref, rhs_ref, acc_ref, sa_ref, sb_ref, *,
                 group, nb, qn):
    blk = MOBA_BLOCK
    tq = qn * blk
    q_first = pl.program_id(2) * qn
    q_last = q_first + qn - 1
    qt = qt_ref[0]
    feat = lax.broadcasted_iota(jnp.int32, (LANES, 1), 0)
    nbr = bias_ref.shape[2]
    for h in range(2):
        rhs_ref[:, h * tq:(h + 1) * tq] = jnp.concatenate(
            [jnp.where(feat // A_HEAD_DIM == h, qt, jnp.zeros_like(qt)), bias_ref[0, h],
             jnp.full((MOBA_BLOCKS_PAD - nbr, tq), MASK_VALUE, BF16)], axis=0)

    keys = group * blk
    blk_of_row = lax.broadcasted_iota(jnp.int32, (keys, MOBA_BLOCKS_PAD), 0) // blk
    lane_blk = lax.broadcasted_iota(jnp.int32, (keys, MOBA_BLOCKS_PAD), 1)

    def scores(g, s_ref):
        j0 = g * group
        off = pl.multiple_of(jnp.minimum(j0, nb - group) * blk, blk)
        onehot = (lane_blk == blk_of_row + j0).astype(BF16)
        lhs_j = jnp.concatenate([k_ref[0, pl.ds(off, keys), :], onehot], axis=1)
        s = jnp.dot(lhs_j, rhs_ref[...], preferred_element_type=F32)
        s_ref[...] = s
        return jnp.max(s, axis=0, keepdims=True)

    def accumulate(g, s_ref, m, s_max):
        j0 = g * group
        m_new = jnp.maximum(m, s_max)
        alpha = jnp.exp2(m - m_new)
        p = jnp.exp2(s_ref[...] - m_new).astype(BF16)
        for h in range(2):
            vt = jnp.concatenate([vt_ref[0, h, jnp.minimum(j0 + u, nb - 1)] for u in range(group)],
                                 axis=1)
            acc_ref[h] = (alpha[:, h * tq:(h + 1) * tq] * acc_ref[h]
                          + jnp.dot(vt, p[:, h * tq:(h + 1) * tq], preferred_element_type=F32))
        return m_new

    def body(i, carry):
        m, max_a = carry
        max_b = scores(2 * i + 1, sb_ref)
        m = accumulate(2 * i, sa_ref, m, max_a)
        max_a = scores(2 * i + 2, sa_ref)
        m = accumulate(2 * i + 1, sb_ref, m, max_b)
        return m, max_a

    max_0 = scores(0, sa_ref)

    own = pl.multiple_of(q_first * blk, blk)
    lhs = jnp.concatenate([k_ref[0, pl.ds(own, tq), :], jnp.zeros((tq, MOBA_BLOCKS_PAD), BF16)],
                          axis=1)
    key_pos = lax.broadcasted_iota(jnp.int32, (tq, 2 * tq), 0)
    qry_pos = lax.broadcasted_iota(jnp.int32, (tq, 2 * tq), 1) % tq
    causal = (key_pos <= qry_pos) & (key_pos // blk == qry_pos // blk)
    s = jnp.where(causal, jnp.dot(lhs, rhs_ref[...], preferred_element_type=F32), MASK_VALUE)
    m0 = jnp.max(s, axis=0, keepdims=True)
    p = jnp.exp2(s - m0).astype(BF16)
    for h in range(2):
        vt_own = jnp.concatenate([vt_ref[0, h, q_first + u] for u in range(qn)], axis=1)
        acc_ref[h] = jnp.dot(vt_own, p[:, h * tq:(h + 1) * tq], preferred_element_type=F32)

    n_groups = (q_last + group - 1) // group
    m, max_a = lax.fori_loop(0, n_groups // 2, body, (m0, max_0))

    @pl.when(n_groups % 2 == 1)
    def _():
        accumulate(n_groups - 1, sa_ref, m, max_a)

    outs = []
    for h in range(2):
        a = acc_ref[h]
        outs.append(a[0:A_HEAD_DIM] / a[A_HEAD_DIM:A_HEAD_DIM + 1])
    o_ref[0] = jnp.concatenate(outs, axis=0).T


def _moba(qt, k, vt, bias):
    bsz, seq, _ = k.shape
    nb = seq // MOBA_BLOCK
    nbr = _moba_gate_rows(nb)
    npair = A_WIDTH // LANES
    group = 4 if nb % 4 == 0 else 1
    qn = 2 if nb % 2 == 0 else 1
    tq = qn * MOBA_BLOCK
    return pl.pallas_call(
        functools.partial(_moba_kernel, group=group, nb=nb, qn=qn),
        grid=(bsz, npair, nb // qn),
        in_specs=[pl.BlockSpec((1, LANES, tq), lambda b, hp, i: (b, hp, i)),
                  pl.BlockSpec((1, seq, LANES), lambda b, hp, i: (b, 0, hp)),
                  pl.BlockSpec((1, 2, nb, V_ROWS, MOBA_BLOCK), lambda b, hp, i: (b, hp, 0, 0, 0)),
                  pl.BlockSpec((1, 2, nbr, tq), lambda b, hp, i: (b, hp, 0, i))],
        out_specs=pl.BlockSpec((1, tq, LANES), lambda b, hp, i: (b, i, hp)),
        out_shape=jax.ShapeDtypeStruct((bsz, seq, A_WIDTH), F32),
        scratch_shapes=[pltpu.VMEM((2 * LANES, 2 * tq), BF16),
                        pltpu.VMEM((2, V_ROWS, tq), F32),
                        pltpu.VMEM((group * MOBA_BLOCK, 2 * tq), F32),
                        pltpu.VMEM((group * MOBA_BLOCK, 2 * tq), F32)],
        compiler_params=_params(("arbitrary", "arbitrary", "arbitrary")),
        name="moba",
    )(qt, k, vt, bias)


def _gla_kernel(q_ref, k_ref, v_ref, g_ref, og_ref, ng_ref, o_ref, st_ref, *, tm):
    @pl.when(pl.program_id(1) == 0)
    def _():
        st_ref[...] = jnp.zeros_like(st_ref)

    q, k, v, g = q_ref[0], k_ref[0], v_ref[0], g_ref[0]
    kw, vw = B_KEY_WIDTH, B_VAL_WIDTH
    row = lax.broadcasted_iota(jnp.int32, (tm, 1), 0)
    r_sub = row % GLA_SUB
    r_chk = row % GLA_CHUNK

    def seg_cumsum(x, r, length):
        s = 1
        while s < length:
            x = x + jnp.where(r >= s, pltpu.roll(x, s, 0), 0.0)
            s *= 2
        return x

    c_sub = seg_cumsum(g, r_sub, GLA_SUB)
    b = seg_cumsum(g, r_chk, GLA_CHUNK)

    head_kv = (lax.broadcasted_iota(jnp.int32, (kw, vw), 0) // B_HEAD_K
               == lax.broadcasted_iota(jnp.int32, (kw, vw), 1) // B_HEAD_V).astype(BF16)
    head_vk = (lax.broadcasted_iota(jnp.int32, (vw, kw), 0) // B_HEAD_V
               == lax.broadcasted_iota(jnp.int32, (vw, kw), 1) // B_HEAD_K)
    head_vv = (lax.broadcasted_iota(jnp.int32, (vw, vw), 0) // B_HEAD_V
               == lax.broadcasted_iota(jnp.int32, (vw, vw), 1) // B_HEAD_V).astype(BF16)

    o_acc = jnp.zeros((tm, vw), F32)
    for d in range(GLA_SUB):
        if d == 0:
            w = q * k
            vv = v
        else:
            w = jnp.where(r_sub >= d,
                          q * pltpu.roll(k, d, 0) * jnp.exp(c_sub - pltpu.roll(c_sub, d, 0)), 0.0)
            vv = pltpu.roll(v, d, 0)
        o_acc = o_acc + jnp.dot(w.astype(BF16), head_kv, preferred_element_type=F32) * vv

    q_sub = q * jnp.exp(c_sub)
    q_chk = q * jnp.exp(b)

    nsub = GLA_CHUNK // GLA_SUB
    rr = lax.broadcasted_iota(jnp.int32, (GLA_CHUNK, 1), 0)
    lane_k = lax.broadcasted_iota(jnp.int32, (1, (nsub - 1) * kw), 1)
    lane_v = lax.broadcasted_iota(jnp.int32, (1, vw), 1)
    outs = []
    for c in range(tm // GLA_CHUNK):
        lo = c * GLA_CHUNK
        bc = b[lo:lo + GLA_CHUNK]
        kc = k[lo:lo + GLA_CHUNK]
        vc = v[lo:lo + GLA_CHUNK].astype(BF16)
        qs = q_sub[lo:lo + GLA_CHUNK]

        rhs_parts, lhs_parts = [], []
        for i in range(1, nsub):
            b_end = bc[i * GLA_SUB - 1:i * GLA_SUB, :]
            decay = jnp.exp(jnp.minimum(b_end - bc, 0.0))
            rhs_parts.append(jnp.where(rr < i * GLA_SUB, kc * decay, 0.0).astype(BF16))
            lhs_parts.append(jnp.where(rr // GLA_SUB == i, qs, 0.0).astype(BF16))
        rhs = jnp.concatenate(rhs_parts, axis=1)
        lhs_t = jnp.concatenate(lhs_parts, axis=1)
        lhs = jnp.concatenate(
            [jnp.where((lane_k % kw) // B_HEAD_K == h, lhs_t, jnp.zeros_like(lhs_t))
             for h in range(B_HEADS)], axis=0)
        attn = lax.dot_general(lhs, rhs, (((1,), (1,)), ((), ())), preferred_element_type=F32)
        res = jnp.dot(attn.astype(BF16), vc, preferred_element_type=F32)
        o_off = jnp.zeros((GLA_CHUNK, vw), F32)
        for h in range(B_HEADS):
            o_off = o_off + jnp.where(lane_v // B_HEAD_V == h,
                                      res[h * GLA_CHUNK:(h + 1) * GLA_CHUNK], 0.0)

        st = st_ref[...]
        o_int = lax.dot_general(q_chk[lo:lo + GLA_CHUNK].astype(BF16), st.astype(BF16),
                                (((1,), (1,)), ((), ())), preferred_element_type=F32)
        b_last = bc[GLA_CHUNK - 1:GLA_CHUNK, :]
        ks = (kc * jnp.exp(b_last - bc)).astype(BF16)
        upd = lax.dot_general(vc, ks, (((0,), (0,)), ((), ())), preferred_element_type=F32)
        st_ref[...] = st * jnp.exp(b_last) + jnp.where(head_vk, upd, 0.0)
        outs.append(o_off + o_int)

    o = o_acc + jnp.concatenate(outs, axis=0)

    sq = o * o
    sq_hi = sq.astype(BF16)
    sq_lo = (sq - sq_hi.astype(F32)).astype(BF16)
    ms = (jnp.dot(sq_hi, head_vv, preferred_element_type=F32)
          + jnp.dot(sq_lo, head_vv, preferred_element_type=F32)) * (1.0 / B_HEAD_V)
    og = og_ref[0]
    o_ref[0] = (o * lax.rsqrt(ms + NORM_EPS) * ng_ref[...]) * (og * jax.nn.sigmoid(og))


def _gla(bq, bk, bv, gl, og, norm_g):
    bsz, seq, _ = bq.shape
    tm = min(256, seq)
    row = lambda n: pl.BlockSpec((1, tm, n), lambda b, i: (b, i, 0))
    return pl.pallas_call(
        functools.partial(_gla_kernel, tm=tm),
        grid=(bsz, seq // tm),
        in_specs=[row(B_KEY_WIDTH), row(B_KEY_WIDTH), row(B_VAL_WIDTH), row(B_KEY_WIDTH),
                  row(B_VAL_WIDTH), _resident((1, B_VAL_WIDTH))],
        out_specs=row(B_VAL_WIDTH),
        out_shape=jax.ShapeDtypeStruct((bsz, seq, B_VAL_WIDTH), F32),
        scratch_shapes=[pltpu.VMEM((B_VAL_WIDTH, B_KEY_WIDTH), F32)],
        compiler_params=_params(("arbitrary", "arbitrary")),
        name="gla",
    )(bq, bk, bv, gl, og, norm_g.reshape(1, B_VAL_WIDTH))


def _ffn_kernel(*refs, tm, bn, dff, final_norm, mixer):
    if mixer:
        x_ref, oa_ref, ob_ref, wa_ref, wb_ref = refs[:5]
        refs = refs[5:]
    else:
        x_ref, refs = refs[0], refs[1:]
    g_ref, wup_ref, cw_ref, cb_ref, wdn_ref, fg_ref, o_ref, halo_ref, ext_ref = refs
    nkb = dff // bn

    @pl.when(pl.program_id(1) == 0)
    def _():
        halo_ref[...] = jnp.zeros_like(halo_ref)

    x = x_ref[0]
    if mixer:
        o_ref[0] = (jnp.dot(oa_ref[0].astype(BF16), wa_ref[...], preferred_element_type=F32)
                    + jnp.dot(ob_ref[0].astype(BF16), wb_ref[...], preferred_element_type=F32))
        x = x + o_ref[0]
    hn = _rms(x, g_ref[...]).astype(BF16)

    def up_stage(kb):
        for part in range(2):
            c0 = part * dff + kb * bn
            up = jnp.dot(hn, wup_ref[:, c0:c0 + bn], preferred_element_type=F32)
            ext_ref[kb % 2, part, 0:SUBLANES, :] = halo_ref[:, c0:c0 + bn]
            ext_ref[kb % 2, part, SUBLANES:, :] = up
            halo_ref[:, c0:c0 + bn] = up[tm - SUBLANES:, :]

    def gate_stage(kb):
        halves = []
        for part in range(2):
            c0 = part * dff + kb * bn
            y = cb_ref[:, c0:c0 + bn]
            for i in range(FFN_CONV):
                off = SUBLANES - (FFN_CONV - 1) + i
                y = y + ext_ref[kb % 2, part, pl.ds(off, tm), :] * cw_ref[i:i + 1, c0:c0 + bn]
            halves.append(y)
        return (halves[0] * _gelu_tanh(halves[1])).astype(BF16)

    up_stage(0)
    if nkb > 1:
        up_stage(1)
    a = gate_stage(0)
    acc = x
    for kb in range(nkb):
        if kb + 2 < nkb:
            up_stage(kb + 2)
        a_next = gate_stage(kb + 1) if kb + 1 < nkb else None
        acc = acc + jnp.dot(a, wdn_ref[kb * bn:(kb + 1) * bn, :], preferred_element_type=F32)
        a = a_next
    if final_norm:
        acc = _rms(acc, fg_ref[...])
    o_ref[0] = acc


def _conv_ffn(x, g, w_up, conv_w, conv_b, w_down, final_g, final_norm, mixer=None):
    bsz, seq, d = x.shape
    dff = w_down.shape[0]
    tm = min(512, seq)
    bn = 1536
    rows = lambda n: pl.BlockSpec((1, tm, n), lambda b, i: (b, i, 0))
    row = rows(d)
    lead_specs, lead_args = [row], [x]
    if mixer is not None:
        o_a, o_b, w_out = mixer
        w = w_out.astype(BF16)
        lead_specs += [rows(A_WIDTH), rows(B_VAL_WIDTH),
                       _resident((A_WIDTH, d)), _resident((B_VAL_WIDTH, d))]
        lead_args += [o_a, o_b, w[:A_WIDTH], w[A_WIDTH:]]
    return pl.pallas_call(
        functools.partial(_ffn_kernel, tm=tm, bn=bn, dff=dff, final_norm=final_norm,
                          mixer=mixer is not None),
        grid=(bsz, seq // tm),
        in_specs=lead_specs + [_resident((1, d)), _resident((d, 2 * dff)),
                               _resident((FFN_CONV, 2 * dff)), _resident((1, 2 * dff)),
                               _resident((dff, d)), _resident((1, d))],
        out_specs=row,
        out_shape=jax.ShapeDtypeStruct((bsz, seq, d), F32),
        scratch_shapes=[pltpu.VMEM((SUBLANES, 2 * dff), F32),
                        pltpu.VMEM((2, 2, tm + SUBLANES, bn), F32)],
        compiler_params=_params(("arbitrary", "arbitrary")),
        name="conv_ffn_final" if final_norm else "conv_ffn",
    )(*lead_args, g.reshape(1, d), w_up.astype(BF16), conv_w, conv_b.reshape(1, 2 * dff),
      w_down.astype(BF16), final_g.reshape(1, d))


def _rglru_kernel(x_ref, g_ref, win_ref, cw_ref, cb_ref, wa_ref, ba_ref, wx_ref, bx_ref,
                  lam_ref, wout_ref, o_ref, ext_ref, halo_ref, hlast_ref, *, tm, drnn, nseq):
    seg = tm // SUBLANES
    shifts = RG_CONV - 1
    bw = drnn // RG_BLOCKS
    seqs = range(nseq)

    @pl.when(pl.program_id(1) == 0)
    def _():
        halo_ref[...] = jnp.zeros_like(halo_ref)
        hlast_ref[...] = jnp.zeros_like(hlast_ref)

    prow = lax.broadcasted_iota(jnp.int32, (tm, tm), 0)
    pcol = lax.broadcasted_iota(jnp.int32, (tm, tm), 1)
    perm = (pcol == (prow % SUBLANES) * seg + prow // SUBLANES).astype(BF16)
    unperm = (prow == (pcol % SUBLANES) * seg + pcol // SUBLANES).astype(BF16)
    first_sub = lax.broadcasted_iota(jnp.int32, (SUBLANES, 1), 0) == 0

    xs = [x_ref[b] for b in seqs]
    hn = [jnp.dot(perm, _rms(xs[b], g_ref[...]).astype(BF16),
                  preferred_element_type=F32).astype(BF16) for b in seqs]
    gate = [jnp.dot(hn[b], win_ref[:, 0:drnn], preferred_element_type=F32) for b in seqs]
    xr_raw = [jnp.dot(hn[b], win_ref[:, drnn:2 * drnn], preferred_element_type=F32) for b in seqs]

    xr = []
    for b in seqs:
        for r in range(shifts):
            lo = (seg - shifts + r) * SUBLANES
            cur = xr_raw[b][lo:lo + SUBLANES]
            ext_ref[b, r * SUBLANES:(r + 1) * SUBLANES, :] = jnp.where(
                first_sub, pltpu.roll(halo_ref[b, r], 1, 0), pltpu.roll(cur, 1, 0))
            halo_ref[b, r] = cur
        ext_ref[b, shifts * SUBLANES:, :] = xr_raw[b]
        acc = cb_ref[...]
        for i in range(RG_CONV):
            acc = acc + ext_ref[b, i * SUBLANES:i * SUBLANES + tm, :] * cw_ref[i:i + 1, :]
        xr.append(acc)

    ys = []
    for b in seqs:
        xb = xr[b].astype(BF16)
        ra, ri = [], []
        for n in range(RG_BLOCKS):
            xn = xb[:, n * bw:(n + 1) * bw]
            ra.append(jnp.dot(xn, wa_ref[n], preferred_element_type=F32))
            ri.append(jnp.dot(xn, wx_ref[n], preferred_element_type=F32))
        r = jax.nn.sigmoid(jnp.concatenate(ra, axis=1) + ba_ref[...])
        ig = jax.nn.sigmoid(jnp.concatenate(ri, axis=1) + bx_ref[...])
        log_a = r * lam_ref[...]
        a = jnp.exp(log_a)
        th = jnp.tanh(log_a)
        one_minus_a2 = -2.0 * th / (1.0 - th)
        u = jnp.sqrt(one_minus_a2) * (ig * xr[b])

        h_loc, a_cum = [u[0:SUBLANES]], [a[0:SUBLANES]]
        for j in range(1, seg):
            aj = a[j * SUBLANES:(j + 1) * SUBLANES]
            h_loc.append(aj * h_loc[-1] + u[j * SUBLANES:(j + 1) * SUBLANES])
            a_cum.append(aj * a_cum[-1])
        h_in = [hlast_ref[b]]
        for s in range(SUBLANES):
            h_in.append(h_loc[-1][s:s + 1] + a_cum[-1][s:s + 1] * h_in[-1])
        hlast_ref[b] = h_in[SUBLANES]
        h_seg = jnp.concatenate(h_in[:SUBLANES], axis=0)
        h = jnp.concatenate([h_loc[j] + a_cum[j] * h_seg for j in range(seg)], axis=0)
        ys.append((h * _gelu_tanh(gate[b])).astype(BF16))

    for b in seqs:
        y_t = jnp.dot(unperm, ys[b], preferred_element_type=F32).astype(BF16)
        o_ref[b] = xs[b] + jnp.dot(y_t, wout_ref[...], preferred_element_type=F32)


def _rglru(x, g, w_in, conv_w, conv_b, w_a, b_a, w_x, b_x, lam, w_out):
    bsz, seq, d = x.shape
    drnn = w_out.shape[0]
    bw = drnn // RG_BLOCKS
    tm = min(256, seq)
    nseq = 2 if bsz % 2 == 0 else 1
    row = pl.BlockSpec((nseq, tm, d), lambda b, i: (b, i, 0))
    lam_c = (-LRU_C) * jax.nn.softplus(-lam.astype(F32))
    return pl.pallas_call(
        functools.partial(_rglru_kernel, tm=tm, drnn=drnn, nseq=nseq),
        grid=(bsz // nseq, seq // tm),
        in_specs=[row, _resident((1, d)), _resident((d, 2 * drnn)), _resident((RG_CONV, drnn)),
                  _resident((1, drnn)), _resident((RG_BLOCKS, bw, bw)), _resident((1, drnn)),
                  _resident((RG_BLOCKS, bw, bw)), _resident((1, drnn)), _resident((1, drnn)),
                  _resident((drnn, d))],
        out_specs=row,
        out_shape=jax.ShapeDtypeStruct((bsz, seq, d), F32),
        scratch_shapes=[pltpu.VMEM((nseq, tm + (RG_CONV - 1) * SUBLANES, drnn), F32),
                        pltpu.VMEM((nseq, RG_CONV - 1, SUBLANES, drnn), F32),
                        pltpu.VMEM((nseq, 1, drnn), F32)],
        compiler_params=_params(("arbitrary", "arbitrary")),
        name="rglru",
    )(x, g.reshape(1, d), w_in.astype(BF16), conv_w, conv_b.reshape(1, drnn),
      w_a.astype(BF16), b_a.reshape(1, drnn), w_x.astype(BF16), b_x.reshape(1, drnn),
      lam_c.reshape(1, drnn), w_out.astype(BF16))


def kernel(x, mix_norm_g, ffn_norm_g, final_norm_g, ev_w_in, ev_w_gk2, ev_b_gk2, ev_gla_norm_g,
           ev_w_out, od_w_in, od_conv_w, od_conv_b, od_w_a, od_b_a, od_w_x, od_b_x, od_lambda,
           od_w_out, ffn_w_up, ffn_conv_w, ffn_conv_b, ffn_w_down):
    depth = mix_norm_g.shape[0]
    h = x
    for l in range(depth):
        j = l // 2
        mixer = None
        if l % 2 == 0:
            q, k, v, bias, bq, bk, bv, gl, og = _front0(h, mix_norm_g[l], ev_w_in[j], ev_w_gk2[j],
                                                      ev_b_gk2[j])
            o_a = _moba(q, k, v, bias)
            o_b = _gla(bq, bk, bv, gl, og, ev_gla_norm_g[j])
            mixer = (o_a, o_b, ev_w_out[j])
        else:
            h = _rglru(h, mix_norm_g[l], od_w_in[j], od_conv_w[j], od_conv_b[j], od_w_a[j],
                       od_b_a[j], od_w_x[j], od_b_x[j], od_lambda[j], od_w_out[j])
        h = _conv_ffn(h, ffn_norm_g[l], ffn_w_up[l], ffn_conv_w[l], ffn_conv_b[l], ffn_w_down[l],
                      final_norm_g, final_norm=(l == depth - 1), mixer=mixer)
    return h
```

```python
import functools

import jax
import jax.numpy as jnp
from jax import lax
from jax.experimental import pallas as pl
from jax.experimental.pallas import tpu as pltpu

F32 = jnp.float32
BF16 = jnp.bfloat16

A_HEADS = 8
A_HEAD_DIM = 64
A_WIDTH = A_HEADS * A_HEAD_DIM
MOBA_BLOCK = 256
MOBA_TOPK = 3
ROPE_THETA = 10000.0
B_HEADS = 8
B_HEAD_K = 32
B_HEAD_V = 64
B_KEY_WIDTH = B_HEADS * B_HEAD_K
B_VAL_WIDTH = B_HEADS * B_HEAD_V
GLA_GATE_RANK = 16
GLA_GATE_NORM = 16.0
RG_BLOCKS = 4
RG_CONV = 4
LRU_C = 8.0
FFN_CONV = 3
NORM_EPS = 1e-6
MASK_VALUE = -1e30

LANES = 128
SUBLANES = 8
BF16_ROWS = 16
LOG2_E = 1.4426950408889634
VMEM_LIMIT = 56 * 1024 * 1024

GLA_CHUNK = 64
GLA_SUB = 16
MOBA_BLOCKS_PAD = LANES
V_ROWS = A_HEAD_DIM + BF16_ROWS


def _params(sem):
    return pltpu.CompilerParams(dimension_semantics=sem, vmem_limit_bytes=VMEM_LIMIT)


def _rms(x, g):
    ms = jnp.mean(x * x, axis=-1, keepdims=True)
    return x * lax.rsqrt(ms + NORM_EPS) * g


def _gelu_tanh(x):
    return 0.5 * x * (1.0 + jnp.tanh(0.7978845608028654 * (x + 0.044715 * (x * x * x))))


def _log_sigmoid(z):
    return jnp.minimum(z, 0.0) - jnp.log1p(jnp.exp(-jnp.abs(z)))


def _moba_gate_rows(nb):
    assert nb <= MOBA_BLOCKS_PAD
    return -(-nb // BF16_ROWS) * BF16_ROWS


def _resident(shape):
    nd = len(shape)
    return pl.BlockSpec(shape, lambda *_: (0,) * nd, pipeline_mode=pl.Buffered(1))


def _front0_kernel(x_ref, g_ref, w_ref, wt_ref, wgk_ref, bgk_ref, cos_ref, sa_ref, sb_ref,
                   cost_ref, sat_ref, sbt_ref,
                   q_ref, k_ref, v_ref, bias_ref, bq_ref, bk_ref, bv_ref, gl_ref, og_ref,
                   km_ref, *, tm):
    @pl.when(pl.program_id(1) == 0)
    def _():
        km_ref[...] = jnp.zeros_like(km_ref)

    hn = _rms(x_ref[0], g_ref[...]).astype(BF16)

    def proj(c0, n):
        return jnp.dot(hn, w_ref[:, c0:c0 + n], preferred_element_type=F32)

    def proj_t(r0, n):
        return lax.dot_general(wt_ref[r0:r0 + n, :], hn, (((1,), (1,)), ((), ())),
                               preferred_element_type=F32)

    def rot(t, cos, sa, sb, axis):
        parts = []
        for c in range(t.shape[axis] // LANES):
            tc = lax.slice_in_dim(t, c * LANES, (c + 1) * LANES, axis=axis)
            parts.append(tc * cos + pltpu.roll(tc, LANES - 32, axis) * sa
                         + pltpu.roll(tc, 32, axis) * sb)
        return jnp.concatenate(parts, axis=axis)

    qt = rot(proj_t(0, A_WIDTH), cost_ref[...], sat_ref[...], sbt_ref[...], 0)
    q_ref[0] = (qt * (A_HEAD_DIM ** -0.5 * LOG2_E)).astype(BF16)
    vt = proj_t(A_WIDTH, A_WIDTH)
    c = 0
    k = rot(proj(c, A_WIDTH), cos_ref[...], sa_ref[...], sb_ref[...], 1); c += A_WIDTH
    k_ref[0] = k.astype(BF16)
    nblk = tm // MOBA_BLOCK
    kmean = jnp.mean(k.reshape(nblk, MOBA_BLOCK, A_WIDTH), axis=1)
    blk0 = pl.program_id(1) * nblk
    for n in range(nblk):
        km_ref[pl.ds(blk0 + n, 1), :] = kmean[n:n + 1]

    nbr = km_ref.shape[0]
    feat = lax.broadcasted_iota(jnp.int32, (LANES, 1), 0)
    brow = lax.broadcasted_iota(jnp.int32, (nbr, 2 * tm), 0)
    browf = brow.astype(F32)
    q_blk = blk0 + (lax.broadcasted_iota(jnp.int32, (1, 2 * tm), 1) % tm) // MOBA_BLOCK

    def split(a):
        hi = a.astype(BF16)
        return hi, (a - hi.astype(F32)).astype(BF16)

    def gate_pair(hp):
        q_pair = qt[hp * LANES:(hp + 1) * LANES]
        q2 = jnp.concatenate([jnp.where(feat // A_HEAD_DIM == h, q_pair, 0.0) for h in range(2)],
                             axis=1)
        km_hi, km_lo = split(km_ref[:, hp * LANES:(hp + 1) * LANES])
        q_hi, q_lo = split(q2)
        gate = jnp.dot(jnp.concatenate([km_hi, km_hi, km_lo], axis=1),
                       jnp.concatenate([q_hi, q_lo, q_hi], axis=0), preferred_element_type=F32)
        g = jnp.where(brow < q_blk, gate, -jnp.inf)
        bias = jnp.full((nbr, 2 * tm), MASK_VALUE, F32)
        for _ in range(MOBA_TOPK):
            mx = jnp.max(g, axis=0, keepdims=True)
            first = jnp.min(jnp.where(g == mx, browf, float(nbr)), axis=0, keepdims=True)
            hit = browf == first
            bias = jnp.where(hit & (mx > -jnp.inf), 0.0, bias)
            g = jnp.where(hit, -jnp.inf, g)
        for h in range(2):
            bias_ref[0, 2 * hp + h] = bias[:, h * tm:(h + 1) * tm].astype(BF16)

    ones_rows = (lax.broadcasted_iota(jnp.int32, (V_ROWS - A_HEAD_DIM, MOBA_BLOCK), 0)
                 == 0).astype(BF16)
    for h in range(A_HEADS):
        for n in range(nblk):
            v_ref[0, h, n, 0:A_HEAD_DIM, :] = vt[h * A_HEAD_DIM:(h + 1) * A_HEAD_DIM,
                                                 n * MOBA_BLOCK:(n + 1) * MOBA_BLOCK].astype(BF16)
            v_ref[0, h, n, A_HEAD_DIM:, :] = ones_rows
    gate_pair(0)
    bq_ref[0] = proj(c, B_KEY_WIDTH) * (B_HEAD_K ** -0.5); c += B_KEY_WIDTH
    bk_ref[0] = proj(c, B_KEY_WIDTH); c += B_KEY_WIDTH
    gate_pair(1)
    bv_ref[0] = proj(c, B_VAL_WIDTH); c += B_VAL_WIDTH
    gate_pair(2)
    og_ref[0] = proj(c, B_VAL_WIDTH); c += B_VAL_WIDTH
    gate_pair(3)
    gk = proj(c, LANES).astype(BF16)
    z = jnp.dot(gk, wgk_ref[...], preferred_element_type=F32) + bgk_ref[...]
    gl_ref[0] = _log_sigmoid(z) / GLA_GATE_NORM


def _front0(x, g, w_in, w_gk2, b_gk2):
    bsz, seq, d = x.shape
    tm = min(1024, seq)
    widths = [A_WIDTH, A_WIDTH, A_WIDTH, B_KEY_WIDTH, B_KEY_WIDTH, B_VAL_WIDTH, GLA_GATE_RANK]
    aq, ak, av, bq, bk, bv, bgk, bog = jnp.split(
        w_in, [sum(widths[:n + 1]) for n in range(len(widths))], axis=1)
    gk_pad = LANES - GLA_GATE_RANK
    w = jnp.concatenate([ak, bq, bk, bv, bog, jnp.pad(bgk, ((0, 0), (0, gk_pad)))],
                        axis=1).astype(BF16)
    wt = jnp.concatenate([aq, av], axis=1).T.astype(BF16)
    wgk = jnp.pad(w_gk2, ((0, gk_pad), (0, 0))).astype(BF16)
    ncol = w.shape[1]

    half = A_HEAD_DIM // 2
    inv = ROPE_THETA ** (-jnp.arange(half, dtype=F32) / half)
    pos = jnp.arange(seq, dtype=jnp.int32).astype(F32)
    first = (jnp.arange(LANES) % A_HEAD_DIM) < half

    def tables(feature_major):
        if feature_major:
            ang, reps, sel = inv[:, None] * pos[None, :], (LANES // half, 1), first[:, None]
        else:
            ang, reps, sel = pos[:, None] * inv[None, :], (1, LANES // half), first[None, :]
        cos, sin = jnp.tile(jnp.cos(ang), reps), jnp.tile(jnp.sin(ang), reps)
        return cos, jnp.where(sel, -sin, 0.0), jnp.where(sel, 0.0, sin)

    nb = seq // MOBA_BLOCK
    nbr = _moba_gate_rows(nb)
    row = lambda n: pl.BlockSpec((1, tm, n), lambda b, i: (b, i, 0))
    tab = pl.BlockSpec((tm, LANES), lambda b, i: (i, 0))
    tab_t = pl.BlockSpec((LANES, tm), lambda b, i: (0, i))
    nt = seq // tm
    outs = pl.pallas_call(
        functools.partial(_front0_kernel, tm=tm),
        grid=(bsz, nt),
        in_specs=[row(d), _resident((1, d)), _resident((d, ncol)), _resident((2 * A_WIDTH, d)),
                  _resident((LANES, B_KEY_WIDTH)), _resident((1, B_KEY_WIDTH)),
                  tab, tab, tab, tab_t, tab_t, tab_t],
        out_specs=[pl.BlockSpec((1, A_WIDTH, tm), lambda b, i: (b, 0, i)), row(A_WIDTH),
                   pl.BlockSpec((1, A_HEADS, tm // MOBA_BLOCK, V_ROWS, MOBA_BLOCK),
                                lambda b, i: (b, 0, i, 0, 0)),
                   pl.BlockSpec((1, A_HEADS, nbr, tm), lambda b, i: (b, 0, 0, i)),
                   row(B_KEY_WIDTH), row(B_KEY_WIDTH), row(B_VAL_WIDTH), row(B_KEY_WIDTH),
                   row(B_VAL_WIDTH)],
        out_shape=[jax.ShapeDtypeStruct((bsz, A_WIDTH, seq), BF16),
                   jax.ShapeDtypeStruct((bsz, seq, A_WIDTH), BF16),
                   jax.ShapeDtypeStruct((bsz, A_HEADS, nb, V_ROWS, MOBA_BLOCK), BF16),
                   jax.ShapeDtypeStruct((bsz, A_HEADS, nbr, seq), BF16),
                   jax.ShapeDtypeStruct((bsz, seq, B_KEY_WIDTH), F32),
                   jax.ShapeDtypeStruct((bsz, seq, B_KEY_WIDTH), F32),
                   jax.ShapeDtypeStruct((bsz, seq, B_VAL_WIDTH), F32),
                   jax.ShapeDtypeStruct((bsz, seq, B_KEY_WIDTH), F32),
                   jax.ShapeDtypeStruct((bsz, seq, B_VAL_WIDTH), F32)],
        scratch_shapes=[pltpu.VMEM((nbr, A_WIDTH), F32)],
        compiler_params=_params(("arbitrary", "arbitrary")),
        name="front0",
    )(x, g.reshape(1, d), w, wt, wgk, b_gk2.reshape(1, B_KEY_WIDTH), *tables(False), *tables(True))
    return outs


def _moba_kernel(qt_ref, k_ref, vt_ref, bias_ref, o_ref, rhs_ref, acc_ref, sa_ref, sb_ref, *,
                 group, nb):
    qb = pl.program_id(2)
    blk = MOBA_BLOCK
    qt = qt_ref[0]
    feat = lax.broadcasted_iota(jnp.int32, (LANES, 1), 0)
    nbr = bias_ref.shape[2]
    for h in range(2):
        rhs_ref[:, h * blk:(h + 1) * blk] = jnp.concatenate(
            [jnp.where(feat // A_HEAD_DIM == h, qt, jnp.zeros_like(qt)), bias_ref[0, h],
             jnp.full((MOBA_BLOCKS_PAD - nbr, blk), MASK_VALUE, BF16)], axis=0)

    keys = group * blk
    blk_of_row = lax.broadcasted_iota(jnp.int32, (keys, MOBA_BLOCKS_PAD), 0) // blk
    lane_blk = lax.broadcasted_iota(jnp.int32, (keys, MOBA_BLOCKS_PAD), 1)

    def scores(g, s_ref):
        j0 = g * group
        off = pl.multiple_of(jnp.minimum(j0, nb - group) * blk, blk)
        onehot = (lane_blk == blk_of_row + j0).astype(BF16)
        lhs_j = jnp.concatenate([k_ref[0, pl.ds(off, keys), :], onehot], axis=1)
        s = jnp.dot(lhs_j, rhs_ref[...], preferred_element_type=F32)
        s_ref[...] = s
        return jnp.max(s, axis=0, keepdims=True)

    def accumulate(g, s_ref, m, s_max):
        j0 = g * group
        m_new = jnp.maximum(m, s_max)
        alpha = jnp.exp2(m - m_new)
        p = jnp.exp2(s_ref[...] - m_new).astype(BF16)
        for h in range(2):
            vt = jnp.concatenate([vt_ref[0, h, jnp.minimum(j0 + u, nb - 1)] for u in range(group)],
                                 axis=1)
            acc_ref[h] = (alpha[:, h * blk:(h + 1) * blk] * acc_ref[h]
                          + jnp.dot(vt, p[:, h * blk:(h + 1) * blk], preferred_element_type=F32))
        return m_new

    def body(i, carry):
        m, max_a = carry
        max_b = scores(2 * i + 1, sb_ref)
        m = accumulate(2 * i, sa_ref, m, max_a)
        max_a = scores(2 * i + 2, sa_ref)
        m = accumulate(2 * i + 1, sb_ref, m, max_b)
        return m, max_a

    max_0 = scores(0, sa_ref)

    own = pl.multiple_of(qb * blk, blk)
    lhs = jnp.concatenate([k_ref[0, pl.ds(own, blk), :], jnp.zeros((blk, MOBA_BLOCKS_PAD), BF16)],
                          axis=1)
    causal = (lax.broadcasted_iota(jnp.int32, (blk, 2 * blk), 0)
              <= lax.broadcasted_iota(jnp.int32, (blk, 2 * blk), 1) % blk)
    s = jnp.where(causal, jnp.dot(lhs, rhs_ref[...], preferred_element_type=F32), MASK_VALUE)
    m0 = jnp.max(s, axis=0, keepdims=True)
    p = jnp.exp2(s - m0).astype(BF16)
    for h in range(2):
        acc_ref[h] = jnp.dot(vt_ref[0, h, qb], p[:, h * blk:(h + 1) * blk],
                             preferred_element_type=F32)

    n_groups = (qb + group - 1) // group
    m, max_a = lax.fori_loop(0, n_groups // 2, body, (m0, max_0))

    @pl.when(n_groups % 2 == 1)
    def _():
        accumulate(n_groups - 1, sa_ref, m, max_a)

    outs = []
    for h in range(2):
        a = acc_ref[h]
        outs.append(a[0:A_HEAD_DIM] / a[A_HEAD_DIM:A_HEAD_DIM + 1])
    o_ref[0] = jnp.concatenate(outs, axis=0).T


def _moba(qt, k, vt, bias):
    bsz, seq, _ = k.shape
    nb = seq // MOBA_BLOCK
    nbr = _moba_gate_rows(nb)
    npair = A_WIDTH // LANES
    group = 4 if nb % 4 == 0 else 1
    return pl.pallas_call(
        functools.partial(_moba_kernel, group=group, nb=nb),
        grid=(bsz, npair, nb),
        in_specs=[pl.BlockSpec((1, LANES, MOBA_BLOCK), lambda b, hp, i: (b, hp, i)),
                  pl.BlockSpec((1, seq, LANES), lambda b, hp, i: (b, 0, hp)),
                  pl.BlockSpec((1, 2, nb, V_ROWS, MOBA_BLOCK), lambda b, hp, i: (b, hp, 0, 0, 0)),
                  pl.BlockSpec((1, 2, nbr, MOBA_BLOCK), lambda b, hp, i: (b, hp, 0, i))],
        out_specs=pl.BlockSpec((1, MOBA_BLOCK, LANES), lambda b, hp, i: (b, i, hp)),
        out_shape=jax.ShapeDtypeStruct((bsz, seq, A_WIDTH), F32),
        scratch_shapes=[pltpu.VMEM((2 * LANES, 2 * MOBA_BLOCK), BF16),
                        pltpu.VMEM((2, V_ROWS, MOBA_BLOCK), F32),
                        pltpu.VMEM((group * MOBA_BLOCK, 2 * MOBA_BLOCK), F32),
                        pltpu.VMEM((group * MOBA_BLOCK, 2 * MOBA_BLOCK), F32)],
        compiler_params=_params(("arbitrary", "arbitrary", "arbitrary")),
        name="moba",
    )(qt, k, vt, bias)


def _gla_kernel(q_ref, k_ref, v_ref, g_ref, og_ref, ng_ref, o_ref, st_ref, *, tm):
    @pl.when(pl.program_id(1) == 0)
    def _():
        st_ref[...] = jnp.zeros_like(st_ref)

    q, k, v, g = q_ref[0], k_ref[0], v_ref[0], g_ref[0]
    kw, vw = B_KEY_WIDTH, B_VAL_WIDTH
    row = lax.broadcasted_iota(jnp.int32, (tm, 1), 0)
    r_sub = row % GLA_SUB
    r_chk = row % GLA_CHUNK

    def seg_cumsum(x, r, length):
        s = 1
        while s < length:
            x = x + jnp.where(r >= s, pltpu.roll(x, s, 0), 0.0)
            s *= 2
        return x

    c_sub = seg_cumsum(g, r_sub, GLA_SUB)
    b = seg_cumsum(g, r_chk, GLA_CHUNK)

    head_kv = (lax.broadcasted_iota(jnp.int32, (kw, vw), 0) // B_HEAD_K
               == lax.broadcasted_iota(jnp.int32, (kw, vw), 1) // B_HEAD_V).astype(BF16)
    head_vk = (lax.broadcasted_iota(jnp.int32, (vw, kw), 0) // B_HEAD_V
               == lax.broadcasted_iota(jnp.int32, (vw, kw), 1) // B_HEAD_K)
    head_vv = (lax.broadcasted_iota(jnp.int32, (vw, vw), 0) // B_HEAD_V
               == lax.broadcasted_iota(jnp.int32, (vw, vw), 1) // B_HEAD_V).astype(BF16)

    o_acc = jnp.zeros((tm, vw), F32)
    for d in range(GLA_SUB):
        if d == 0:
            w = q * k
            vv = v
        else:
            w = jnp.where(r_sub >= d,
                          q * pltpu.roll(k, d, 0) * jnp.exp(c_sub - pltpu.roll(c_sub, d, 0)), 0.0)
            vv = pltpu.roll(v, d, 0)
        o_acc = o_acc + jnp.dot(w.astype(BF16), head_kv, preferred_element_type=F32) * vv

    q_sub = q * jnp.exp(c_sub)
    q_chk = q * jnp.exp(b)

    nsub = GLA_CHUNK // GLA_SUB
    rr = lax.broadcasted_iota(jnp.int32, (GLA_CHUNK, 1), 0)
    lane_k = lax.broadcasted_iota(jnp.int32, (1, (nsub - 1) * kw), 1)
    lane_v = lax.broadcasted_iota(jnp.int32, (1, vw), 1)
    outs = []
    for c in range(tm // GLA_CHUNK):
        lo = c * GLA_CHUNK
        bc = b[lo:lo + GLA_CHUNK]
        kc = k[lo:lo + GLA_CHUNK]
        vc = v[lo:lo + GLA_CHUNK].astype(BF16)
        qs = q_sub[lo:lo + GLA_CHUNK]

        rhs_parts, lhs_parts = [], []
        for i in range(1, nsub):
            b_end = bc[i * GLA_SUB - 1:i * GLA_SUB, :]
            decay = jnp.exp(jnp.minimum(b_end - bc, 0.0))
            rhs_parts.append(jnp.where(rr < i * GLA_SUB, kc * decay, 0.0).astype(BF16))
            lhs_parts.append(jnp.where(rr // GLA_SUB == i, qs, 0.0).astype(BF16))
        rhs = jnp.concatenate(rhs_parts, axis=1)
        lhs_t = jnp.concatenate(lhs_parts, axis=1)
        lhs = jnp.concatenate(
            [jnp.where((lane_k % kw) // B_HEAD_K == h, lhs_t, jnp.zeros_like(lhs_t))
             for h in range(B_HEADS)], axis=0)
        attn = lax.dot_general(lhs, rhs, (((1,), (1,)), ((), ())), preferred_element_type=F32)
        res = jnp.dot(attn.astype(BF16), vc, preferred_element_type=F32)
        o_off = jnp.zeros((GLA_CHUNK, vw), F32)
        for h in range(B_HEADS):
            o_off = o_off + jnp.where(lane_v // B_HEAD_V == h,
                                      res[h * GLA_CHUNK:(h + 1) * GLA_CHUNK], 0.0)

        st = st_ref[...]
        o_int = lax.dot_general(q_chk[lo:lo + GLA_CHUNK].astype(BF16), st.astype(BF16),
                                (((1,), (1,)), ((), ())), preferred_element_type=F32)
        b_last = bc[GLA_CHUNK - 1:GLA_CHUNK, :]
        ks = (kc * jnp.exp(b_last - bc)).astype(BF16)
        upd = lax.dot_general(vc, ks, (((0,), (0,)), ((), ())), preferred_element_type=F32)
        st_ref[...] = st * jnp.exp(b_last) + jnp.where(head_vk, upd, 0.0)
        outs.append(o_off + o_int)

    o = o_acc + jnp.concatenate(outs, axis=0)

    sq = o * o
    sq_hi = sq.astype(BF16)
    sq_lo = (sq - sq_hi.astype(F32)).astype(BF16)
    ms = (jnp.dot(sq_hi, head_vv, preferred_element_type=F32)
          + jnp.dot(sq_lo, head_vv, preferred_element_type=F32)) * (1.0 / B_HEAD_V)
    og = og_ref[0]
    o_ref[0] = (o * lax.rsqrt(ms + NORM_EPS) * ng_ref[...]) * (og * jax.nn.sigmoid(og))


def _gla(bq, bk, bv, gl, og, norm_g):
    bsz, seq, _ = bq.shape
    tm = min(256, seq)
    row = lambda n: pl.BlockSpec((1, tm, n), lambda b, i: (b, i, 0))
    return pl.pallas_call(
        functools.partial(_gla_kernel, tm=tm),
        grid=(bsz, seq // tm),
        in_specs=[row(B_KEY_WIDTH), row(B_KEY_WIDTH), row(B_VAL_WIDTH), row(B_KEY_WIDTH),
                  row(B_VAL_WIDTH), _resident((1, B_VAL_WIDTH))],
        out_specs=row(B_VAL_WIDTH),
        out_shape=jax.ShapeDtypeStruct((bsz, seq, B_VAL_WIDTH), F32),
        scratch_shapes=[pltpu.VMEM((B_VAL_WIDTH, B_KEY_WIDTH), F32)],
        compiler_params=_params(("arbitrary", "arbitrary")),
        name="gla",
    )(bq, bk, bv, gl, og, norm_g.reshape(1, B_VAL_WIDTH))


def _ffn_kernel(*refs, tm, bn, dff, final_norm, mixer):
    if mixer:
        x_ref, oa_ref, ob_ref, wa_ref, wb_ref = refs[:5]
        refs = refs[5:]
    else:
        x_ref, refs = refs[0], refs[1:]
    g_ref, wup_ref, cw_ref, cb_ref, wdn_ref, fg_ref, o_ref, halo_ref, ext_ref = refs
    nkb = dff // bn

    @pl.when(pl.program_id(1) == 0)
    def _():
        halo_ref[...] = jnp.zeros_like(halo_ref)

    x = x_ref[0]
    if mixer:
        o_ref[0] = (jnp.dot(oa_ref[0].astype(BF16), wa_ref[...], preferred_element_type=F32)
                    + jnp.dot(ob_ref[0].astype(BF16), wb_ref[...], preferred_element_type=F32))
        x = x + o_ref[0]
    hn = _rms(x, g_ref[...]).astype(BF16)

    def up_stage(kb):
        for part in range(2):
            c0 = part * dff + kb * bn
            up = jnp.dot(hn, wup_ref[:, c0:c0 + bn], preferred_element_type=F32)
            ext_ref[kb % 2, part, 0:SUBLANES, :] = halo_ref[:, c0:c0 + bn]
            ext_ref[kb % 2, part, SUBLANES:, :] = up
            halo_ref[:, c0:c0 + bn] = up[tm - SUBLANES:, :]

    def gate_stage(kb):
        halves = []
        for part in range(2):
            c0 = part * dff + kb * bn
            y = cb_ref[:, c0:c0 + bn]
            for i in range(FFN_CONV):
                off = SUBLANES - (FFN_CONV - 1) + i
                y = y + ext_ref[kb % 2, part, pl.ds(off, tm), :] * cw_ref[i:i + 1, c0:c0 + bn]
            halves.append(y)
        return (halves[0] * _gelu_tanh(halves[1])).astype(BF16)

    up_stage(0)
    if nkb > 1:
        up_stage(1)
    a = gate_stage(0)
    acc = x
    for kb in range(nkb):
        if kb + 2 < nkb:
            up_stage(kb + 2)
        a_next = gate_stage(kb + 1) if kb + 1 < nkb else None
        acc = acc + jnp.dot(a, wdn_ref[kb * bn:(kb + 1) * bn, :], preferred_element_type=F32)
        a = a_next
    if final_norm:
        acc = _rms(acc, fg_ref[...])
    o_ref[0] = acc


def _conv_ffn(x, g, w_up, conv_w, conv_b, w_down, final_g, final_norm, mixer=None):
    bsz, seq, d = x.shape
    dff = w_down.shape[0]
    tm = min(512, seq)
    bn = 1536
    rows = lambda n: pl.BlockSpec((1, tm, n), lambda b, i: (b, i, 0))
    row = rows(d)
    lead_specs, lead_args = [row], [x]
    if mixer is not None:
        o_a, o_b, w_out = mixer
        w = w_out.astype(BF16)
        lead_specs += [rows(A_WIDTH), rows(B_VAL_WIDTH),
                       _resident((A_WIDTH, d)), _resident((B_VAL_WIDTH, d))]
        lead_args += [o_a, o_b, w[:A_WIDTH], w[A_WIDTH:]]
    return pl.pallas_call(
        functools.partial(_ffn_kernel, tm=tm, bn=bn, dff=dff, final_norm=final_norm,
                          mixer=mixer is not None),
        grid=(bsz, seq // tm),
        in_specs=lead_specs + [_resident((1, d)), _resident((d, 2 * dff)),
                               _resident((FFN_CONV, 2 * dff)), _resident((1, 2 * dff)),
                               _resident((dff, d)), _resident((1, d))],
        out_specs=row,
        out_shape=jax.ShapeDtypeStruct((bsz, seq, d), F32),
        scratch_shapes=[pltpu.VMEM((SUBLANES, 2 * dff), F32),
                        pltpu.VMEM((2, 2, tm + SUBLANES, bn), F32)],
        compiler_params=_params(("arbitrary", "arbitrary")),
        name="conv_ffn_final" if final_norm else "conv_ffn",
    )(*lead_args, g.reshape(1, d), w_up.astype(BF16), conv_w, conv_b.reshape(1, 2 * dff),
      w_down.astype(BF16), final_g.reshape(1, d))


def _rglru_kernel(x_ref, g_ref, win_ref, cw_ref, cb_ref, wa_ref, ba_ref, wx_ref, bx_ref,
                  lam_ref, wout_ref, o_ref, ext_ref, halo_ref, hlast_ref, *, tm, drnn, nseq):
    seg = tm // SUBLANES
    shifts = RG_CONV - 1
    bw = drnn // RG_BLOCKS
    seqs = range(nseq)

    @pl.when(pl.program_id(1) == 0)
    def _():
        halo_ref[...] = jnp.zeros_like(halo_ref)
        hlast_ref[...] = jnp.zeros_like(hlast_ref)

    prow = lax.broadcasted_iota(jnp.int32, (tm, tm), 0)
    pcol = lax.broadcasted_iota(jnp.int32, (tm, tm), 1)
    perm = (pcol == (prow % SUBLANES) * seg + prow // SUBLANES).astype(BF16)
    unperm = (prow == (pcol % SUBLANES) * seg + pcol // SUBLANES).astype(BF16)
    first_sub = lax.broadcasted_iota(jnp.int32, (SUBLANES, 1), 0) == 0

    xs = [x_ref[b] for b in seqs]
    hn = [jnp.dot(perm, _rms(xs[b], g_ref[...]).astype(BF16),
                  preferred_element_type=F32).astype(BF16) for b in seqs]
    gate = [jnp.dot(hn[b], win_ref[:, 0:drnn], preferred_element_type=F32) for b in seqs]
    xr_raw = [jnp.dot(hn[b], win_ref[:, drnn:2 * drnn], preferred_element_type=F32) for b in seqs]

    xr = []
    for b in seqs:
        for r in range(shifts):
            lo = (seg - shifts + r) * SUBLANES
            cur = xr_raw[b][lo:lo + SUBLANES]
            ext_ref[b, r * SUBLANES:(r + 1) * SUBLANES, :] = jnp.where(
                first_sub, pltpu.roll(halo_ref[b, r], 1, 0), pltpu.roll(cur, 1, 0))
            halo_ref[b, r] = cur
        ext_ref[b, shifts * SUBLANES:, :] = xr_raw[b]
        acc = cb_ref[...]
        for i in range(RG_CONV):
            acc = acc + ext_ref[b, i * SUBLANES:i * SUBLANES + tm, :] * cw_ref[i:i + 1, :]
        xr.append(acc)

    ys = []
    for b in seqs:
        xb = xr[b].astype(BF16)
        ra, ri = [], []
        for n in range(RG_BLOCKS):
            xn = xb[:, n * bw:(n + 1) * bw]
            ra.append(jnp.dot(xn, wa_ref[n], preferred_element_type=F32))
            ri.append(jnp.dot(xn, wx_ref[n], preferred_element_type=F32))
        r = jax.nn.sigmoid(jnp.concatenate(ra, axis=1) + ba_ref[...])
        ig = jax.nn.sigmoid(jnp.concatenate(ri, axis=1) + bx_ref[...])
        log_a = r * lam_ref[...]
        a = jnp.exp(log_a)
        th = jnp.tanh(log_a)
        one_minus_a2 = -2.0 * th / (1.0 - th)
        u = jnp.sqrt(one_minus_a2) * (ig * xr[b])

        h_loc, a_cum = [u[0:SUBLANES]], [a[0:SUBLANES]]
        for j in range(1, seg):
            aj = a[j * SUBLANES:(j + 1) * SUBLANES]
            h_loc.append(aj * h_loc[-1] + u[j * SUBLANES:(j + 1) * SUBLANES])
            a_cum.append(aj * a_cum[-1])
        h_in = [hlast_ref[b]]
        for s in range(SUBLANES):
            h_in.append(h_loc[-1][s:s + 1] + a_cum[-1][s:s + 1] * h_in[-1])
        hlast_ref[b] = h_in[SUBLANES]
        h_seg = jnp.concatenate(h_in[:SUBLANES], axis=0)
        h = jnp.concatenate([h_loc[j] + a_cum[j] * h_seg for j in range(seg)], axis=0)
        ys.append((h * _gelu_tanh(gate[b])).astype(BF16))

    for b in seqs:
        y_t = jnp.dot(unperm, ys[b], preferred_element_type=F32).astype(BF16)
        o_ref[b] = xs[b] + jnp.dot(y_t, wout_ref[...], preferred_element_type=F32)


def _rglru(x, g, w_in, conv_w, conv_b, w_a, b_a, w_x, b_x, lam, w_out):
    bsz, seq, d = x.shape
    drnn = w_out.shape[0]
    bw = drnn // RG_BLOCKS
    tm = min(256, seq)
    nseq = 2 if bsz % 2 == 0 else 1
    row = pl.BlockSpec((nseq, tm, d), lambda b, i: (b, i, 0))
    lam_c = (-LRU_C) * jax.nn.softplus(-lam.astype(F32))
    return pl.pallas_call(
        functools.partial(_rglru_kernel, tm=tm, drnn=drnn, nseq=nseq),
        grid=(bsz // nseq, seq // tm),
        in_specs=[row, _resident((1, d)), _resident((d, 2 * drnn)), _resident((RG_CONV, drnn)),
                  _resident((1, drnn)), _resident((RG_BLOCKS, bw, bw)), _resident((1, drnn)),
                  _resident((RG_BLOCKS, bw, bw)), _resident((1, drnn)), _resident((1, drnn)),
                  _resident((drnn, d))],
        out_specs=row,
        out_shape=jax.ShapeDtypeStruct((bsz, seq, d), F32),
        scratch_shapes=[pltpu.VMEM((nseq, tm + (RG_CONV - 1) * SUBLANES, drnn), F32),
                        pltpu.VMEM((nseq, RG_CONV - 1, SUBLANES, drnn), F32),
                        pltpu.VMEM((nseq, 1, drnn), F32)],
        compiler_params=_params(("arbitrary", "arbitrary")),
        name="rglru",
    )(x, g.reshape(1, d), w_in.astype(BF16), conv_w, conv_b.reshape(1, drnn),
      w_a.astype(BF16), b_a.reshape(1, drnn), w_x.astype(BF16), b_x.reshape(1, drnn),
      lam_c.reshape(1, drnn), w_out.astype(BF16))


def kernel(x, mix_norm_g, ffn_norm_g, final_norm_g, ev_w_in, ev_w_gk2, ev_b_gk2, ev_gla_norm_g,
           ev_w_out, od_w_in, od_conv_w, od_conv_b, od_w_a, od_b_a, od_w_x, od_b_x, od_lambda,
           od_w_out, ffn_w_up, ffn_conv_w, ffn_conv_b, ffn_w_down):
    depth = mix_norm_g.shape[0]
    h = x
    for l in range(depth):
        j = l // 2
        mixer = None
        if l % 2 == 0:
            q, k, v, bias, bq, bk, bv, gl, og = _front0(h, mix_norm_g[l], ev_w_in[j], ev_w_gk2[j],
                                                      ev_b_gk2[j])
            o_a = _moba(q, k, v, bias)
            o_b = _gla(bq, bk, bv, gl, og, ev_gla_norm_g[j])
            mixer = (o_a, o_b, ev_w_out[j])
        else:
            h = _rglru(h, mix_norm_g[l], od_w_in[j], od_conv_w[j], od_conv_b[j], od_w_a[j],
                       od_b_a[j], od_w_x[j], od_b_x[j], od_lambda[j], od_w_out[j])
        h = _conv_ffn(h, ffn_norm_g[l], ffn_w_up[l], ffn_conv_w[l], ffn_conv_b[l], ffn_w_down[l],
                      final_norm_g, final_norm=(l == depth - 1), mixer=mixer)
    return h
```

```python
import functools

import jax
import jax.numpy as jnp
from jax import lax
from jax.experimental import pallas as pl
from jax.experimental.pallas import tpu as pltpu

F32 = jnp.float32
BF16 = jnp.bfloat16

A_HEADS = 8
A_HEAD_DIM = 64
A_WIDTH = A_HEADS * A_HEAD_DIM
MOBA_BLOCK = 256
MOBA_TOPK = 3
ROPE_THETA = 10000.0
B_HEADS = 8
B_HEAD_K = 32
B_HEAD_V = 64
B_KEY_WIDTH = B_HEADS * B_HEAD_K
B_VAL_WIDTH = B_HEADS * B_HEAD_V
GLA_GATE_RANK = 16
GLA_GATE_NORM = 16.0
RG_BLOCKS = 4
RG_CONV = 4
LRU_C = 8.0
FFN_CONV = 3
NORM_EPS = 1e-6
MASK_VALUE = -1e30

LANES = 128
SUBLANES = 8
BF16_ROWS = 16
LOG2_E = 1.4426950408889634
VMEM_LIMIT = 56 * 1024 * 1024

GLA_CHUNK = 64
MOBA_BLOCKS_PAD = LANES
V_ROWS = A_HEAD_DIM + BF16_ROWS


def _params(sem):
    return pltpu.CompilerParams(dimension_semantics=sem, vmem_limit_bytes=VMEM_LIMIT)


def _rms(x, g):
    ms = jnp.mean(x * x, axis=-1, keepdims=True)
    return x * lax.rsqrt(ms + NORM_EPS) * g


def _gelu_tanh(x):
    return 0.5 * x * (1.0 + jnp.tanh(0.7978845608028654 * (x + 0.044715 * (x * x * x))))


def _log_sigmoid(z):
    return jnp.minimum(z, 0.0) - jnp.log1p(jnp.exp(-jnp.abs(z)))


def _moba_gate_rows(nb):
    assert nb <= MOBA_BLOCKS_PAD
    return -(-nb // BF16_ROWS) * BF16_ROWS


def _resident(shape):
    nd = len(shape)
    return pl.BlockSpec(shape, lambda *_: (0,) * nd, pipeline_mode=pl.Buffered(1))


def _front0_kernel(x_ref, g_ref, w_ref, wt_ref, wgk_ref, bgk_ref, cos_ref, sa_ref, sb_ref,
                   cost_ref, sat_ref, sbt_ref,
                   q_ref, k_ref, v_ref, bias_ref, bq_ref, bk_ref, bv_ref, gl_ref, og_ref,
                   km_ref, *, tm):
    @pl.when(pl.program_id(1) == 0)
    def _():
        km_ref[...] = jnp.zeros_like(km_ref)

    hn = _rms(x_ref[0], g_ref[...]).astype(BF16)

    def proj(c0, n):
        return jnp.dot(hn, w_ref[:, c0:c0 + n], preferred_element_type=F32)

    def proj_t(r0, n):
        return lax.dot_general(wt_ref[r0:r0 + n, :], hn, (((1,), (1,)), ((), ())),
                               preferred_element_type=F32)

    def rot(t, cos, sa, sb, axis):
        parts = []
        for c in range(t.shape[axis] // LANES):
            tc = lax.slice_in_dim(t, c * LANES, (c + 1) * LANES, axis=axis)
            parts.append(tc * cos + pltpu.roll(tc, LANES - 32, axis) * sa
                         + pltpu.roll(tc, 32, axis) * sb)
        return jnp.concatenate(parts, axis=axis)

    qt = rot(proj_t(0, A_WIDTH), cost_ref[...], sat_ref[...], sbt_ref[...], 0)
    q_ref[0] = (qt * (A_HEAD_DIM ** -0.5 * LOG2_E)).astype(BF16)
    vt = proj_t(A_WIDTH, A_WIDTH)
    c = 0
    k = rot(proj(c, A_WIDTH), cos_ref[...], sa_ref[...], sb_ref[...], 1); c += A_WIDTH
    k_ref[0] = k.astype(BF16)
    nblk = tm // MOBA_BLOCK
    kmean = jnp.mean(k.reshape(nblk, MOBA_BLOCK, A_WIDTH), axis=1)
    blk0 = pl.program_id(1) * nblk
    for n in range(nblk):
        km_ref[pl.ds(blk0 + n, 1), :] = kmean[n:n + 1]

    nbr = km_ref.shape[0]
    feat = lax.broadcasted_iota(jnp.int32, (LANES, 1), 0)
    brow = lax.broadcasted_iota(jnp.int32, (nbr, 2 * tm), 0)
    browf = brow.astype(F32)
    q_blk = blk0 + (lax.broadcasted_iota(jnp.int32, (1, 2 * tm), 1) % tm) // MOBA_BLOCK

    def split(a):
        hi = a.astype(BF16)
        return hi, (a - hi.astype(F32)).astype(BF16)

    def gate_pair(hp):
        q_pair = qt[hp * LANES:(hp + 1) * LANES]
        q2 = jnp.concatenate([jnp.where(feat // A_HEAD_DIM == h, q_pair, 0.0) for h in range(2)],
                             axis=1)
        km_hi, km_lo = split(km_ref[:, hp * LANES:(hp + 1) * LANES])
        q_hi, q_lo = split(q2)
        gate = jnp.dot(jnp.concatenate([km_hi, km_hi, km_lo], axis=1),
                       jnp.concatenate([q_hi, q_lo, q_hi], axis=0), preferred_element_type=F32)
        g = jnp.where(brow < q_blk, gate, -jnp.inf)
        bias = jnp.full((nbr, 2 * tm), MASK_VALUE, F32)
        for _ in range(MOBA_TOPK):
            mx = jnp.max(g, axis=0, keepdims=True)
            first = jnp.min(jnp.where(g == mx, browf, float(nbr)), axis=0, keepdims=True)
            hit = browf == first
            bias = jnp.where(hit & (mx > -jnp.inf), 0.0, bias)
            g = jnp.where(hit, -jnp.inf, g)
        for h in range(2):
            bias_ref[0, 2 * hp + h] = bias[:, h * tm:(h + 1) * tm].astype(BF16)

    ones_rows = (lax.broadcasted_iota(jnp.int32, (V_ROWS - A_HEAD_DIM, MOBA_BLOCK), 0)
                 == 0).astype(BF16)
    for h in range(A_HEADS):
        for n in range(nblk):
            v_ref[0, h, n, 0:A_HEAD_DIM, :] = vt[h * A_HEAD_DIM:(h + 1) * A_HEAD_DIM,
                                                 n * MOBA_BLOCK:(n + 1) * MOBA_BLOCK].astype(BF16)
            v_ref[0, h, n, A_HEAD_DIM:, :] = ones_rows
    gate_pair(0)
    bq_ref[0] = proj(c, B_KEY_WIDTH) * (B_HEAD_K ** -0.5); c += B_KEY_WIDTH
    bk_ref[0] = proj(c, B_KEY_WIDTH); c += B_KEY_WIDTH
    gate_pair(1)
    bv_ref[0] = proj(c, B_VAL_WIDTH); c += B_VAL_WIDTH
    gate_pair(2)
    og_ref[0] = proj(c, B_VAL_WIDTH); c += B_VAL_WIDTH
    gate_pair(3)
    gk = proj(c, LANES).astype(BF16)
    z = jnp.dot(gk, wgk_ref[...], preferred_element_type=F32) + bgk_ref[...]
    gl_ref[0] = _log_sigmoid(z) / GLA_GATE_NORM


def _front0(x, g, w_in, w_gk2, b_gk2):
    bsz, seq, d = x.shape
    tm = min(1024, seq)
    widths = [A_WIDTH, A_WIDTH, A_WIDTH, B_KEY_WIDTH, B_KEY_WIDTH, B_VAL_WIDTH, GLA_GATE_RANK]
    aq, ak, av, bq, bk, bv, bgk, bog = jnp.split(
        w_in, [sum(widths[:n + 1]) for n in range(len(widths))], axis=1)
    gk_pad = LANES - GLA_GATE_RANK
    w = jnp.concatenate([ak, bq, bk, bv, bog, jnp.pad(bgk, ((0, 0), (0, gk_pad)))],
                        axis=1).astype(BF16)
    wt = jnp.concatenate([aq, av], axis=1).T.astype(BF16)
    wgk = jnp.pad(w_gk2, ((0, gk_pad), (0, 0))).astype(BF16)
    ncol = w.shape[1]

    half = A_HEAD_DIM // 2
    inv = ROPE_THETA ** (-jnp.arange(half, dtype=F32) / half)
    pos = jnp.arange(seq, dtype=jnp.int32).astype(F32)
    first = (jnp.arange(LANES) % A_HEAD_DIM) < half

    def tables(feature_major):
        if feature_major:
            ang, reps, sel = inv[:, None] * pos[None, :], (LANES // half, 1), first[:, None]
        else:
            ang, reps, sel = pos[:, None] * inv[None, :], (1, LANES // half), first[None, :]
        cos, sin = jnp.tile(jnp.cos(ang), reps), jnp.tile(jnp.sin(ang), reps)
        return cos, jnp.where(sel, -sin, 0.0), jnp.where(sel, 0.0, sin)

    nb = seq // MOBA_BLOCK
    nbr = _moba_gate_rows(nb)
    row = lambda n: pl.BlockSpec((1, tm, n), lambda b, i: (b, i, 0))
    tab = pl.BlockSpec((tm, LANES), lambda b, i: (i, 0))
    tab_t = pl.BlockSpec((LANES, tm), lambda b, i: (0, i))
    nt = seq // tm
    outs = pl.pallas_call(
        functools.partial(_front0_kernel, tm=tm),
        grid=(bsz, nt),
        in_specs=[row(d), _resident((1, d)), _resident((d, ncol)), _resident((2 * A_WIDTH, d)),
                  _resident((LANES, B_KEY_WIDTH)), _resident((1, B_KEY_WIDTH)),
                  tab, tab, tab, tab_t, tab_t, tab_t],
        out_specs=[pl.BlockSpec((1, A_WIDTH, tm), lambda b, i: (b, 0, i)), row(A_WIDTH),
                   pl.BlockSpec((1, A_HEADS, tm // MOBA_BLOCK, V_ROWS, MOBA_BLOCK),
                                lambda b, i: (b, 0, i, 0, 0)),
                   pl.BlockSpec((1, A_HEADS, nbr, tm), lambda b, i: (b, 0, 0, i)),
                   row(B_KEY_WIDTH), row(B_KEY_WIDTH), row(B_VAL_WIDTH), row(B_KEY_WIDTH),
                   row(B_VAL_WIDTH)],
        out_shape=[jax.ShapeDtypeStruct((bsz, A_WIDTH, seq), BF16),
                   jax.ShapeDtypeStruct((bsz, seq, A_WIDTH), BF16),
                   jax.ShapeDtypeStruct((bsz, A_HEADS, nb, V_ROWS, MOBA_BLOCK), BF16),
                   jax.ShapeDtypeStruct((bsz, A_HEADS, nbr, seq), BF16),
                   jax.ShapeDtypeStruct((bsz, seq, B_KEY_WIDTH), F32),
                   jax.ShapeDtypeStruct((bsz, seq, B_KEY_WIDTH), F32),
                   jax.ShapeDtypeStruct((bsz, seq, B_VAL_WIDTH), F32),
                   jax.ShapeDtypeStruct((bsz, seq, B_KEY_WIDTH), F32),
                   jax.ShapeDtypeStruct((bsz, seq, B_VAL_WIDTH), F32)],
        scratch_shapes=[pltpu.VMEM((nbr, A_WIDTH), F32)],
        compiler_params=_params(("arbitrary", "arbitrary")),
        name="front0",
    )(x, g.reshape(1, d), w, wt, wgk, b_gk2.reshape(1, B_KEY_WIDTH), *tables(False), *tables(True))
    return outs


def _moba_kernel(qt_ref, k_ref, vt_ref, bias_ref, o_ref, rhs_ref, acc_ref, sa_ref, sb_ref, *,
                 group, nb):
    qb = pl.program_id(2)
    blk = MOBA_BLOCK
    qt = qt_ref[0]
    feat = lax.broadcasted_iota(jnp.int32, (LANES, 1), 0)
    nbr = bias_ref.shape[2]
    for h in range(2):
        rhs_ref[:, h * blk:(h + 1) * blk] = jnp.concatenate(
            [jnp.where(feat // A_HEAD_DIM == h, qt, jnp.zeros_like(qt)), bias_ref[0, h],
             jnp.full((MOBA_BLOCKS_PAD - nbr, blk), MASK_VALUE, BF16)], axis=0)

    keys = group * blk
    blk_of_row = lax.broadcasted_iota(jnp.int32, (keys, MOBA_BLOCKS_PAD), 0) // blk
    lane_blk = lax.broadcasted_iota(jnp.int32, (keys, MOBA_BLOCKS_PAD), 1)

    def scores(g, s_ref):
        j0 = g * group
        off = pl.multiple_of(jnp.minimum(j0, nb - group) * blk, blk)
        onehot = (lane_blk == blk_of_row + j0).astype(BF16)
        lhs_j = jnp.concatenate([k_ref[0, pl.ds(off, keys), :], onehot], axis=1)
        s = jnp.dot(lhs_j, rhs_ref[...], preferred_element_type=F32)
        s_ref[...] = s
        return jnp.max(s, axis=0, keepdims=True)

    def accumulate(g, s_ref, m, s_max):
        j0 = g * group
        m_new = jnp.maximum(m, s_max)
        alpha = jnp.exp2(m - m_new)
        p = jnp.exp2(s_ref[...] - m_new).astype(BF16)
        for h in range(2):
            vt = jnp.concatenate([vt_ref[0, h, jnp.minimum(j0 + u, nb - 1)] for u in range(group)],
                                 axis=1)
            acc_ref[h] = (alpha[:, h * blk:(h + 1) * blk] * acc_ref[h]
                          + jnp.dot(vt, p[:, h * blk:(h + 1) * blk], preferred_element_type=F32))
        return m_new

    def body(i, carry):
        m, max_a = carry
        max_b = scores(2 * i + 1, sb_ref)
        m = accumulate(2 * i, sa_ref, m, max_a)
        max_a = scores(2 * i + 2, sa_ref)
        m = accumulate(2 * i + 1, sb_ref, m, max_b)
        return m, max_a

    max_0 = scores(0, sa_ref)

    own = pl.multiple_of(qb * blk, blk)
    lhs = jnp.concatenate([k_ref[0, pl.ds(own, blk), :], jnp.zeros((blk, MOBA_BLOCKS_PAD), BF16)],
                          axis=1)
    causal = (lax.broadcasted_iota(jnp.int32, (blk, 2 * blk), 0)
              <= lax.broadcasted_iota(jnp.int32, (blk, 2 * blk), 1) % blk)
    s = jnp.where(causal, jnp.dot(lhs, rhs_ref[...], preferred_element_type=F32), MASK_VALUE)
    m0 = jnp.max(s, axis=0, keepdims=True)
    p = jnp.exp2(s - m0).astype(BF16)
    for h in range(2):
        acc_ref[h] = jnp.dot(vt_ref[0, h, qb], p[:, h * blk:(h + 1) * blk],
                             preferred_element_type=F32)

    n_groups = (qb + group - 1) // group
    m, max_a = lax.fori_loop(0, n_groups // 2, body, (m0, max_0))

    @pl.when(n_groups % 2 == 1)
    def _():
        accumulate(n_groups - 1, sa_ref, m, max_a)

    outs = []
    for h in range(2):
        a = acc_ref[h]
        outs.append(a[0:A_HEAD_DIM] / a[A_HEAD_DIM:A_HEAD_DIM + 1])
    o_ref[0] = jnp.concatenate(outs, axis=0).T


def _moba(qt, k, vt, bias):
    bsz, seq, _ = k.shape
    nb = seq // MOBA_BLOCK
    nbr = _moba_gate_rows(nb)
    npair = A_WIDTH // LANES
    group = 4 if nb % 4 == 0 else 1
    return pl.pallas_call(
        functools.partial(_moba_kernel, group=group, nb=nb),
        grid=(bsz, npair, nb),
        in_specs=[pl.BlockSpec((1, LANES, MOBA_BLOCK), lambda b, hp, i: (b, hp, i)),
                  pl.BlockSpec((1, seq, LANES), lambda b, hp, i: (b, 0, hp)),
                  pl.BlockSpec((1, 2, nb, V_ROWS, MOBA_BLOCK), lambda b, hp, i: (b, hp, 0, 0, 0)),
                  pl.BlockSpec((1, 2, nbr, MOBA_BLOCK), lambda b, hp, i: (b, hp, 0, i))],
        out_specs=pl.BlockSpec((1, MOBA_BLOCK, LANES), lambda b, hp, i: (b, i, hp)),
        out_shape=jax.ShapeDtypeStruct((bsz, seq, A_WIDTH), F32),
        scratch_shapes=[pltpu.VMEM((2 * LANES, 2 * MOBA_BLOCK), BF16),
                        pltpu.VMEM((2, V_ROWS, MOBA_BLOCK), F32),
                        pltpu.VMEM((group * MOBA_BLOCK, 2 * MOBA_BLOCK), F32),
                        pltpu.VMEM((group * MOBA_BLOCK, 2 * MOBA_BLOCK), F32)],
        compiler_params=_params(("arbitrary", "arbitrary", "arbitrary")),
        name="moba",
    )(qt, k, vt, bias)


def _gla_kernel(q_ref, k_ref, v_ref, g_ref, og_ref, ng_ref, o_ref, st_ref, *, tm):
    @pl.when(pl.program_id(1) == 0)
    def _():
        st_ref[...] = jnp.zeros_like(st_ref)

    q, k, v, g = q_ref[0], k_ref[0], v_ref[0], g_ref[0]
    kw, vw = B_KEY_WIDTH, B_VAL_WIDTH
    row = lax.broadcasted_iota(jnp.int32, (tm, 1), 0)

    pre, post, tot, upper = {1: g}, {1: jnp.zeros_like(g)}, {1: g}, {}
    n = 1
    while n < GLA_CHUNK:
        upper[n] = (row % (2 * n)) >= n
        below = pltpu.roll(tot[n], n, 0)
        above = pltpu.roll(tot[n], tm - n, 0)
        pre[2 * n] = pre[n] + jnp.where(upper[n], below, 0.0)
        post[2 * n] = post[n] + jnp.where(upper[n], 0.0, above)
        tot[2 * n] = tot[n] + jnp.where(upper[n], below, above)
        n *= 2
    levels = sorted(upper)

    q_lvl = {n: jnp.where(upper[n], q * jnp.exp(pre[n]), 0.0).astype(BF16) for n in levels}
    k_lvl = {n: jnp.where(upper[n], 0.0, k * jnp.exp(post[n])).astype(BF16) for n in levels}
    q_bf, k_bf = q.astype(BF16), k.astype(BF16)
    q_chk = (q * jnp.exp(pre[GLA_CHUNK])).astype(BF16)
    k_chk = (k * jnp.exp(post[GLA_CHUNK])).astype(BF16)

    head_vk = (lax.broadcasted_iota(jnp.int32, (vw, kw), 0) // B_HEAD_V
               == lax.broadcasted_iota(jnp.int32, (vw, kw), 1) // B_HEAD_K)
    head_vv = (lax.broadcasted_iota(jnp.int32, (vw, vw), 0) // B_HEAD_V
               == lax.broadcasted_iota(jnp.int32, (vw, vw), 1) // B_HEAD_V).astype(BF16)

    tt = lax.broadcasted_iota(jnp.int32, (B_HEADS * GLA_CHUNK, GLA_CHUNK), 0) % GLA_CHUNK
    ss = lax.broadcasted_iota(jnp.int32, (B_HEADS * GLA_CHUNK, GLA_CHUNK), 1)
    same_row = tt == ss
    splits_at = {n: (tt // (2 * n) == ss // (2 * n)) & (tt // n != ss // n) for n in levels}
    lane_k = lax.broadcasted_iota(jnp.int32, (1, kw), 1)
    lane_v = lax.broadcasted_iota(jnp.int32, (1, vw), 1)

    def stack_heads(x):
        return jnp.concatenate([jnp.where(lane_k // B_HEAD_K == h, x, jnp.zeros_like(x))
                                for h in range(B_HEADS)], axis=0)

    def scores(lhs, rhs):
        return lax.dot_general(stack_heads(lhs), rhs, (((1,), (1,)), ((), ())),
                               preferred_element_type=F32)

    outs = []
    for c in range(tm // GLA_CHUNK):
        lo = c * GLA_CHUNK
        rows = slice(lo, lo + GLA_CHUNK)
        vc = v[rows].astype(BF16)

        attn = jnp.where(same_row, scores(q_bf[rows], k_bf[rows]), 0.0)
        for n in levels:
            attn = jnp.where(splits_at[n], scores(q_lvl[n][rows], k_lvl[n][rows]), attn)
        res = jnp.dot(attn.astype(BF16), vc, preferred_element_type=F32)
        o_in = jnp.zeros((GLA_CHUNK, vw), F32)
        for h in range(B_HEADS):
            o_in = o_in + jnp.where(lane_v // B_HEAD_V == h,
                                    res[h * GLA_CHUNK:(h + 1) * GLA_CHUNK], 0.0)

        st = st_ref[...]
        o_st = lax.dot_general(q_chk[rows], st.astype(BF16), (((1,), (1,)), ((), ())),
                               preferred_element_type=F32)
        upd = lax.dot_general(vc, k_chk[rows], (((0,), (0,)), ((), ())),
                              preferred_element_type=F32)
        decay = jnp.exp(tot[GLA_CHUNK][lo:lo + 1, :])
        st_ref[...] = st * decay + jnp.where(head_vk, upd, 0.0)
        outs.append(o_in + o_st)

    o = jnp.concatenate(outs, axis=0)

    sq = o * o
    sq_hi = sq.astype(BF16)
    sq_lo = (sq - sq_hi.astype(F32)).astype(BF16)
    ms = (jnp.dot(sq_hi, head_vv, preferred_element_type=F32)
          + jnp.dot(sq_lo, head_vv, preferred_element_type=F32)) * (1.0 / B_HEAD_V)
    og = og_ref[0]
    o_ref[0] = (o * lax.rsqrt(ms + NORM_EPS) * ng_ref[...]) * (og * jax.nn.sigmoid(og))


def _gla(bq, bk, bv, gl, og, norm_g):
    bsz, seq, _ = bq.shape
    tm = min(256, seq)
    row = lambda n: pl.BlockSpec((1, tm, n), lambda b, i: (b, i, 0))
    return pl.pallas_call(
        functools.partial(_gla_kernel, tm=tm),
        grid=(bsz, seq // tm),
        in_specs=[row(B_KEY_WIDTH), row(B_KEY_WIDTH), row(B_VAL_WIDTH), row(B_KEY_WIDTH),
                  row(B_VAL_WIDTH), _resident((1, B_VAL_WIDTH))],
        out_specs=row(B_VAL_WIDTH),
        out_shape=jax.ShapeDtypeStruct((bsz, seq, B_VAL_WIDTH), F32),
        scratch_shapes=[pltpu.VMEM((B_VAL_WIDTH, B_KEY_WIDTH), F32)],
        compiler_params=_params(("arbitrary", "arbitrary")),
        name="gla",
    )(bq, bk, bv, gl, og, norm_g.reshape(1, B_VAL_WIDTH))


def _ffn_kernel(*refs, tm, bn, dff, final_norm, mixer):
    if mixer:
        x_ref, oa_ref, ob_ref, wa_ref, wb_ref = refs[:5]
        refs = refs[5:]
    else:
        x_ref, refs = refs[0], refs[1:]
    g_ref, wup_ref, cw_ref, cb_ref, wdn_ref, fg_ref, o_ref, halo_ref, ext_ref = refs
    nkb = dff // bn

    @pl.when(pl.program_id(1) == 0)
    def _():
        halo_ref[...] = jnp.zeros_like(halo_ref)

    x = x_ref[0]
    if mixer:
        o_ref[0] = (jnp.dot(oa_ref[0].astype(BF16), wa_ref[...], preferred_element_type=F32)
                    + jnp.dot(ob_ref[0].astype(BF16), wb_ref[...], preferred_element_type=F32))
        x = x + o_ref[0]
    hn = _rms(x, g_ref[...]).astype(BF16)

    def up_stage(kb):
        for part in range(2):
            c0 = part * dff + kb * bn
            up = jnp.dot(hn, wup_ref[:, c0:c0 + bn], preferred_element_type=F32)
            ext_ref[kb % 2, part, 0:SUBLANES, :] = halo_ref[:, c0:c0 + bn]
            ext_ref[kb % 2, part, SUBLANES:, :] = up
            halo_ref[:, c0:c0 + bn] = up[tm - SUBLANES:, :]

    def gate_stage(kb):
        halves = []
        for part in range(2):
            c0 = part * dff + kb * bn
            y = cb_ref[:, c0:c0 + bn]
            for i in range(FFN_CONV):
                off = SUBLANES - (FFN_CONV - 1) + i
                y = y + ext_ref[kb % 2, part, pl.ds(off, tm), :] * cw_ref[i:i + 1, c0:c0 + bn]
            halves.append(y)
        return (halves[0] * _gelu_tanh(halves[1])).astype(BF16)

    up_stage(0)
    if nkb > 1:
        up_stage(1)
    a = gate_stage(0)
    acc = x
    for kb in range(nkb):
        if kb + 2 < nkb:
            up_stage(kb + 2)
        a_next = gate_stage(kb + 1) if kb + 1 < nkb else None
        acc = acc + jnp.dot(a, wdn_ref[kb * bn:(kb + 1) * bn, :], preferred_element_type=F32)
        a = a_next
    if final_norm:
        acc = _rms(acc, fg_ref[...])
    o_ref[0] = acc


def _conv_ffn(x, g, w_up, conv_w, conv_b, w_down, final_g, final_norm, mixer=None):
    bsz, seq, d = x.shape
    dff = w_down.shape[0]
    tm = min(512, seq)
    bn = 1536
    rows = lambda n: pl.BlockSpec((1, tm, n), lambda b, i: (b, i, 0))
    row = rows(d)
    lead_specs, lead_args = [row], [x]
    if mixer is not None:
        o_a, o_b, w_out = mixer
        w = w_out.astype(BF16)
        lead_specs += [rows(A_WIDTH), rows(B_VAL_WIDTH),
                       _resident((A_WIDTH, d)), _resident((B_VAL_WIDTH, d))]
        lead_args += [o_a, o_b, w[:A_WIDTH], w[A_WIDTH:]]
    return pl.pallas_call(
        functools.partial(_ffn_kernel, tm=tm, bn=bn, dff=dff, final_norm=final_norm,
                          mixer=mixer is not None),
        grid=(bsz, seq // tm),
        in_specs=lead_specs + [_resident((1, d)), _resident((d, 2 * dff)),
                               _resident((FFN_CONV, 2 * dff)), _resident((1, 2 * dff)),
                               _resident((dff, d)), _resident((1, d))],
        out_specs=row,
        out_shape=jax.ShapeDtypeStruct((bsz, seq, d), F32),
        scratch_shapes=[pltpu.VMEM((SUBLANES, 2 * dff), F32),
                        pltpu.VMEM((2, 2, tm + SUBLANES, bn), F32)],
        compiler_params=_params(("arbitrary", "arbitrary")),
        name="conv_ffn_final" if final_norm else "conv_ffn",
    )(*lead_args, g.reshape(1, d), w_up.astype(BF16), conv_w, conv_b.reshape(1, 2 * dff),
      w_down.astype(BF16), final_g.reshape(1, d))


def _rglru_kernel(x_ref, g_ref, win_ref, cw_ref, cb_ref, wa_ref, ba_ref, wx_ref, bx_ref,
                  lam_ref, wout_ref, o_ref, ext_ref, halo_ref, hlast_ref, *, tm, drnn, nseq):
    seg = tm // SUBLANES
    shifts = RG_CONV - 1
    bw = drnn // RG_BLOCKS
    seqs = range(nseq)

    @pl.when(pl.program_id(1) == 0)
    def _():
        halo_ref[...] = jnp.zeros_like(halo_ref)
        hlast_ref[...] = jnp.zeros_like(hlast_ref)

    prow = lax.broadcasted_iota(jnp.int32, (tm, tm), 0)
    pcol = lax.broadcasted_iota(jnp.int32, (tm, tm), 1)
    perm = (pcol == (prow % SUBLANES) * seg + prow // SUBLANES).astype(BF16)
    unperm = (prow == (pcol % SUBLANES) * seg + pcol // SUBLANES).astype(BF16)
    first_sub = lax.broadcasted_iota(jnp.int32, (SUBLANES, 1), 0) == 0

    xs = [x_ref[b] for b in seqs]
    hn = [jnp.dot(perm, _rms(xs[b], g_ref[...]).astype(BF16),
                  preferred_element_type=F32).astype(BF16) for b in seqs]
    gate = [jnp.dot(hn[b], win_ref[:, 0:drnn], preferred_element_type=F32) for b in seqs]
    xr_raw = [jnp.dot(hn[b], win_ref[:, drnn:2 * drnn], preferred_element_type=F32) for b in seqs]

    xr = []
    for b in seqs:
        for r in range(shifts):
            lo = (seg - shifts + r) * SUBLANES
            cur = xr_raw[b][lo:lo + SUBLANES]
            ext_ref[b, r * SUBLANES:(r + 1) * SUBLANES, :] = jnp.where(
                first_sub, pltpu.roll(halo_ref[b, r], 1, 0), pltpu.roll(cur, 1, 0))
            halo_ref[b, r] = cur
        ext_ref[b, shifts * SUBLANES:, :] = xr_raw[b]
        acc = cb_ref[...]
        for i in range(RG_CONV):
            acc = acc + ext_ref[b, i * SUBLANES:i * SUBLANES + tm, :] * cw_ref[i:i + 1, :]
        xr.append(acc)

    ys = []
    for b in seqs:
        xb = xr[b].astype(BF16)
        ra, ri = [], []
        for n in range(RG_BLOCKS):
            xn = xb[:, n * bw:(n + 1) * bw]
            ra.append(jnp.dot(xn, wa_ref[n], preferred_element_type=F32))
            ri.append(jnp.dot(xn, wx_ref[n], preferred_element_type=F32))
        r = jax.nn.sigmoid(jnp.concatenate(ra, axis=1) + ba_ref[...])
        ig = jax.nn.sigmoid(jnp.concatenate(ri, axis=1) + bx_ref[...])
        log_a = r * lam_ref[...]
        a = jnp.exp(log_a)
        th = jnp.tanh(log_a)
        one_minus_a2 = -2.0 * th / (1.0 - th)
        u = jnp.sqrt(one_minus_a2) * (ig * xr[b])

        h_loc, a_cum = [u[0:SUBLANES]], [a[0:SUBLANES]]
        for j in range(1, seg):
            aj = a[j * SUBLANES:(j + 1) * SUBLANES]
            h_loc.append(aj * h_loc[-1] + u[j * SUBLANES:(j + 1) * SUBLANES])
            a_cum.append(aj * a_cum[-1])
        h_in = [hlast_ref[b]]
        for s in range(SUBLANES):
            h_in.append(h_loc[-1][s:s + 1] + a_cum[-1][s:s + 1] * h_in[-1])
        hlast_ref[b] = h_in[SUBLANES]
        h_seg = jnp.concatenate(h_in[:SUBLANES], axis=0)
        h = jnp.concatenate([h_loc[j] + a_cum[j] * h_seg for j in range(seg)], axis=0)
        ys.append((h * _gelu_tanh(gate[b])).astype(BF16))

    for b in seqs:
        y_t = jnp.dot(unperm, ys[b], preferred_element_type=F32).astype(BF16)
        o_ref[b] = xs[b] + jnp.dot(y_t, wout_ref[...], preferred_element_type=F32)


def _rglru(x, g, w_in, conv_w, conv_b, w_a, b_a, w_x, b_x, lam, w_out):
    bsz, seq, d = x.shape
    drnn = w_out.shape[0]
    bw = drnn // RG_BLOCKS
    tm = min(256, seq)
    nseq = 2 if bsz % 2 == 0 else 1
    row = pl.BlockSpec((nseq, tm, d), lambda b, i: (b, i, 0))
    lam_c = (-LRU_C) * jax.nn.softplus(-lam.astype(F32))
    return pl.pallas_call(
        functools.partial(_rglru_kernel, tm=tm, drnn=drnn, nseq=nseq),
        grid=(bsz // nseq, seq // tm),
        in_specs=[row, _resident((1, d)), _resident((d, 2 * drnn)), _resident((RG_CONV, drnn)),
                  _resident((1, drnn)), _resident((RG_BLOCKS, bw, bw)), _resident((1, drnn)),
                  _resident((RG_BLOCKS, bw, bw)), _resident((1, drnn)), _resident((1, drnn)),
                  _resident((drnn, d))],
        out_specs=row,
        out_shape=jax.ShapeDtypeStruct((bsz, seq, d), F32),
        scratch_shapes=[pltpu.VMEM((nseq, tm + (RG_CONV - 1) * SUBLANES, drnn), F32),
                        pltpu.VMEM((nseq, RG_CONV - 1, SUBLANES, drnn), F32),
                        pltpu.VMEM((nseq, 1, drnn), F32)],
        compiler_params=_params(("arbitrary", "arbitrary")),
        name="rglru",
    )(x, g.reshape(1, d), w_in.astype(BF16), conv_w, conv_b.reshape(1, drnn),
      w_a.astype(BF16), b_a.reshape(1, drnn), w_x.astype(BF16), b_x.reshape(1, drnn),
      lam_c.reshape(1, drnn), w_out.astype(BF16))


def kernel(x, mix_norm_g, ffn_norm_g, final_norm_g, ev_w_in, ev_w_gk2, ev_b_gk2, ev_gla_norm_g,
           ev_w_out, od_w_in, od_conv_w, od_conv_b, od_w_a, od_b_a, od_w_x, od_b_x, od_lambda,
           od_w_out, ffn_w_up, ffn_conv_w, ffn_conv_b, ffn_w_down):
    depth = mix_norm_g.shape[0]
    h = x
    for l in range(depth):
        j = l // 2
        mixer = None
        if l % 2 == 0:
            q, k, v, bias, bq, bk, bv, gl, og = _front0(h, mix_norm_g[l], ev_w_in[j], ev_w_gk2[j],
                                                      ev_b_gk2[j])
            o_a = _moba(q, k, v, bias)
            o_b = _gla(bq, bk, bv, gl, og, ev_gla_norm_g[j])
            mixer = (o_a, o_b, ev_w_out[j])
        else:
            h = _rglru(h, mix_norm_g[l], od_w_in[j], od_conv_w[j], od_conv_b[j], od_w_a[j],
                       od_b_a[j], od_w_x[j], od_b_x[j], od_lambda[j], od_w_out[j])
        h = _conv_ffn(h, ffn_norm_g[l], ffn_w_up[l], ffn_conv_w[l], ffn_conv_b[l], ffn_w_down[l],
                      final_norm_g, final_norm=(l == depth - 1), mixer=mixer)
    return h
```

```python
import functools

import jax
import jax.numpy as jnp
from jax import lax
from jax.experimental import pallas as pl
from jax.experimental.pallas import tpu as pltpu

F32 = jnp.float32
BF16 = jnp.bfloat16

A_HEADS = 8
A_HEAD_DIM = 64
A_WIDTH = A_HEADS * A_HEAD_DIM
MOBA_BLOCK = 256
MOBA_TOPK = 3
ROPE_THETA = 10000.0
B_HEADS = 8
B_HEAD_K = 32
B_HEAD_V = 64
B_KEY_WIDTH = B_HEADS * B_HEAD_K
B_VAL_WIDTH = B_HEADS * B_HEAD_V
GLA_GATE_RANK = 16
GLA_GATE_NORM = 16.0
RG_BLOCKS = 4
RG_CONV = 4
LRU_C = 8.0
FFN_CONV = 3
NORM_EPS = 1e-6
MASK_VALUE = -1e30

LANES = 128
SUBLANES = 8
BF16_ROWS = 16
LOG2_E = 1.4426950408889634
VMEM_LIMIT = 56 * 1024 * 1024

GLA_CHUNK = 64
MOBA_BLOCKS_PAD = LANES
V_ROWS = A_HEAD_DIM + BF16_ROWS


def _params(sem):
    return pltpu.CompilerParams(dimension_semantics=sem, vmem_limit_bytes=VMEM_LIMIT)


def _rms(x, g):
    ms = jnp.mean(x * x, axis=-1, keepdims=True)
    return x * lax.rsqrt(ms + NORM_EPS) * g


def _gelu_tanh(x):
    return 0.5 * x * (1.0 + jnp.tanh(0.7978845608028654 * (x + 0.044715 * (x * x * x))))


def _log_sigmoid(z):
    return jnp.minimum(z, 0.0) - jnp.log1p(jnp.exp(-jnp.abs(z)))


def _moba_gate_rows(nb):
    assert nb <= MOBA_BLOCKS_PAD
    return -(-nb // BF16_ROWS) * BF16_ROWS


def _resident(shape):
    nd = len(shape)
    return pl.BlockSpec(shape, lambda *_: (0,) * nd, pipeline_mode=pl.Buffered(1))


def _front0_kernel(x_ref, g_ref, w_ref, wt_ref, wgk_ref, bgk_ref, cos_ref, sa_ref, sb_ref,
                   cost_ref, sat_ref, sbt_ref,
                   q_ref, k_ref, v_ref, bias_ref, bq_ref, bk_ref, bv_ref, gl_ref, og_ref,
                   km_ref, *, tm):
    @pl.when(pl.program_id(1) == 0)
    def _():
        km_ref[...] = jnp.zeros_like(km_ref)

    hn = _rms(x_ref[0], g_ref[...]).astype(BF16)

    def proj(c0, n):
        return jnp.dot(hn, w_ref[:, c0:c0 + n], preferred_element_type=F32)

    def proj_t(r0, n):
        return lax.dot_general(wt_ref[r0:r0 + n, :], hn, (((1,), (1,)), ((), ())),
                               preferred_element_type=F32)

    def rot(t, cos, sa, sb, axis):
        parts = []
        for c in range(t.shape[axis] // LANES):
            tc = lax.slice_in_dim(t, c * LANES, (c + 1) * LANES, axis=axis)
            parts.append(tc * cos + pltpu.roll(tc, LANES - 32, axis) * sa
                         + pltpu.roll(tc, 32, axis) * sb)
        return jnp.concatenate(parts, axis=axis)

    qt = rot(proj_t(0, A_WIDTH), cost_ref[...], sat_ref[...], sbt_ref[...], 0)
    q_ref[0] = (qt * (A_HEAD_DIM ** -0.5 * LOG2_E)).astype(BF16)
    vt = proj_t(A_WIDTH, A_WIDTH)
    c = 0
    k = rot(proj(c, A_WIDTH), cos_ref[...], sa_ref[...], sb_ref[...], 1); c += A_WIDTH
    k_ref[0] = k.astype(BF16)
    nblk = tm // MOBA_BLOCK
    kmean = jnp.mean(k.reshape(nblk, MOBA_BLOCK, A_WIDTH), axis=1)
    blk0 = pl.program_id(1) * nblk
    for n in range(nblk):
        km_ref[pl.ds(blk0 + n, 1), :] = kmean[n:n + 1]

    nbr = km_ref.shape[0]
    feat = lax.broadcasted_iota(jnp.int32, (LANES, 1), 0)
    brow = lax.broadcasted_iota(jnp.int32, (nbr, 2 * tm), 0)
    browf = brow.astype(F32)
    q_blk = blk0 + (lax.broadcasted_iota(jnp.int32, (1, 2 * tm), 1) % tm) // MOBA_BLOCK

    def split(a):
        hi = a.astype(BF16)
        return hi, (a - hi.astype(F32)).astype(BF16)

    def gate_pair(hp):
        q_pair = qt[hp * LANES:(hp + 1) * LANES]
        q2 = jnp.concatenate([jnp.where(feat // A_HEAD_DIM == h, q_pair, 0.0) for h in range(2)],
                             axis=1)
        km_hi, km_lo = split(km_ref[:, hp * LANES:(hp + 1) * LANES])
        q_hi, q_lo = split(q2)
        gate = jnp.dot(jnp.concatenate([km_hi, km_hi, km_lo], axis=1),
                       jnp.concatenate([q_hi, q_lo, q_hi], axis=0), preferred_element_type=F32)
        g = jnp.where(brow < q_blk, gate, -jnp.inf)
        bias = jnp.full((nbr, 2 * tm), MASK_VALUE, F32)
        for _ in range(MOBA_TOPK):
            mx = jnp.max(g, axis=0, keepdims=True)
            first = jnp.min(jnp.where(g == mx, browf, float(nbr)), axis=0, keepdims=True)
            hit = browf == first
            bias = jnp.where(hit & (mx > -jnp.inf), 0.0, bias)
            g = jnp.where(hit, -jnp.inf, g)
        for h in range(2):
            bias_ref[0, 2 * hp + h] = bias[:, h * tm:(h + 1) * tm].astype(BF16)

    ones_rows = (lax.broadcasted_iota(jnp.int32, (V_ROWS - A_HEAD_DIM, MOBA_BLOCK), 0)
                 == 0).astype(BF16)
    for h in range(A_HEADS):
        for n in range(nblk):
            v_ref[0, h, n, 0:A_HEAD_DIM, :] = vt[h * A_HEAD_DIM:(h + 1) * A_HEAD_DIM,
                                                 n * MOBA_BLOCK:(n + 1) * MOBA_BLOCK].astype(BF16)
            v_ref[0, h, n, A_HEAD_DIM:, :] = ones_rows
    gate_pair(0)
    bq_ref[0] = proj(c, B_KEY_WIDTH) * (B_HEAD_K ** -0.5); c += B_KEY_WIDTH
    bk_ref[0] = proj(c, B_KEY_WIDTH); c += B_KEY_WIDTH
    gate_pair(1)
    bv_ref[0] = proj(c, B_VAL_WIDTH); c += B_VAL_WIDTH
    gate_pair(2)
    og_ref[0] = proj(c, B_VAL_WIDTH); c += B_VAL_WIDTH
    gate_pair(3)
    gk = proj(c, LANES).astype(BF16)
    z = jnp.dot(gk, wgk_ref[...], preferred_element_type=F32) + bgk_ref[...]
    gl_ref[0] = _log_sigmoid(z) / GLA_GATE_NORM


def _front0(x, g, w_in, w_gk2, b_gk2):
    bsz, seq, d = x.shape
    tm = min(1024, seq)
    widths = [A_WIDTH, A_WIDTH, A_WIDTH, B_KEY_WIDTH, B_KEY_WIDTH, B_VAL_WIDTH, GLA_GATE_RANK]
    aq, ak, av, bq, bk, bv, bgk, bog = jnp.split(
        w_in, [sum(widths[:n + 1]) for n in range(len(widths))], axis=1)
    gk_pad = LANES - GLA_GATE_RANK
    w = jnp.concatenate([ak, bq, bk, bv, bog, jnp.pad(bgk, ((0, 0), (0, gk_pad)))],
                        axis=1).astype(BF16)
    wt = jnp.concatenate([aq, av], axis=1).T.astype(BF16)
    wgk = jnp.pad(w_gk2, ((0, gk_pad), (0, 0))).astype(BF16)
    ncol = w.shape[1]

    half = A_HEAD_DIM // 2
    inv = ROPE_THETA ** (-jnp.arange(half, dtype=F32) / half)
    pos = jnp.arange(seq, dtype=jnp.int32).astype(F32)
    first = (jnp.arange(LANES) % A_HEAD_DIM) < half

    def tables(feature_major):
        if feature_major:
            ang, reps, sel = inv[:, None] * pos[None, :], (LANES // half, 1), first[:, None]
        else:
            ang, reps, sel = pos[:, None] * inv[None, :], (1, LANES // half), first[None, :]
        cos, sin = jnp.tile(jnp.cos(ang), reps), jnp.tile(jnp.sin(ang), reps)
        return cos, jnp.where(sel, -sin, 0.0), jnp.where(sel, 0.0, sin)

    nb = seq // MOBA_BLOCK
    nbr = _moba_gate_rows(nb)
    row = lambda n: pl.BlockSpec((1, tm, n), lambda b, i: (b, i, 0))
    tab = pl.BlockSpec((tm, LANES), lambda b, i: (i, 0))
    tab_t = pl.BlockSpec((LANES, tm), lambda b, i: (0, i))
    nt = seq // tm
    outs = pl.pallas_call(
        functools.partial(_front0_kernel, tm=tm),
        grid=(bsz, nt),
        in_specs=[row(d), _resident((1, d)), _resident((d, ncol)), _resident((2 * A_WIDTH, d)),
                  _resident((LANES, B_KEY_WIDTH)), _resident((1, B_KEY_WIDTH)),
                  tab, tab, tab, tab_t, tab_t, tab_t],
        out_specs=[pl.BlockSpec((1, A_WIDTH, tm), lambda b, i: (b, 0, i)), row(A_WIDTH),
                   pl.BlockSpec((1, A_HEADS, tm // MOBA_BLOCK, V_ROWS, MOBA_BLOCK),
                                lambda b, i: (b, 0, i, 0, 0)),
                   pl.BlockSpec((1, A_HEADS, nbr, tm), lambda b, i: (b, 0, 0, i)),
                   row(B_KEY_WIDTH), row(B_KEY_WIDTH), row(B_VAL_WIDTH), row(B_KEY_WIDTH),
                   row(B_VAL_WIDTH)],
        out_shape=[jax.ShapeDtypeStruct((bsz, A_WIDTH, seq), BF16),
                   jax.ShapeDtypeStruct((bsz, seq, A_WIDTH), BF16),
                   jax.ShapeDtypeStruct((bsz, A_HEADS, nb, V_ROWS, MOBA_BLOCK), BF16),
                   jax.ShapeDtypeStruct((bsz, A_HEADS, nbr, seq), BF16),
                   jax.ShapeDtypeStruct((bsz, seq, B_KEY_WIDTH), F32),
                   jax.ShapeDtypeStruct((bsz, seq, B_KEY_WIDTH), F32),
                   jax.ShapeDtypeStruct((bsz, seq, B_VAL_WIDTH), F32),
                   jax.ShapeDtypeStruct((bsz, seq, B_KEY_WIDTH), F32),
                   jax.ShapeDtypeStruct((bsz, seq, B_VAL_WIDTH), F32)],
        scratch_shapes=[pltpu.VMEM((nbr, A_WIDTH), F32)],
        compiler_params=_params(("arbitrary", "arbitrary")),
        name="front0",
    )(x, g.reshape(1, d), w, wt, wgk, b_gk2.reshape(1, B_KEY_WIDTH), *tables(False), *tables(True))
    return outs


def _moba_kernel(qt_ref, k_ref, vt_ref, bias_ref, o_ref, rhs_ref, acc_ref, sa_ref, sb_ref, *,
                 group, nb):
    qb = pl.program_id(2)
    blk = MOBA_BLOCK
    qt = qt_ref[0]
    feat = lax.broadcasted_iota(jnp.int32, (LANES, 1), 0)
    nbr = bias_ref.shape[2]
    for h in range(2):
        rhs_ref[:, h * blk:(h + 1) * blk] = jnp.concatenate(
            [jnp.where(feat // A_HEAD_DIM == h, qt, jnp.zeros_like(qt)), bias_ref[0, h],
             jnp.full((MOBA_BLOCKS_PAD - nbr, blk), MASK_VALUE, BF16)], axis=0)

    keys = group * blk
    blk_of_row = lax.broadcasted_iota(jnp.int32, (keys, MOBA_BLOCKS_PAD), 0) // blk
    lane_blk = lax.broadcasted_iota(jnp.int32, (keys, MOBA_BLOCKS_PAD), 1)

    def scores(g, s_ref):
        j0 = g * group
        off = pl.multiple_of(jnp.minimum(j0, nb - group) * blk, blk)
        onehot = (lane_blk == blk_of_row + j0).astype(BF16)
        lhs_j = jnp.concatenate([k_ref[0, pl.ds(off, keys), :], onehot], axis=1)
        s = jnp.dot(lhs_j, rhs_ref[...], preferred_element_type=F32)
        s_ref[...] = s
        return jnp.max(s, axis=0, keepdims=True)

    def accumulate(g, s_ref, m, s_max):
        j0 = g * group
        m_new = jnp.maximum(m, s_max)
        alpha = jnp.exp2(m - m_new)
        p = jnp.exp2(s_ref[...] - m_new).astype(BF16)
        for h in range(2):
            vt = jnp.concatenate([vt_ref[0, h, jnp.minimum(j0 + u, nb - 1)] for u in range(group)],
                                 axis=1)
            acc_ref[h] = (alpha[:, h * blk:(h + 1) * blk] * acc_ref[h]
                          + jnp.dot(vt, p[:, h * blk:(h + 1) * blk], preferred_element_type=F32))
        return m_new

    def body(i, carry):
        m, max_a = carry
        max_b = scores(2 * i + 1, sb_ref)
        m = accumulate(2 * i, sa_ref, m, max_a)
        max_a = scores(2 * i + 2, sa_ref)
        m = accumulate(2 * i + 1, sb_ref, m, max_b)
        return m, max_a

    max_0 = scores(0, sa_ref)

    own = pl.multiple_of(qb * blk, blk)
    lhs = jnp.concatenate([k_ref[0, pl.ds(own, blk), :], jnp.zeros((blk, MOBA_BLOCKS_PAD), BF16)],
                          axis=1)
    causal = (lax.broadcasted_iota(jnp.int32, (blk, 2 * blk), 0)
              <= lax.broadcasted_iota(jnp.int32, (blk, 2 * blk), 1) % blk)
    s = jnp.where(causal, jnp.dot(lhs, rhs_ref[...], preferred_element_type=F32), MASK_VALUE)
    m0 = jnp.max(s, axis=0, keepdims=True)
    p = jnp.exp2(s - m0).astype(BF16)
    for h in range(2):
        acc_ref[h] = jnp.dot(vt_ref[0, h, qb], p[:, h * blk:(h + 1) * blk],
                             preferred_element_type=F32)

    n_groups = (qb + group - 1) // group
    m, max_a = lax.fori_loop(0, n_groups // 2, body, (m0, max_0))

    @pl.when(n_groups % 2 == 1)
    def _():
        accumulate(n_groups - 1, sa_ref, m, max_a)

    outs = []
    for h in range(2):
        a = acc_ref[h]
        outs.append(a[0:A_HEAD_DIM] / a[A_HEAD_DIM:A_HEAD_DIM + 1])
    o_ref[0] = jnp.concatenate(outs, axis=0).T


def _moba(qt, k, vt, bias):
    bsz, seq, _ = k.shape
    nb = seq // MOBA_BLOCK
    nbr = _moba_gate_rows(nb)
    npair = A_WIDTH // LANES
    group = 4 if nb % 4 == 0 else 1
    return pl.pallas_call(
        functools.partial(_moba_kernel, group=group, nb=nb),
        grid=(bsz, npair, nb),
        in_specs=[pl.BlockSpec((1, LANES, MOBA_BLOCK), lambda b, hp, i: (b, hp, i)),
                  pl.BlockSpec((1, seq, LANES), lambda b, hp, i: (b, 0, hp)),
                  pl.BlockSpec((1, 2, nb, V_ROWS, MOBA_BLOCK), lambda b, hp, i: (b, hp, 0, 0, 0)),
                  pl.BlockSpec((1, 2, nbr, MOBA_BLOCK), lambda b, hp, i: (b, hp, 0, i))],
        out_specs=pl.BlockSpec((1, MOBA_BLOCK, LANES), lambda b, hp, i: (b, i, hp)),
        out_shape=jax.ShapeDtypeStruct((bsz, seq, A_WIDTH), F32),
        scratch_shapes=[pltpu.VMEM((2 * LANES, 2 * MOBA_BLOCK), BF16),
                        pltpu.VMEM((2, V_ROWS, MOBA_BLOCK), F32),
                        pltpu.VMEM((group * MOBA_BLOCK, 2 * MOBA_BLOCK), F32),
                        pltpu.VMEM((group * MOBA_BLOCK, 2 * MOBA_BLOCK), F32)],
        compiler_params=_params(("arbitrary", "arbitrary", "arbitrary")),
        name="moba",
    )(qt, k, vt, bias)


def _gla_kernel(q_ref, k_ref, v_ref, g_ref, og_ref, ng_ref, o_ref, st_ref, *, tm):
    @pl.when(pl.program_id(1) == 0)
    def _():
        st_ref[...] = jnp.zeros_like(st_ref)

    q, k, v, g = q_ref[0], k_ref[0], v_ref[0], g_ref[0]
    kw, vw = B_KEY_WIDTH, B_VAL_WIDTH
    row = lax.broadcasted_iota(jnp.int32, (tm, 1), 0)

    pre, post, tot, upper = {1: g}, {1: jnp.zeros_like(g)}, {1: g}, {}
    n = 1
    while n < GLA_CHUNK:
        upper[n] = (row % (2 * n)) >= n
        below = pltpu.roll(tot[n], n, 0)
        above = pltpu.roll(tot[n], tm - n, 0)
        pre[2 * n] = pre[n] + jnp.where(upper[n], below, 0.0)
        post[2 * n] = post[n] + jnp.where(upper[n], 0.0, above)
        tot[2 * n] = tot[n] + jnp.where(upper[n], below, above)
        n *= 2
    levels = sorted(upper)

    q_lvl = {n: jnp.where(upper[n], q * jnp.exp(pre[n]), 0.0).astype(BF16) for n in levels}
    k_lvl = {n: jnp.where(upper[n], 0.0, k * jnp.exp(post[n])).astype(BF16) for n in levels}
    q_bf, k_bf = q.astype(BF16), k.astype(BF16)
    q_chk = (q * jnp.exp(pre[GLA_CHUNK])).astype(BF16)
    k_chk = (k * jnp.exp(post[GLA_CHUNK])).astype(BF16)

    head_vk = (lax.broadcasted_iota(jnp.int32, (vw, kw), 0) // B_HEAD_V
               == lax.broadcasted_iota(jnp.int32, (vw, kw), 1) // B_HEAD_K)
    head_vv = (lax.broadcasted_iota(jnp.int32, (vw, vw), 0) // B_HEAD_V
               == lax.broadcasted_iota(jnp.int32, (vw, vw), 1) // B_HEAD_V).astype(BF16)

    tt = lax.broadcasted_iota(jnp.int32, (B_HEADS * GLA_CHUNK, GLA_CHUNK), 0) % GLA_CHUNK
    ss = lax.broadcasted_iota(jnp.int32, (B_HEADS * GLA_CHUNK, GLA_CHUNK), 1)
    same_row = tt == ss
    splits_at = {n: (tt // (2 * n) == ss // (2 * n)) & (tt // n != ss // n) for n in levels}
    lane_k = lax.broadcasted_iota(jnp.int32, (1, kw), 1)
    lane_v = lax.broadcasted_iota(jnp.int32, (1, vw), 1)

    def stack_heads(x):
        return jnp.concatenate([jnp.where(lane_k // B_HEAD_K == h, x, jnp.zeros_like(x))
                                for h in range(B_HEADS)], axis=0)

    def scores(lhs, rhs):
        return lax.dot_general(stack_heads(lhs), rhs, (((1,), (1,)), ((), ())),
                               preferred_element_type=F32)

    outs = []
    for c in range(tm // GLA_CHUNK):
        lo = c * GLA_CHUNK
        rows = slice(lo, lo + GLA_CHUNK)
        vc = v[rows].astype(BF16)

        attn = jnp.where(same_row, scores(q_bf[rows], k_bf[rows]), 0.0)
        for n in levels:
            attn = jnp.where(splits_at[n], scores(q_lvl[n][rows], k_lvl[n][rows]), attn)
        res = jnp.dot(attn.astype(BF16), vc, preferred_element_type=F32)
        o_in = jnp.zeros((GLA_CHUNK, vw), F32)
        for h in range(B_HEADS):
            o_in = o_in + jnp.where(lane_v // B_HEAD_V == h,
                                    res[h * GLA_CHUNK:(h + 1) * GLA_CHUNK], 0.0)

        st = st_ref[...]
        o_st = lax.dot_general(q_chk[rows], st.astype(BF16), (((1,), (1,)), ((), ())),
                               preferred_element_type=F32)
        upd = lax.dot_general(vc, k_chk[rows], (((0,), (0,)), ((), ())),
                              preferred_element_type=F32)
        decay = jnp.exp(tot[GLA_CHUNK][lo:lo + 1, :])
        st_ref[...] = st * decay + jnp.where(head_vk, upd, 0.0)
        outs.append(o_in + o_st)

    o = jnp.concatenate(outs, axis=0)

    sq = o * o
    sq_hi = sq.astype(BF16)
    sq_lo = (sq - sq_hi.astype(F32)).astype(BF16)
    ms = (jnp.dot(sq_hi, head_vv, preferred_element_type=F32)
          + jnp.dot(sq_lo, head_vv, preferred_element_type=F32)) * (1.0 / B_HEAD_V)
    og = og_ref[0]
    o_ref[0] = (o * lax.rsqrt(ms + NORM_EPS) * ng_ref[...]) * (og * jax.nn.sigmoid(og))


def _gla(bq, bk, bv, gl, og, norm_g):
    bsz, seq, _ = bq.shape
    tm = min(512, seq)
    row = lambda n: pl.BlockSpec((1, tm, n), lambda b, i: (b, i, 0))
    return pl.pallas_call(
        functools.partial(_gla_kernel, tm=tm),
        grid=(bsz, seq // tm),
        in_specs=[row(B_KEY_WIDTH), row(B_KEY_WIDTH), row(B_VAL_WIDTH), row(B_KEY_WIDTH),
                  row(B_VAL_WIDTH), _resident((1, B_VAL_WIDTH))],
        out_specs=row(B_VAL_WIDTH),
        out_shape=jax.ShapeDtypeStruct((bsz, seq, B_VAL_WIDTH), F32),
        scratch_shapes=[pltpu.VMEM((B_VAL_WIDTH, B_KEY_WIDTH), F32)],
        compiler_params=_params(("arbitrary", "arbitrary")),
        name="gla",
    )(bq, bk, bv, gl, og, norm_g.reshape(1, B_VAL_WIDTH))


def _ffn_kernel(*refs, tm, bn, dff, final_norm, mixer):
    if mixer:
        x_ref, oa_ref, ob_ref, wa_ref, wb_ref = refs[:5]
        refs = refs[5:]
    else:
        x_ref, refs = refs[0], refs[1:]
    g_ref, wup_ref, cw_ref, cb_ref, wdn_ref, fg_ref, o_ref, halo_ref, ext_ref = refs
    nkb = dff // bn

    @pl.when(pl.program_id(1) == 0)
    def _():
        halo_ref[...] = jnp.zeros_like(halo_ref)

    x = x_ref[0]
    if mixer:
        o_ref[0] = (jnp.dot(oa_ref[0].astype(BF16), wa_ref[...], preferred_element_type=F32)
                    + jnp.dot(ob_ref[0].astype(BF16), wb_ref[...], preferred_element_type=F32))
        x = x + o_ref[0]
    hn = _rms(x, g_ref[...]).astype(BF16)

    def up_stage(kb):
        for part in range(2):
            c0 = part * dff + kb * bn
            up = jnp.dot(hn, wup_ref[:, c0:c0 + bn], preferred_element_type=F32)
            ext_ref[kb % 2, part, 0:SUBLANES, :] = halo_ref[:, c0:c0 + bn]
            ext_ref[kb % 2, part, SUBLANES:, :] = up
            halo_ref[:, c0:c0 + bn] = up[tm - SUBLANES:, :]

    def gate_stage(kb):
        halves = []
        for part in range(2):
            c0 = part * dff + kb * bn
            y = cb_ref[:, c0:c0 + bn]
            for i in range(FFN_CONV):
                off = SUBLANES - (FFN_CONV - 1) + i
                y = y + ext_ref[kb % 2, part, pl.ds(off, tm), :] * cw_ref[i:i + 1, c0:c0 + bn]
            halves.append(y)
        return (halves[0] * _gelu_tanh(halves[1])).astype(BF16)

    up_stage(0)
    if nkb > 1:
        up_stage(1)
    a = gate_stage(0)
    acc = x
    for kb in range(nkb):
        if kb + 2 < nkb:
            up_stage(kb + 2)
        a_next = gate_stage(kb + 1) if kb + 1 < nkb else None
        acc = acc + jnp.dot(a, wdn_ref[kb * bn:(kb + 1) * bn, :], preferred_element_type=F32)
        a = a_next
    if final_norm:
        acc = _rms(acc, fg_ref[...])
    o_ref[0] = acc


def _conv_ffn(x, g, w_up, conv_w, conv_b, w_down, final_g, final_norm, mixer=None):
    bsz, seq, d = x.shape
    dff = w_down.shape[0]
    tm = min(512, seq)
    bn = 1536
    rows = lambda n: pl.BlockSpec((1, tm, n), lambda b, i: (b, i, 0))
    row = rows(d)
    lead_specs, lead_args = [row], [x]
    if mixer is not None:
        o_a, o_b, w_out = mixer
        w = w_out.astype(BF16)
        lead_specs += [rows(A_WIDTH), rows(B_VAL_WIDTH),
                       _resident((A_WIDTH, d)), _resident((B_VAL_WIDTH, d))]
        lead_args += [o_a, o_b, w[:A_WIDTH], w[A_WIDTH:]]
    return pl.pallas_call(
        functools.partial(_ffn_kernel, tm=tm, bn=bn, dff=dff, final_norm=final_norm,
                          mixer=mixer is not None),
        grid=(bsz, seq // tm),
        in_specs=lead_specs + [_resident((1, d)), _resident((d, 2 * dff)),
                               _resident((FFN_CONV, 2 * dff)), _resident((1, 2 * dff)),
                               _resident((dff, d)), _resident((1, d))],
        out_specs=row,
        out_shape=jax.ShapeDtypeStruct((bsz, seq, d), F32),
        scratch_shapes=[pltpu.VMEM((SUBLANES, 2 * dff), F32),
                        pltpu.VMEM((2, 2, tm + SUBLANES, bn), F32)],
        compiler_params=_params(("arbitrary", "arbitrary")),
        name="conv_ffn_final" if final_norm else "conv_ffn",
    )(*lead_args, g.reshape(1, d), w_up.astype(BF16), conv_w, conv_b.reshape(1, 2 * dff),
      w_down.astype(BF16), final_g.reshape(1, d))


def _rglru_kernel(x_ref, g_ref, win_ref, cw_ref, cb_ref, wa_ref, ba_ref, wx_ref, bx_ref,
                  lam_ref, wout_ref, o_ref, ext_ref, halo_ref, hlast_ref, *, tm, drnn, nseq):
    seg = tm // SUBLANES
    shifts = RG_CONV - 1
    bw = drnn // RG_BLOCKS
    seqs = range(nseq)

    @pl.when(pl.program_id(1) == 0)
    def _():
        halo_ref[...] = jnp.zeros_like(halo_ref)
        hlast_ref[...] = jnp.zeros_like(hlast_ref)

    prow = lax.broadcasted_iota(jnp.int32, (tm, tm), 0)
    pcol = lax.broadcasted_iota(jnp.int32, (tm, tm), 1)
    perm = (pcol == (prow % SUBLANES) * seg + prow // SUBLANES).astype(BF16)
    unperm = (prow == (pcol % SUBLANES) * seg + pcol // SUBLANES).astype(BF16)
    first_sub = lax.broadcasted_iota(jnp.int32, (SUBLANES, 1), 0) == 0

    xs = [x_ref[b] for b in seqs]
    hn = [jnp.dot(perm, _rms(xs[b], g_ref[...]).astype(BF16),
                  preferred_element_type=F32).astype(BF16) for b in seqs]
    gate = [jnp.dot(hn[b], win_ref[:, 0:drnn], preferred_element_type=F32) for b in seqs]
    xr_raw = [jnp.dot(hn[b], win_ref[:, drnn:2 * drnn], preferred_element_type=F32) for b in seqs]

    xr = []
    for b in seqs:
        for r in range(shifts):
            lo = (seg - shifts + r) * SUBLANES
            cur = xr_raw[b][lo:lo + SUBLANES]
            ext_ref[b, r * SUBLANES:(r + 1) * SUBLANES, :] = jnp.where(
                first_sub, pltpu.roll(halo_ref[b, r], 1, 0), pltpu.roll(cur, 1, 0))
            halo_ref[b, r] = cur
        ext_ref[b, shifts * SUBLANES:, :] = xr_raw[b]
        acc = cb_ref[...]
        for i in range(RG_CONV):
            acc = acc + ext_ref[b, i * SUBLANES:i * SUBLANES + tm, :] * cw_ref[i:i + 1, :]
        xr.append(acc)

    ys = []
    for b in seqs:
        xb = xr[b].astype(BF16)
        ra, ri = [], []
        for n in range(RG_BLOCKS):
            xn = xb[:, n * bw:(n + 1) * bw]
            ra.append(jnp.dot(xn, wa_ref[n], preferred_element_type=F32))
            ri.append(jnp.dot(xn, wx_ref[n], preferred_element_type=F32))
        r = jax.nn.sigmoid(jnp.concatenate(ra, axis=1) + ba_ref[...])
        ig = jax.nn.sigmoid(jnp.concatenate(ri, axis=1) + bx_ref[...])
        log_a = r * lam_ref[...]
        a = jnp.exp(log_a)
        th = jnp.tanh(log_a)
        one_minus_a2 = -2.0 * th / (1.0 - th)
        u = jnp.sqrt(one_minus_a2) * (ig * xr[b])

        h_loc, a_cum = [u[0:SUBLANES]], [a[0:SUBLANES]]
        for j in range(1, seg):
            aj = a[j * SUBLANES:(j + 1) * SUBLANES]
            h_loc.append(aj * h_loc[-1] + u[j * SUBLANES:(j + 1) * SUBLANES])
            a_cum.append(aj * a_cum[-1])
        h_in = [hlast_ref[b]]
        for s in range(SUBLANES):
            h_in.append(h_loc[-1][s:s + 1] + a_cum[-1][s:s + 1] * h_in[-1])
        hlast_ref[b] = h_in[SUBLANES]
        h_seg = jnp.concatenate(h_in[:SUBLANES], axis=0)
        h = jnp.concatenate([h_loc[j] + a_cum[j] * h_seg for j in range(seg)], axis=0)
        ys.append((h * _gelu_tanh(gate[b])).astype(BF16))

    for b in seqs:
        y_t = jnp.dot(unperm, ys[b], preferred_element_type=F32).astype(BF16)
        o_ref[b] = xs[b] + jnp.dot(y_t, wout_ref[...], preferred_element_type=F32)


def _rglru(x, g, w_in, conv_w, conv_b, w_a, b_a, w_x, b_x, lam, w_out):
    bsz, seq, d = x.shape
    drnn = w_out.shape[0]
    bw = drnn // RG_BLOCKS
    tm = min(256, seq)
    nseq = 2 if bsz % 2 == 0 else 1
    row = pl.BlockSpec((nseq, tm, d), lambda b, i: (b, i, 0))
    lam_c = (-LRU_C) * jax.nn.softplus(-lam.astype(F32))
    return pl.pallas_call(
        functools.partial(_rglru_kernel, tm=tm, drnn=drnn, nseq=nseq),
        grid=(bsz // nseq, seq // tm),
        in_specs=[row, _resident((1, d)), _resident((d, 2 * drnn)), _resident((RG_CONV, drnn)),
                  _resident((1, drnn)), _resident((RG_BLOCKS, bw, bw)), _resident((1, drnn)),
                  _resident((RG_BLOCKS, bw, bw)), _resident((1, drnn)), _resident((1, drnn)),
                  _resident((drnn, d))],
        out_specs=row,
        out_shape=jax.ShapeDtypeStruct((bsz, seq, d), F32),
        scratch_shapes=[pltpu.VMEM((nseq, tm + (RG_CONV - 1) * SUBLANES, drnn), F32),
                        pltpu.VMEM((nseq, RG_CONV - 1, SUBLANES, drnn), F32),
                        pltpu.VMEM((nseq, 1, drnn), F32)],
        compiler_params=_params(("arbitrary", "arbitrary")),
        name="rglru",
    )(x, g.reshape(1, d), w_in.astype(BF16), conv_w, conv_b.reshape(1, drnn),
      w_a.astype(BF16), b_a.reshape(1, drnn), w_x.astype(BF16), b_x.reshape(1, drnn),
      lam_c.reshape(1, drnn), w_out.astype(BF16))


def kernel(x, mix_norm_g, ffn_norm_g, final_norm_g, ev_w_in, ev_w_gk2, ev_b_gk2, ev_gla_norm_g,
           ev_w_out, od_w_in, od_conv_w, od_conv_b, od_w_a, od_b_a, od_w_x, od_b_x, od_lambda,
           od_w_out, ffn_w_up, ffn_conv_w, ffn_conv_b, ffn_w_down):
    depth = mix_norm_g.shape[0]
    h = x
    for l in range(depth):
        j = l // 2
        mixer = None
        if l % 2 == 0:
            q, k, v, bias, bq, bk, bv, gl, og = _front0(h, mix_norm_g[l], ev_w_in[j], ev_w_gk2[j],
                                                      ev_b_gk2[j])
            o_a = _moba(q, k, v, bias)
            o_b = _gla(bq, bk, bv, gl, og, ev_gla_norm_g[j])
            mixer = (o_a, o_b, ev_w_out[j])
        else:
            h = _rglru(h, mix_norm_g[l], od_w_in[j], od_conv_w[j], od_conv_b[j], od_w_a[j],
                       od_b_a[j], od_w_x[j], od_b_x[j], od_lambda[j], od_w_out[j])
        h = _conv_ffn(h, ffn_norm_g[l], ffn_w_up[l], ffn_conv_w[l], ffn_conv_b[l], ffn_w_down[l],
                      final_norm_g, final_norm=(l == depth - 1), mixer=mixer)
    return h
```

```python
import functools

import jax
import jax.numpy as jnp
from jax import lax
from jax.experimental import pallas as pl
from jax.experimental.pallas import tpu as pltpu

F32 = jnp.float32
BF16 = jnp.bfloat16

A_HEADS = 8
A_HEAD_DIM = 64
A_WIDTH = A_HEADS * A_HEAD_DIM
MOBA_BLOCK = 256
MOBA_TOPK = 3
ROPE_THETA = 10000.0
B_HEADS = 8
B_HEAD_K = 32
B_HEAD_V = 64
B_KEY_WIDTH = B_HEADS * B_HEAD_K
B_VAL_WIDTH = B_HEADS * B_HEAD_V
GLA_GATE_RANK = 16
GLA_GATE_NORM = 16.0
RG_BLOCKS = 4
RG_CONV = 4
LRU_C = 8.0
FFN_CONV = 3
NORM_EPS = 1e-6
MASK_VALUE = -1e30

LANES = 128
SUBLANES = 8
BF16_ROWS = 16
LOG2_E = 1.4426950408889634
VMEM_LIMIT = 56 * 1024 * 1024

GLA_CHUNK = 64
MOBA_BLOCKS_PAD = LANES
V_ROWS = A_HEAD_DIM + BF16_ROWS


def _params(sem):
    return pltpu.CompilerParams(dimension_semantics=sem, vmem_limit_bytes=VMEM_LIMIT)


def _rms(x, g):
    ms = jnp.mean(x * x, axis=-1, keepdims=True)
    return x * lax.rsqrt(ms + NORM_EPS) * g


def _gelu_tanh(x):
    return 0.5 * x * (1.0 + jnp.tanh(0.7978845608028654 * (x + 0.044715 * (x * x * x))))


def _log_sigmoid(z):
    return jnp.minimum(z, 0.0) - jnp.log1p(jnp.exp(-jnp.abs(z)))


def _moba_gate_rows(nb):
    assert nb <= MOBA_BLOCKS_PAD
    return -(-nb // BF16_ROWS) * BF16_ROWS


def _resident(shape):
    nd = len(shape)
    return pl.BlockSpec(shape, lambda *_: (0,) * nd, pipeline_mode=pl.Buffered(1))


def _front0_kernel(x_ref, g_ref, w_ref, wt_ref, wgk_ref, bgk_ref, cos_ref, sa_ref, sb_ref,
                   cost_ref, sat_ref, sbt_ref,
                   q_ref, k_ref, v_ref, bias_ref, bq_ref, bk_ref, bv_ref, gl_ref, og_ref,
                   km_ref, *, tm):
    @pl.when(pl.program_id(1) == 0)
    def _():
        km_ref[...] = jnp.zeros_like(km_ref)

    hn = _rms(x_ref[0], g_ref[...]).astype(BF16)

    def proj(c0, n):
        return jnp.dot(hn, w_ref[:, c0:c0 + n], preferred_element_type=F32)

    def proj_t(r0, n):
        return lax.dot_general(wt_ref[r0:r0 + n, :], hn, (((1,), (1,)), ((), ())),
                               preferred_element_type=F32)

    def rot(t, cos, sa, sb, axis):
        parts = []
        for c in range(t.shape[axis] // LANES):
            tc = lax.slice_in_dim(t, c * LANES, (c + 1) * LANES, axis=axis)
            parts.append(tc * cos + pltpu.roll(tc, LANES - 32, axis) * sa
                         + pltpu.roll(tc, 32, axis) * sb)
        return jnp.concatenate(parts, axis=axis)

    qt = rot(proj_t(0, A_WIDTH), cost_ref[...], sat_ref[...], sbt_ref[...], 0)
    q_ref[0] = (qt * (A_HEAD_DIM ** -0.5 * LOG2_E)).astype(BF16)
    vt = proj_t(A_WIDTH, A_WIDTH)
    c = 0
    k = rot(proj(c, A_WIDTH), cos_ref[...], sa_ref[...], sb_ref[...], 1); c += A_WIDTH
    k_ref[0] = k.astype(BF16)
    nblk = tm // MOBA_BLOCK
    kmean = jnp.mean(k.reshape(nblk, MOBA_BLOCK, A_WIDTH), axis=1)
    blk0 = pl.program_id(1) * nblk
    for n in range(nblk):
        km_ref[pl.ds(blk0 + n, 1), :] = kmean[n:n + 1]

    nbr = km_ref.shape[0]
    feat = lax.broadcasted_iota(jnp.int32, (LANES, 1), 0)
    brow = lax.broadcasted_iota(jnp.int32, (nbr, 2 * tm), 0)
    browf = brow.astype(F32)
    q_blk = blk0 + (lax.broadcasted_iota(jnp.int32, (1, 2 * tm), 1) % tm) // MOBA_BLOCK

    def split(a):
        hi = a.astype(BF16)
        return hi, (a - hi.astype(F32)).astype(BF16)

    def gate_pair(hp):
        q_pair = qt[hp * LANES:(hp + 1) * LANES]
        q2 = jnp.concatenate([jnp.where(feat // A_HEAD_DIM == h, q_pair, 0.0) for h in range(2)],
                             axis=1)
        km_hi, km_lo = split(km_ref[:, hp * LANES:(hp + 1) * LANES])
        q_hi, q_lo = split(q2)
        gate = jnp.dot(jnp.concatenate([km_hi, km_hi, km_lo], axis=1),
                       jnp.concatenate([q_hi, q_lo, q_hi], axis=0), preferred_element_type=F32)
        g = jnp.where(brow < q_blk, gate, -jnp.inf)
        bias = jnp.full((nbr, 2 * tm), MASK_VALUE, F32)
        for _ in range(MOBA_TOPK):
            mx = jnp.max(g, axis=0, keepdims=True)
            first = jnp.min(jnp.where(g == mx, browf, float(nbr)), axis=0, keepdims=True)
            hit = browf == first
            bias = jnp.where(hit & (mx > -jnp.inf), 0.0, bias)
            g = jnp.where(hit, -jnp.inf, g)
        for h in range(2):
            bias_ref[0, 2 * hp + h] = bias[:, h * tm:(h + 1) * tm].astype(BF16)

    ones_rows = (lax.broadcasted_iota(jnp.int32, (V_ROWS - A_HEAD_DIM, MOBA_BLOCK), 0)
                 == 0).astype(BF16)
    for h in range(A_HEADS):
        for n in range(nblk):
            v_ref[0, h, n, 0:A_HEAD_DIM, :] = vt[h * A_HEAD_DIM:(h + 1) * A_HEAD_DIM,
                                                 n * MOBA_BLOCK:(n + 1) * MOBA_BLOCK].astype(BF16)
            v_ref[0, h, n, A_HEAD_DIM:, :] = ones_rows
    gate_pair(0)
    bq_ref[0] = proj(c, B_KEY_WIDTH) * (B_HEAD_K ** -0.5); c += B_KEY_WIDTH
    bk_ref[0] = proj(c, B_KEY_WIDTH); c += B_KEY_WIDTH
    gate_pair(1)
    bv_ref[0] = proj(c, B_VAL_WIDTH); c += B_VAL_WIDTH
    gate_pair(2)
    og_ref[0] = proj(c, B_VAL_WIDTH); c += B_VAL_WIDTH
    gate_pair(3)
    gk = proj(c, LANES).astype(BF16)
    z = jnp.dot(gk, wgk_ref[...], preferred_element_type=F32) + bgk_ref[...]
    gl_ref[0] = _log_sigmoid(z) / GLA_GATE_NORM


def _front0(x, g, w_in, w_gk2, b_gk2):
    bsz, seq, d = x.shape
    tm = min(1024, seq)
    widths = [A_WIDTH, A_WIDTH, A_WIDTH, B_KEY_WIDTH, B_KEY_WIDTH, B_VAL_WIDTH, GLA_GATE_RANK]
    aq, ak, av, bq, bk, bv, bgk, bog = jnp.split(
        w_in, [sum(widths[:n + 1]) for n in range(len(widths))], axis=1)
    gk_pad = LANES - GLA_GATE_RANK
    w = jnp.concatenate([ak, bq, bk, bv, bog, jnp.pad(bgk, ((0, 0), (0, gk_pad)))],
                        axis=1).astype(BF16)
    wt = jnp.concatenate([aq, av], axis=1).T.astype(BF16)
    wgk = jnp.pad(w_gk2, ((0, gk_pad), (0, 0))).astype(BF16)
    ncol = w.shape[1]

    half = A_HEAD_DIM // 2
    inv = ROPE_THETA ** (-jnp.arange(half, dtype=F32) / half)
    pos = jnp.arange(seq, dtype=jnp.int32).astype(F32)
    first = (jnp.arange(LANES) % A_HEAD_DIM) < half

    def tables(feature_major):
        if feature_major:
            ang, reps, sel = inv[:, None] * pos[None, :], (LANES // half, 1), first[:, None]
        else:
            ang, reps, sel = pos[:, None] * inv[None, :], (1, LANES // half), first[None, :]
        cos, sin = jnp.tile(jnp.cos(ang), reps), jnp.tile(jnp.sin(ang), reps)
        return cos, jnp.where(sel, -sin, 0.0), jnp.where(sel, 0.0, sin)

    nb = seq // MOBA_BLOCK
    nbr = _moba_gate_rows(nb)
    row = lambda n: pl.BlockSpec((1, tm, n), lambda b, i: (b, i, 0))
    tab = pl.BlockSpec((tm, LANES), lambda b, i: (i, 0))
    tab_t = pl.BlockSpec((LANES, tm), lambda b, i: (0, i))
    nt = seq // tm
    outs = pl.pallas_call(
        functools.partial(_front0_kernel, tm=tm),
        grid=(bsz, nt),
        in_specs=[row(d), _resident((1, d)), _resident((d, ncol)), _resident((2 * A_WIDTH, d)),
                  _resident((LANES, B_KEY_WIDTH)), _resident((1, B_KEY_WIDTH)),
                  tab, tab, tab, tab_t, tab_t, tab_t],
        out_specs=[pl.BlockSpec((1, A_WIDTH, tm), lambda b, i: (b, 0, i)), row(A_WIDTH),
                   pl.BlockSpec((1, A_HEADS, tm // MOBA_BLOCK, V_ROWS, MOBA_BLOCK),
                                lambda b, i: (b, 0, i, 0, 0)),
                   pl.BlockSpec((1, A_HEADS, nbr, tm), lambda b, i: (b, 0, 0, i)),
                   row(B_KEY_WIDTH), row(B_KEY_WIDTH), row(B_VAL_WIDTH), row(B_KEY_WIDTH),
                   row(B_VAL_WIDTH)],
        out_shape=[jax.ShapeDtypeStruct((bsz, A_WIDTH, seq), BF16),
                   jax.ShapeDtypeStruct((bsz, seq, A_WIDTH), BF16),
                   jax.ShapeDtypeStruct((bsz, A_HEADS, nb, V_ROWS, MOBA_BLOCK), BF16),
                   jax.ShapeDtypeStruct((bsz, A_HEADS, nbr, seq), BF16),
                   jax.ShapeDtypeStruct((bsz, seq, B_KEY_WIDTH), F32),
                   jax.ShapeDtypeStruct((bsz, seq, B_KEY_WIDTH), F32),
                   jax.ShapeDtypeStruct((bsz, seq, B_VAL_WIDTH), F32),
                   jax.ShapeDtypeStruct((bsz, seq, B_KEY_WIDTH), F32),
                   jax.ShapeDtypeStruct((bsz, seq, B_VAL_WIDTH), F32)],
        scratch_shapes=[pltpu.VMEM((nbr, A_WIDTH), F32)],
        compiler_params=_params(("arbitrary", "arbitrary")),
        name="front0",
    )(x, g.reshape(1, d), w, wt, wgk, b_gk2.reshape(1, B_KEY_WIDTH), *tables(False), *tables(True))
    return outs


def _moba_kernel(qt_ref, k_ref, vt_ref, bias_ref, o_ref, rhs_ref, acc_ref, sa_ref, sb_ref, *,
                 group, nb):
    qb = pl.program_id(2)
    blk = MOBA_BLOCK
    qt = qt_ref[0]
    feat = lax.broadcasted_iota(jnp.int32, (LANES, 1), 0)
    nbr = bias_ref.shape[2]
    for h in range(2):
        rhs_ref[:, h * blk:(h + 1) * blk] = jnp.concatenate(
            [jnp.where(feat // A_HEAD_DIM == h, qt, jnp.zeros_like(qt)), bias_ref[0, h],
             jnp.full((MOBA_BLOCKS_PAD - nbr, blk), MASK_VALUE, BF16)], axis=0)

    keys = group * blk
    blk_of_row = lax.broadcasted_iota(jnp.int32, (keys, MOBA_BLOCKS_PAD), 0) // blk
    lane_blk = lax.broadcasted_iota(jnp.int32, (keys, MOBA_BLOCKS_PAD), 1)

    def scores(g, s_ref):
        j0 = g * group
        off = pl.multiple_of(jnp.minimum(j0, nb - group) * blk, blk)
        onehot = (lane_blk == blk_of_row + j0).astype(BF16)
        lhs_j = jnp.concatenate([k_ref[0, pl.ds(off, keys), :], onehot], axis=1)
        s = jnp.dot(lhs_j, rhs_ref[...], preferred_element_type=F32)
        s_ref[...] = s
        return jnp.max(s, axis=0, keepdims=True)

    def accumulate(g, s_ref, m, s_max):
        j0 = g * group
        m_new = jnp.maximum(m, s_max)
        alpha = jnp.exp2(m - m_new)
        p = jnp.exp2(s_ref[...] - m_new).astype(BF16)
        for h in range(2):
            vt = jnp.concatenate([vt_ref[0, h, jnp.minimum(j0 + u, nb - 1)] for u in range(group)],
                                 axis=1)
            acc_ref[h] = (alpha[:, h * blk:(h + 1) * blk] * acc_ref[h]
                          + jnp.dot(vt, p[:, h * blk:(h + 1) * blk], preferred_element_type=F32))
        return m_new

    def body(i, carry):
        m, max_a = carry
        max_b = scores(2 * i + 1, sb_ref)
        m = accumulate(2 * i, sa_ref, m, max_a)
        max_a = scores(2 * i + 2, sa_ref)
        m = accumulate(2 * i + 1, sb_ref, m, max_b)
        return m, max_a

    max_0 = scores(0, sa_ref)

    own = pl.multiple_of(qb * blk, blk)
    lhs = jnp.concatenate([k_ref[0, pl.ds(own, blk), :], jnp.zeros((blk, MOBA_BLOCKS_PAD), BF16)],
                          axis=1)
    causal = (lax.broadcasted_iota(jnp.int32, (blk, 2 * blk), 0)
              <= lax.broadcasted_iota(jnp.int32, (blk, 2 * blk), 1) % blk)
    s = jnp.where(causal, jnp.dot(lhs, rhs_ref[...], preferred_element_type=F32), MASK_VALUE)
    m0 = jnp.max(s, axis=0, keepdims=True)
    p = jnp.exp2(s - m0).astype(BF16)
    for h in range(2):
        acc_ref[h] = jnp.dot(vt_ref[0, h, qb], p[:, h * blk:(h + 1) * blk],
                             preferred_element_type=F32)

    n_groups = (qb + group - 1) // group
    n_pairs = jnp.maximum(n_groups - 1, 0) // 2
    m, max_a = lax.fori_loop(0, n_pairs, body, (m0, max_0))
    left = n_groups - 2 * n_pairs

    @pl.when(left == 1)
    def _():
        accumulate(n_groups - 1, sa_ref, m, max_a)

    @pl.when(left == 2)
    def _():
        max_b = scores(n_groups - 1, sb_ref)
        m_mid = accumulate(n_groups - 2, sa_ref, m, max_a)
        accumulate(n_groups - 1, sb_ref, m_mid, max_b)

    outs = []
    for h in range(2):
        a = acc_ref[h]
        outs.append(a[0:A_HEAD_DIM] / a[A_HEAD_DIM:A_HEAD_DIM + 1])
    o_ref[0] = jnp.concatenate(outs, axis=0).T


def _moba(qt, k, vt, bias):
    bsz, seq, _ = k.shape
    nb = seq // MOBA_BLOCK
    nbr = _moba_gate_rows(nb)
    npair = A_WIDTH // LANES
    group = 4 if nb % 4 == 0 else 1
    return pl.pallas_call(
        functools.partial(_moba_kernel, group=group, nb=nb),
        grid=(bsz, npair, nb),
        in_specs=[pl.BlockSpec((1, LANES, MOBA_BLOCK), lambda b, hp, i: (b, hp, i)),
                  pl.BlockSpec((1, seq, LANES), lambda b, hp, i: (b, 0, hp)),
                  pl.BlockSpec((1, 2, nb, V_ROWS, MOBA_BLOCK), lambda b, hp, i: (b, hp, 0, 0, 0)),
                  pl.BlockSpec((1, 2, nbr, MOBA_BLOCK), lambda b, hp, i: (b, hp, 0, i))],
        out_specs=pl.BlockSpec((1, MOBA_BLOCK, LANES), lambda b, hp, i: (b, i, hp)),
        out_shape=jax.ShapeDtypeStruct((bsz, seq, A_WIDTH), F32),
        scratch_shapes=[pltpu.VMEM((2 * LANES, 2 * MOBA_BLOCK), BF16),
                        pltpu.VMEM((2, V_ROWS, MOBA_BLOCK), F32),
                        pltpu.VMEM((group * MOBA_BLOCK, 2 * MOBA_BLOCK), F32),
                        pltpu.VMEM((group * MOBA_BLOCK, 2 * MOBA_BLOCK), F32)],
        compiler_params=_params(("arbitrary", "arbitrary", "arbitrary")),
        name="moba",
    )(qt, k, vt, bias)


def _gla_kernel(q_ref, k_ref, v_ref, g_ref, og_ref, ng_ref, o_ref, st_ref, *, tm):
    @pl.when(pl.program_id(1) == 0)
    def _():
        st_ref[...] = jnp.zeros_like(st_ref)

    q, k, v, g = q_ref[0], k_ref[0], v_ref[0], g_ref[0]
    kw, vw = B_KEY_WIDTH, B_VAL_WIDTH
    row = lax.broadcasted_iota(jnp.int32, (tm, 1), 0)

    pre, post, tot, upper = {1: g}, {1: jnp.zeros_like(g)}, {1: g}, {}
    n = 1
    while n < GLA_CHUNK:
        upper[n] = (row % (2 * n)) >= n
        below = pltpu.roll(tot[n], n, 0)
        above = pltpu.roll(tot[n], tm - n, 0)
        pre[2 * n] = pre[n] + jnp.where(upper[n], below, 0.0)
        post[2 * n] = post[n] + jnp.where(upper[n], 0.0, above)
        tot[2 * n] = tot[n] + jnp.where(upper[n], below, above)
        n *= 2
    levels = sorted(upper)

    q_lvl = {n: jnp.where(upper[n], q * jnp.exp(pre[n]), 0.0).astype(BF16) for n in levels}
    k_lvl = {n: jnp.where(upper[n], 0.0, k * jnp.exp(post[n])).astype(BF16) for n in levels}
    q_bf, k_bf = q.astype(BF16), k.astype(BF16)
    q_chk = (q * jnp.exp(pre[GLA_CHUNK])).astype(BF16)
    k_chk = (k * jnp.exp(post[GLA_CHUNK])).astype(BF16)

    head_vk = (lax.broadcasted_iota(jnp.int32, (vw, kw), 0) // B_HEAD_V
               == lax.broadcasted_iota(jnp.int32, (vw, kw), 1) // B_HEAD_K)
    head_vv = (lax.broadcasted_iota(jnp.int32, (vw, vw), 0) // B_HEAD_V
               == lax.broadcasted_iota(jnp.int32, (vw, vw), 1) // B_HEAD_V).astype(BF16)

    tt = lax.broadcasted_iota(jnp.int32, (B_HEADS * GLA_CHUNK, GLA_CHUNK), 0) % GLA_CHUNK
    ss = lax.broadcasted_iota(jnp.int32, (B_HEADS * GLA_CHUNK, GLA_CHUNK), 1)
    same_row = tt == ss
    splits_at = {n: (tt // (2 * n) == ss // (2 * n)) & (tt // n != ss // n) for n in levels}
    lane_k = lax.broadcasted_iota(jnp.int32, (1, kw), 1)
    lane_v = lax.broadcasted_iota(jnp.int32, (1, vw), 1)

    def stack_heads(x):
        return jnp.concatenate([jnp.where(lane_k // B_HEAD_K == h, x, jnp.zeros_like(x))
                                for h in range(B_HEADS)], axis=0)

    def scores(lhs, rhs):
        return lax.dot_general(stack_heads(lhs), rhs, (((1,), (1,)), ((), ())),
                               preferred_element_type=F32)

    outs = []
    for c in range(tm // GLA_CHUNK):
        lo = c * GLA_CHUNK
        rows = slice(lo, lo + GLA_CHUNK)
        vc = v[rows].astype(BF16)

        attn = jnp.where(same_row, scores(q_bf[rows], k_bf[rows]), 0.0)
        for n in levels:
            attn = jnp.where(splits_at[n], scores(q_lvl[n][rows], k_lvl[n][rows]), attn)
        res = jnp.dot(attn.astype(BF16), vc, preferred_element_type=F32)
        o_in = jnp.zeros((GLA_CHUNK, vw), F32)
        for h in range(B_HEADS):
            o_in = o_in + jnp.where(lane_v // B_HEAD_V == h,
                                    res[h * GLA_CHUNK:(h + 1) * GLA_CHUNK], 0.0)

        st = st_ref[...]
        o_st = lax.dot_general(q_chk[rows], st.astype(BF16), (((1,), (1,)), ((), ())),
                               preferred_element_type=F32)
        upd = lax.dot_general(vc, k_chk[rows], (((0,), (0,)), ((), ())),
                              preferred_element_type=F32)
        decay = jnp.exp(tot[GLA_CHUNK][lo:lo + 1, :])
        st_ref[...] = st * decay + jnp.where(head_vk, upd, 0.0)
        outs.append(o_in + o_st)

    o = jnp.concatenate(outs, axis=0)

    sq = o * o
    sq_hi = sq.astype(BF16)
    sq_lo = (sq - sq_hi.astype(F32)).astype(BF16)
    ms = (jnp.dot(sq_hi, head_vv, preferred_element_type=F32)
          + jnp.dot(sq_lo, head_vv, preferred_element_type=F32)) * (1.0 / B_HEAD_V)
    og = og_ref[0]
    o_ref[0] = (o * lax.rsqrt(ms + NORM_EPS) * ng_ref[...]) * (og * jax.nn.sigmoid(og))


def _gla(bq, bk, bv, gl, og, norm_g):
    bsz, seq, _ = bq.shape
    tm = min(512, seq)
    row = lambda n: pl.BlockSpec((1, tm, n), lambda b, i: (b, i, 0))
    return pl.pallas_call(
        functools.partial(_gla_kernel, tm=tm),
        grid=(bsz, seq // tm),
        in_specs=[row(B_KEY_WIDTH), row(B_KEY_WIDTH), row(B_VAL_WIDTH), row(B_KEY_WIDTH),
                  row(B_VAL_WIDTH), _resident((1, B_VAL_WIDTH))],
        out_specs=row(B_VAL_WIDTH),
        out_shape=jax.ShapeDtypeStruct((bsz, seq, B_VAL_WIDTH), F32),
        scratch_shapes=[pltpu.VMEM((B_VAL_WIDTH, B_KEY_WIDTH), F32)],
        compiler_params=_params(("arbitrary", "arbitrary")),
        name="gla",
    )(bq, bk, bv, gl, og, norm_g.reshape(1, B_VAL_WIDTH))


def _ffn_kernel(*refs, tm, bn, dff, final_norm, mixer):
    if mixer:
        x_ref, oa_ref, ob_ref, wa_ref, wb_ref = refs[:5]
        refs = refs[5:]
    else:
        x_ref, refs = refs[0], refs[1:]
    g_ref, wup_ref, cw_ref, cb_ref, wdn_ref, fg_ref, o_ref, halo_ref, ext_ref = refs
    nkb = dff // bn

    @pl.when(pl.program_id(1) == 0)
    def _():
        halo_ref[...] = jnp.zeros_like(halo_ref)

    x = x_ref[0]
    if mixer:
        o_ref[0] = (jnp.dot(oa_ref[0].astype(BF16), wa_ref[...], preferred_element_type=F32)
                    + jnp.dot(ob_ref[0].astype(BF16), wb_ref[...], preferred_element_type=F32))
        x = x + o_ref[0]
    hn = _rms(x, g_ref[...]).astype(BF16)

    def up_stage(kb):
        for part in range(2):
            c0 = part * dff + kb * bn
            up = jnp.dot(hn, wup_ref[:, c0:c0 + bn], preferred_element_type=F32)
            ext_ref[kb % 2, part, 0:SUBLANES, :] = halo_ref[:, c0:c0 + bn]
            ext_ref[kb % 2, part, SUBLANES:, :] = up
            halo_ref[:, c0:c0 + bn] = up[tm - SUBLANES:, :]

    def gate_stage(kb):
        halves = []
        for part in range(2):
            c0 = part * dff + kb * bn
            y = cb_ref[:, c0:c0 + bn]
            for i in range(FFN_CONV):
                off = SUBLANES - (FFN_CONV - 1) + i
                y = y + ext_ref[kb % 2, part, pl.ds(off, tm), :] * cw_ref[i:i + 1, c0:c0 + bn]
            halves.append(y)
        return (halves[0] * _gelu_tanh(halves[1])).astype(BF16)

    up_stage(0)
    if nkb > 1:
        up_stage(1)
    a = gate_stage(0)
    acc = x
    for kb in range(nkb):
        if kb + 2 < nkb:
            up_stage(kb + 2)
        a_next = gate_stage(kb + 1) if kb + 1 < nkb else None
        acc = acc + jnp.dot(a, wdn_ref[kb * bn:(kb + 1) * bn, :], preferred_element_type=F32)
        a = a_next
    if final_norm:
        acc = _rms(acc, fg_ref[...])
    o_ref[0] = acc


def _conv_ffn(x, g, w_up, conv_w, conv_b, w_down, final_g, final_norm, mixer=None):
    bsz, seq, d = x.shape
    dff = w_down.shape[0]
    tm = min(512, seq)
    bn = 1536
    rows = lambda n: pl.BlockSpec((1, tm, n), lambda b, i: (b, i, 0))
    row = rows(d)
    lead_specs, lead_args = [row], [x]
    if mixer is not None:
        o_a, o_b, w_out = mixer
        w = w_out.astype(BF16)
        lead_specs += [rows(A_WIDTH), rows(B_VAL_WIDTH),
                       _resident((A_WIDTH, d)), _resident((B_VAL_WIDTH, d))]
        lead_args += [o_a, o_b, w[:A_WIDTH], w[A_WIDTH:]]
    return pl.pallas_call(
        functools.partial(_ffn_kernel, tm=tm, bn=bn, dff=dff, final_norm=final_norm,
                          mixer=mixer is not None),
        grid=(bsz, seq // tm),
        in_specs=lead_specs + [_resident((1, d)), _resident((d, 2 * dff)),
                               _resident((FFN_CONV, 2 * dff)), _resident((1, 2 * dff)),
                               _resident((dff, d)), _resident((1, d))],
        out_specs=row,
        out_shape=jax.ShapeDtypeStruct((bsz, seq, d), F32),
        scratch_shapes=[pltpu.VMEM((SUBLANES, 2 * dff), F32),
                        pltpu.VMEM((2, 2, tm + SUBLANES, bn), F32)],
        compiler_params=_params(("arbitrary", "arbitrary")),
        name="conv_ffn_final" if final_norm else "conv_ffn",
    )(*lead_args, g.reshape(1, d), w_up.astype(BF16), conv_w, conv_b.reshape(1, 2 * dff),
      w_down.astype(BF16), final_g.reshape(1, d))


def _rglru_kernel(x_ref, g_ref, win_ref, cw_ref, cb_ref, wa_ref, ba_ref, wx_ref, bx_ref,
                  lam_ref, wout_ref, o_ref, ext_ref, halo_ref, hlast_ref, *, tm, drnn, nseq):
    seg = tm // SUBLANES
    shifts = RG_CONV - 1
    bw = drnn // RG_BLOCKS
    seqs = range(nseq)

    @pl.when(pl.program_id(1) == 0)
    def _():
        halo_ref[...] = jnp.zeros_like(halo_ref)
        hlast_ref[...] = jnp.zeros_like(hlast_ref)

    prow = lax.broadcasted_iota(jnp.int32, (tm, tm), 0)
    pcol = lax.broadcasted_iota(jnp.int32, (tm, tm), 1)
    perm = (pcol == (prow % SUBLANES) * seg + prow // SUBLANES).astype(BF16)
    unperm = (prow == (pcol % SUBLANES) * seg + pcol // SUBLANES).astype(BF16)
    first_sub = lax.broadcasted_iota(jnp.int32, (SUBLANES, 1), 0) == 0

    xs = [x_ref[b] for b in seqs]
    hn = [jnp.dot(perm, _rms(xs[b], g_ref[...]).astype(BF16),
                  preferred_element_type=F32).astype(BF16) for b in seqs]
    gate = [jnp.dot(hn[b], win_ref[:, 0:drnn], preferred_element_type=F32) for b in seqs]
    xr_raw = [jnp.dot(hn[b], win_ref[:, drnn:2 * drnn], preferred_element_type=F32) for b in seqs]

    xr = []
    for b in seqs:
        for r in range(shifts):
            lo = (seg - shifts + r) * SUBLANES
            cur = xr_raw[b][lo:lo + SUBLANES]
            ext_ref[b, r * SUBLANES:(r + 1) * SUBLANES, :] = jnp.where(
                first_sub, pltpu.roll(halo_ref[b, r], 1, 0), pltpu.roll(cur, 1, 0))
            halo_ref[b, r] = cur
        ext_ref[b, shifts * SUBLANES:, :] = xr_raw[b]
        acc = cb_ref[...]
        for i in range(RG_CONV):
            acc = acc + ext_ref[b, i * SUBLANES:i * SUBLANES + tm, :] * cw_ref[i:i + 1, :]
        xr.append(acc)

    ys = []
    for b in seqs:
        xb = xr[b].astype(BF16)
        ra, ri = [], []
        for n in range(RG_BLOCKS):
            xn = xb[:, n * bw:(n + 1) * bw]
            ra.append(jnp.dot(xn, wa_ref[n], preferred_element_type=F32))
            ri.append(jnp.dot(xn, wx_ref[n], preferred_element_type=F32))
        r = jax.nn.sigmoid(jnp.concatenate(ra, axis=1) + ba_ref[...])
        ig = jax.nn.sigmoid(jnp.concatenate(ri, axis=1) + bx_ref[...])
        log_a = r * lam_ref[...]
        a = jnp.exp(log_a)
        th = jnp.tanh(log_a)
        one_minus_a2 = -2.0 * th / (1.0 - th)
        u = jnp.sqrt(one_minus_a2) * (ig * xr[b])

        h_loc, a_cum = [u[0:SUBLANES]], [a[0:SUBLANES]]
        for j in range(1, seg):
            aj = a[j * SUBLANES:(j + 1) * SUBLANES]
            h_loc.append(aj * h_loc[-1] + u[j * SUBLANES:(j + 1) * SUBLANES])
            a_cum.append(aj * a_cum[-1])
        h_in = [hlast_ref[b]]
        for s in range(SUBLANES):
            h_in.append(h_loc[-1][s:s + 1] + a_cum[-1][s:s + 1] * h_in[-1])
        hlast_ref[b] = h_in[SUBLANES]
        h_seg = jnp.concatenate(h_in[:SUBLANES], axis=0)
        h = jnp.concatenate([h_loc[j] + a_cum[j] * h_seg for j in range(seg)], axis=0)
        ys.append((h * _gelu_tanh(gate[b])).astype(BF16))

    for b in seqs:
        y_t = jnp.dot(unperm, ys[b], preferred_element_type=F32).astype(BF16)
        o_ref[b] = xs[b] + jnp.dot(y_t, wout_ref[...], preferred_element_type=F32)


def _rglru(x, g, w_in, conv_w, conv_b, w_a, b_a, w_x, b_x, lam, w_out):
    bsz, seq, d = x.shape
    drnn = w_out.shape[0]
    bw = drnn // RG_BLOCKS
    tm = min(256, seq)
    nseq = 2 if bsz % 2 == 0 else 1
    row = pl.BlockSpec((nseq, tm, d), lambda b, i: (b, i, 0))
    lam_c = (-LRU_C) * jax.nn.softplus(-lam.astype(F32))
    return pl.pallas_call(
        functools.partial(_rglru_kernel, tm=tm, drnn=drnn, nseq=nseq),
        grid=(bsz // nseq, seq // tm),
        in_specs=[row, _resident((1, d)), _resident((d, 2 * drnn)), _resident((RG_CONV, drnn)),
                  _resident((1, drnn)), _resident((RG_BLOCKS, bw, bw)), _resident((1, drnn)),
                  _resident((RG_BLOCKS, bw, bw)), _resident((1, drnn)), _resident((1, drnn)),
                  _resident((drnn, d))],
        out_specs=row,
        out_shape=jax.ShapeDtypeStruct((bsz, seq, d), F32),
        scratch_shapes=[pltpu.VMEM((nseq, tm + (RG_CONV - 1) * SUBLANES, drnn), F32),
                        pltpu.VMEM((nseq, RG_CONV - 1, SUBLANES, drnn), F32),
                        pltpu.VMEM((nseq, 1, drnn), F32)],
        compiler_params=_params(("arbitrary", "arbitrary")),
        name="rglru",
    )(x, g.reshape(1, d), w_in.astype(BF16), conv_w, conv_b.reshape(1, drnn),
      w_a.astype(BF16), b_a.reshape(1, drnn), w_x.astype(BF16), b_x.reshape(1, drnn),
      lam_c.reshape(1, drnn), w_out.astype(BF16))


def kernel(x, mix_norm_g, ffn_norm_g, final_norm_g, ev_w_in, ev_w_gk2, ev_b_gk2, ev_gla_norm_g,
           ev_w_out, od_w_in, od_conv_w, od_conv_b, od_w_a, od_b_a, od_w_x, od_b_x, od_lambda,
           od_w_out, ffn_w_up, ffn_conv_w, ffn_conv_b, ffn_w_down):
    depth = mix_norm_g.shape[0]
    h = x
    for l in range(depth):
        j = l // 2
        mixer = None
        if l % 2 == 0:
            q, k, v, bias, bq, bk, bv, gl, og = _front0(h, mix_norm_g[l], ev_w_in[j], ev_w_gk2[j],
                                                      ev_b_gk2[j])
            o_a = _moba(q, k, v, bias)
            o_b = _gla(bq, bk, bv, gl, og, ev_gla_norm_g[j])
            mixer = (o_a, o_b, ev_w_out[j])
        else:
            h = _rglru(h, mix_norm_g[l], od_w_in[j], od_conv_w[j], od_conv_b[j], od_w_a[j],
                       od_b_a[j], od_w_x[j], od_b_x[j], od_lambda[j], od_w_out[j])
        h = _conv_ffn(h, ffn_norm_g[l], ffn_w_up[l], ffn_conv_w[l], ffn_conv_b[l], ffn_w_down[l],
                      final_norm_g, final_norm=(l == depth - 1), mixer=mixer)
    return h
```

```python
import functools

import jax
import jax.numpy as jnp
from jax import lax
from jax.experimental import pallas as pl
from jax.experimental.pallas import tpu as pltpu

F32 = jnp.float32
BF16 = jnp.bfloat16

A_HEADS = 8
A_HEAD_DIM = 64
A_WIDTH = A_HEADS * A_HEAD_DIM
MOBA_BLOCK = 256
MOBA_TOPK = 3
ROPE_THETA = 10000.0
B_HEADS = 8
B_HEAD_K = 32
B_HEAD_V = 64
B_KEY_WIDTH = B_HEADS * B_HEAD_K
B_VAL_WIDTH = B_HEADS * B_HEAD_V
GLA_GATE_RANK = 16
GLA_GATE_NORM = 16.0
RG_BLOCKS = 4
RG_CONV = 4
LRU_C = 8.0
FFN_CONV = 3
NORM_EPS = 1e-6
MASK_VALUE = -1e30

LANES = 128
SUBLANES = 8
BF16_ROWS = 16
LOG2_E = 1.4426950408889634
VMEM_LIMIT = 56 * 1024 * 1024

GLA_CHUNK = 64
MOBA_BLOCKS_PAD = LANES
V_ROWS = A_HEAD_DIM + BF16_ROWS


def _params(sem):
    return pltpu.CompilerParams(dimension_semantics=sem, vmem_limit_bytes=VMEM_LIMIT)


def _rms(x, g):
    ms = jnp.mean(x * x, axis=-1, keepdims=True)
    return x * lax.rsqrt(ms + NORM_EPS) * g


def _gelu_tanh(x):
    return 0.5 * x * (1.0 + jnp.tanh(0.7978845608028654 * (x + 0.044715 * (x * x * x))))


def _log_sigmoid(z):
    return jnp.minimum(z, 0.0) - jnp.log1p(jnp.exp(-jnp.abs(z)))


def _moba_gate_rows(nb):
    assert nb <= MOBA_BLOCKS_PAD
    return -(-nb // BF16_ROWS) * BF16_ROWS


def _resident(shape):
    nd = len(shape)
    return pl.BlockSpec(shape, lambda *_: (0,) * nd, pipeline_mode=pl.Buffered(1))


def _front0_kernel(x_ref, g_ref, w_ref, wt_ref, wgk_ref, bgk_ref, cos_ref, sa_ref, sb_ref,
                   cost_ref, sat_ref, sbt_ref,
                   q_ref, k_ref, v_ref, bias_ref, bq_ref, bk_ref, bv_ref, gl_ref, og_ref,
                   km_ref, *, tm):
    @pl.when(pl.program_id(1) == 0)
    def _():
        km_ref[...] = jnp.zeros_like(km_ref)

    hn = _rms(x_ref[0], g_ref[...]).astype(BF16)

    def proj(c0, n):
        return jnp.dot(hn, w_ref[:, c0:c0 + n], preferred_element_type=F32)

    def proj_t(r0, n):
        return lax.dot_general(wt_ref[r0:r0 + n, :], hn, (((1,), (1,)), ((), ())),
                               preferred_element_type=F32)

    def rot(t, cos, sa, sb, axis):
        parts = []
        for c in range(t.shape[axis] // LANES):
            tc = lax.slice_in_dim(t, c * LANES, (c + 1) * LANES, axis=axis)
            parts.append(tc * cos + pltpu.roll(tc, LANES - 32, axis) * sa
                         + pltpu.roll(tc, 32, axis) * sb)
        return jnp.concatenate(parts, axis=axis)

    qt = rot(proj_t(0, A_WIDTH), cost_ref[...], sat_ref[...], sbt_ref[...], 0)
    q_ref[0] = (qt * (A_HEAD_DIM ** -0.5 * LOG2_E)).astype(BF16)
    vt = proj_t(A_WIDTH, A_WIDTH)
    c = 0
    k = rot(proj(c, A_WIDTH), cos_ref[...], sa_ref[...], sb_ref[...], 1); c += A_WIDTH
    k_ref[0] = k.astype(BF16)
    nblk = tm // MOBA_BLOCK
    kmean = jnp.mean(k.reshape(nblk, MOBA_BLOCK, A_WIDTH), axis=1)
    blk0 = pl.program_id(1) * nblk
    for n in range(nblk):
        km_ref[pl.ds(blk0 + n, 1), :] = kmean[n:n + 1]

    nbr = km_ref.shape[0]
    feat = lax.broadcasted_iota(jnp.int32, (LANES, 1), 0)
    brow = lax.broadcasted_iota(jnp.int32, (nbr, 2 * tm), 0)
    browf = brow.astype(F32)
    q_blk = blk0 + (lax.broadcasted_iota(jnp.int32, (1, 2 * tm), 1) % tm) // MOBA_BLOCK

    def split(a):
        hi = a.astype(BF16)
        return hi, (a - hi.astype(F32)).astype(BF16)

    def gate_pair(hp):
        q_pair = qt[hp * LANES:(hp + 1) * LANES]
        q2 = jnp.concatenate([jnp.where(feat // A_HEAD_DIM == h, q_pair, 0.0) for h in range(2)],
                             axis=1)
        km_hi, km_lo = split(km_ref[:, hp * LANES:(hp + 1) * LANES])
        q_hi, q_lo = split(q2)
        gate = jnp.dot(jnp.concatenate([km_hi, km_hi, km_lo], axis=1),
                       jnp.concatenate([q_hi, q_lo, q_hi], axis=0), preferred_element_type=F32)
        g = jnp.where(brow < q_blk, gate, -jnp.inf)
        bias = jnp.full((nbr, 2 * tm), MASK_VALUE, F32)
        for _ in range(MOBA_TOPK):
            mx = jnp.max(g, axis=0, keepdims=True)
            first = jnp.min(jnp.where(g == mx, browf, float(nbr)), axis=0, keepdims=True)
            hit = browf == first
            bias = jnp.where(hit & (mx > -jnp.inf), 0.0, bias)
            g = jnp.where(hit, -jnp.inf, g)
        for h in range(2):
            bias_ref[0, 2 * hp + h] = bias[:, h * tm:(h + 1) * tm].astype(BF16)

    ones_rows = (lax.broadcasted_iota(jnp.int32, (V_ROWS - A_HEAD_DIM, MOBA_BLOCK), 0)
                 == 0).astype(BF16)
    for h in range(A_HEADS):
        for n in range(nblk):
            v_ref[0, h, n, 0:A_HEAD_DIM, :] = vt[h * A_HEAD_DIM:(h + 1) * A_HEAD_DIM,
                                                 n * MOBA_BLOCK:(n + 1) * MOBA_BLOCK].astype(BF16)
            v_ref[0, h, n, A_HEAD_DIM:, :] = ones_rows
    gate_pair(0)
    bq_ref[0] = proj(c, B_KEY_WIDTH) * (B_HEAD_K ** -0.5); c += B_KEY_WIDTH
    bk_ref[0] = proj(c, B_KEY_WIDTH); c += B_KEY_WIDTH
    gate_pair(1)
    bv_ref[0] = proj(c, B_VAL_WIDTH); c += B_VAL_WIDTH
    gate_pair(2)
    og_ref[0] = proj(c, B_VAL_WIDTH); c += B_VAL_WIDTH
    gate_pair(3)
    gk = proj(c, LANES).astype(BF16)
    z = jnp.dot(gk, wgk_ref[...], preferred_element_type=F32) + bgk_ref[...]
    gl_ref[0] = _log_sigmoid(z) / GLA_GATE_NORM


def _front0(x, g, w_in, w_gk2, b_gk2):
    bsz, seq, d = x.shape
    tm = min(1024, seq)
    widths = [A_WIDTH, A_WIDTH, A_WIDTH, B_KEY_WIDTH, B_KEY_WIDTH, B_VAL_WIDTH, GLA_GATE_RANK]
    aq, ak, av, bq, bk, bv, bgk, bog = jnp.split(
        w_in, [sum(widths[:n + 1]) for n in range(len(widths))], axis=1)
    gk_pad = LANES - GLA_GATE_RANK
    w = jnp.concatenate([ak, bq, bk, bv, bog, jnp.pad(bgk, ((0, 0), (0, gk_pad)))],
                        axis=1).astype(BF16)
    wt = jnp.concatenate([aq, av], axis=1).T.astype(BF16)
    wgk = jnp.pad(w_gk2, ((0, gk_pad), (0, 0))).astype(BF16)
    ncol = w.shape[1]

    half = A_HEAD_DIM // 2
    inv = ROPE_THETA ** (-jnp.arange(half, dtype=F32) / half)
    pos = jnp.arange(seq, dtype=jnp.int32).astype(F32)
    first = (jnp.arange(LANES) % A_HEAD_DIM) < half

    def tables(feature_major):
        if feature_major:
            ang, reps, sel = inv[:, None] * pos[None, :], (LANES // half, 1), first[:, None]
        else:
            ang, reps, sel = pos[:, None] * inv[None, :], (1, LANES // half), first[None, :]
        cos, sin = jnp.tile(jnp.cos(ang), reps), jnp.tile(jnp.sin(ang), reps)
        return cos, jnp.where(sel, -sin, 0.0), jnp.where(sel, 0.0, sin)

    nb = seq // MOBA_BLOCK
    nbr = _moba_gate_rows(nb)
    row = lambda n: pl.BlockSpec((1, tm, n), lambda b, i: (b, i, 0))
    tab = pl.BlockSpec((tm, LANES), lambda b, i: (i, 0))
    tab_t = pl.BlockSpec((LANES, tm), lambda b, i: (0, i))
    nt = seq // tm
    outs = pl.pallas_call(
        functools.partial(_front0_kernel, tm=tm),
        grid=(bsz, nt),
        in_specs=[row(d), _resident((1, d)), _resident((d, ncol)), _resident((2 * A_WIDTH, d)),
                  _resident((LANES, B_KEY_WIDTH)), _resident((1, B_KEY_WIDTH)),
                  tab, tab, tab, tab_t, tab_t, tab_t],
        out_specs=[pl.BlockSpec((1, A_WIDTH, tm), lambda b, i: (b, 0, i)), row(A_WIDTH),
                   pl.BlockSpec((1, A_HEADS, tm // MOBA_BLOCK, V_ROWS, MOBA_BLOCK),
                                lambda b, i: (b, 0, i, 0, 0)),
                   pl.BlockSpec((1, A_HEADS, nbr, tm), lambda b, i: (b, 0, 0, i)),
                   row(B_KEY_WIDTH), row(B_KEY_WIDTH), row(B_VAL_WIDTH), row(B_KEY_WIDTH),
                   row(B_VAL_WIDTH)],
        out_shape=[jax.ShapeDtypeStruct((bsz, A_WIDTH, seq), BF16),
                   jax.ShapeDtypeStruct((bsz, seq, A_WIDTH), BF16),
                   jax.ShapeDtypeStruct((bsz, A_HEADS, nb, V_ROWS, MOBA_BLOCK), BF16),
                   jax.ShapeDtypeStruct((bsz, A_HEADS, nbr, seq), BF16),
                   jax.ShapeDtypeStruct((bsz, seq, B_KEY_WIDTH), F32),
                   jax.ShapeDtypeStruct((bsz, seq, B_KEY_WIDTH), F32),
                   jax.ShapeDtypeStruct((bsz, seq, B_VAL_WIDTH), F32),
                   jax.ShapeDtypeStruct((bsz, seq, B_KEY_WIDTH), F32),
                   jax.ShapeDtypeStruct((bsz, seq, B_VAL_WIDTH), F32)],
        scratch_shapes=[pltpu.VMEM((nbr, A_WIDTH), F32)],
        compiler_params=_params(("arbitrary", "arbitrary")),
        name="front0",
    )(x, g.reshape(1, d), w, wt, wgk, b_gk2.reshape(1, B_KEY_WIDTH), *tables(False), *tables(True))
    return outs


def _moba_kernel(qt_ref, k_ref, vt_ref, bias_ref, o_ref, rhs_ref, acc_ref, sa_ref, sb_ref, *,
                 group, nb):
    qb = pl.program_id(2)
    blk = MOBA_BLOCK
    qt = qt_ref[0]
    feat = lax.broadcasted_iota(jnp.int32, (LANES, 1), 0)
    nbr = bias_ref.shape[2]
    for h in range(2):
        rhs_ref[:, h * blk:(h + 1) * blk] = jnp.concatenate(
            [jnp.where(feat // A_HEAD_DIM == h, qt, jnp.zeros_like(qt)), bias_ref[0, h],
             jnp.full((MOBA_BLOCKS_PAD - nbr, blk), MASK_VALUE, BF16)], axis=0)

    keys = group * blk
    blk_of_row = lax.broadcasted_iota(jnp.int32, (keys, MOBA_BLOCKS_PAD), 0) // blk
    lane_blk = lax.broadcasted_iota(jnp.int32, (keys, MOBA_BLOCKS_PAD), 1)

    def scores(g, s_ref):
        j0 = g * group
        off = pl.multiple_of(jnp.minimum(j0, nb - group) * blk, blk)
        onehot = (lane_blk == blk_of_row + j0).astype(BF16)
        lhs_j = jnp.concatenate([k_ref[0, pl.ds(off, keys), :], onehot], axis=1)
        s = jnp.dot(lhs_j, rhs_ref[...], preferred_element_type=F32)
        s_ref[...] = s
        return jnp.max(s, axis=0, keepdims=True)

    def accumulate(g, s_ref, m, s_max):
        j0 = g * group
        m_new = jnp.maximum(m, s_max)
        alpha = jnp.exp2(m - m_new)
        p = jnp.exp2(s_ref[...] - m_new).astype(BF16)
        for h in range(2):
            vt = jnp.concatenate([vt_ref[0, h, jnp.minimum(j0 + u, nb - 1)] for u in range(group)],
                                 axis=1)
            acc_ref[h] = (alpha[:, h * blk:(h + 1) * blk] * acc_ref[h]
                          + jnp.dot(vt, p[:, h * blk:(h + 1) * blk], preferred_element_type=F32))
        return m_new

    def body(i, carry):
        m, max_a = carry
        max_b = scores(2 * i + 1, sb_ref)
        m = accumulate(2 * i, sa_ref, m, max_a)
        max_a = scores(2 * i + 2, sa_ref)
        m = accumulate(2 * i + 1, sb_ref, m, max_b)
        return m, max_a

    max_0 = scores(0, sa_ref)

    own = pl.multiple_of(qb * blk, blk)
    lhs = jnp.concatenate([k_ref[0, pl.ds(own, blk), :], jnp.zeros((blk, MOBA_BLOCKS_PAD), BF16)],
                          axis=1)
    causal = (lax.broadcasted_iota(jnp.int32, (blk, 2 * blk), 0)
              <= lax.broadcasted_iota(jnp.int32, (blk, 2 * blk), 1) % blk)
    s = jnp.where(causal, jnp.dot(lhs, rhs_ref[...], preferred_element_type=F32), MASK_VALUE)
    m0 = jnp.max(s, axis=0, keepdims=True)
    p = jnp.exp2(s - m0).astype(BF16)
    for h in range(2):
        acc_ref[h] = jnp.dot(vt_ref[0, h, qb], p[:, h * blk:(h + 1) * blk],
                             preferred_element_type=F32)

    n_groups = (qb + group - 1) // group
    n_pairs = jnp.maximum(n_groups - 1, 0) // 2

    def two_pairs(i, carry):
        return body(2 * i + 1, body(2 * i, carry))

    carry = lax.fori_loop(0, n_pairs // 2, two_pairs, (m0, max_0))
    m, max_a = lax.cond(n_pairs % 2 == 1, lambda c: body(n_pairs - 1, c), lambda c: c, carry)
    left = n_groups - 2 * n_pairs

    @pl.when(left == 1)
    def _():
        accumulate(n_groups - 1, sa_ref, m, max_a)

    @pl.when(left == 2)
    def _():
        max_b = scores(n_groups - 1, sb_ref)
        m_mid = accumulate(n_groups - 2, sa_ref, m, max_a)
        accumulate(n_groups - 1, sb_ref, m_mid, max_b)

    outs = []
    for h in range(2):
        a = acc_ref[h]
        outs.append(a[0:A_HEAD_DIM] / a[A_HEAD_DIM:A_HEAD_DIM + 1])
    o_ref[0] = jnp.concatenate(outs, axis=0).T


def _moba(qt, k, vt, bias):
    bsz, seq, _ = k.shape
    nb = seq // MOBA_BLOCK
    nbr = _moba_gate_rows(nb)
    npair = A_WIDTH // LANES
    group = 4 if nb % 4 == 0 else 1
    return pl.pallas_call(
        functools.partial(_moba_kernel, group=group, nb=nb),
        grid=(bsz, npair, nb),
        in_specs=[pl.BlockSpec((1, LANES, MOBA_BLOCK), lambda b, hp, i: (b, hp, i)),
                  pl.BlockSpec((1, seq, LANES), lambda b, hp, i: (b, 0, hp)),
                  pl.BlockSpec((1, 2, nb, V_ROWS, MOBA_BLOCK), lambda b, hp, i: (b, hp, 0, 0, 0)),
                  pl.BlockSpec((1, 2, nbr, MOBA_BLOCK), lambda b, hp, i: (b, hp, 0, i))],
        out_specs=pl.BlockSpec((1, MOBA_BLOCK, LANES), lambda b, hp, i: (b, i, hp)),
        out_shape=jax.ShapeDtypeStruct((bsz, seq, A_WIDTH), F32),
        scratch_shapes=[pltpu.VMEM((2 * LANES, 2 * MOBA_BLOCK), BF16),
                        pltpu.VMEM((2, V_ROWS, MOBA_BLOCK), F32),
                        pltpu.VMEM((group * MOBA_BLOCK, 2 * MOBA_BLOCK), F32),
                        pltpu.VMEM((group * MOBA_BLOCK, 2 * MOBA_BLOCK), F32)],
        compiler_params=_params(("arbitrary", "arbitrary", "arbitrary")),
        name="moba",
    )(qt, k, vt, bias)


def _gla_kernel(q_ref, k_ref, v_ref, g_ref, og_ref, ng_ref, o_ref, st_ref, *, tm):
    @pl.when(pl.program_id(1) == 0)
    def _():
        st_ref[...] = jnp.zeros_like(st_ref)

    q, k, v, g = q_ref[0], k_ref[0], v_ref[0], g_ref[0]
    kw, vw = B_KEY_WIDTH, B_VAL_WIDTH
    row = lax.broadcasted_iota(jnp.int32, (tm, 1), 0)

    pre, post, tot, upper = {1: g}, {1: jnp.zeros_like(g)}, {1: g}, {}
    n = 1
    while n < GLA_CHUNK:
        upper[n] = (row % (2 * n)) >= n
        below = pltpu.roll(tot[n], n, 0)
        above = pltpu.roll(tot[n], tm - n, 0)
        pre[2 * n] = pre[n] + jnp.where(upper[n], below, 0.0)
        post[2 * n] = post[n] + jnp.where(upper[n], 0.0, above)
        tot[2 * n] = tot[n] + jnp.where(upper[n], below, above)
        n *= 2
    levels = sorted(upper)

    q_lvl = {n: jnp.where(upper[n], q * jnp.exp(pre[n]), 0.0).astype(BF16) for n in levels}
    k_lvl = {n: jnp.where(upper[n], 0.0, k * jnp.exp(post[n])).astype(BF16) for n in levels}
    q_bf, k_bf = q.astype(BF16), k.astype(BF16)
    q_chk = (q * jnp.exp(pre[GLA_CHUNK])).astype(BF16)
    k_chk = (k * jnp.exp(post[GLA_CHUNK])).astype(BF16)

    head_vk = (lax.broadcasted_iota(jnp.int32, (vw, kw), 0) // B_HEAD_V
               == lax.broadcasted_iota(jnp.int32, (vw, kw), 1) // B_HEAD_K)
    head_vv = (lax.broadcasted_iota(jnp.int32, (vw, vw), 0) // B_HEAD_V
               == lax.broadcasted_iota(jnp.int32, (vw, vw), 1) // B_HEAD_V).astype(BF16)

    tt = lax.broadcasted_iota(jnp.int32, (B_HEADS * GLA_CHUNK, GLA_CHUNK), 0) % GLA_CHUNK
    ss = lax.broadcasted_iota(jnp.int32, (B_HEADS * GLA_CHUNK, GLA_CHUNK), 1)
    same_row = tt == ss
    splits_at = {n: (tt // (2 * n) == ss // (2 * n)) & (tt // n != ss // n) for n in levels}
    lane_k = lax.broadcasted_iota(jnp.int32, (1, kw), 1)
    lane_v = lax.broadcasted_iota(jnp.int32, (1, vw), 1)

    def stack_heads(x):
        return jnp.concatenate([jnp.where(lane_k // B_HEAD_K == h, x, jnp.zeros_like(x))
                                for h in range(B_HEADS)], axis=0)

    def scores(lhs, rhs):
        return lax.dot_general(stack_heads(lhs), rhs, (((1,), (1,)), ((), ())),
                               preferred_element_type=F32)

    outs = []
    for c in range(tm // GLA_CHUNK):
        lo = c * GLA_CHUNK
        rows = slice(lo, lo + GLA_CHUNK)
        vc = v[rows].astype(BF16)

        attn = jnp.where(same_row, scores(q_bf[rows], k_bf[rows]), 0.0)
        for n in levels:
            attn = jnp.where(splits_at[n], scores(q_lvl[n][rows], k_lvl[n][rows]), attn)
        res = jnp.dot(attn.astype(BF16), vc, preferred_element_type=F32)
        o_in = jnp.zeros((GLA_CHUNK, vw), F32)
        for h in range(B_HEADS):
            o_in = o_in + jnp.where(lane_v // B_HEAD_V == h,
                                    res[h * GLA_CHUNK:(h + 1) * GLA_CHUNK], 0.0)

        st = st_ref[...]
        o_st = lax.dot_general(q_chk[rows], st.astype(BF16), (((1,), (1,)), ((), ())),
                               preferred_element_type=F32)
        upd = lax.dot_general(vc, k_chk[rows], (((0,), (0,)), ((), ())),
                              preferred_element_type=F32)
        decay = jnp.exp(tot[GLA_CHUNK][lo:lo + 1, :])
        st_ref[...] = st * decay + jnp.where(head_vk, upd, 0.0)
        outs.append(o_in + o_st)

    o = jnp.concatenate(outs, axis=0)

    sq = o * o
    sq_hi = sq.astype(BF16)
    sq_lo = (sq - sq_hi.astype(F32)).astype(BF16)
    ms = (jnp.dot(sq_hi, head_vv, preferred_element_type=F32)
          + jnp.dot(sq_lo, head_vv, preferred_element_type=F32)) * (1.0 / B_HEAD_V)
    og = og_ref[0]
    o_ref[0] = (o * lax.rsqrt(ms + NORM_EPS) * ng_ref[...]) * (og * jax.nn.sigmoid(og))


def _gla(bq, bk, bv, gl, og, norm_g):
    bsz, seq, _ = bq.shape
    tm = min(512, seq)
    row = lambda n: pl.BlockSpec((1, tm, n), lambda b, i: (b, i, 0))
    return pl.pallas_call(
        functools.partial(_gla_kernel, tm=tm),
        grid=(bsz, seq // tm),
        in_specs=[row(B_KEY_WIDTH), row(B_KEY_WIDTH), row(B_VAL_WIDTH), row(B_KEY_WIDTH),
                  row(B_VAL_WIDTH), _resident((1, B_VAL_WIDTH))],
        out_specs=row(B_VAL_WIDTH),
        out_shape=jax.ShapeDtypeStruct((bsz, seq, B_VAL_WIDTH), F32),
        scratch_shapes=[pltpu.VMEM((B_VAL_WIDTH, B_KEY_WIDTH), F32)],
        compiler_params=_params(("arbitrary", "arbitrary")),
        name="gla",
    )(bq, bk, bv, gl, og, norm_g.reshape(1, B_VAL_WIDTH))


def _ffn_kernel(*refs, tm, bn, dff, final_norm, mixer):
    if mixer:
        x_ref, oa_ref, ob_ref, wa_ref, wb_ref = refs[:5]
        refs = refs[5:]
    else:
        x_ref, refs = refs[0], refs[1:]
    g_ref, wup_ref, cw_ref, cb_ref, wdn_ref, fg_ref, o_ref, halo_ref, ext_ref = refs
    nkb = dff // bn

    @pl.when(pl.program_id(1) == 0)
    def _():
        halo_ref[...] = jnp.zeros_like(halo_ref)

    x = x_ref[0]
    if mixer:
        o_ref[0] = (jnp.dot(oa_ref[0].astype(BF16), wa_ref[...], preferred_element_type=F32)
                    + jnp.dot(ob_ref[0].astype(BF16), wb_ref[...], preferred_element_type=F32))
        x = x + o_ref[0]
    hn = _rms(x, g_ref[...]).astype(BF16)

    def up_stage(kb):
        for part in range(2):
            c0 = part * dff + kb * bn
            up = jnp.dot(hn, wup_ref[:, c0:c0 + bn], preferred_element_type=F32)
            ext_ref[kb % 2, part, 0:SUBLANES, :] = halo_ref[:, c0:c0 + bn]
            ext_ref[kb % 2, part, SUBLANES:, :] = up
            halo_ref[:, c0:c0 + bn] = up[tm - SUBLANES:, :]

    def gate_stage(kb):
        halves = []
        for part in range(2):
            c0 = part * dff + kb * bn
            y = cb_ref[:, c0:c0 + bn]
            for i in range(FFN_CONV):
                off = SUBLANES - (FFN_CONV - 1) + i
                y = y + ext_ref[kb % 2, part, pl.ds(off, tm), :] * cw_ref[i:i + 1, c0:c0 + bn]
            halves.append(y)
        return (halves[0] * _gelu_tanh(halves[1])).astype(BF16)

    up_stage(0)
    if nkb > 1:
        up_stage(1)
    a = gate_stage(0)
    acc = x
    for kb in range(nkb):
        if kb + 2 < nkb:
            up_stage(kb + 2)
        a_next = gate_stage(kb + 1) if kb + 1 < nkb else None
        acc = acc + jnp.dot(a, wdn_ref[kb * bn:(kb + 1) * bn, :], preferred_element_type=F32)
        a = a_next
    if final_norm:
        acc = _rms(acc, fg_ref[...])
    o_ref[0] = acc


def _conv_ffn(x, g, w_up, conv_w, conv_b, w_down, final_g, final_norm, mixer=None):
    bsz, seq, d = x.shape
    dff = w_down.shape[0]
    tm = min(512, seq)
    bn = 1536
    rows = lambda n: pl.BlockSpec((1, tm, n), lambda b, i: (b, i, 0))
    row = rows(d)
    lead_specs, lead_args = [row], [x]
    if mixer is not None:
        o_a, o_b, w_out = mixer
        w = w_out.astype(BF16)
        lead_specs += [rows(A_WIDTH), rows(B_VAL_WIDTH),
                       _resident((A_WIDTH, d)), _resident((B_VAL_WIDTH, d))]
        lead_args += [o_a, o_b, w[:A_WIDTH], w[A_WIDTH:]]
    return pl.pallas_call(
        functools.partial(_ffn_kernel, tm=tm, bn=bn, dff=dff, final_norm=final_norm,
                          mixer=mixer is not None),
        grid=(bsz, seq // tm),
        in_specs=lead_specs + [_resident((1, d)), _resident((d, 2 * dff)),
                               _resident((FFN_CONV, 2 * dff)), _resident((1, 2 * dff)),
                               _resident((dff, d)), _resident((1, d))],
        out_specs=row,
        out_shape=jax.ShapeDtypeStruct((bsz, seq, d), F32),
        scratch_shapes=[pltpu.VMEM((SUBLANES, 2 * dff), F32),
                        pltpu.VMEM((2, 2, tm + SUBLANES, bn), F32)],
        compiler_params=_params(("arbitrary", "arbitrary")),
        name="conv_ffn_final" if final_norm else "conv_ffn",
    )(*lead_args, g.reshape(1, d), w_up.astype(BF16), conv_w, conv_b.reshape(1, 2 * dff),
      w_down.astype(BF16), final_g.reshape(1, d))


def _rglru_kernel(x_ref, g_ref, win_ref, cw_ref, cb_ref, wa_ref, ba_ref, wx_ref, bx_ref,
                  lam_ref, wout_ref, o_ref, ext_ref, halo_ref, hlast_ref, *, tm, drnn, nseq):
    seg = tm // SUBLANES
    shifts = RG_CONV - 1
    bw = drnn // RG_BLOCKS
    seqs = range(nseq)

    @pl.when(pl.program_id(1) == 0)
    def _():
        halo_ref[...] = jnp.zeros_like(halo_ref)
        hlast_ref[...] = jnp.zeros_like(hlast_ref)

    prow = lax.broadcasted_iota(jnp.int32, (tm, tm), 0)
    pcol = lax.broadcasted_iota(jnp.int32, (tm, tm), 1)
    perm = (pcol == (prow % SUBLANES) * seg + prow // SUBLANES).astype(BF16)
    unperm = (prow == (pcol % SUBLANES) * seg + pcol // SUBLANES).astype(BF16)
    first_sub = lax.broadcasted_iota(jnp.int32, (SUBLANES, 1), 0) == 0

    xs = [x_ref[b] for b in seqs]
    hn = [jnp.dot(perm, _rms(xs[b], g_ref[...]).astype(BF16),
                  preferred_element_type=F32).astype(BF16) for b in seqs]
    gate = [jnp.dot(hn[b], win_ref[:, 0:drnn], preferred_element_type=F32) for b in seqs]
    xr_raw = [jnp.dot(hn[b], win_ref[:, drnn:2 * drnn], preferred_element_type=F32) for b in seqs]

    xr = []
    for b in seqs:
        for r in range(shifts):
            lo = (seg - shifts + r) * SUBLANES
            cur = xr_raw[b][lo:lo + SUBLANES]
            ext_ref[b, r * SUBLANES:(r + 1) * SUBLANES, :] = jnp.where(
                first_sub, pltpu.roll(halo_ref[b, r], 1, 0), pltpu.roll(cur, 1, 0))
            halo_ref[b, r] = cur
        ext_ref[b, shifts * SUBLANES:, :] = xr_raw[b]
        acc = cb_ref[...]
        for i in range(RG_CONV):
            acc = acc + ext_ref[b, i * SUBLANES:i * SUBLANES + tm, :] * cw_ref[i:i + 1, :]
        xr.append(acc)

    ys = []
    for b in seqs:
        xb = xr[b].astype(BF16)
        ra, ri = [], []
        for n in range(RG_BLOCKS):
            xn = xb[:, n * bw:(n + 1) * bw]
            ra.append(jnp.dot(xn, wa_ref[n], preferred_element_type=F32))
            ri.append(jnp.dot(xn, wx_ref[n], preferred_element_type=F32))
        r = jax.nn.sigmoid(jnp.concatenate(ra, axis=1) + ba_ref[...])
        ig = jax.nn.sigmoid(jnp.concatenate(ri, axis=1) + bx_ref[...])
        log_a = r * lam_ref[...]
        a = jnp.exp(log_a)
        th = jnp.tanh(log_a)
        one_minus_a2 = -2.0 * th / (1.0 - th)
        u = jnp.sqrt(one_minus_a2) * (ig * xr[b])

        h_loc, a_cum = [u[0:SUBLANES]], [a[0:SUBLANES]]
        for j in range(1, seg):
            aj = a[j * SUBLANES:(j + 1) * SUBLANES]
            h_loc.append(aj * h_loc[-1] + u[j * SUBLANES:(j + 1) * SUBLANES])
            a_cum.append(aj * a_cum[-1])
        h_in = [hlast_ref[b]]
        for s in range(SUBLANES):
            h_in.append(h_loc[-1][s:s + 1] + a_cum[-1][s:s + 1] * h_in[-1])
        hlast_ref[b] = h_in[SUBLANES]
        h_seg = jnp.concatenate(h_in[:SUBLANES], axis=0)
        h = jnp.concatenate([h_loc[j] + a_cum[j] * h_seg for j in range(seg)], axis=0)
        ys.append((h * _gelu_tanh(gate[b])).astype(BF16))

    for b in seqs:
        y_t = jnp.dot(unperm, ys[b], preferred_element_type=F32).astype(BF16)
        o_ref[b] = xs[b] + jnp.dot(y_t, wout_ref[...], preferred_element_type=F32)


def _rglru(x, g, w_in, conv_w, conv_b, w_a, b_a, w_x, b_x, lam, w_out):
    bsz, seq, d = x.shape
    drnn = w_out.shape[0]
    bw = drnn // RG_BLOCKS
    tm = min(256, seq)
    nseq = 2 if bsz % 2 == 0 else 1
    row = pl.BlockSpec((nseq, tm, d), lambda b, i: (b, i, 0))
    lam_c = (-LRU_C) * jax.nn.softplus(-lam.astype(F32))
    return pl.pallas_call(
        functools.partial(_rglru_kernel, tm=tm, drnn=drnn, nseq=nseq),
        grid=(bsz // nseq, seq // tm),
        in_specs=[row, _resident((1, d)), _resident((d, 2 * drnn)), _resident((RG_CONV, drnn)),
                  _resident((1, drnn)), _resident((RG_BLOCKS, bw, bw)), _resident((1, drnn)),
                  _resident((RG_BLOCKS, bw, bw)), _resident((1, drnn)), _resident((1, drnn)),
                  _resident((drnn, d))],
        out_specs=row,
        out_shape=jax.ShapeDtypeStruct((bsz, seq, d), F32),
        scratch_shapes=[pltpu.VMEM((nseq, tm + (RG_CONV - 1) * SUBLANES, drnn), F32),
                        pltpu.VMEM((nseq, RG_CONV - 1, SUBLANES, drnn), F32),
                        pltpu.VMEM((nseq, 1, drnn), F32)],
        compiler_params=_params(("arbitrary", "arbitrary")),
        name="rglru",
    )(x, g.reshape(1, d), w_in.astype(BF16), conv_w, conv_b.reshape(1, drnn),
      w_a.astype(BF16), b_a.reshape(1, drnn), w_x.astype(BF16), b_x.reshape(1, drnn),
      lam_c.reshape(1, drnn), w_out.astype(BF16))


def kernel(x, mix_norm_g, ffn_norm_g, final_norm_g, ev_w_in, ev_w_gk2, ev_b_gk2, ev_gla_norm_g,
           ev_w_out, od_w_in, od_conv_w, od_conv_b, od_w_a, od_b_a, od_w_x, od_b_x, od_lambda,
           od_w_out, ffn_w_up, ffn_conv_w, ffn_conv_b, ffn_w_down):
    depth = mix_norm_g.shape[0]
    h = x
    for l in range(depth):
        j = l // 2
        mixer = None
        if l % 2 == 0:
            q, k, v, bias, bq, bk, bv, gl, og = _front0(h, mix_norm_g[l], ev_w_in[j], ev_w_gk2[j],
                                                      ev_b_gk2[j])
            o_a = _moba(q, k, v, bias)
            o_b = _gla(bq, bk, bv, gl, og, ev_gla_norm_g[j])
            mixer = (o_a, o_b, ev_w_out[j])
        else:
            h = _rglru(h, mix_norm_g[l], od_w_in[j], od_conv_w[j], od_conv_b[j], od_w_a[j],
                       od_b_a[j], od_w_x[j], od_b_x[j], od_lambda[j], od_w_out[j])
        h = _conv_ffn(h, ffn_norm_g[l], ffn_w_up[l], ffn_conv_w[l], ffn_conv_b[l], ffn_w_down[l],
                      final_norm_g, final_norm=(l == depth - 1), mixer=mixer)
    return h
```

```python
import functools

import jax
import jax.numpy as jnp
from jax import lax
from jax.experimental import pallas as pl
from jax.experimental.pallas import tpu as pltpu

F32 = jnp.float32
BF16 = jnp.bfloat16

A_HEADS = 8
A_HEAD_DIM = 64
A_WIDTH = A_HEADS * A_HEAD_DIM
MOBA_BLOCK = 256
MOBA_TOPK = 3
ROPE_THETA = 10000.0
B_HEADS = 8
B_HEAD_K = 32
B_HEAD_V = 64
B_KEY_WIDTH = B_HEADS * B_HEAD_K
B_VAL_WIDTH = B_HEADS * B_HEAD_V
GLA_GATE_RANK = 16
GLA_GATE_NORM = 16.0
RG_BLOCKS = 4
RG_CONV = 4
LRU_C = 8.0
FFN_CONV = 3
NORM_EPS = 1e-6
MASK_VALUE = -1e30

LANES = 128
SUBLANES = 8
BF16_ROWS = 16
LOG2_E = 1.4426950408889634
VMEM_LIMIT = 56 * 1024 * 1024

GLA_CHUNK = 64
MOBA_BLOCKS_PAD = LANES
V_ROWS = A_HEAD_DIM + BF16_ROWS


def _params(sem):
    return pltpu.CompilerParams(dimension_semantics=sem, vmem_limit_bytes=VMEM_LIMIT)


def _rms(x, g):
    ms = jnp.mean(x * x, axis=-1, keepdims=True)
    return x * lax.rsqrt(ms + NORM_EPS) * g


def _gelu_tanh(x):
    return 0.5 * x * (1.0 + jnp.tanh(0.7978845608028654 * (x + 0.044715 * (x * x * x))))


def _log_sigmoid(z):
    return jnp.minimum(z, 0.0) - jnp.log1p(jnp.exp(-jnp.abs(z)))


def _moba_gate_rows(nb):
    assert nb <= MOBA_BLOCKS_PAD
    return -(-nb // BF16_ROWS) * BF16_ROWS


def _resident(shape):
    nd = len(shape)
    return pl.BlockSpec(shape, lambda *_: (0,) * nd, pipeline_mode=pl.Buffered(1))


def _front0_kernel(x_ref, g_ref, w_ref, wt_ref, wgk_ref, bgk_ref, cos_ref, sa_ref, sb_ref,
                   cost_ref, sat_ref, sbt_ref,
                   q_ref, k_ref, v_ref, bias_ref, bq_ref, bk_ref, bv_ref, gl_ref, og_ref,
                   km_ref, *, tm):
    @pl.when(pl.program_id(1) == 0)
    def _():
        km_ref[...] = jnp.zeros_like(km_ref)

    hn = _rms(x_ref[0], g_ref[...]).astype(BF16)

    def proj(c0, n):
        return jnp.dot(hn, w_ref[:, c0:c0 + n], preferred_element_type=F32)

    def proj_t(r0, n):
        return lax.dot_general(wt_ref[r0:r0 + n, :], hn, (((1,), (1,)), ((), ())),
                               preferred_element_type=F32)

    def rot(t, cos, sa, sb, axis):
        parts = []
        for c in range(t.shape[axis] // LANES):
            tc = lax.slice_in_dim(t, c * LANES, (c + 1) * LANES, axis=axis)
            parts.append(tc * cos + pltpu.roll(tc, LANES - 32, axis) * sa
                         + pltpu.roll(tc, 32, axis) * sb)
        return jnp.concatenate(parts, axis=axis)

    qt = rot(proj_t(0, A_WIDTH), cost_ref[...], sat_ref[...], sbt_ref[...], 0)
    q_ref[0] = (qt * (A_HEAD_DIM ** -0.5 * LOG2_E)).astype(BF16)
    vt = proj_t(A_WIDTH, A_WIDTH)
    c = 0
    k = rot(proj(c, A_WIDTH), cos_ref[...], sa_ref[...], sb_ref[...], 1); c += A_WIDTH
    k_ref[0] = k.astype(BF16)
    nblk = tm // MOBA_BLOCK
    kmean = jnp.mean(k.reshape(nblk, MOBA_BLOCK, A_WIDTH), axis=1)
    blk0 = pl.program_id(1) * nblk
    for n in range(nblk):
        km_ref[pl.ds(blk0 + n, 1), :] = kmean[n:n + 1]

    nbr = km_ref.shape[0]
    feat = lax.broadcasted_iota(jnp.int32, (LANES, 1), 0)
    brow = lax.broadcasted_iota(jnp.int32, (nbr, 2 * tm), 0)
    browf = brow.astype(F32)
    q_blk = blk0 + (lax.broadcasted_iota(jnp.int32, (1, 2 * tm), 1) % tm) // MOBA_BLOCK

    def split(a):
        hi = a.astype(BF16)
        return hi, (a - hi.astype(F32)).astype(BF16)

    def gate_pair(hp):
        q_pair = qt[hp * LANES:(hp + 1) * LANES]
        q2 = jnp.concatenate([jnp.where(feat // A_HEAD_DIM == h, q_pair, 0.0) for h in range(2)],
                             axis=1)
        km_hi, km_lo = split(km_ref[:, hp * LANES:(hp + 1) * LANES])
        q_hi, q_lo = split(q2)
        gate = jnp.dot(jnp.concatenate([km_hi, km_hi, km_lo], axis=1),
                       jnp.concatenate([q_hi, q_lo, q_hi], axis=0), preferred_element_type=F32)
        g = jnp.where(brow < q_blk, gate, -jnp.inf)
        bias = jnp.full((nbr, 2 * tm), MASK_VALUE, F32)
        for _ in range(MOBA_TOPK):
            mx = jnp.max(g, axis=0, keepdims=True)
            first = jnp.min(jnp.where(g == mx, browf, float(nbr)), axis=0, keepdims=True)
            hit = browf == first
            bias = jnp.where(hit & (mx > -jnp.inf), 0.0, bias)
            g = jnp.where(hit, -jnp.inf, g)
        for h in range(2):
            bias_ref[0, 2 * hp + h] = bias[:, h * tm:(h + 1) * tm].astype(BF16)

    ones_rows = (lax.broadcasted_iota(jnp.int32, (V_ROWS - A_HEAD_DIM, MOBA_BLOCK), 0)
                 == 0).astype(BF16)
    for h in range(A_HEADS):
        for n in range(nblk):
            v_ref[0, h, n, 0:A_HEAD_DIM, :] = vt[h * A_HEAD_DIM:(h + 1) * A_HEAD_DIM,
                                                 n * MOBA_BLOCK:(n + 1) * MOBA_BLOCK].astype(BF16)
            v_ref[0, h, n, A_HEAD_DIM:, :] = ones_rows
    gate_pair(0)
    bq_ref[0] = proj(c, B_KEY_WIDTH) * (B_HEAD_K ** -0.5); c += B_KEY_WIDTH
    bk_ref[0] = proj(c, B_KEY_WIDTH); c += B_KEY_WIDTH
    gate_pair(1)
    bv_ref[0] = proj(c, B_VAL_WIDTH); c += B_VAL_WIDTH
    gate_pair(2)
    og_ref[0] = proj(c, B_VAL_WIDTH); c += B_VAL_WIDTH
    gate_pair(3)
    gk = proj(c, LANES).astype(BF16)
    z = jnp.dot(gk, wgk_ref[...], preferred_element_type=F32) + bgk_ref[...]
    gl_ref[0] = _log_sigmoid(z) / GLA_GATE_NORM


def _front0(x, g, w_in, w_gk2, b_gk2):
    bsz, seq, d = x.shape
    tm = min(1024, seq)
    widths = [A_WIDTH, A_WIDTH, A_WIDTH, B_KEY_WIDTH, B_KEY_WIDTH, B_VAL_WIDTH, GLA_GATE_RANK]
    aq, ak, av, bq, bk, bv, bgk, bog = jnp.split(
        w_in, [sum(widths[:n + 1]) for n in range(len(widths))], axis=1)
    gk_pad = LANES - GLA_GATE_RANK
    w = jnp.concatenate([ak, bq, bk, bv, bog, jnp.pad(bgk, ((0, 0), (0, gk_pad)))],
                        axis=1).astype(BF16)
    wt = jnp.concatenate([aq, av], axis=1).T.astype(BF16)
    wgk = jnp.pad(w_gk2, ((0, gk_pad), (0, 0))).astype(BF16)
    ncol = w.shape[1]

    half = A_HEAD_DIM // 2
    inv = ROPE_THETA ** (-jnp.arange(half, dtype=F32) / half)
    pos = jnp.arange(seq, dtype=jnp.int32).astype(F32)
    first = (jnp.arange(LANES) % A_HEAD_DIM) < half

    def tables(feature_major):
        if feature_major:
            ang, reps, sel = inv[:, None] * pos[None, :], (LANES // half, 1), first[:, None]
        else:
            ang, reps, sel = pos[:, None] * inv[None, :], (1, LANES // half), first[None, :]
        cos, sin = jnp.tile(jnp.cos(ang), reps), jnp.tile(jnp.sin(ang), reps)
        return cos, jnp.where(sel, -sin, 0.0), jnp.where(sel, 0.0, sin)

    nb = seq // MOBA_BLOCK
    nbr = _moba_gate_rows(nb)
    row = lambda n: pl.BlockSpec((1, tm, n), lambda b, i: (b, i, 0))
    tab = pl.BlockSpec((tm, LANES), lambda b, i: (i, 0))
    tab_t = pl.BlockSpec((LANES, tm), lambda b, i: (0, i))
    nt = seq // tm
    outs = pl.pallas_call(
        functools.partial(_front0_kernel, tm=tm),
        grid=(bsz, nt),
        in_specs=[row(d), _resident((1, d)), _resident((d, ncol)), _resident((2 * A_WIDTH, d)),
                  _resident((LANES, B_KEY_WIDTH)), _resident((1, B_KEY_WIDTH)),
                  tab, tab, tab, tab_t, tab_t, tab_t],
        out_specs=[pl.BlockSpec((1, A_WIDTH, tm), lambda b, i: (b, 0, i)), row(A_WIDTH),
                   pl.BlockSpec((1, A_HEADS, tm // MOBA_BLOCK, V_ROWS, MOBA_BLOCK),
                                lambda b, i: (b, 0, i, 0, 0)),
                   pl.BlockSpec((1, A_HEADS, nbr, tm), lambda b, i: (b, 0, 0, i)),
                   row(B_KEY_WIDTH), row(B_KEY_WIDTH), row(B_VAL_WIDTH), row(B_KEY_WIDTH),
                   row(B_VAL_WIDTH)],
        out_shape=[jax.ShapeDtypeStruct((bsz, A_WIDTH, seq), BF16),
                   jax.ShapeDtypeStruct((bsz, seq, A_WIDTH), BF16),
                   jax.ShapeDtypeStruct((bsz, A_HEADS, nb, V_ROWS, MOBA_BLOCK), BF16),
                   jax.ShapeDtypeStruct((bsz, A_HEADS, nbr, seq), BF16),
                   jax.ShapeDtypeStruct((bsz, seq, B_KEY_WIDTH), F32),
                   jax.ShapeDtypeStruct((bsz, seq, B_KEY_WIDTH), F32),
                   jax.ShapeDtypeStruct((bsz, seq, B_VAL_WIDTH), F32),
                   jax.ShapeDtypeStruct((bsz, seq, B_KEY_WIDTH), F32),
                   jax.ShapeDtypeStruct((bsz, seq, B_VAL_WIDTH), F32)],
        scratch_shapes=[pltpu.VMEM((nbr, A_WIDTH), F32)],
        compiler_params=_params(("arbitrary", "arbitrary")),
        name="front0",
    )(x, g.reshape(1, d), w, wt, wgk, b_gk2.reshape(1, B_KEY_WIDTH), *tables(False), *tables(True))
    return outs


def _moba_kernel(qt_ref, k_ref, vt_ref, bias_ref, o_ref, rhs_ref, acc_ref, sa_ref, sb_ref, *,
                 group, nb, qn):
    blk = MOBA_BLOCK
    tq = qn * blk
    q_first = pl.program_id(2) * qn
    q_last = q_first + qn - 1
    qt = qt_ref[0]
    feat = lax.broadcasted_iota(jnp.int32, (LANES, 1), 0)
    nbr = bias_ref.shape[2]
    for h in range(2):
        rhs_ref[:, h * tq:(h + 1) * tq] = jnp.concatenate(
            [jnp.where(feat // A_HEAD_DIM == h, qt, jnp.zeros_like(qt)), bias_ref[0, h],
             jnp.full((MOBA_BLOCKS_PAD - nbr, tq), MASK_VALUE, BF16)], axis=0)

    keys = group * blk
    blk_of_row = lax.broadcasted_iota(jnp.int32, (keys, MOBA_BLOCKS_PAD), 0) // blk
    lane_blk = lax.broadcasted_iota(jnp.int32, (keys, MOBA_BLOCKS_PAD), 1)

    def scores(g, s_ref):
        j0 = g * group
        off = pl.multiple_of(jnp.minimum(j0, nb - group) * blk, blk)
        onehot = (lane_blk == blk_of_row + j0).astype(BF16)
        lhs_j = jnp.concatenate([k_ref[0, pl.ds(off, keys), :], onehot], axis=1)
        s = jnp.dot(lhs_j, rhs_ref[...], preferred_element_type=F32)
        s_ref[...] = s
        return jnp.max(s, axis=0, keepdims=True)

    def accumulate(g, s_ref, m, s_max):
        j0 = g * group
        m_new = jnp.maximum(m, s_max)
        alpha = jnp.exp2(m - m_new)
        p = jnp.exp2(s_ref[...] - m_new).astype(BF16)
        for h in range(2):
            vt = jnp.concatenate([vt_ref[0, h, jnp.minimum(j0 + u, nb - 1)] for u in range(group)],
                                 axis=1)
            acc_ref[h] = (alpha[:, h * tq:(h + 1) * tq] * acc_ref[h]
                          + jnp.dot(vt, p[:, h * tq:(h + 1) * tq], preferred_element_type=F32))
        return m_new

    def body(i, carry):
        m, max_a = carry
        max_b = scores(2 * i + 1, sb_ref)
        m = accumulate(2 * i, sa_ref, m, max_a)
        max_a = scores(2 * i + 2, sa_ref)
        m = accumulate(2 * i + 1, sb_ref, m, max_b)
        return m, max_a

    max_0 = scores(0, sa_ref)

    own = pl.multiple_of(q_first * blk, blk)
    lhs = jnp.concatenate([k_ref[0, pl.ds(own, tq), :], jnp.zeros((tq, MOBA_BLOCKS_PAD), BF16)],
                          axis=1)
    key_pos = lax.broadcasted_iota(jnp.int32, (tq, 2 * tq), 0)
    qry_pos = lax.broadcasted_iota(jnp.int32, (tq, 2 * tq), 1) % tq
    causal = (key_pos <= qry_pos) & (key_pos // blk == qry_pos // blk)
    s = jnp.where(causal, jnp.dot(lhs, rhs_ref[...], preferred_element_type=F32), MASK_VALUE)
    m0 = jnp.max(s, axis=0, keepdims=True)
    p = jnp.exp2(s - m0).astype(BF16)
    for h in range(2):
        vt_own = jnp.concatenate([vt_ref[0, h, q_first + u] for u in range(qn)], axis=1)
        acc_ref[h] = jnp.dot(vt_own, p[:, h * tq:(h + 1) * tq], preferred_element_type=F32)

    n_groups = (q_last + group - 1) // group
    n_pairs = jnp.maximum(n_groups - 1, 0) // 2

    def two_pairs(i, carry):
        return body(2 * i + 1, body(2 * i, carry))

    carry = lax.fori_loop(0, n_pairs // 2, two_pairs, (m0, max_0))
    m, max_a = lax.cond(n_pairs % 2 == 1, lambda c: body(n_pairs - 1, c), lambda c: c, carry)
    left = n_groups - 2 * n_pairs

    @pl.when(left == 1)
    def _():
        accumulate(n_groups - 1, sa_ref, m, max_a)

    @pl.when(left == 2)
    def _():
        max_b = scores(n_groups - 1, sb_ref)
        m_mid = accumulate(n_groups - 2, sa_ref, m, max_a)
        accumulate(n_groups - 1, sb_ref, m_mid, max_b)

    outs = []
    for h in range(2):
        a = acc_ref[h]
        outs.append(a[0:A_HEAD_DIM] / a[A_HEAD_DIM:A_HEAD_DIM + 1])
    o_ref[0] = jnp.concatenate(outs, axis=0).T


def _moba(qt, k, vt, bias):
    bsz, seq, _ = k.shape
    nb = seq // MOBA_BLOCK
    nbr = _moba_gate_rows(nb)
    npair = A_WIDTH // LANES
    group = 4 if nb % 4 == 0 else 1
    qn = 2 if nb % 2 == 0 else 1
    tq = qn * MOBA_BLOCK
    return pl.pallas_call(
        functools.partial(_moba_kernel, group=group, nb=nb, qn=qn),
        grid=(bsz, npair, nb // qn),
        in_specs=[pl.BlockSpec((1, LANES, tq), lambda b, hp, i: (b, hp, i)),
                  pl.BlockSpec((1, seq, LANES), lambda b, hp, i: (b, 0, hp)),
                  pl.BlockSpec((1, 2, nb, V_ROWS, MOBA_BLOCK), lambda b, hp, i: (b, hp, 0, 0, 0)),
                  pl.BlockSpec((1, 2, nbr, tq), lambda b, hp, i: (b, hp, 0, i))],
        out_specs=pl.BlockSpec((1, tq, LANES), lambda b, hp, i: (b, i, hp)),
        out_shape=jax.ShapeDtypeStruct((bsz, seq, A_WIDTH), F32),
        scratch_shapes=[pltpu.VMEM((2 * LANES, 2 * tq), BF16),
                        pltpu.VMEM((2, V_ROWS, tq), F32),
                        pltpu.VMEM((group * MOBA_BLOCK, 2 * tq), F32),
                        pltpu.VMEM((group * MOBA_BLOCK, 2 * tq), F32)],
        compiler_params=_params(("arbitrary", "arbitrary", "arbitrary")),
        name="moba",
    )(qt, k, vt, bias)


def _gla_kernel(q_ref, k_ref, v_ref, g_ref, og_ref, ng_ref, o_ref, st_ref, *, tm):
    @pl.when(pl.program_id(1) == 0)
    def _():
        st_ref[...] = jnp.zeros_like(st_ref)

    q, k, v, g = q_ref[0], k_ref[0], v_ref[0], g_ref[0]
    kw, vw = B_KEY_WIDTH, B_VAL_WIDTH
    row = lax.broadcasted_iota(jnp.int32, (tm, 1), 0)

    pre, post, tot, upper = {1: g}, {1: jnp.zeros_like(g)}, {1: g}, {}
    n = 1
    while n < GLA_CHUNK:
        upper[n] = (row % (2 * n)) >= n
        below = pltpu.roll(tot[n], n, 0)
        above = pltpu.roll(tot[n], tm - n, 0)
        pre[2 * n] = pre[n] + jnp.where(upper[n], below, 0.0)
        post[2 * n] = post[n] + jnp.where(upper[n], 0.0, above)
        tot[2 * n] = tot[n] + jnp.where(upper[n], below, above)
        n *= 2
    levels = sorted(upper)

    q_lvl = {n: jnp.where(upper[n], q * jnp.exp(pre[n]), 0.0).astype(BF16) for n in levels}
    k_lvl = {n: jnp.where(upper[n], 0.0, k * jnp.exp(post[n])).astype(BF16) for n in levels}
    q_bf, k_bf = q.astype(BF16), k.astype(BF16)
    q_chk = (q * jnp.exp(pre[GLA_CHUNK])).astype(BF16)
    k_chk = (k * jnp.exp(post[GLA_CHUNK])).astype(BF16)

    head_vk = (lax.broadcasted_iota(jnp.int32, (vw, kw), 0) // B_HEAD_V
               == lax.broadcasted_iota(jnp.int32, (vw, kw), 1) // B_HEAD_K)
    head_vv = (lax.broadcasted_iota(jnp.int32, (vw, vw), 0) // B_HEAD_V
               == lax.broadcasted_iota(jnp.int32, (vw, vw), 1) // B_HEAD_V).astype(BF16)

    tt = lax.broadcasted_iota(jnp.int32, (B_HEADS * GLA_CHUNK, GLA_CHUNK), 0) % GLA_CHUNK
    ss = lax.broadcasted_iota(jnp.int32, (B_HEADS * GLA_CHUNK, GLA_CHUNK), 1)
    same_row = tt == ss
    splits_at = {n: (tt // (2 * n) == ss // (2 * n)) & (tt // n != ss // n) for n in levels}
    lane_k = lax.broadcasted_iota(jnp.int32, (1, kw), 1)
    lane_v = lax.broadcasted_iota(jnp.int32, (1, vw), 1)

    def stack_heads(x):
        return jnp.concatenate([jnp.where(lane_k // B_HEAD_K == h, x, jnp.zeros_like(x))
                                for h in range(B_HEADS)], axis=0)

    def scores(lhs, rhs):
        return lax.dot_general(stack_heads(lhs), rhs, (((1,), (1,)), ((), ())),
                               preferred_element_type=F32)

    outs = []
    for c in range(tm // GLA_CHUNK):
        lo = c * GLA_CHUNK
        rows = slice(lo, lo + GLA_CHUNK)
        vc = v[rows].astype(BF16)

        attn = jnp.where(same_row, scores(q_bf[rows], k_bf[rows]), 0.0)
        for n in levels:
            attn = jnp.where(splits_at[n], scores(q_lvl[n][rows], k_lvl[n][rows]), attn)
        res = jnp.dot(attn.astype(BF16), vc, preferred_element_type=F32)
        o_in = jnp.zeros((GLA_CHUNK, vw), F32)
        for h in range(B_HEADS):
            o_in = o_in + jnp.where(lane_v // B_HEAD_V == h,
                                    res[h * GLA_CHUNK:(h + 1) * GLA_CHUNK], 0.0)

        st = st_ref[...]
        o_st = lax.dot_general(q_chk[rows], st.astype(BF16), (((1,), (1,)), ((), ())),
                               preferred_element_type=F32)
        upd = lax.dot_general(vc, k_chk[rows], (((0,), (0,)), ((), ())),
                              preferred_element_type=F32)
        decay = jnp.exp(tot[GLA_CHUNK][lo:lo + 1, :])
        st_ref[...] = st * decay + jnp.where(head_vk, upd, 0.0)
        outs.append(o_in + o_st)

    o = jnp.concatenate(outs, axis=0)

    sq = o * o
    sq_hi = sq.astype(BF16)
    sq_lo = (sq - sq_hi.astype(F32)).astype(BF16)
    ms = (jnp.dot(sq_hi, head_vv, preferred_element_type=F32)
          + jnp.dot(sq_lo, head_vv, preferred_element_type=F32)) * (1.0 / B_HEAD_V)
    og = og_ref[0]
    o_ref[0] = (o * lax.rsqrt(ms + NORM_EPS) * ng_ref[...]) * (og * jax.nn.sigmoid(og))


def _gla(bq, bk, bv, gl, og, norm_g):
    bsz, seq, _ = bq.shape
    tm = min(512, seq)
    row = lambda n: pl.BlockSpec((1, tm, n), lambda b, i: (b, i, 0))
    return pl.pallas_call(
        functools.partial(_gla_kernel, tm=tm),
        grid=(bsz, seq // tm),
        in_specs=[row(B_KEY_WIDTH), row(B_KEY_WIDTH), row(B_VAL_WIDTH), row(B_KEY_WIDTH),
                  row(B_VAL_WIDTH), _resident((1, B_VAL_WIDTH))],
        out_specs=row(B_VAL_WIDTH),
        out_shape=jax.ShapeDtypeStruct((bsz, seq, B_VAL_WIDTH), F32),
        scratch_shapes=[pltpu.VMEM((B_VAL_WIDTH, B_KEY_WIDTH), F32)],
        compiler_params=_params(("arbitrary", "arbitrary")),
        name="gla",
    )(bq, bk, bv, gl, og, norm_g.reshape(1, B_VAL_WIDTH))


def _ffn_kernel(*refs, tm, bn, dff, final_norm, mixer):
    if mixer:
        x_ref, oa_ref, ob_ref, wa_ref, wb_ref = refs[:5]
        refs = refs[5:]
    else:
        x_ref, refs = refs[0], refs[1:]
    g_ref, wup_ref, cw_ref, cb_ref, wdn_ref, fg_ref, o_ref, halo_ref, ext_ref = refs
    nkb = dff // bn

    @pl.when(pl.program_id(1) == 0)
    def _():
        halo_ref[...] = jnp.zeros_like(halo_ref)

    x = x_ref[0]
    if mixer:
        o_ref[0] = (jnp.dot(oa_ref[0].astype(BF16), wa_ref[...], preferred_element_type=F32)
                    + jnp.dot(ob_ref[0].astype(BF16), wb_ref[...], preferred_element_type=F32))
        x = x + o_ref[0]
    hn = _rms(x, g_ref[...]).astype(BF16)

    def up_stage(kb):
        for part in range(2):
            c0 = part * dff + kb * bn
            up = jnp.dot(hn, wup_ref[:, c0:c0 + bn], preferred_element_type=F32)
            ext_ref[kb % 2, part, 0:SUBLANES, :] = halo_ref[:, c0:c0 + bn]
            ext_ref[kb % 2, part, SUBLANES:, :] = up
            halo_ref[:, c0:c0 + bn] = up[tm - SUBLANES:, :]

    def gate_stage(kb):
        halves = []
        for part in range(2):
            c0 = part * dff + kb * bn
            y = cb_ref[:, c0:c0 + bn]
            for i in range(FFN_CONV):
                off = SUBLANES - (FFN_CONV - 1) + i
                y = y + ext_ref[kb % 2, part, pl.ds(off, tm), :] * cw_ref[i:i + 1, c0:c0 + bn]
            halves.append(y)
        return (halves[0] * _gelu_tanh(halves[1])).astype(BF16)

    up_stage(0)
    if nkb > 1:
        up_stage(1)
    a = gate_stage(0)
    acc = x
    for kb in range(nkb):
        if kb + 2 < nkb:
            up_stage(kb + 2)
        a_next = gate_stage(kb + 1) if kb + 1 < nkb else None
        acc = acc + jnp.dot(a, wdn_ref[kb * bn:(kb + 1) * bn, :], preferred_element_type=F32)
        a = a_next
    if final_norm:
        acc = _rms(acc, fg_ref[...])
    o_ref[0] = acc


def _conv_ffn(x, g, w_up, conv_w, conv_b, w_down, final_g, final_norm, mixer=None):
    bsz, seq, d = x.shape
    dff = w_down.shape[0]
    tm = min(512, seq)
    bn = 1536
    rows = lambda n: pl.BlockSpec((1, tm, n), lambda b, i: (b, i, 0))
    row = rows(d)
    lead_specs, lead_args = [row], [x]
    if mixer is not None:
        o_a, o_b, w_out = mixer
        w = w_out.astype(BF16)
        lead_specs += [rows(A_WIDTH), rows(B_VAL_WIDTH),
                       _resident((A_WIDTH, d)), _resident((B_VAL_WIDTH, d))]
        lead_args += [o_a, o_b, w[:A_WIDTH], w[A_WIDTH:]]
    return pl.pallas_call(
        functools.partial(_ffn_kernel, tm=tm, bn=bn, dff=dff, final_norm=final_norm,
                          mixer=mixer is not None),
        grid=(bsz, seq // tm),
        in_specs=lead_specs + [_resident((1, d)), _resident((d, 2 * dff)),
                               _resident((FFN_CONV, 2 * dff)), _resident((1, 2 * dff)),
                               _resident((dff, d)), _resident((1, d))],
        out_specs=row,
        out_shape=jax.ShapeDtypeStruct((bsz, seq, d), F32),
        scratch_shapes=[pltpu.VMEM((SUBLANES, 2 * dff), F32),
                        pltpu.VMEM((2, 2, tm + SUBLANES, bn), F32)],
        compiler_params=_params(("arbitrary", "arbitrary")),
        name="conv_ffn_final" if final_norm else "conv_ffn",
    )(*lead_args, g.reshape(1, d), w_up.astype(BF16), conv_w, conv_b.reshape(1, 2 * dff),
      w_down.astype(BF16), final_g.reshape(1, d))


def _rglru_kernel(x_ref, g_ref, win_ref, cw_ref, cb_ref, wa_ref, ba_ref, wx_ref, bx_ref,
                  lam_ref, wout_ref, o_ref, ext_ref, halo_ref, hlast_ref, *, tm, drnn, nseq):
    seg = tm // SUBLANES
    shifts = RG_CONV - 1
    bw = drnn // RG_BLOCKS
    seqs = range(nseq)

    @pl.when(pl.program_id(1) == 0)
    def _():
        halo_ref[...] = jnp.zeros_like(halo_ref)
        hlast_ref[...] = jnp.zeros_like(hlast_ref)

    prow = lax.broadcasted_iota(jnp.int32, (tm, tm), 0)
    pcol = lax.broadcasted_iota(jnp.int32, (tm, tm), 1)
    perm = (pcol == (prow % SUBLANES) * seg + prow // SUBLANES).astype(BF16)
    unperm = (prow == (pcol % SUBLANES) * seg + pcol // SUBLANES).astype(BF16)
    first_sub = lax.broadcasted_iota(jnp.int32, (SUBLANES, 1), 0) == 0

    xs = [x_ref[b] for b in seqs]
    hn = [jnp.dot(perm, _rms(xs[b], g_ref[...]).astype(BF16),
                  preferred_element_type=F32).astype(BF16) for b in seqs]
    gate = [jnp.dot(hn[b], win_ref[:, 0:drnn], preferred_element_type=F32) for b in seqs]
    xr_raw = [jnp.dot(hn[b], win_ref[:, drnn:2 * drnn], preferred_element_type=F32) for b in seqs]

    xr = []
    for b in seqs:
        for r in range(shifts):
            lo = (seg - shifts + r) * SUBLANES
            cur = xr_raw[b][lo:lo + SUBLANES]
            ext_ref[b, r * SUBLANES:(r + 1) * SUBLANES, :] = jnp.where(
                first_sub, pltpu.roll(halo_ref[b, r], 1, 0), pltpu.roll(cur, 1, 0))
            halo_ref[b, r] = cur
        ext_ref[b, shifts * SUBLANES:, :] = xr_raw[b]
        acc = cb_ref[...]
        for i in range(RG_CONV):
            acc = acc + ext_ref[b, i * SUBLANES:i * SUBLANES + tm, :] * cw_ref[i:i + 1, :]
        xr.append(acc)

    ys = []
    for b in seqs:
        xb = xr[b].astype(BF16)
        ra, ri = [], []
        for n in range(RG_BLOCKS):
            xn = xb[:, n * bw:(n + 1) * bw]
            ra.append(jnp.dot(xn, wa_ref[n], preferred_element_type=F32))
            ri.append(jnp.dot(xn, wx_ref[n], preferred_element_type=F32))
        r = jax.nn.sigmoid(jnp.concatenate(ra, axis=1) + ba_ref[...])
        ig = jax.nn.sigmoid(jnp.concatenate(ri, axis=1) + bx_ref[...])
        log_a = r * lam_ref[...]
        a = jnp.exp(log_a)
        th = jnp.tanh(log_a)
        one_minus_a2 = -2.0 * th / (1.0 - th)
        u = jnp.sqrt(one_minus_a2) * (ig * xr[b])

        h_loc, a_cum = [u[0:SUBLANES]], [a[0:SUBLANES]]
        for j in range(1, seg):
            aj = a[j * SUBLANES:(j + 1) * SUBLANES]
            h_loc.append(aj * h_loc[-1] + u[j * SUBLANES:(j + 1) * SUBLANES])
            a_cum.append(aj * a_cum[-1])
        h_in = [hlast_ref[b]]
        for s in range(SUBLANES):
            h_in.append(h_loc[-1][s:s + 1] + a_cum[-1][s:s + 1] * h_in[-1])
        hlast_ref[b] = h_in[SUBLANES]
        h_seg = jnp.concatenate(h_in[:SUBLANES], axis=0)
        h = jnp.concatenate([h_loc[j] + a_cum[j] * h_seg for j in range(seg)], axis=0)
        ys.append((h * _gelu_tanh(gate[b])).astype(BF16))

    for b in seqs:
        y_t = jnp.dot(unperm, ys[b], preferred_element_type=F32).astype(BF16)
        o_ref[b] = xs[b] + jnp.dot(y_t, wout_ref[...], preferred_element_type=F32)


def _rglru(x, g, w_in, conv_w, conv_b, w_a, b_a, w_x, b_x, lam, w_out):
    bsz, seq, d = x.shape
    drnn = w_out.shape[0]
    bw = drnn // RG_BLOCKS
    tm = min(256, seq)
    nseq = 2 if bsz % 2 == 0 else 1
    row = pl.BlockSpec((nseq, tm, d), lambda b, i: (b, i, 0))
    lam_c = (-LRU_C) * jax.nn.softplus(-lam.astype(F32))
    return pl.pallas_call(
        functools.partial(_rglru_kernel, tm=tm, drnn=drnn, nseq=nseq),
        grid=(bsz // nseq, seq // tm),
        in_specs=[row, _resident((1, d)), _resident((d, 2 * drnn)), _resident((RG_CONV, drnn)),
                  _resident((1, drnn)), _resident((RG_BLOCKS, bw, bw)), _resident((1, drnn)),
                  _resident((RG_BLOCKS, bw, bw)), _resident((1, drnn)), _resident((1, drnn)),
                  _resident((drnn, d))],
        out_specs=row,
        out_shape=jax.ShapeDtypeStruct((bsz, seq, d), F32),
        scratch_shapes=[pltpu.VMEM((nseq, tm + (RG_CONV - 1) * SUBLANES, drnn), F32),
                        pltpu.VMEM((nseq, RG_CONV - 1, SUBLANES, drnn), F32),
                        pltpu.VMEM((nseq, 1, drnn), F32)],
        compiler_params=_params(("arbitrary", "arbitrary")),
        name="rglru",
    )(x, g.reshape(1, d), w_in.astype(BF16), conv_w, conv_b.reshape(1, drnn),
      w_a.astype(BF16), b_a.reshape(1, drnn), w_x.astype(BF16), b_x.reshape(1, drnn),
      lam_c.reshape(1, drnn), w_out.astype(BF16))


def kernel(x, mix_norm_g, ffn_norm_g, final_norm_g, ev_w_in, ev_w_gk2, ev_b_gk2, ev_gla_norm_g,
           ev_w_out, od_w_in, od_conv_w, od_conv_b, od_w_a, od_b_a, od_w_x, od_b_x, od_lambda,
           od_w_out, ffn_w_up, ffn_conv_w, ffn_conv_b, ffn_w_down):
    depth = mix_norm_g.shape[0]
    h = x
    for l in range(depth):
        j = l // 2
        mixer = None
        if l % 2 == 0:
            q, k, v, bias, bq, bk, bv, gl, og = _front0(h, mix_norm_g[l], ev_w_in[j], ev_w_gk2[j],
                                                      ev_b_gk2[j])
            o_a = _moba(q, k, v, bias)
            o_b = _gla(bq, bk, bv, gl, og, ev_gla_norm_g[j])
            mixer = (o_a, o_b, ev_w_out[j])
        else:
            h = _rglru(h, mix_norm_g[l], od_w_in[j], od_conv_w[j], od_conv_b[j], od_w_a[j],
                       od_b_a[j], od_w_x[j], od_b_x[j], od_lambda[j], od_w_out[j])
        h = _conv_ffn(h, ffn_norm_g[l], ffn_w_up[l], ffn_conv_w[l], ffn_conv_b[l], ffn_w_down[l],
                      final_norm_g, final_norm=(l == depth - 1), mixer=mixer)
    return h
```

```python
import functools

import jax
import jax.numpy as jnp
from jax import lax
from jax.experimental import pallas as pl
from jax.experimental.pallas import tpu as pltpu

F32 = jnp.float32
BF16 = jnp.bfloat16

A_HEADS = 8
A_HEAD_DIM = 64
A_WIDTH = A_HEADS * A_HEAD_DIM
MOBA_BLOCK = 256
MOBA_TOPK = 3
ROPE_THETA = 10000.0
B_HEADS = 8
B_HEAD_K = 32
B_HEAD_V = 64
B_KEY_WIDTH = B_HEADS * B_HEAD_K
B_VAL_WIDTH = B_HEADS * B_HEAD_V
GLA_GATE_RANK = 16
GLA_GATE_NORM = 16.0
RG_BLOCKS = 4
RG_CONV = 4
LRU_C = 8.0
FFN_CONV = 3
NORM_EPS = 1e-6
MASK_VALUE = -1e30

LANES = 128
SUBLANES = 8
BF16_ROWS = 16
LOG2_E = 1.4426950408889634
VMEM_LIMIT = 56 * 1024 * 1024

GLA_CHUNK = 64
MOBA_BLOCKS_PAD = LANES
V_ROWS = A_HEAD_DIM + BF16_ROWS


def _params(sem):
    return pltpu.CompilerParams(dimension_semantics=sem, vmem_limit_bytes=VMEM_LIMIT)


def _rms(x, g):
    ms = jnp.mean(x * x, axis=-1, keepdims=True)
    return x * lax.rsqrt(ms + NORM_EPS) * g


def _gelu_tanh(x):
    return 0.5 * x * (1.0 + jnp.tanh(0.7978845608028654 * (x + 0.044715 * (x * x * x))))


def _log_sigmoid(z):
    return jnp.minimum(z, 0.0) - jnp.log1p(jnp.exp(-jnp.abs(z)))


def _moba_gate_rows(nb):
    assert nb <= MOBA_BLOCKS_PAD
    return -(-nb // BF16_ROWS) * BF16_ROWS


def _resident(shape):
    nd = len(shape)
    return pl.BlockSpec(shape, lambda *_: (0,) * nd, pipeline_mode=pl.Buffered(1))


def _front0_kernel(x_ref, g_ref, w_ref, wt_ref, wgk_ref, bgk_ref, cos_ref, sa_ref, sb_ref,
                   cost_ref, sat_ref, sbt_ref,
                   q_ref, k_ref, v_ref, bias_ref, bq_ref, bk_ref, bv_ref, gl_ref, og_ref,
                   km_ref, *, tm):
    @pl.when(pl.program_id(1) == 0)
    def _():
        km_ref[...] = jnp.zeros_like(km_ref)

    hn = _rms(x_ref[0], g_ref[...]).astype(BF16)

    def proj(c0, n):
        return jnp.dot(hn, w_ref[:, c0:c0 + n], preferred_element_type=F32)

    def proj_t(r0, n):
        return lax.dot_general(wt_ref[r0:r0 + n, :], hn, (((1,), (1,)), ((), ())),
                               preferred_element_type=F32)

    def rot(t, cos, sa, sb, axis):
        parts = []
        for c in range(t.shape[axis] // LANES):
            tc = lax.slice_in_dim(t, c * LANES, (c + 1) * LANES, axis=axis)
            parts.append(tc * cos + pltpu.roll(tc, LANES - 32, axis) * sa
                         + pltpu.roll(tc, 32, axis) * sb)
        return jnp.concatenate(parts, axis=axis)

    qt = rot(proj_t(0, A_WIDTH), cost_ref[...], sat_ref[...], sbt_ref[...], 0)
    q_ref[0] = (qt * (A_HEAD_DIM ** -0.5 * LOG2_E)).astype(BF16)
    vt = proj_t(A_WIDTH, A_WIDTH)
    c = 0
    k = rot(proj(c, A_WIDTH), cos_ref[...], sa_ref[...], sb_ref[...], 1); c += A_WIDTH
    k_ref[0] = k.astype(BF16)
    nblk = tm // MOBA_BLOCK
    kmean = jnp.mean(k.reshape(nblk, MOBA_BLOCK, A_WIDTH), axis=1)
    blk0 = pl.program_id(1) * nblk
    for n in range(nblk):
        km_ref[pl.ds(blk0 + n, 1), :] = kmean[n:n + 1]

    nbr = km_ref.shape[0]
    feat = lax.broadcasted_iota(jnp.int32, (LANES, 1), 0)
    brow = lax.broadcasted_iota(jnp.int32, (nbr, 2 * tm), 0)
    browf = brow.astype(F32)
    q_blk = blk0 + (lax.broadcasted_iota(jnp.int32, (1, 2 * tm), 1) % tm) // MOBA_BLOCK

    def split(a):
        hi = a.astype(BF16)
        return hi, (a - hi.astype(F32)).astype(BF16)

    def gate_pair(hp):
        q_pair = qt[hp * LANES:(hp + 1) * LANES]
        q2 = jnp.concatenate([jnp.where(feat // A_HEAD_DIM == h, q_pair, 0.0) for h in range(2)],
                             axis=1)
        km_hi, km_lo = split(km_ref[:, hp * LANES:(hp + 1) * LANES])
        q_hi, q_lo = split(q2)
        gate = jnp.dot(jnp.concatenate([km_hi, km_hi, km_lo], axis=1),
                       jnp.concatenate([q_hi, q_lo, q_hi], axis=0), preferred_element_type=F32)
        g = jnp.where(brow < q_blk, gate, -jnp.inf)
        bias = jnp.full((nbr, 2 * tm), MASK_VALUE, F32)
        for _ in range(MOBA_TOPK):
            mx = jnp.max(g, axis=0, keepdims=True)
            first = jnp.min(jnp.where(g == mx, browf, float(nbr)), axis=0, keepdims=True)
            hit = browf == first
            bias = jnp.where(hit & (mx > -jnp.inf), 0.0, bias)
            g = jnp.where(hit, -jnp.inf, g)
        for h in range(2):
            bias_ref[0, 2 * hp + h] = bias[:, h * tm:(h + 1) * tm].astype(BF16)

    ones_rows = (lax.broadcasted_iota(jnp.int32, (V_ROWS - A_HEAD_DIM, MOBA_BLOCK), 0)
                 == 0).astype(BF16)
    for h in range(A_HEADS):
        for n in range(nblk):
            v_ref[0, h, n, 0:A_HEAD_DIM, :] = vt[h * A_HEAD_DIM:(h + 1) * A_HEAD_DIM,
                                                 n * MOBA_BLOCK:(n + 1) * MOBA_BLOCK].astype(BF16)
            v_ref[0, h, n, A_HEAD_DIM:, :] = ones_rows
    gate_pair(0)
    bq_ref[0] = proj(c, B_KEY_WIDTH) * (B_HEAD_K ** -0.5); c += B_KEY_WIDTH
    bk_ref[0] = proj(c, B_KEY_WIDTH); c += B_KEY_WIDTH
    gate_pair(1)
    bv_ref[0] = proj(c, B_VAL_WIDTH); c += B_VAL_WIDTH
    gate_pair(2)
    og_ref[0] = proj(c, B_VAL_WIDTH); c += B_VAL_WIDTH
    gate_pair(3)
    gk = proj(c, LANES).astype(BF16)
    z = jnp.dot(gk, wgk_ref[...], preferred_element_type=F32) + bgk_ref[...]
    gl_ref[0] = _log_sigmoid(z) / GLA_GATE_NORM


def _front0(x, g, w_in, w_gk2, b_gk2):
    bsz, seq, d = x.shape
    tm = min(1024, seq)
    widths = [A_WIDTH, A_WIDTH, A_WIDTH, B_KEY_WIDTH, B_KEY_WIDTH, B_VAL_WIDTH, GLA_GATE_RANK]
    aq, ak, av, bq, bk, bv, bgk, bog = jnp.split(
        w_in, [sum(widths[:n + 1]) for n in range(len(widths))], axis=1)
    gk_pad = LANES - GLA_GATE_RANK
    w = jnp.concatenate([ak, bq, bk, bv, bog, jnp.pad(bgk, ((0, 0), (0, gk_pad)))],
                        axis=1).astype(BF16)
    wt = jnp.concatenate([aq, av], axis=1).T.astype(BF16)
    wgk = jnp.pad(w_gk2, ((0, gk_pad), (0, 0))).astype(BF16)
    ncol = w.shape[1]

    half = A_HEAD_DIM // 2
    inv = ROPE_THETA ** (-jnp.arange(half, dtype=F32) / half)
    pos = jnp.arange(seq, dtype=jnp.int32).astype(F32)
    first = (jnp.arange(LANES) % A_HEAD_DIM) < half

    def tables(feature_major):
        if feature_major:
            ang, reps, sel = inv[:, None] * pos[None, :], (LANES // half, 1), first[:, None]
        else:
            ang, reps, sel = pos[:, None] * inv[None, :], (1, LANES // half), first[None, :]
        cos, sin = jnp.tile(jnp.cos(ang), reps), jnp.tile(jnp.sin(ang), reps)
        return cos, jnp.where(sel, -sin, 0.0), jnp.where(sel, 0.0, sin)

    nb = seq // MOBA_BLOCK
    nbr = _moba_gate_rows(nb)
    row = lambda n: pl.BlockSpec((1, tm, n), lambda b, i: (b, i, 0))
    tab = pl.BlockSpec((tm, LANES), lambda b, i: (i, 0))
    tab_t = pl.BlockSpec((LANES, tm), lambda b, i: (0, i))
    nt = seq // tm
    outs = pl.pallas_call(
        functools.partial(_front0_kernel, tm=tm),
        grid=(bsz, nt),
        in_specs=[row(d), _resident((1, d)), _resident((d, ncol)), _resident((2 * A_WIDTH, d)),
                  _resident((LANES, B_KEY_WIDTH)), _resident((1, B_KEY_WIDTH)),
                  tab, tab, tab, tab_t, tab_t, tab_t],
        out_specs=[pl.BlockSpec((1, A_WIDTH, tm), lambda b, i: (b, 0, i)), row(A_WIDTH),
                   pl.BlockSpec((1, A_HEADS, tm // MOBA_BLOCK, V_ROWS, MOBA_BLOCK),
                                lambda b, i: (b, 0, i, 0, 0)),
                   pl.BlockSpec((1, A_HEADS, nbr, tm), lambda b, i: (b, 0, 0, i)),
                   row(B_KEY_WIDTH), row(B_KEY_WIDTH), row(B_VAL_WIDTH), row(B_KEY_WIDTH),
                   row(B_VAL_WIDTH)],
        out_shape=[jax.ShapeDtypeStruct((bsz, A_WIDTH, seq), BF16),
                   jax.ShapeDtypeStruct((bsz, seq, A_WIDTH), BF16),
                   jax.ShapeDtypeStruct((bsz, A_HEADS, nb, V_ROWS, MOBA_BLOCK), BF16),
                   jax.ShapeDtypeStruct((bsz, A_HEADS, nbr, seq), BF16),
                   jax.ShapeDtypeStruct((bsz, seq, B_KEY_WIDTH), F32),
                   jax.ShapeDtypeStruct((bsz, seq, B_KEY_WIDTH), F32),
                   jax.ShapeDtypeStruct((bsz, seq, B_VAL_WIDTH), F32),
                   jax.ShapeDtypeStruct((bsz, seq, B_KEY_WIDTH), F32),
                   jax.ShapeDtypeStruct((bsz, seq, B_VAL_WIDTH), F32)],
        scratch_shapes=[pltpu.VMEM((nbr, A_WIDTH), F32)],
        compiler_params=_params(("arbitrary", "arbitrary")),
        name="front0",
    )(x, g.reshape(1, d), w, wt, wgk, b_gk2.reshape(1, B_KEY_WIDTH), *tables(False), *tables(True))
    return outs


def _moba_kernel(qt_ref, k_ref, vt_ref, bias_ref, o_ref, rhs_ref, acc_ref, sa_ref, sb_ref, *,
                 group, nb, qn):
    blk = MOBA_BLOCK
    tq = qn * blk
    q_first = pl.program_id(2) * qn
    q_last = q_first + qn - 1
    qt = qt_ref[0]
    feat = lax.broadcasted_iota(jnp.int32, (LANES, 1), 0)
    nbr = bias_ref.shape[2]
    for h in range(2):
        rhs_ref[:, h * tq:(h + 1) * tq] = jnp.concatenate(
            [jnp.where(feat // A_HEAD_DIM == h, qt, jnp.zeros_like(qt)), bias_ref[0, h],
             jnp.full((MOBA_BLOCKS_PAD - nbr, tq), MASK_VALUE, BF16)], axis=0)

    keys = group * blk
    blk_of_row = lax.broadcasted_iota(jnp.int32, (keys, MOBA_BLOCKS_PAD), 0) // blk
    lane_blk = lax.broadcasted_iota(jnp.int32, (keys, MOBA_BLOCKS_PAD), 1)

    def scores(g, s_ref):
        j0 = g * group
        off = pl.multiple_of(jnp.minimum(j0, nb - group) * blk, blk)
        onehot = (lane_blk == blk_of_row + j0).astype(BF16)
        lhs_j = jnp.concatenate([k_ref[0, pl.ds(off, keys), :], onehot], axis=1)
        s = jnp.dot(lhs_j, rhs_ref[...], preferred_element_type=F32)
        s_ref[...] = s
        return jnp.max(s, axis=0, keepdims=True)

    def accumulate(g, s_ref, m, s_max):
        j0 = g * group
        m_new = jnp.maximum(m, s_max)
        alpha = jnp.exp2(m - m_new)
        p = jnp.exp2(s_ref[...] - m_new).astype(BF16)
        for h in range(2):
            vt = jnp.concatenate([vt_ref[0, h, jnp.minimum(j0 + u, nb - 1)] for u in range(group)],
                                 axis=1)
            acc_ref[h] = (alpha[:, h * tq:(h + 1) * tq] * acc_ref[h]
                          + jnp.dot(vt, p[:, h * tq:(h + 1) * tq], preferred_element_type=F32))
        return m_new

    def body(i, carry):
        m, max_a = carry
        max_b = scores(2 * i + 1, sb_ref)
        m = accumulate(2 * i, sa_ref, m, max_a)
        max_a = scores(2 * i + 2, sa_ref)
        m = accumulate(2 * i + 1, sb_ref, m, max_b)
        return m, max_a

    max_0 = scores(0, sa_ref)

    own = pl.multiple_of(q_first * blk, blk)
    lhs = jnp.concatenate([k_ref[0, pl.ds(own, tq), :], jnp.zeros((tq, MOBA_BLOCKS_PAD), BF16)],
                          axis=1)
    key_pos = lax.broadcasted_iota(jnp.int32, (tq, 2 * tq), 0)
    qry_pos = lax.broadcasted_iota(jnp.int32, (tq, 2 * tq), 1) % tq
    causal = (key_pos <= qry_pos) & (key_pos // blk == qry_pos // blk)
    s = jnp.where(causal, jnp.dot(lhs, rhs_ref[...], preferred_element_type=F32), MASK_VALUE)
    m0 = jnp.max(s, axis=0, keepdims=True)
    p = jnp.exp2(s - m0).astype(BF16)
    for h in range(2):
        vt_own = jnp.concatenate([vt_ref[0, h, q_first + u] for u in range(qn)], axis=1)
        acc_ref[h] = jnp.dot(vt_own, p[:, h * tq:(h + 1) * tq], preferred_element_type=F32)

    n_groups = (q_last + group - 1) // group
    n_pairs = jnp.maximum(n_groups - 1, 0) // 2

    def two_pairs(i, carry):
        return body(2 * i + 1, body(2 * i, carry))

    carry = lax.fori_loop(0, n_pairs // 2, two_pairs, (m0, max_0))
    m, max_a = lax.cond(n_pairs % 2 == 1, lambda c: body(n_pairs - 1, c), lambda c: c, carry)
    left = n_groups - 2 * n_pairs

    @pl.when(left == 1)
    def _():
        accumulate(n_groups - 1, sa_ref, m, max_a)

    @pl.when(left == 2)
    def _():
        max_b = scores(n_groups - 1, sb_ref)
        m_mid = accumulate(n_groups - 2, sa_ref, m, max_a)
        accumulate(n_groups - 1, sb_ref, m_mid, max_b)

    outs = []
    for h in range(2):
        a = acc_ref[h]
        outs.append(a[0:A_HEAD_DIM] / a[A_HEAD_DIM:A_HEAD_DIM + 1])
    o_ref[0] = jnp.concatenate(outs, axis=0).T


def _moba(qt, k, vt, bias):
    bsz, seq, _ = k.shape
    nb = seq // MOBA_BLOCK
    nbr = _moba_gate_rows(nb)
    npair = A_WIDTH // LANES
    group = 2 if nb % 2 == 0 else 1
    qn = 2 if nb % 2 == 0 else 1
    tq = qn * MOBA_BLOCK
    return pl.pallas_call(
        functools.partial(_moba_kernel, group=group, nb=nb, qn=qn),
        grid=(bsz, npair, nb // qn),
        in_specs=[pl.BlockSpec((1, LANES, tq), lambda b, hp, i: (b, hp, i)),
                  pl.BlockSpec((1, seq, LANES), lambda b, hp, i: (b, 0, hp)),
                  pl.BlockSpec((1, 2, nb, V_ROWS, MOBA_BLOCK), lambda b, hp, i: (b, hp, 0, 0, 0)),
                  pl.BlockSpec((1, 2, nbr, tq), lambda b, hp, i: (b, hp, 0, i))],
        out_specs=pl.BlockSpec((1, tq, LANES), lambda b, hp, i: (b, i, hp)),
        out_shape=jax.ShapeDtypeStruct((bsz, seq, A_WIDTH), F32),
        scratch_shapes=[pltpu.VMEM((2 * LANES, 2 * tq), BF16),
                        pltpu.VMEM((2, V_ROWS, tq), F32),
                        pltpu.VMEM((group * MOBA_BLOCK, 2 * tq), F32),
                        pltpu.VMEM((group * MOBA_BLOCK, 2 * tq), F32)],
        compiler_params=_params(("arbitrary", "arbitrary", "arbitrary")),
        name="moba",
    )(qt, k, vt, bias)


def _gla_kernel(q_ref, k_ref, v_ref, g_ref, og_ref, ng_ref, o_ref, st_ref, *, tm):
    @pl.when(pl.program_id(1) == 0)
    def _():
        st_ref[...] = jnp.zeros_like(st_ref)

    q, k, v, g = q_ref[0], k_ref[0], v_ref[0], g_ref[0]
    kw, vw = B_KEY_WIDTH, B_VAL_WIDTH
    row = lax.broadcasted_iota(jnp.int32, (tm, 1), 0)

    pre, post, tot, upper = {1: g}, {1: jnp.zeros_like(g)}, {1: g}, {}
    n = 1
    while n < GLA_CHUNK:
        upper[n] = (row % (2 * n)) >= n
        below = pltpu.roll(tot[n], n, 0)
        above = pltpu.roll(tot[n], tm - n, 0)
        pre[2 * n] = pre[n] + jnp.where(upper[n], below, 0.0)
        post[2 * n] = post[n] + jnp.where(upper[n], 0.0, above)
        tot[2 * n] = tot[n] + jnp.where(upper[n], below, above)
        n *= 2
    levels = sorted(upper)

    q_lvl = {n: jnp.where(upper[n], q * jnp.exp(pre[n]), 0.0).astype(BF16) for n in levels}
    k_lvl = {n: jnp.where(upper[n], 0.0, k * jnp.exp(post[n])).astype(BF16) for n in levels}
    q_bf, k_bf = q.astype(BF16), k.astype(BF16)
    q_chk = (q * jnp.exp(pre[GLA_CHUNK])).astype(BF16)
    k_chk = (k * jnp.exp(post[GLA_CHUNK])).astype(BF16)

    head_vk = (lax.broadcasted_iota(jnp.int32, (vw, kw), 0) // B_HEAD_V
               == lax.broadcasted_iota(jnp.int32, (vw, kw), 1) // B_HEAD_K)
    head_vv = (lax.broadcasted_iota(jnp.int32, (vw, vw), 0) // B_HEAD_V
               == lax.broadcasted_iota(jnp.int32, (vw, vw), 1) // B_HEAD_V).astype(BF16)

    tt = lax.broadcasted_iota(jnp.int32, (B_HEADS * GLA_CHUNK, GLA_CHUNK), 0) % GLA_CHUNK
    ss = lax.broadcasted_iota(jnp.int32, (B_HEADS * GLA_CHUNK, GLA_CHUNK), 1)
    same_row = tt == ss
    splits_at = {n: (tt // (2 * n) == ss // (2 * n)) & (tt // n != ss // n) for n in levels}
    lane_k = lax.broadcasted_iota(jnp.int32, (1, kw), 1)
    lane_v = lax.broadcasted_iota(jnp.int32, (1, vw), 1)

    def stack_heads(x):
        return jnp.concatenate([jnp.where(lane_k // B_HEAD_K == h, x, jnp.zeros_like(x))
                                for h in range(B_HEADS)], axis=0)

    def scores(lhs, rhs):
        return lax.dot_general(stack_heads(lhs), rhs, (((1,), (1,)), ((), ())),
                               preferred_element_type=F32)

    outs = []
    for c in range(tm // GLA_CHUNK):
        lo = c * GLA_CHUNK
        rows = slice(lo, lo + GLA_CHUNK)
        vc = v[rows].astype(BF16)

        attn = jnp.where(same_row, scores(q_bf[rows], k_bf[rows]), 0.0)
        for n in levels:
            attn = jnp.where(splits_at[n], scores(q_lvl[n][rows], k_lvl[n][rows]), attn)
        res = jnp.dot(attn.astype(BF16), vc, preferred_element_type=F32)
        o_in = jnp.zeros((GLA_CHUNK, vw), F32)
        for h in range(B_HEADS):
            o_in = o_in + jnp.where(lane_v // B_HEAD_V == h,
                                    res[h * GLA_CHUNK:(h + 1) * GLA_CHUNK], 0.0)

        st = st_ref[...]
        o_st = lax.dot_general(q_chk[rows], st.astype(BF16), (((1,), (1,)), ((), ())),
                               preferred_element_type=F32)
        upd = lax.dot_general(vc, k_chk[rows], (((0,), (0,)), ((), ())),
                              preferred_element_type=F32)
        decay = jnp.exp(tot[GLA_CHUNK][lo:lo + 1, :])
        st_ref[...] = st * decay + jnp.where(head_vk, upd, 0.0)
        outs.append(o_in + o_st)

    o = jnp.concatenate(outs, axis=0)

    sq = o * o
    sq_hi = sq.astype(BF16)
    sq_lo = (sq - sq_hi.astype(F32)).astype(BF16)
    ms = (jnp.dot(sq_hi, head_vv, preferred_element_type=F32)
          + jnp.dot(sq_lo, head_vv, preferred_element_type=F32)) * (1.0 / B_HEAD_V)
    og = og_ref[0]
    o_ref[0] = (o * lax.rsqrt(ms + NORM_EPS) * ng_ref[...]) * (og * jax.nn.sigmoid(og))


def _gla(bq, bk, bv, gl, og, norm_g):
    bsz, seq, _ = bq.shape
    tm = min(512, seq)
    row = lambda n: pl.BlockSpec((1, tm, n), lambda b, i: (b, i, 0))
    return pl.pallas_call(
        functools.partial(_gla_kernel, tm=tm),
        grid=(bsz, seq // tm),
        in_specs=[row(B_KEY_WIDTH), row(B_KEY_WIDTH), row(B_VAL_WIDTH), row(B_KEY_WIDTH),
                  row(B_VAL_WIDTH), _resident((1, B_VAL_WIDTH))],
        out_specs=row(B_VAL_WIDTH),
        out_shape=jax.ShapeDtypeStruct((bsz, seq, B_VAL_WIDTH), F32),
        scratch_shapes=[pltpu.VMEM((B_VAL_WIDTH, B_KEY_WIDTH), F32)],
        compiler_params=_params(("arbitrary", "arbitrary")),
        name="gla",
    )(bq, bk, bv, gl, og, norm_g.reshape(1, B_VAL_WIDTH))


def _ffn_kernel(*refs, tm, bn, dff, final_norm, mixer):
    if mixer:
        x_ref, oa_ref, ob_ref, wa_ref, wb_ref = refs[:5]
        refs = refs[5:]
    else:
        x_ref, refs = refs[0], refs[1:]
    g_ref, wup_ref, cw_ref, cb_ref, wdn_ref, fg_ref, o_ref, halo_ref, ext_ref = refs
    nkb = dff // bn

    @pl.when(pl.program_id(1) == 0)
    def _():
        halo_ref[...] = jnp.zeros_like(halo_ref)

    x = x_ref[0]
    if mixer:
        o_ref[0] = (jnp.dot(oa_ref[0].astype(BF16), wa_ref[...], preferred_element_type=F32)
                    + jnp.dot(ob_ref[0].astype(BF16), wb_ref[...], preferred_element_type=F32))
        x = x + o_ref[0]
    hn = _rms(x, g_ref[...]).astype(BF16)

    def up_stage(kb):
        for part in range(2):
            c0 = part * dff + kb * bn
            up = jnp.dot(hn, wup_ref[:, c0:c0 + bn], preferred_element_type=F32)
            ext_ref[kb % 2, part, 0:SUBLANES, :] = halo_ref[:, c0:c0 + bn]
            ext_ref[kb % 2, part, SUBLANES:, :] = up
            halo_ref[:, c0:c0 + bn] = up[tm - SUBLANES:, :]

    def gate_stage(kb):
        halves = []
        for part in range(2):
            c0 = part * dff + kb * bn
            y = cb_ref[:, c0:c0 + bn]
            for i in range(FFN_CONV):
                off = SUBLANES - (FFN_CONV - 1) + i
                y = y + ext_ref[kb % 2, part, pl.ds(off, tm), :] * cw_ref[i:i + 1, c0:c0 + bn]
            halves.append(y)
        return (halves[0] * _gelu_tanh(halves[1])).astype(BF16)

    up_stage(0)
    if nkb > 1:
        up_stage(1)
    a = gate_stage(0)
    acc = x
    for kb in range(nkb):
        if kb + 2 < nkb:
            up_stage(kb + 2)
        a_next = gate_stage(kb + 1) if kb + 1 < nkb else None
        acc = acc + jnp.dot(a, wdn_ref[kb * bn:(kb + 1) * bn, :], preferred_element_type=F32)
        a = a_next
    if final_norm:
        acc = _rms(acc, fg_ref[...])
    o_ref[0] = acc


def _conv_ffn(x, g, w_up, conv_w, conv_b, w_down, final_g, final_norm, mixer=None):
    bsz, seq, d = x.shape
    dff = w_down.shape[0]
    tm = min(512, seq)
    bn = 1536
    rows = lambda n: pl.BlockSpec((1, tm, n), lambda b, i: (b, i, 0))
    row = rows(d)
    lead_specs, lead_args = [row], [x]
    if mixer is not None:
        o_a, o_b, w_out = mixer
        w = w_out.astype(BF16)
        lead_specs += [rows(A_WIDTH), rows(B_VAL_WIDTH),
                       _resident((A_WIDTH, d)), _resident((B_VAL_WIDTH, d))]
        lead_args += [o_a, o_b, w[:A_WIDTH], w[A_WIDTH:]]
    return pl.pallas_call(
        functools.partial(_ffn_kernel, tm=tm, bn=bn, dff=dff, final_norm=final_norm,
                          mixer=mixer is not None),
        grid=(bsz, seq // tm),
        in_specs=lead_specs + [_resident((1, d)), _resident((d, 2 * dff)),
                               _resident((FFN_CONV, 2 * dff)), _resident((1, 2 * dff)),
                               _resident((dff, d)), _resident((1, d))],
        out_specs=row,
        out_shape=jax.ShapeDtypeStruct((bsz, seq, d), F32),
        scratch_shapes=[pltpu.VMEM((SUBLANES, 2 * dff), F32),
                        pltpu.VMEM((2, 2, tm + SUBLANES, bn), F32)],
        compiler_params=_params(("arbitrary", "arbitrary")),
        name="conv_ffn_final" if final_norm else "conv_ffn",
    )(*lead_args, g.reshape(1, d), w_up.astype(BF16), conv_w, conv_b.reshape(1, 2 * dff),
      w_down.astype(BF16), final_g.reshape(1, d))


def _rglru_kernel(x_ref, g_ref, win_ref, cw_ref, cb_ref, wa_ref, ba_ref, wx_ref, bx_ref,
                  lam_ref, wout_ref, o_ref, ext_ref, halo_ref, hlast_ref, *, tm, drnn, nseq):
    seg = tm // SUBLANES
    shifts = RG_CONV - 1
    bw = drnn // RG_BLOCKS
    seqs = range(nseq)

    @pl.when(pl.program_id(1) == 0)
    def _():
        halo_ref[...] = jnp.zeros_like(halo_ref)
        hlast_ref[...] = jnp.zeros_like(hlast_ref)

    prow = lax.broadcasted_iota(jnp.int32, (tm, tm), 0)
    pcol = lax.broadcasted_iota(jnp.int32, (tm, tm), 1)
    perm = (pcol == (prow % SUBLANES) * seg + prow // SUBLANES).astype(BF16)
    unperm = (prow == (pcol % SUBLANES) * seg + pcol // SUBLANES).astype(BF16)
    first_sub = lax.broadcasted_iota(jnp.int32, (SUBLANES, 1), 0) == 0

    xs = [x_ref[b] for b in seqs]
    hn = [jnp.dot(perm, _rms(xs[b], g_ref[...]).astype(BF16),
                  preferred_element_type=F32).astype(BF16) for b in seqs]
    gate = [jnp.dot(hn[b], win_ref[:, 0:drnn], preferred_element_type=F32) for b in seqs]
    xr_raw = [jnp.dot(hn[b], win_ref[:, drnn:2 * drnn], preferred_element_type=F32) for b in seqs]

    xr = []
    for b in seqs:
        for r in range(shifts):
            lo = (seg - shifts + r) * SUBLANES
            cur = xr_raw[b][lo:lo + SUBLANES]
            ext_ref[b, r * SUBLANES:(r + 1) * SUBLANES, :] = jnp.where(
                first_sub, pltpu.roll(halo_ref[b, r], 1, 0), pltpu.roll(cur, 1, 0))
            halo_ref[b, r] = cur
        ext_ref[b, shifts * SUBLANES:, :] = xr_raw[b]
        acc = cb_ref[...]
        for i in range(RG_CONV):
            acc = acc + ext_ref[b, i * SUBLANES:i * SUBLANES + tm, :] * cw_ref[i:i + 1, :]
        xr.append(acc)

    ys = []
    for b in seqs:
        xb = xr[b].astype(BF16)
        ra, ri = [], []
        for n in range(RG_BLOCKS):
            xn = xb[:, n * bw:(n + 1) * bw]
            ra.append(jnp.dot(xn, wa_ref[n], preferred_element_type=F32))
            ri.append(jnp.dot(xn, wx_ref[n], preferred_element_type=F32))
        r = jax.nn.sigmoid(jnp.concatenate(ra, axis=1) + ba_ref[...])
        ig = jax.nn.sigmoid(jnp.concatenate(ri, axis=1) + bx_ref[...])
        log_a = r * lam_ref[...]
        a = jnp.exp(log_a)
        th = jnp.tanh(log_a)
        one_minus_a2 = -2.0 * th / (1.0 - th)
        u = jnp.sqrt(one_minus_a2) * (ig * xr[b])

        h_loc, a_cum = [u[0:SUBLANES]], [a[0:SUBLANES]]
        for j in range(1, seg):
            aj = a[j * SUBLANES:(j + 1) * SUBLANES]
            h_loc.append(aj * h_loc[-1] + u[j * SUBLANES:(j + 1) * SUBLANES])
            a_cum.append(aj * a_cum[-1])
        h_in = [hlast_ref[b]]
        for s in range(SUBLANES):
            h_in.append(h_loc[-1][s:s + 1] + a_cum[-1][s:s + 1] * h_in[-1])
        hlast_ref[b] = h_in[SUBLANES]
        h_seg = jnp.concatenate(h_in[:SUBLANES], axis=0)
        h = jnp.concatenate([h_loc[j] + a_cum[j] * h_seg for j in range(seg)], axis=0)
        ys.append((h * _gelu_tanh(gate[b])).astype(BF16))

    for b in seqs:
        y_t = jnp.dot(unperm, ys[b], preferred_element_type=F32).astype(BF16)
        o_ref[b] = xs[b] + jnp.dot(y_t, wout_ref[...], preferred_element_type=F32)


def _rglru(x, g, w_in, conv_w, conv_b, w_a, b_a, w_x, b_x, lam, w_out):
    bsz, seq, d = x.shape
    drnn = w_out.shape[0]
    bw = drnn // RG_BLOCKS
    tm = min(256, seq)
    nseq = 2 if bsz % 2 == 0 else 1
    row = pl.BlockSpec((nseq, tm, d), lambda b, i: (b, i, 0))
    lam_c = (-LRU_C) * jax.nn.softplus(-lam.astype(F32))
    return pl.pallas_call(
        functools.partial(_rglru_kernel, tm=tm, drnn=drnn, nseq=nseq),
        grid=(bsz // nseq, seq // tm),
        in_specs=[row, _resident((1, d)), _resident((d, 2 * drnn)), _resident((RG_CONV, drnn)),
                  _resident((1, drnn)), _resident((RG_BLOCKS, bw, bw)), _resident((1, drnn)),
                  _resident((RG_BLOCKS, bw, bw)), _resident((1, drnn)), _resident((1, drnn)),
                  _resident((drnn, d))],
        out_specs=row,
        out_shape=jax.ShapeDtypeStruct((bsz, seq, d), F32),
        scratch_shapes=[pltpu.VMEM((nseq, tm + (RG_CONV - 1) * SUBLANES, drnn), F32),
                        pltpu.VMEM((nseq, RG_CONV - 1, SUBLANES, drnn), F32),
                        pltpu.VMEM((nseq, 1, drnn), F32)],
        compiler_params=_params(("arbitrary", "arbitrary")),
        name="rglru",
    )(x, g.reshape(1, d), w_in.astype(BF16), conv_w, conv_b.reshape(1, drnn),
      w_a.astype(BF16), b_a.reshape(1, drnn), w_x.astype(BF16), b_x.reshape(1, drnn),
      lam_c.reshape(1, drnn), w_out.astype(BF16))


def kernel(x, mix_norm_g, ffn_norm_g, final_norm_g, ev_w_in, ev_w_gk2, ev_b_gk2, ev_gla_norm_g,
           ev_w_out, od_w_in, od_conv_w, od_conv_b, od_w_a, od_b_a, od_w_x, od_b_x, od_lambda,
           od_w_out, ffn_w_up, ffn_conv_w, ffn_conv_b, ffn_w_down):
    depth = mix_norm_g.shape[0]
    h = x
    for l in range(depth):
        j = l // 2
        mixer = None
        if l % 2 == 0:
            q, k, v, bias, bq, bk, bv, gl, og = _front0(h, mix_norm_g[l], ev_w_in[j], ev_w_gk2[j],
                                                      ev_b_gk2[j])
            o_a = _moba(q, k, v, bias)
            o_b = _gla(bq, bk, bv, gl, og, ev_gla_norm_g[j])
            mixer = (o_a, o_b, ev_w_out[j])
        else:
            h = _rglru(h, mix_norm_g[l], od_w_in[j], od_conv_w[j], od_conv_b[j], od_w_a[j],
                       od_b_a[j], od_w_x[j], od_b_x[j], od_lambda[j], od_w_out[j])
        h = _conv_ffn(h, ffn_norm_g[l], ffn_w_up[l], ffn_conv_w[l], ffn_conv_b[l], ffn_w_down[l],
                      final_norm_g, final_norm=(l == depth - 1), mixer=mixer)
    return h
```

```python
import functools

import jax
import jax.numpy as jnp
from jax import lax
from jax.experimental import pallas as pl
from jax.experimental.pallas import tpu as pltpu

F32 = jnp.float32
BF16 = jnp.bfloat16

A_HEADS = 8
A_HEAD_DIM = 64
A_WIDTH = A_HEADS * A_HEAD_DIM
MOBA_BLOCK = 256
MOBA_TOPK = 3
ROPE_THETA = 10000.0
B_HEADS = 8
B_HEAD_K = 32
B_HEAD_V = 64
B_KEY_WIDTH = B_HEADS * B_HEAD_K
B_VAL_WIDTH = B_HEADS * B_HEAD_V
GLA_GATE_RANK = 16
GLA_GATE_NORM = 16.0
RG_BLOCKS = 4
RG_CONV = 4
LRU_C = 8.0
FFN_CONV = 3
NORM_EPS = 1e-6
MASK_VALUE = -1e30

LANES = 128
SUBLANES = 8
BF16_ROWS = 16
LOG2_E = 1.4426950408889634
VMEM_LIMIT = 56 * 1024 * 1024

GLA_CHUNK = 64
MOBA_BLOCKS_PAD = LANES
V_ROWS = A_HEAD_DIM + BF16_ROWS


def _params(sem):
    return pltpu.CompilerParams(dimension_semantics=sem, vmem_limit_bytes=VMEM_LIMIT)


def _rms(x, g):
    ms = jnp.mean(x * x, axis=-1, keepdims=True)
    return x * lax.rsqrt(ms + NORM_EPS) * g


def _gelu_tanh(x):
    return 0.5 * x * (1.0 + jnp.tanh(0.7978845608028654 * (x + 0.044715 * (x * x * x))))


def _log_sigmoid(z):
    return jnp.minimum(z, 0.0) - jnp.log1p(jnp.exp(-jnp.abs(z)))


def _moba_gate_rows(nb):
    assert nb <= MOBA_BLOCKS_PAD
    return -(-nb // BF16_ROWS) * BF16_ROWS


def _resident(shape):
    nd = len(shape)
    return pl.BlockSpec(shape, lambda *_: (0,) * nd, pipeline_mode=pl.Buffered(1))


def _front0_kernel(x_ref, g_ref, w_ref, wt_ref, wgk_ref, bgk_ref, cos_ref, sa_ref, sb_ref,
                   cost_ref, sat_ref, sbt_ref,
                   q_ref, k_ref, v_ref, bias_ref, bq_ref, bk_ref, bv_ref, gl_ref, og_ref,
                   km_ref, *, tm):
    @pl.when(pl.program_id(1) == 0)
    def _():
        km_ref[...] = jnp.zeros_like(km_ref)

    hn = _rms(x_ref[0], g_ref[...]).astype(BF16)

    def proj(c0, n):
        return jnp.dot(hn, w_ref[:, c0:c0 + n], preferred_element_type=F32)

    def proj_t(r0, n):
        return lax.dot_general(wt_ref[r0:r0 + n, :], hn, (((1,), (1,)), ((), ())),
                               preferred_element_type=F32)

    def rot(t, cos, sa, sb, axis):
        parts = []
        for c in range(t.shape[axis] // LANES):
            tc = lax.slice_in_dim(t, c * LANES, (c + 1) * LANES, axis=axis)
            parts.append(tc * cos + pltpu.roll(tc, LANES - 32, axis) * sa
                         + pltpu.roll(tc, 32, axis) * sb)
        return jnp.concatenate(parts, axis=axis)

    qt = rot(proj_t(0, A_WIDTH), cost_ref[...], sat_ref[...], sbt_ref[...], 0)
    q_ref[0] = (qt * (A_HEAD_DIM ** -0.5 * LOG2_E)).astype(BF16)
    vt = proj_t(A_WIDTH, A_WIDTH)
    c = 0
    k = rot(proj(c, A_WIDTH), cos_ref[...], sa_ref[...], sb_ref[...], 1); c += A_WIDTH
    k_ref[0] = k.astype(BF16)
    nblk = tm // MOBA_BLOCK
    kmean = jnp.mean(k.reshape(nblk, MOBA_BLOCK, A_WIDTH), axis=1)
    blk0 = pl.program_id(1) * nblk
    for n in range(nblk):
        km_ref[pl.ds(blk0 + n, 1), :] = kmean[n:n + 1]

    nbr = km_ref.shape[0]
    feat = lax.broadcasted_iota(jnp.int32, (LANES, 1), 0)
    brow = lax.broadcasted_iota(jnp.int32, (nbr, 2 * tm), 0)
    browf = brow.astype(F32)
    q_blk = blk0 + (lax.broadcasted_iota(jnp.int32, (1, 2 * tm), 1) % tm) // MOBA_BLOCK

    def split(a):
        hi = a.astype(BF16)
        return hi, (a - hi.astype(F32)).astype(BF16)

    def gate_pair(hp):
        q_pair = qt[hp * LANES:(hp + 1) * LANES]
        q2 = jnp.concatenate([jnp.where(feat // A_HEAD_DIM == h, q_pair, 0.0) for h in range(2)],
                             axis=1)
        km_hi, km_lo = split(km_ref[:, hp * LANES:(hp + 1) * LANES])
        q_hi, q_lo = split(q2)
        gate = jnp.dot(jnp.concatenate([km_hi, km_hi, km_lo], axis=1),
                       jnp.concatenate([q_hi, q_lo, q_hi], axis=0), preferred_element_type=F32)
        g = jnp.where(brow < q_blk, gate, -jnp.inf)
        bias = jnp.full((nbr, 2 * tm), MASK_VALUE, F32)
        for _ in range(MOBA_TOPK):
            mx = jnp.max(g, axis=0, keepdims=True)
            first = jnp.min(jnp.where(g == mx, browf, float(nbr)), axis=0, keepdims=True)
            hit = browf == first
            bias = jnp.where(hit & (mx > -jnp.inf), 0.0, bias)
            g = jnp.where(hit, -jnp.inf, g)
        for h in range(2):
            bias_ref[0, 2 * hp + h] = bias[:, h * tm:(h + 1) * tm].astype(BF16)

    ones_rows = (lax.broadcasted_iota(jnp.int32, (V_ROWS - A_HEAD_DIM, MOBA_BLOCK), 0)
                 == 0).astype(BF16)
    for h in range(A_HEADS):
        for n in range(nblk):
            v_ref[0, h, n, 0:A_HEAD_DIM, :] = vt[h * A_HEAD_DIM:(h + 1) * A_HEAD_DIM,
                                                 n * MOBA_BLOCK:(n + 1) * MOBA_BLOCK].astype(BF16)
            v_ref[0, h, n, A_HEAD_DIM:, :] = ones_rows
    gate_pair(0)
    bq_ref[0] = proj(c, B_KEY_WIDTH) * (B_HEAD_K ** -0.5); c += B_KEY_WIDTH
    bk_ref[0] = proj(c, B_KEY_WIDTH); c += B_KEY_WIDTH
    gate_pair(1)
    bv_ref[0] = proj(c, B_VAL_WIDTH); c += B_VAL_WIDTH
    gate_pair(2)
    og_ref[0] = proj(c, B_VAL_WIDTH); c += B_VAL_WIDTH
    gate_pair(3)
    gk = proj(c, LANES).astype(BF16)
    z = jnp.dot(gk, wgk_ref[...], preferred_element_type=F32) + bgk_ref[...]
    gl_ref[0] = _log_sigmoid(z) / GLA_GATE_NORM


def _front0(x, g, w_in, w_gk2, b_gk2):
    bsz, seq, d = x.shape
    tm = min(1024, seq)
    widths = [A_WIDTH, A_WIDTH, A_WIDTH, B_KEY_WIDTH, B_KEY_WIDTH, B_VAL_WIDTH, GLA_GATE_RANK]
    aq, ak, av, bq, bk, bv, bgk, bog = jnp.split(
        w_in, [sum(widths[:n + 1]) for n in range(len(widths))], axis=1)
    gk_pad = LANES - GLA_GATE_RANK
    w = jnp.concatenate([ak, bq, bk, bv, bog, jnp.pad(bgk, ((0, 0), (0, gk_pad)))],
                        axis=1).astype(BF16)
    wt = jnp.concatenate([aq, av], axis=1).T.astype(BF16)
    wgk = jnp.pad(w_gk2, ((0, gk_pad), (0, 0))).astype(BF16)
    ncol = w.shape[1]

    half = A_HEAD_DIM // 2
    inv = ROPE_THETA ** (-jnp.arange(half, dtype=F32) / half)
    pos = jnp.arange(seq, dtype=jnp.int32).astype(F32)
    first = (jnp.arange(LANES) % A_HEAD_DIM) < half

    def tables(feature_major):
        if feature_major:
            ang, reps, sel = inv[:, None] * pos[None, :], (LANES // half, 1), first[:, None]
        else:
            ang, reps, sel = pos[:, None] * inv[None, :], (1, LANES // half), first[None, :]
        cos, sin = jnp.tile(jnp.cos(ang), reps), jnp.tile(jnp.sin(ang), reps)
        return cos, jnp.where(sel, -sin, 0.0), jnp.where(sel, 0.0, sin)

    nb = seq // MOBA_BLOCK
    nbr = _moba_gate_rows(nb)
    row = lambda n: pl.BlockSpec((1, tm, n), lambda b, i: (b, i, 0))
    tab = pl.BlockSpec((tm, LANES), lambda b, i: (i, 0))
    tab_t = pl.BlockSpec((LANES, tm), lambda b, i: (0, i))
    nt = seq // tm
    outs = pl.pallas_call(
        functools.partial(_front0_kernel, tm=tm),
        grid=(bsz, nt),
        in_specs=[row(d), _resident((1, d)), _resident((d, ncol)), _resident((2 * A_WIDTH, d)),
                  _resident((LANES, B_KEY_WIDTH)), _resident((1, B_KEY_WIDTH)),
                  tab, tab, tab, tab_t, tab_t, tab_t],
        out_specs=[pl.BlockSpec((1, A_WIDTH, tm), lambda b, i: (b, 0, i)), row(A_WIDTH),
                   pl.BlockSpec((1, A_HEADS, tm // MOBA_BLOCK, V_ROWS, MOBA_BLOCK),
                                lambda b, i: (b, 0, i, 0, 0)),
                   pl.BlockSpec((1, A_HEADS, nbr, tm), lambda b, i: (b, 0, 0, i)),
                   row(B_KEY_WIDTH), row(B_KEY_WIDTH), row(B_VAL_WIDTH), row(B_KEY_WIDTH),
                   row(B_VAL_WIDTH)],
        out_shape=[jax.ShapeDtypeStruct((bsz, A_WIDTH, seq), BF16),
                   jax.ShapeDtypeStruct((bsz, seq, A_WIDTH), BF16),
                   jax.ShapeDtypeStruct((bsz, A_HEADS, nb, V_ROWS, MOBA_BLOCK), BF16),
                   jax.ShapeDtypeStruct((bsz, A_HEADS, nbr, seq), BF16),
                   jax.ShapeDtypeStruct((bsz, seq, B_KEY_WIDTH), F32),
                   jax.ShapeDtypeStruct((bsz, seq, B_KEY_WIDTH), F32),
                   jax.ShapeDtypeStruct((bsz, seq, B_VAL_WIDTH), F32),
                   jax.ShapeDtypeStruct((bsz, seq, B_KEY_WIDTH), F32),
                   jax.ShapeDtypeStruct((bsz, seq, B_VAL_WIDTH), F32)],
        scratch_shapes=[pltpu.VMEM((nbr, A_WIDTH), F32)],
        compiler_params=_params(("arbitrary", "arbitrary")),
        name="front0",
    )(x, g.reshape(1, d), w, wt, wgk, b_gk2.reshape(1, B_KEY_WIDTH), *tables(False), *tables(True))
    return outs


def _moba_kernel(qt_ref, k_ref, vt_ref, bias_ref, o_ref, rhs_ref, acc_ref, sa_ref, sb_ref, *,
                 group, nb, qn):
    blk = MOBA_BLOCK
    tq = qn * blk
    q_first = pl.program_id(2) * qn
    q_last = q_first + qn - 1
    qt = qt_ref[0]
    feat = lax.broadcasted_iota(jnp.int32, (LANES, 1), 0)
    nbr = bias_ref.shape[2]
    for h in range(2):
        rhs_ref[:, h * tq:(h + 1) * tq] = jnp.concatenate(
            [jnp.where(feat // A_HEAD_DIM == h, qt, jnp.zeros_like(qt)), bias_ref[0, h],
             jnp.full((MOBA_BLOCKS_PAD - nbr, tq), MASK_VALUE, BF16)], axis=0)

    keys = group * blk
    blk_of_row = lax.broadcasted_iota(jnp.int32, (keys, MOBA_BLOCKS_PAD), 0) // blk
    lane_blk = lax.broadcasted_iota(jnp.int32, (keys, MOBA_BLOCKS_PAD), 1)

    def scores(g, s_ref):
        j0 = g * group
        off = pl.multiple_of(jnp.minimum(j0, nb - group) * blk, blk)
        onehot = (lane_blk == blk_of_row + j0).astype(BF16)
        lhs_j = jnp.concatenate([k_ref[0, pl.ds(off, keys), :], onehot], axis=1)
        s = jnp.dot(lhs_j, rhs_ref[...], preferred_element_type=F32)
        s_ref[...] = s
        return jnp.max(s, axis=0, keepdims=True)

    def accumulate(g, s_ref, m, s_max):
        j0 = g * group
        m_new = jnp.maximum(m, s_max)
        alpha = jnp.exp2(m - m_new)
        p = jnp.exp2(s_ref[...] - m_new).astype(BF16)
        for h in range(2):
            vt = jnp.concatenate([vt_ref[0, h, jnp.minimum(j0 + u, nb - 1)] for u in range(group)],
                                 axis=1)
            acc_ref[h] = (alpha[:, h * tq:(h + 1) * tq] * acc_ref[h]
                          + jnp.dot(vt, p[:, h * tq:(h + 1) * tq], preferred_element_type=F32))
        return m_new

    def body(i, carry):
        m, max_a = carry
        max_b = scores(2 * i + 1, sb_ref)
        m = accumulate(2 * i, sa_ref, m, max_a)
        max_a = scores(2 * i + 2, sa_ref)
        m = accumulate(2 * i + 1, sb_ref, m, max_b)
        return m, max_a

    max_0 = scores(0, sa_ref)

    own = pl.multiple_of(q_first * blk, blk)
    lhs = jnp.concatenate([k_ref[0, pl.ds(own, tq), :], jnp.zeros((tq, MOBA_BLOCKS_PAD), BF16)],
                          axis=1)
    key_pos = lax.broadcasted_iota(jnp.int32, (tq, 2 * tq), 0)
    qry_pos = lax.broadcasted_iota(jnp.int32, (tq, 2 * tq), 1) % tq
    causal = (key_pos <= qry_pos) & (key_pos // blk == qry_pos // blk)
    s = jnp.where(causal, jnp.dot(lhs, rhs_ref[...], preferred_element_type=F32), MASK_VALUE)
    m0 = jnp.max(s, axis=0, keepdims=True)
    p = jnp.exp2(s - m0).astype(BF16)
    for h in range(2):
        vt_own = jnp.concatenate([vt_ref[0, h, q_first + u] for u in range(qn)], axis=1)
        acc_ref[h] = jnp.dot(vt_own, p[:, h * tq:(h + 1) * tq], preferred_element_type=F32)

    n_groups = (q_last + group - 1) // group
    n_pairs = jnp.maximum(n_groups - 1, 0) // 2

    def two_pairs(i, carry):
        return body(2 * i + 1, body(2 * i, carry))

    carry = lax.fori_loop(0, n_pairs // 2, two_pairs, (m0, max_0))
    m, max_a = lax.cond(n_pairs % 2 == 1, lambda c: body(n_pairs - 1, c), lambda c: c, carry)
    left = n_groups - 2 * n_pairs

    @pl.when(left == 1)
    def _():
        accumulate(n_groups - 1, sa_ref, m, max_a)

    @pl.when(left == 2)
    def _():
        max_b = scores(n_groups - 1, sb_ref)
        m_mid = accumulate(n_groups - 2, sa_ref, m, max_a)
        accumulate(n_groups - 1, sb_ref, m_mid, max_b)

    outs = []
    for h in range(2):
        a = acc_ref[h]
        outs.append(a[0:A_HEAD_DIM] / a[A_HEAD_DIM:A_HEAD_DIM + 1])
    o_ref[0] = jnp.concatenate(outs, axis=0).T


def _moba(qt, k, vt, bias):
    bsz, seq, _ = k.shape
    nb = seq // MOBA_BLOCK
    nbr = _moba_gate_rows(nb)
    npair = A_WIDTH // LANES
    group = 1
    qn = 2 if nb % 2 == 0 else 1
    tq = qn * MOBA_BLOCK
    return pl.pallas_call(
        functools.partial(_moba_kernel, group=group, nb=nb, qn=qn),
        grid=(bsz, npair, nb // qn),
        in_specs=[pl.BlockSpec((1, LANES, tq), lambda b, hp, i: (b, hp, i)),
                  pl.BlockSpec((1, seq, LANES), lambda b, hp, i: (b, 0, hp)),
                  pl.BlockSpec((1, 2, nb, V_ROWS, MOBA_BLOCK), lambda b, hp, i: (b, hp, 0, 0, 0)),
                  pl.BlockSpec((1, 2, nbr, tq), lambda b, hp, i: (b, hp, 0, i))],
        out_specs=pl.BlockSpec((1, tq, LANES), lambda b, hp, i: (b, i, hp)),
        out_shape=jax.ShapeDtypeStruct((bsz, seq, A_WIDTH), F32),
        scratch_shapes=[pltpu.VMEM((2 * LANES, 2 * tq), BF16),
                        pltpu.VMEM((2, V_ROWS, tq), F32),
                        pltpu.VMEM((group * MOBA_BLOCK, 2 * tq), F32),
                        pltpu.VMEM((group * MOBA_BLOCK, 2 * tq), F32)],
        compiler_params=_params(("arbitrary", "arbitrary", "arbitrary")),
        name="moba",
    )(qt, k, vt, bias)


def _gla_kernel(q_ref, k_ref, v_ref, g_ref, og_ref, ng_ref, o_ref, st_ref, *, tm):
    @pl.when(pl.program_id(1) == 0)
    def _():
        st_ref[...] = jnp.zeros_like(st_ref)

    q, k, v, g = q_ref[0], k_ref[0], v_ref[0], g_ref[0]
    kw, vw = B_KEY_WIDTH, B_VAL_WIDTH
    row = lax.broadcasted_iota(jnp.int32, (tm, 1), 0)

    pre, post, tot, upper = {1: g}, {1: jnp.zeros_like(g)}, {1: g}, {}
    n = 1
    while n < GLA_CHUNK:
        upper[n] = (row % (2 * n)) >= n
        below = pltpu.roll(tot[n], n, 0)
        above = pltpu.roll(tot[n], tm - n, 0)
        pre[2 * n] = pre[n] + jnp.where(upper[n], below, 0.0)
        post[2 * n] = post[n] + jnp.where(upper[n], 0.0, above)
        tot[2 * n] = tot[n] + jnp.where(upper[n], below, above)
        n *= 2
    levels = sorted(upper)

    q_lvl = {n: jnp.where(upper[n], q * jnp.exp(pre[n]), 0.0).astype(BF16) for n in levels}
    k_lvl = {n: jnp.where(upper[n], 0.0, k * jnp.exp(post[n])).astype(BF16) for n in levels}
    q_bf, k_bf = q.astype(BF16), k.astype(BF16)
    q_chk = (q * jnp.exp(pre[GLA_CHUNK])).astype(BF16)
    k_chk = (k * jnp.exp(post[GLA_CHUNK])).astype(BF16)

    head_vk = (lax.broadcasted_iota(jnp.int32, (vw, kw), 0) // B_HEAD_V
               == lax.broadcasted_iota(jnp.int32, (vw, kw), 1) // B_HEAD_K)
    head_vv = (lax.broadcasted_iota(jnp.int32, (vw, vw), 0) // B_HEAD_V
               == lax.broadcasted_iota(jnp.int32, (vw, vw), 1) // B_HEAD_V).astype(BF16)

    tt = lax.broadcasted_iota(jnp.int32, (B_HEADS * GLA_CHUNK, GLA_CHUNK), 0) % GLA_CHUNK
    ss = lax.broadcasted_iota(jnp.int32, (B_HEADS * GLA_CHUNK, GLA_CHUNK), 1)
    same_row = tt == ss
    splits_at = {n: (tt // (2 * n) == ss // (2 * n)) & (tt // n != ss // n) for n in levels}
    lane_k = lax.broadcasted_iota(jnp.int32, (1, kw), 1)
    lane_v = lax.broadcasted_iota(jnp.int32, (1, vw), 1)

    def stack_heads(x):
        return jnp.concatenate([jnp.where(lane_k // B_HEAD_K == h, x, jnp.zeros_like(x))
                                for h in range(B_HEADS)], axis=0)

    def scores(lhs, rhs):
        return lax.dot_general(stack_heads(lhs), rhs, (((1,), (1,)), ((), ())),
                               preferred_element_type=F32)

    outs = []
    for c in range(tm // GLA_CHUNK):
        lo = c * GLA_CHUNK
        rows = slice(lo, lo + GLA_CHUNK)
        vc = v[rows].astype(BF16)

        attn = jnp.where(same_row, scores(q_bf[rows], k_bf[rows]), 0.0)
        for n in levels:
            attn = jnp.where(splits_at[n], scores(q_lvl[n][rows], k_lvl[n][rows]), attn)
        res = jnp.dot(attn.astype(BF16), vc, preferred_element_type=F32)
        o_in = jnp.zeros((GLA_CHUNK, vw), F32)
        for h in range(B_HEADS):
            o_in = o_in + jnp.where(lane_v // B_HEAD_V == h,
                                    res[h * GLA_CHUNK:(h + 1) * GLA_CHUNK], 0.0)

        st = st_ref[...]
        o_st = lax.dot_general(q_chk[rows], st.astype(BF16), (((1,), (1,)), ((), ())),
                               preferred_element_type=F32)
        upd = lax.dot_general(vc, k_chk[rows], (((0,), (0,)), ((), ())),
                              preferred_element_type=F32)
        decay = jnp.exp(tot[GLA_CHUNK][lo:lo + 1, :])
        st_ref[...] = st * decay + jnp.where(head_vk, upd, 0.0)
        outs.append(o_in + o_st)

    o = jnp.concatenate(outs, axis=0)

    sq = o * o
    sq_hi = sq.astype(BF16)
    sq_lo = (sq - sq_hi.astype(F32)).astype(BF16)
    ms = (jnp.dot(sq_hi, head_vv, preferred_element_type=F32)
          + jnp.dot(sq_lo, head_vv, preferred_element_type=F32)) * (1.0 / B_HEAD_V)
    og = og_ref[0]
    o_ref[0] = (o * lax.rsqrt(ms + NORM_EPS) * ng_ref[...]) * (og * jax.nn.sigmoid(og))


def _gla(bq, bk, bv, gl, og, norm_g):
    bsz, seq, _ = bq.shape
    tm = min(512, seq)
    row = lambda n: pl.BlockSpec((1, tm, n), lambda b, i: (b, i, 0))
    return pl.pallas_call(
        functools.partial(_gla_kernel, tm=tm),
        grid=(bsz, seq // tm),
        in_specs=[row(B_KEY_WIDTH), row(B_KEY_WIDTH), row(B_VAL_WIDTH), row(B_KEY_WIDTH),
                  row(B_VAL_WIDTH), _resident((1, B_VAL_WIDTH))],
        out_specs=row(B_VAL_WIDTH),
        out_shape=jax.ShapeDtypeStruct((bsz, seq, B_VAL_WIDTH), F32),
        scratch_shapes=[pltpu.VMEM((B_VAL_WIDTH, B_KEY_WIDTH), F32)],
        compiler_params=_params(("arbitrary", "arbitrary")),
        name="gla",
    )(bq, bk, bv, gl, og, norm_g.reshape(1, B_VAL_WIDTH))


def _ffn_kernel(*refs, tm, bn, dff, final_norm, mixer):
    if mixer:
        x_ref, oa_ref, ob_ref, wa_ref, wb_ref = refs[:5]
        refs = refs[5:]
    else:
        x_ref, refs = refs[0], refs[1:]
    g_ref, wup_ref, cw_ref, cb_ref, wdn_ref, fg_ref, o_ref, halo_ref, ext_ref = refs
    nkb = dff // bn

    @pl.when(pl.program_id(1) == 0)
    def _():
        halo_ref[...] = jnp.zeros_like(halo_ref)

    x = x_ref[0]
    if mixer:
        o_ref[0] = (jnp.dot(oa_ref[0].astype(BF16), wa_ref[...], preferred_element_type=F32)
                    + jnp.dot(ob_ref[0].astype(BF16), wb_ref[...], preferred_element_type=F32))
        x = x + o_ref[0]
    hn = _rms(x, g_ref[...]).astype(BF16)

    def up_stage(kb):
        for part in range(2):
            c0 = part * dff + kb * bn
            up = jnp.dot(hn, wup_ref[:, c0:c0 + bn], preferred_element_type=F32)
            ext_ref[kb % 2, part, 0:SUBLANES, :] = halo_ref[:, c0:c0 + bn]
            ext_ref[kb % 2, part, SUBLANES:, :] = up
            halo_ref[:, c0:c0 + bn] = up[tm - SUBLANES:, :]

    def gate_stage(kb):
        halves = []
        for part in range(2):
            c0 = part * dff + kb * bn
            y = cb_ref[:, c0:c0 + bn]
            for i in range(FFN_CONV):
                off = SUBLANES - (FFN_CONV - 1) + i
                y = y + ext_ref[kb % 2, part, pl.ds(off, tm), :] * cw_ref[i:i + 1, c0:c0 + bn]
            halves.append(y)
        return (halves[0] * _gelu_tanh(halves[1])).astype(BF16)

    up_stage(0)
    if nkb > 1:
        up_stage(1)
    a = gate_stage(0)
    acc = x
    for kb in range(nkb):
        if kb + 2 < nkb:
            up_stage(kb + 2)
        a_next = gate_stage(kb + 1) if kb + 1 < nkb else None
        acc = acc + jnp.dot(a, wdn_ref[kb * bn:(kb + 1) * bn, :], preferred_element_type=F32)
        a = a_next
    if final_norm:
        acc = _rms(acc, fg_ref[...])
    o_ref[0] = acc


def _conv_ffn(x, g, w_up, conv_w, conv_b, w_down, final_g, final_norm, mixer=None):
    bsz, seq, d = x.shape
    dff = w_down.shape[0]
    tm = min(512, seq)
    bn = 1536
    rows = lambda n: pl.BlockSpec((1, tm, n), lambda b, i: (b, i, 0))
    row = rows(d)
    lead_specs, lead_args = [row], [x]
    if mixer is not None:
        o_a, o_b, w_out = mixer
        w = w_out.astype(BF16)
        lead_specs += [rows(A_WIDTH), rows(B_VAL_WIDTH),
                       _resident((A_WIDTH, d)), _resident((B_VAL_WIDTH, d))]
        lead_args += [o_a, o_b, w[:A_WIDTH], w[A_WIDTH:]]
    return pl.pallas_call(
        functools.partial(_ffn_kernel, tm=tm, bn=bn, dff=dff, final_norm=final_norm,
                          mixer=mixer is not None),
        grid=(bsz, seq // tm),
        in_specs=lead_specs + [_resident((1, d)), _resident((d, 2 * dff)),
                               _resident((FFN_CONV, 2 * dff)), _resident((1, 2 * dff)),
                               _resident((dff, d)), _resident((1, d))],
        out_specs=row,
        out_shape=jax.ShapeDtypeStruct((bsz, seq, d), F32),
        scratch_shapes=[pltpu.VMEM((SUBLANES, 2 * dff), F32),
                        pltpu.VMEM((2, 2, tm + SUBLANES, bn), F32)],
        compiler_params=_params(("arbitrary", "arbitrary")),
        name="conv_ffn_final" if final_norm else "conv_ffn",
    )(*lead_args, g.reshape(1, d), w_up.astype(BF16), conv_w, conv_b.reshape(1, 2 * dff),
      w_down.astype(BF16), final_g.reshape(1, d))


def _rglru_kernel(x_ref, g_ref, win_ref, cw_ref, cb_ref, wa_ref, ba_ref, wx_ref, bx_ref,
                  lam_ref, wout_ref, o_ref, ext_ref, halo_ref, hlast_ref, *, tm, drnn, nseq):
    seg = tm // SUBLANES
    shifts = RG_CONV - 1
    bw = drnn // RG_BLOCKS
    seqs = range(nseq)

    @pl.when(pl.program_id(1) == 0)
    def _():
        halo_ref[...] = jnp.zeros_like(halo_ref)
        hlast_ref[...] = jnp.zeros_like(hlast_ref)

    prow = lax.broadcasted_iota(jnp.int32, (tm, tm), 0)
    pcol = lax.broadcasted_iota(jnp.int32, (tm, tm), 1)
    perm = (pcol == (prow % SUBLANES) * seg + prow // SUBLANES).astype(BF16)
    unperm = (prow == (pcol % SUBLANES) * seg + pcol // SUBLANES).astype(BF16)
    first_sub = lax.broadcasted_iota(jnp.int32, (SUBLANES, 1), 0) == 0

    xs = [x_ref[b] for b in seqs]
    hn = [jnp.dot(perm, _rms(xs[b], g_ref[...]).astype(BF16),
                  preferred_element_type=F32).astype(BF16) for b in seqs]
    gate = [jnp.dot(hn[b], win_ref[:, 0:drnn], preferred_element_type=F32) for b in seqs]
    xr_raw = [jnp.dot(hn[b], win_ref[:, drnn:2 * drnn], preferred_element_type=F32) for b in seqs]

    xr = []
    for b in seqs:
        for r in range(shifts):
            lo = (seg - shifts + r) * SUBLANES
            cur = xr_raw[b][lo:lo + SUBLANES]
            ext_ref[b, r * SUBLANES:(r + 1) * SUBLANES, :] = jnp.where(
                first_sub, pltpu.roll(halo_ref[b, r], 1, 0), pltpu.roll(cur, 1, 0))
            halo_ref[b, r] = cur
        ext_ref[b, shifts * SUBLANES:, :] = xr_raw[b]
        acc = cb_ref[...]
        for i in range(RG_CONV):
            acc = acc + ext_ref[b, i * SUBLANES:i * SUBLANES + tm, :] * cw_ref[i:i + 1, :]
        xr.append(acc)

    ys = []
    for b in seqs:
        xb = xr[b].astype(BF16)
        ra, ri = [], []
        for n in range(RG_BLOCKS):
            xn = xb[:, n * bw:(n + 1) * bw]
            ra.append(jnp.dot(xn, wa_ref[n], preferred_element_type=F32))
            ri.append(jnp.dot(xn, wx_ref[n], preferred_element_type=F32))
        r = jax.nn.sigmoid(jnp.concatenate(ra, axis=1) + ba_ref[...])
        ig = jax.nn.sigmoid(jnp.concatenate(ri, axis=1) + bx_ref[...])
        log_a = r * lam_ref[...]
        a = jnp.exp(log_a)
        th = jnp.tanh(log_a)
        one_minus_a2 = -2.0 * th / (1.0 - th)
        u = jnp.sqrt(one_minus_a2) * (ig * xr[b])

        h_loc, a_cum = [u[0:SUBLANES]], [a[0:SUBLANES]]
        for j in range(1, seg):
            aj = a[j * SUBLANES:(j + 1) * SUBLANES]
            h_loc.append(aj * h_loc[-1] + u[j * SUBLANES:(j + 1) * SUBLANES])
            a_cum.append(aj * a_cum[-1])
        h_in = [hlast_ref[b]]
        for s in range(SUBLANES):
            h_in.append(h_loc[-1][s:s + 1] + a_cum[-1][s:s + 1] * h_in[-1])
        hlast_ref[b] = h_in[SUBLANES]
        h_seg = jnp.concatenate(h_in[:SUBLANES], axis=0)
        h = jnp.concatenate([h_loc[j] + a_cum[j] * h_seg for j in range(seg)], axis=0)
        ys.append((h * _gelu_tanh(gate[b])).astype(BF16))

    for b in seqs:
        y_t = jnp.dot(unperm, ys[b], preferred_element_type=F32).astype(BF16)
        o_ref[b] = xs[b] + jnp.dot(y_t, wout_ref[...], preferred_element_type=F32)


def _rglru(x, g, w_in, conv_w, conv_b, w_a, b_a, w_x, b_x, lam, w_out):
    bsz, seq, d = x.shape
    drnn = w_out.shape[0]
    bw = drnn // RG_BLOCKS
    tm = min(256, seq)
    nseq = 2 if bsz % 2 == 0 else 1
    row = pl.BlockSpec((nseq, tm, d), lambda b, i: (b, i, 0))
    lam_c = (-LRU_C) * jax.nn.softplus(-lam.astype(F32))
    return pl.pallas_call(
        functools.partial(_rglru_kernel, tm=tm, drnn=drnn, nseq=nseq),
        grid=(bsz // nseq, seq // tm),
        in_specs=[row, _resident((1, d)), _resident((d, 2 * drnn)), _resident((RG_CONV, drnn)),
                  _resident((1, drnn)), _resident((RG_BLOCKS, bw, bw)), _resident((1, drnn)),
                  _resident((RG_BLOCKS, bw, bw)), _resident((1, drnn)), _resident((1, drnn)),
                  _resident((drnn, d))],
        out_specs=row,
        out_shape=jax.ShapeDtypeStruct((bsz, seq, d), F32),
        scratch_shapes=[pltpu.VMEM((nseq, tm + (RG_CONV - 1) * SUBLANES, drnn), F32),
                        pltpu.VMEM((nseq, RG_CONV - 1, SUBLANES, drnn), F32),
                        pltpu.VMEM((nseq, 1, drnn), F32)],
        compiler_params=_params(("arbitrary", "arbitrary")),
        name="rglru",
    )(x, g.reshape(1, d), w_in.astype(BF16), conv_w, conv_b.reshape(1, drnn),
      w_a.astype(BF16), b_a.reshape(1, drnn), w_x.astype(BF16), b_x.reshape(1, drnn),
      lam_c.reshape(1, drnn), w_out.astype(BF16))


def kernel(x, mix_norm_g, ffn_norm_g, final_norm_g, ev_w_in, ev_w_gk2, ev_b_gk2, ev_gla_norm_g,
           ev_w_out, od_w_in, od_conv_w, od_conv_b, od_w_a, od_b_a, od_w_x, od_b_x, od_lambda,
           od_w_out, ffn_w_up, ffn_conv_w, ffn_conv_b, ffn_w_down):
    depth = mix_norm_g.shape[0]
    h = x
    for l in range(depth):
        j = l // 2
        mixer = None
        if l % 2 == 0:
            q, k, v, bias, bq, bk, bv, gl, og = _front0(h, mix_norm_g[l], ev_w_in[j], ev_w_gk2[j],
                                                      ev_b_gk2[j])
            o_a = _moba(q, k, v, bias)
            o_b = _gla(bq, bk, bv, gl, og, ev_gla_norm_g[j])
            mixer = (o_a, o_b, ev_w_out[j])
        else:
            h = _rglru(h, mix_norm_g[l], od_w_in[j], od_conv_w[j], od_conv_b[j], od_w_a[j],
                       od_b_a[j], od_w_x[j], od_b_x[j], od_lambda[j], od_w_out[j])
        h = _conv_ffn(h, ffn_norm_g[l], ffn_w_up[l], ffn_conv_w[l], ffn_conv_b[l], ffn_w_down[l],
                      final_norm_g, final_norm=(l == depth - 1), mixer=mixer)
    return h
```

```python
import functools

import jax
import jax.numpy as jnp
from jax import lax
from jax.experimental import pallas as pl
from jax.experimental.pallas import tpu as pltpu

F32 = jnp.float32
BF16 = jnp.bfloat16

A_HEADS = 8
A_HEAD_DIM = 64
A_WIDTH = A_HEADS * A_HEAD_DIM
MOBA_BLOCK = 256
MOBA_TOPK = 3
ROPE_THETA = 10000.0
B_HEADS = 8
B_HEAD_K = 32
B_HEAD_V = 64
B_KEY_WIDTH = B_HEADS * B_HEAD_K
B_VAL_WIDTH = B_HEADS * B_HEAD_V
GLA_GATE_RANK = 16
GLA_GATE_NORM = 16.0
RG_BLOCKS = 4
RG_CONV = 4
LRU_C = 8.0
FFN_CONV = 3
NORM_EPS = 1e-6
MASK_VALUE = -1e30

LANES = 128
SUBLANES = 8
BF16_ROWS = 16
LOG2_E = 1.4426950408889634
VMEM_LIMIT = 56 * 1024 * 1024

GLA_CHUNK = 64
MOBA_BLOCKS_PAD = LANES
V_ROWS = A_HEAD_DIM + BF16_ROWS


def _params(sem):
    return pltpu.CompilerParams(dimension_semantics=sem, vmem_limit_bytes=VMEM_LIMIT)


def _rms(x, g):
    ms = jnp.mean(x * x, axis=-1, keepdims=True)
    return x * lax.rsqrt(ms + NORM_EPS) * g


def _gelu_tanh(x):
    return 0.5 * x * (1.0 + jnp.tanh(0.7978845608028654 * (x + 0.044715 * (x * x * x))))


def _log_sigmoid(z):
    return jnp.minimum(z, 0.0) - jnp.log1p(jnp.exp(-jnp.abs(z)))


def _moba_gate_rows(nb):
    assert nb <= MOBA_BLOCKS_PAD
    return -(-nb // BF16_ROWS) * BF16_ROWS


def _resident(shape):
    nd = len(shape)
    return pl.BlockSpec(shape, lambda *_: (0,) * nd, pipeline_mode=pl.Buffered(1))


def _front0_kernel(x_ref, g_ref, w_ref, wt_ref, wgk_ref, bgk_ref, cos_ref, sa_ref, sb_ref,
                   cost_ref, sat_ref, sbt_ref,
                   q_ref, k_ref, v_ref, bias_ref, bq_ref, bk_ref, bv_ref, gl_ref, og_ref,
                   km_ref, *, tm):
    @pl.when(pl.program_id(1) == 0)
    def _():
        km_ref[...] = jnp.zeros_like(km_ref)

    hn = _rms(x_ref[0], g_ref[...]).astype(BF16)

    def proj(c0, n):
        return jnp.dot(hn, w_ref[:, c0:c0 + n], preferred_element_type=F32)

    def proj_t(r0, n):
        return lax.dot_general(wt_ref[r0:r0 + n, :], hn, (((1,), (1,)), ((), ())),
                               preferred_element_type=F32)

    def rot(t, cos, sa, sb, axis):
        parts = []
        for c in range(t.shape[axis] // LANES):
            tc = lax.slice_in_dim(t, c * LANES, (c + 1) * LANES, axis=axis)
            parts.append(tc * cos + pltpu.roll(tc, LANES - 32, axis) * sa
                         + pltpu.roll(tc, 32, axis) * sb)
        return jnp.concatenate(parts, axis=axis)

    qt = rot(proj_t(0, A_WIDTH), cost_ref[...], sat_ref[...], sbt_ref[...], 0)
    q_ref[0] = (qt * (A_HEAD_DIM ** -0.5 * LOG2_E)).astype(BF16)
    vt = proj_t(A_WIDTH, A_WIDTH)
    c = 0
    k = rot(proj(c, A_WIDTH), cos_ref[...], sa_ref[...], sb_ref[...], 1); c += A_WIDTH
    k_ref[0] = k.astype(BF16)
    nblk = tm // MOBA_BLOCK
    kmean = jnp.mean(k.reshape(nblk, MOBA_BLOCK, A_WIDTH), axis=1)
    blk0 = pl.program_id(1) * nblk
    for n in range(nblk):
        km_ref[pl.ds(blk0 + n, 1), :] = kmean[n:n + 1]

    nbr = km_ref.shape[0]
    feat = lax.broadcasted_iota(jnp.int32, (LANES, 1), 0)
    brow = lax.broadcasted_iota(jnp.int32, (nbr, 2 * tm), 0)
    browf = brow.astype(F32)
    q_blk = blk0 + (lax.broadcasted_iota(jnp.int32, (1, 2 * tm), 1) % tm) // MOBA_BLOCK

    def split(a):
        hi = a.astype(BF16)
        return hi, (a - hi.astype(F32)).astype(BF16)

    def gate_pair(hp):
        q_pair = qt[hp * LANES:(hp + 1) * LANES]
        q2 = jnp.concatenate([jnp.where(feat // A_HEAD_DIM == h, q_pair, 0.0) for h in range(2)],
                             axis=1)
        km_hi, km_lo = split(km_ref[:, hp * LANES:(hp + 1) * LANES])
        q_hi, q_lo = split(q2)
        gate = jnp.dot(jnp.concatenate([km_hi, km_hi, km_lo], axis=1),
                       jnp.concatenate([q_hi, q_lo, q_hi], axis=0), preferred_element_type=F32)
        g = jnp.where(brow < q_blk, gate, -jnp.inf)
        bias = jnp.full((nbr, 2 * tm), MASK_VALUE, F32)
        for _ in range(MOBA_TOPK):
            mx = jnp.max(g, axis=0, keepdims=True)
            first = jnp.min(jnp.where(g == mx, browf, float(nbr)), axis=0, keepdims=True)
            hit = browf == first
            bias = jnp.where(hit & (mx > -jnp.inf), 0.0, bias)
            g = jnp.where(hit, -jnp.inf, g)
        for h in range(2):
            bias_ref[0, 2 * hp + h] = bias[:, h * tm:(h + 1) * tm].astype(BF16)

    ones_rows = (lax.broadcasted_iota(jnp.int32, (V_ROWS - A_HEAD_DIM, MOBA_BLOCK), 0)
                 == 0).astype(BF16)
    for h in range(A_HEADS):
        for n in range(nblk):
            v_ref[0, h, n, 0:A_HEAD_DIM, :] = vt[h * A_HEAD_DIM:(h + 1) * A_HEAD_DIM,
                                                 n * MOBA_BLOCK:(n + 1) * MOBA_BLOCK].astype(BF16)
            v_ref[0, h, n, A_HEAD_DIM:, :] = ones_rows
    gate_pair(0)
    bq_ref[0] = proj(c, B_KEY_WIDTH) * (B_HEAD_K ** -0.5); c += B_KEY_WIDTH
    bk_ref[0] = proj(c, B_KEY_WIDTH); c += B_KEY_WIDTH
    gate_pair(1)
    bv_ref[0] = proj(c, B_VAL_WIDTH); c += B_VAL_WIDTH
    gate_pair(2)
    og_ref[0] = proj(c, B_VAL_WIDTH); c += B_VAL_WIDTH
    gate_pair(3)
    gk = proj(c, LANES).astype(BF16)
    z = jnp.dot(gk, wgk_ref[...], preferred_element_type=F32) + bgk_ref[...]
    gl_ref[0] = _log_sigmoid(z) / GLA_GATE_NORM


def _front0(x, g, w_in, w_gk2, b_gk2):
    bsz, seq, d = x.shape
    tm = min(1024, seq)
    widths = [A_WIDTH, A_WIDTH, A_WIDTH, B_KEY_WIDTH, B_KEY_WIDTH, B_VAL_WIDTH, GLA_GATE_RANK]
    aq, ak, av, bq, bk, bv, bgk, bog = jnp.split(
        w_in, [sum(widths[:n + 1]) for n in range(len(widths))], axis=1)
    gk_pad = LANES - GLA_GATE_RANK
    w = jnp.concatenate([ak, bq, bk, bv, bog, jnp.pad(bgk, ((0, 0), (0, gk_pad)))],
                        axis=1).astype(BF16)
    wt = jnp.concatenate([aq, av], axis=1).T.astype(BF16)
    wgk = jnp.pad(w_gk2, ((0, gk_pad), (0, 0))).astype(BF16)
    ncol = w.shape[1]

    half = A_HEAD_DIM // 2
    inv = ROPE_THETA ** (-jnp.arange(half, dtype=F32) / half)
    pos = jnp.arange(seq, dtype=jnp.int32).astype(F32)
    first = (jnp.arange(LANES) % A_HEAD_DIM) < half

    def tables(feature_major):
        if feature_major:
            ang, reps, sel = inv[:, None] * pos[None, :], (LANES // half, 1), first[:, None]
        else:
            ang, reps, sel = pos[:, None] * inv[None, :], (1, LANES // half), first[None, :]
        cos, sin = jnp.tile(jnp.cos(ang), reps), jnp.tile(jnp.sin(ang), reps)
        return cos, jnp.where(sel, -sin, 0.0), jnp.where(sel, 0.0, sin)

    nb = seq // MOBA_BLOCK
    nbr = _moba_gate_rows(nb)
    row = lambda n: pl.BlockSpec((1, tm, n), lambda b, i: (b, i, 0))
    tab = pl.BlockSpec((tm, LANES), lambda b, i: (i, 0))
    tab_t = pl.BlockSpec((LANES, tm), lambda b, i: (0, i))
    nt = seq // tm
    outs = pl.pallas_call(
        functools.partial(_front0_kernel, tm=tm),
        grid=(bsz, nt),
        in_specs=[row(d), _resident((1, d)), _resident((d, ncol)), _resident((2 * A_WIDTH, d)),
                  _resident((LANES, B_KEY_WIDTH)), _resident((1, B_KEY_WIDTH)),
                  tab, tab, tab, tab_t, tab_t, tab_t],
        out_specs=[pl.BlockSpec((1, A_WIDTH, tm), lambda b, i: (b, 0, i)), row(A_WIDTH),
                   pl.BlockSpec((1, A_HEADS, tm // MOBA_BLOCK, V_ROWS, MOBA_BLOCK),
                                lambda b, i: (b, 0, i, 0, 0)),
                   pl.BlockSpec((1, A_HEADS, nbr, tm), lambda b, i: (b, 0, 0, i)),
                   row(B_KEY_WIDTH), row(B_KEY_WIDTH), row(B_VAL_WIDTH), row(B_KEY_WIDTH),
                   row(B_VAL_WIDTH)],
        out_shape=[jax.ShapeDtypeStruct((bsz, A_WIDTH, seq), BF16),
                   jax.ShapeDtypeStruct((bsz, seq, A_WIDTH), BF16),
                   jax.ShapeDtypeStruct((bsz, A_HEADS, nb, V_ROWS, MOBA_BLOCK), BF16),
                   jax.ShapeDtypeStruct((bsz, A_HEADS, nbr, seq), BF16),
                   jax.ShapeDtypeStruct((bsz, seq, B_KEY_WIDTH), F32),
                   jax.ShapeDtypeStruct((bsz, seq, B_KEY_WIDTH), F32),
                   jax.ShapeDtypeStruct((bsz, seq, B_VAL_WIDTH), F32),
                   jax.ShapeDtypeStruct((bsz, seq, B_KEY_WIDTH), F32),
                   jax.ShapeDtypeStruct((bsz, seq, B_VAL_WIDTH), F32)],
        scratch_shapes=[pltpu.VMEM((nbr, A_WIDTH), F32)],
        compiler_params=_params(("arbitrary", "arbitrary")),
        name="front0",
    )(x, g.reshape(1, d), w, wt, wgk, b_gk2.reshape(1, B_KEY_WIDTH), *tables(False), *tables(True))
    return outs


def _moba_kernel(qt_ref, k_ref, vt_ref, bias_ref, o_ref, rhs_ref, acc_ref, sa_ref, sb_ref, *,
                 group, nb, qn):
    blk = MOBA_BLOCK
    tq = qn * blk
    q_first = pl.program_id(2) * qn
    q_last = q_first + qn - 1
    qt = qt_ref[0]
    feat = lax.broadcasted_iota(jnp.int32, (LANES, 1), 0)
    nbr = bias_ref.shape[2]
    for h in range(2):
        rhs_ref[:, h * tq:(h + 1) * tq] = jnp.concatenate(
            [jnp.where(feat // A_HEAD_DIM == h, qt, jnp.zeros_like(qt)), bias_ref[0, h],
             jnp.full((MOBA_BLOCKS_PAD - nbr, tq), MASK_VALUE, BF16)], axis=0)

    keys = group * blk
    blk_of_row = lax.broadcasted_iota(jnp.int32, (keys, MOBA_BLOCKS_PAD), 0) // blk
    lane_blk = lax.broadcasted_iota(jnp.int32, (keys, MOBA_BLOCKS_PAD), 1)

    def scores(g, s_ref):
        j0 = g * group
        off = pl.multiple_of(jnp.minimum(j0, nb - group) * blk, blk)
        onehot = (lane_blk == blk_of_row + j0).astype(BF16)
        lhs_j = jnp.concatenate([k_ref[0, pl.ds(off, keys), :], onehot], axis=1)
        s = jnp.dot(lhs_j, rhs_ref[...], preferred_element_type=F32)
        s_ref[...] = s
        return jnp.max(s, axis=0, keepdims=True)

    def accumulate(g, s_ref, m, s_max):
        j0 = g * group
        m_new = jnp.maximum(m, s_max)
        alpha = jnp.exp2(m - m_new)
        p = jnp.exp2(s_ref[...] - m_new).astype(BF16)
        for h in range(2):
            vt = jnp.concatenate([vt_ref[0, h, jnp.minimum(j0 + u, nb - 1)] for u in range(group)],
                                 axis=1)
            acc_ref[h] = (alpha[:, h * tq:(h + 1) * tq] * acc_ref[h]
                          + jnp.dot(vt, p[:, h * tq:(h + 1) * tq], preferred_element_type=F32))
        return m_new

    def body(i, carry):
        m, max_a = carry
        max_b = scores(2 * i + 1, sb_ref)
        m = accumulate(2 * i, sa_ref, m, max_a)
        max_a = scores(2 * i + 2, sa_ref)
        m = accumulate(2 * i + 1, sb_ref, m, max_b)
        return m, max_a

    max_0 = scores(0, sa_ref)

    own = pl.multiple_of(q_first * blk, blk)
    lhs = jnp.concatenate([k_ref[0, pl.ds(own, tq), :], jnp.zeros((tq, MOBA_BLOCKS_PAD), BF16)],
                          axis=1)
    key_pos = lax.broadcasted_iota(jnp.int32, (tq, 2 * tq), 0)
    qry_pos = lax.broadcasted_iota(jnp.int32, (tq, 2 * tq), 1) % tq
    causal = (key_pos <= qry_pos) & (key_pos // blk == qry_pos // blk)
    s = jnp.where(causal, jnp.dot(lhs, rhs_ref[...], preferred_element_type=F32), MASK_VALUE)
    m0 = jnp.max(s, axis=0, keepdims=True)
    p = jnp.exp2(s - m0).astype(BF16)
    for h in range(2):
        vt_own = jnp.concatenate([vt_ref[0, h, q_first + u] for u in range(qn)], axis=1)
        acc_ref[h] = jnp.dot(vt_own, p[:, h * tq:(h + 1) * tq], preferred_element_type=F32)

    n_groups = (q_last + group - 1) // group
    n_pairs = jnp.maximum(n_groups - 1, 0) // 2

    def two_pairs(i, carry):
        return body(2 * i + 1, body(2 * i, carry))

    def four_pairs(i, carry):
        return two_pairs(2 * i + 1, two_pairs(2 * i, carry))

    carry = lax.fori_loop(0, n_pairs // 4, four_pairs, (m0, max_0))
    carry = lax.cond(n_pairs % 4 >= 2, lambda c: two_pairs(n_pairs // 2 - 1, c), lambda c: c, carry)
    m, max_a = lax.cond(n_pairs % 2 == 1, lambda c: body(n_pairs - 1, c), lambda c: c, carry)
    left = n_groups - 2 * n_pairs

    @pl.when(left == 1)
    def _():
        accumulate(n_groups - 1, sa_ref, m, max_a)

    @pl.when(left == 2)
    def _():
        max_b = scores(n_groups - 1, sb_ref)
        m_mid = accumulate(n_groups - 2, sa_ref, m, max_a)
        accumulate(n_groups - 1, sb_ref, m_mid, max_b)

    outs = []
    for h in range(2):
        a = acc_ref[h]
        outs.append(a[0:A_HEAD_DIM] / a[A_HEAD_DIM:A_HEAD_DIM + 1])
    o_ref[0] = jnp.concatenate(outs, axis=0).T


def _moba(qt, k, vt, bias):
    bsz, seq, _ = k.shape
    nb = seq // MOBA_BLOCK
    nbr = _moba_gate_rows(nb)
    npair = A_WIDTH // LANES
    group = 2 if nb % 2 == 0 else 1
    qn = 2 if nb % 2 == 0 else 1
    tq = qn * MOBA_BLOCK
    return pl.pallas_call(
        functools.partial(_moba_kernel, group=group, nb=nb, qn=qn),
        grid=(bsz, npair, nb // qn),
        in_specs=[pl.BlockSpec((1, LANES, tq), lambda b, hp, i: (b, hp, i)),
                  pl.BlockSpec((1, seq, LANES), lambda b, hp, i: (b, 0, hp)),
                  pl.BlockSpec((1, 2, nb, V_ROWS, MOBA_BLOCK), lambda b, hp, i: (b, hp, 0, 0, 0)),
                  pl.BlockSpec((1, 2, nbr, tq), lambda b, hp, i: (b, hp, 0, i))],
        out_specs=pl.BlockSpec((1, tq, LANES), lambda b, hp, i: (b, i, hp)),
        out_shape=jax.ShapeDtypeStruct((bsz, seq, A_WIDTH), F32),
        scratch_shapes=[pltpu.VMEM((2 * LANES, 2 * tq), BF16),
                        pltpu.VMEM((2, V_ROWS, tq), F32),
                        pltpu.VMEM((group * MOBA_BLOCK, 2 * tq), F32),
                        pltpu.VMEM((group * MOBA_BLOCK, 2 * tq), F32)],
        compiler_params=_params(("arbitrary", "arbitrary", "arbitrary")),
        name="moba",
    )(qt, k, vt, bias)


def _gla_kernel(q_ref, k_ref, v_ref, g_ref, og_ref, ng_ref, o_ref, st_ref, *, tm):
    @pl.when(pl.program_id(1) == 0)
    def _():
        st_ref[...] = jnp.zeros_like(st_ref)

    q, k, v, g = q_ref[0], k_ref[0], v_ref[0], g_ref[0]
    kw, vw = B_KEY_WIDTH, B_VAL_WIDTH
    row = lax.broadcasted_iota(jnp.int32, (tm, 1), 0)

    pre, post, tot, upper = {1: g}, {1: jnp.zeros_like(g)}, {1: g}, {}
    n = 1
    while n < GLA_CHUNK:
        upper[n] = (row % (2 * n)) >= n
        below = pltpu.roll(tot[n], n, 0)
        above = pltpu.roll(tot[n], tm - n, 0)
        pre[2 * n] = pre[n] + jnp.where(upper[n], below, 0.0)
        post[2 * n] = post[n] + jnp.where(upper[n], 0.0, above)
        tot[2 * n] = tot[n] + jnp.where(upper[n], below, above)
        n *= 2
    levels = sorted(upper)

    q_lvl = {n: jnp.where(upper[n], q * jnp.exp(pre[n]), 0.0).astype(BF16) for n in levels}
    k_lvl = {n: jnp.where(upper[n], 0.0, k * jnp.exp(post[n])).astype(BF16) for n in levels}
    q_bf, k_bf = q.astype(BF16), k.astype(BF16)
    q_chk = (q * jnp.exp(pre[GLA_CHUNK])).astype(BF16)
    k_chk = (k * jnp.exp(post[GLA_CHUNK])).astype(BF16)

    head_vk = (lax.broadcasted_iota(jnp.int32, (vw, kw), 0) // B_HEAD_V
               == lax.broadcasted_iota(jnp.int32, (vw, kw), 1) // B_HEAD_K)
    head_vv = (lax.broadcasted_iota(jnp.int32, (vw, vw), 0) // B_HEAD_V
               == lax.broadcasted_iota(jnp.int32, (vw, vw), 1) // B_HEAD_V).astype(BF16)

    tt = lax.broadcasted_iota(jnp.int32, (B_HEADS * GLA_CHUNK, GLA_CHUNK), 0) % GLA_CHUNK
    ss = lax.broadcasted_iota(jnp.int32, (B_HEADS * GLA_CHUNK, GLA_CHUNK), 1)
    same_row = tt == ss
    splits_at = {n: (tt // (2 * n) == ss // (2 * n)) & (tt // n != ss // n) for n in levels}
    lane_k = lax.broadcasted_iota(jnp.int32, (1, kw), 1)
    lane_v = lax.broadcasted_iota(jnp.int32, (1, vw), 1)

    def stack_heads(x):
        return jnp.concatenate([jnp.where(lane_k // B_HEAD_K == h, x, jnp.zeros_like(x))
                                for h in range(B_HEADS)], axis=0)

    def scores(lhs, rhs):
        return lax.dot_general(stack_heads(lhs), rhs, (((1,), (1,)), ((), ())),
                               preferred_element_type=F32)

    outs = []
    for c in range(tm // GLA_CHUNK):
        lo = c * GLA_CHUNK
        rows = slice(lo, lo + GLA_CHUNK)
        vc = v[rows].astype(BF16)

        attn = jnp.where(same_row, scores(q_bf[rows], k_bf[rows]), 0.0)
        for n in levels:
            attn = jnp.where(splits_at[n], scores(q_lvl[n][rows], k_lvl[n][rows]), attn)
        res = jnp.dot(attn.astype(BF16), vc, preferred_element_type=F32)
        o_in = jnp.zeros((GLA_CHUNK, vw), F32)
        for h in range(B_HEADS):
            o_in = o_in + jnp.where(lane_v // B_HEAD_V == h,
                                    res[h * GLA_CHUNK:(h + 1) * GLA_CHUNK], 0.0)

        st = st_ref[...]
        o_st = lax.dot_general(q_chk[rows], st.astype(BF16), (((1,), (1,)), ((), ())),
                               preferred_element_type=F32)
        upd = lax.dot_general(vc, k_chk[rows], (((0,), (0,)), ((), ())),
                              preferred_element_type=F32)
        decay = jnp.exp(tot[GLA_CHUNK][lo:lo + 1, :])
        st_ref[...] = st * decay + jnp.where(head_vk, upd, 0.0)
        outs.append(o_in + o_st)

    o = jnp.concatenate(outs, axis=0)

    sq = o * o
    sq_hi = sq.astype(BF16)
    sq_lo = (sq - sq_hi.astype(F32)).astype(BF16)
    ms = (jnp.dot(sq_hi, head_vv, preferred_element_type=F32)
          + jnp.dot(sq_lo, head_vv, preferred_element_type=F32)) * (1.0 / B_HEAD_V)
    og = og_ref[0]
    o_ref[0] = (o * lax.rsqrt(ms + NORM_EPS) * ng_ref[...]) * (og * jax.nn.sigmoid(og))


def _gla(bq, bk, bv, gl, og, norm_g):
    bsz, seq, _ = bq.shape
    tm = min(512, seq)
    row = lambda n: pl.BlockSpec((1, tm, n), lambda b, i: (b, i, 0))
    return pl.pallas_call(
        functools.partial(_gla_kernel, tm=tm),
        grid=(bsz, seq // tm),
        in_specs=[row(B_KEY_WIDTH), row(B_KEY_WIDTH), row(B_VAL_WIDTH), row(B_KEY_WIDTH),
                  row(B_VAL_WIDTH), _resident((1, B_VAL_WIDTH))],
        out_specs=row(B_VAL_WIDTH),
        out_shape=jax.ShapeDtypeStruct((bsz, seq, B_VAL_WIDTH), F32),
        scratch_shapes=[pltpu.VMEM((B_VAL_WIDTH, B_KEY_WIDTH), F32)],
        compiler_params=_params(("arbitrary", "arbitrary")),
        name="gla",
    )(bq, bk, bv, gl, og, norm_g.reshape(1, B_VAL_WIDTH))


def _ffn_kernel(*refs, tm, bn, dff, final_norm, mixer):
    if mixer:
        x_ref, oa_ref, ob_ref, wa_ref, wb_ref = refs[:5]
        refs = refs[5:]
    else:
        x_ref, refs = refs[0], refs[1:]
    g_ref, wup_ref, cw_ref, cb_ref, wdn_ref, fg_ref, o_ref, halo_ref, ext_ref = refs
    nkb = dff // bn

    @pl.when(pl.program_id(1) == 0)
    def _():
        halo_ref[...] = jnp.zeros_like(halo_ref)

    x = x_ref[0]
    if mixer:
        o_ref[0] = (jnp.dot(oa_ref[0].astype(BF16), wa_ref[...], preferred_element_type=F32)
                    + jnp.dot(ob_ref[0].astype(BF16), wb_ref[...], preferred_element_type=F32))
        x = x + o_ref[0]
    hn = _rms(x, g_ref[...]).astype(BF16)

    def up_stage(kb):
        for part in range(2):
            c0 = part * dff + kb * bn
            up = jnp.dot(hn, wup_ref[:, c0:c0 + bn], preferred_element_type=F32)
            ext_ref[kb % 2, part, 0:SUBLANES, :] = halo_ref[:, c0:c0 + bn]
            ext_ref[kb % 2, part, SUBLANES:, :] = up
            halo_ref[:, c0:c0 + bn] = up[tm - SUBLANES:, :]

    def gate_stage(kb):
        halves = []
        for part in range(2):
            c0 = part * dff + kb * bn
            y = cb_ref[:, c0:c0 + bn]
            for i in range(FFN_CONV):
                off = SUBLANES - (FFN_CONV - 1) + i
                y = y + ext_ref[kb % 2, part, pl.ds(off, tm), :] * cw_ref[i:i + 1, c0:c0 + bn]
            halves.append(y)
        return (halves[0] * _gelu_tanh(halves[1])).astype(BF16)

    up_stage(0)
    if nkb > 1:
        up_stage(1)
    a = gate_stage(0)
    acc = x
    for kb in range(nkb):
        if kb + 2 < nkb:
            up_stage(kb + 2)
        a_next = gate_stage(kb + 1) if kb + 1 < nkb else None
        acc = acc + jnp.dot(a, wdn_ref[kb * bn:(kb + 1) * bn, :], preferred_element_type=F32)
        a = a_next
    if final_norm:
        acc = _rms(acc, fg_ref[...])
    o_ref[0] = acc


def _conv_ffn(x, g, w_up, conv_w, conv_b, w_down, final_g, final_norm, mixer=None):
    bsz, seq, d = x.shape
    dff = w_down.shape[0]
    tm = min(512, seq)
    bn = 1536
    rows = lambda n: pl.BlockSpec((1, tm, n), lambda b, i: (b, i, 0))
    row = rows(d)
    lead_specs, lead_args = [row], [x]
    if mixer is not None:
        o_a, o_b, w_out = mixer
        w = w_out.astype(BF16)
        lead_specs += [rows(A_WIDTH), rows(B_VAL_WIDTH),
                       _resident((A_WIDTH, d)), _resident((B_VAL_WIDTH, d))]
        lead_args += [o_a, o_b, w[:A_WIDTH], w[A_WIDTH:]]
    return pl.pallas_call(
        functools.partial(_ffn_kernel, tm=tm, bn=bn, dff=dff, final_norm=final_norm,
                          mixer=mixer is not None),
        grid=(bsz, seq // tm),
        in_specs=lead_specs + [_resident((1, d)), _resident((d, 2 * dff)),
                               _resident((FFN_CONV, 2 * dff)), _resident((1, 2 * dff)),
                               _resident((dff, d)), _resident((1, d))],
        out_specs=row,
        out_shape=jax.ShapeDtypeStruct((bsz, seq, d), F32),
        scratch_shapes=[pltpu.VMEM((SUBLANES, 2 * dff), F32),
                        pltpu.VMEM((2, 2, tm + SUBLANES, bn), F32)],
        compiler_params=_params(("arbitrary", "arbitrary")),
        name="conv_ffn_final" if final_norm else "conv_ffn",
    )(*lead_args, g.reshape(1, d), w_up.astype(BF16), conv_w, conv_b.reshape(1, 2 * dff),
      w_down.astype(BF16), final_g.reshape(1, d))


def _rglru_kernel(x_ref, g_ref, win_ref, cw_ref, cb_ref, wa_ref, ba_ref, wx_ref, bx_ref,
                  lam_ref, wout_ref, o_ref, ext_ref, halo_ref, hlast_ref, *, tm, drnn, nseq):
    seg = tm // SUBLANES
    shifts = RG_CONV - 1
    bw = drnn // RG_BLOCKS
    seqs = range(nseq)

    @pl.when(pl.program_id(1) == 0)
    def _():
        halo_ref[...] = jnp.zeros_like(halo_ref)
        hlast_ref[...] = jnp.zeros_like(hlast_ref)

    prow = lax.broadcasted_iota(jnp.int32, (tm, tm), 0)
    pcol = lax.broadcasted_iota(jnp.int32, (tm, tm), 1)
    perm = (pcol == (prow % SUBLANES) * seg + prow // SUBLANES).astype(BF16)
    unperm = (prow == (pcol % SUBLANES) * seg + pcol // SUBLANES).astype(BF16)
    first_sub = lax.broadcasted_iota(jnp.int32, (SUBLANES, 1), 0) == 0

    xs = [x_ref[b] for b in seqs]
    hn = [jnp.dot(perm, _rms(xs[b], g_ref[...]).astype(BF16),
                  preferred_element_type=F32).astype(BF16) for b in seqs]
    gate = [jnp.dot(hn[b], win_ref[:, 0:drnn], preferred_element_type=F32) for b in seqs]
    xr_raw = [jnp.dot(hn[b], win_ref[:, drnn:2 * drnn], preferred_element_type=F32) for b in seqs]

    xr = []
    for b in seqs:
        for r in range(shifts):
            lo = (seg - shifts + r) * SUBLANES
            cur = xr_raw[b][lo:lo + SUBLANES]
            ext_ref[b, r * SUBLANES:(r + 1) * SUBLANES, :] = jnp.where(
                first_sub, pltpu.roll(halo_ref[b, r], 1, 0), pltpu.roll(cur, 1, 0))
            halo_ref[b, r] = cur
        ext_ref[b, shifts * SUBLANES:, :] = xr_raw[b]
        acc = cb_ref[...]
        for i in range(RG_CONV):
            acc = acc + ext_ref[b, i * SUBLANES:i * SUBLANES + tm, :] * cw_ref[i:i + 1, :]
        xr.append(acc)

    ys = []
    for b in seqs:
        xb = xr[b].astype(BF16)
        ra, ri = [], []
        for n in range(RG_BLOCKS):
            xn = xb[:, n * bw:(n + 1) * bw]
            ra.append(jnp.dot(xn, wa_ref[n], preferred_element_type=F32))
            ri.append(jnp.dot(xn, wx_ref[n], preferred_element_type=F32))
        r = jax.nn.sigmoid(jnp.concatenate(ra, axis=1) + ba_ref[...])
        ig = jax.nn.sigmoid(jnp.concatenate(ri, axis=1) + bx_ref[...])
        log_a = r * lam_ref[...]
        a = jnp.exp(log_a)
        th = jnp.tanh(log_a)
        one_minus_a2 = -2.0 * th / (1.0 - th)
        u = jnp.sqrt(one_minus_a2) * (ig * xr[b])

        h_loc, a_cum = [u[0:SUBLANES]], [a[0:SUBLANES]]
        for j in range(1, seg):
            aj = a[j * SUBLANES:(j + 1) * SUBLANES]
            h_loc.append(aj * h_loc[-1] + u[j * SUBLANES:(j + 1) * SUBLANES])
            a_cum.append(aj * a_cum[-1])
        h_in = [hlast_ref[b]]
        for s in range(SUBLANES):
            h_in.append(h_loc[-1][s:s + 1] + a_cum[-1][s:s + 1] * h_in[-1])
        hlast_ref[b] = h_in[SUBLANES]
        h_seg = jnp.concatenate(h_in[:SUBLANES], axis=0)
        h = jnp.concatenate([h_loc[j] + a_cum[j] * h_seg for j in range(seg)], axis=0)
        ys.append((h * _gelu_tanh(gate[b])).astype(BF16))

    for b in seqs:
        y_t = jnp.dot(unperm, ys[b], preferred_element_type=F32).astype(BF16)
        o_ref[b] = xs[b] + jnp.dot(y_t, wout_ref[...], preferred_element_type=F32)


def _rglru(x, g, w_in, conv_w, conv_b, w_a, b_a, w_x, b_x, lam, w_out):
    bsz, seq, d = x.shape
    drnn = w_out.shape[0]
    bw = drnn // RG_BLOCKS
    tm = min(256, seq)
    nseq = 2 if bsz % 2 == 0 else 1
    row = pl.BlockSpec((nseq, tm, d), lambda b, i: (b, i, 0))
    lam_c = (-LRU_C) * jax.nn.softplus(-lam.astype(F32))
    return pl.pallas_call(
        functools.partial(_rglru_kernel, tm=tm, drnn=drnn, nseq=nseq),
        grid=(bsz // nseq, seq // tm),
        in_specs=[row, _resident((1, d)), _resident((d, 2 * drnn)), _resident((RG_CONV, drnn)),
                  _resident((1, drnn)), _resident((RG_BLOCKS, bw, bw)), _resident((1, drnn)),
                  _resident((RG_BLOCKS, bw, bw)), _resident((1, drnn)), _resident((1, drnn)),
                  _resident((drnn, d))],
        out_specs=row,
        out_shape=jax.ShapeDtypeStruct((bsz, seq, d), F32),
        scratch_shapes=[pltpu.VMEM((nseq, tm + (RG_CONV - 1) * SUBLANES, drnn), F32),
                        pltpu.VMEM((nseq, RG_CONV - 1, SUBLANES, drnn), F32),
                        pltpu.VMEM((nseq, 1, drnn), F32)],
        compiler_params=_params(("arbitrary", "arbitrary")),
        name="rglru",
    )(x, g.reshape(1, d), w_in.astype(BF16), conv_w, conv_b.reshape(1, drnn),
      w_a.astype(BF16), b_a.reshape(1, drnn), w_x.astype(BF16), b_x.reshape(1, drnn),
      lam_c.reshape(1, drnn), w_out.astype(BF16))


def kernel(x, mix_norm_g, ffn_norm_g, final_norm_g, ev_w_in, ev_w_gk2, ev_b_gk2, ev_gla_norm_g,
           ev_w_out, od_w_in, od_conv_w, od_conv_b, od_w_a, od_b_a, od_w_x, od_b_x, od_lambda,
           od_w_out, ffn_w_up, ffn_conv_w, ffn_conv_b, ffn_w_down):
    depth = mix_norm_g.shape[0]
    h = x
    for l in range(depth):
        j = l // 2
        mixer = None
        if l % 2 == 0:
            q, k, v, bias, bq, bk, bv, gl, og = _front0(h, mix_norm_g[l], ev_w_in[j], ev_w_gk2[j],
                                                      ev_b_gk2[j])
            o_a = _moba(q, k, v, bias)
            o_b = _gla(bq, bk, bv, gl, og, ev_gla_norm_g[j])
            mixer = (o_a, o_b, ev_w_out[j])
        else:
            h = _rglru(h, mix_norm_g[l], od_w_in[j], od_conv_w[j], od_conv_b[j], od_w_a[j],
                       od_b_a[j], od_w_x[j], od_b_x[j], od_lambda[j], od_w_out[j])
        h = _conv_ffn(h, ffn_norm_g[l], ffn_w_up[l], ffn_conv_w[l], ffn_conv_b[l], ffn_w_down[l],
                      final_norm_g, final_norm=(l == depth - 1), mixer=mixer)
    return h
```
